```python
import math
import jax
import jax.numpy as jnp
from jax import lax
import numpy as np

D_MODEL = 2048
BATCH = 2
SEQ = 4096
DEPTH = 1
DEC_BATCH = 32
DEC_SEQ = 8
PAST_LEN = 8192
PAGE_SIZE = 128

H_A = 8
DK_A = 128
DV_A = 128
HGRN_CHUNK = 64
H_B = 8
KV_B = 2
G_B = H_B // KV_B
HD_B = 128
CMP_STRIDE = 16
CMP_BLOCK = 2 * CMP_STRIDE
CMP_HID = 128
SEL_BLOCK = 64
N_SEL = 16
WINDOW = 512
Q_BLOCK = 128
FORCE_SCORE = 1.0e6
NUM_BUCKETS = 32
REL_MAX_DIST = 128
D_FF = 4 * D_MODEL
N_MOD = 6
EPS = 1e-6
W_A = H_A * DV_A
W_B = H_B * HD_B
IN_SPLITS = (H_A * DK_A, H_A * DK_A, H_A * DV_A, H_A * DV_A, H_B * HD_B) + (KV_B * HD_B,) * 6 + (3 * H_B, D_MODEL, D_MODEL)
IN_COLS = sum(IN_SPLITS)

kernel_name = 'hgrn2_nsa_hybrid_step'


def rmsnorm(x, gain):
    xf = x.astype(jnp.float32)
    return xf * lax.rsqrt(jnp.mean(xf * xf, axis=-1, keepdims=True) + EPS) * gain.astype(jnp.float32)


def masked_softmax(s, mask):
    s = jnp.where(mask, s.astype(jnp.float32), -jnp.inf)
    m = jnp.max(s, axis=-1, keepdims=True)
    m = jnp.where(jnp.isfinite(m), m, 0.0)
    e = jnp.exp(s - m)
    return e / jnp.maximum(jnp.sum(e, axis=-1, keepdims=True), 1e-30)


def rel_bucket(rel):
    n = jnp.maximum(rel, 0)
    exact = NUM_BUCKETS // 2
    nf = jnp.maximum(n, 1).astype(jnp.float32)
    large = exact + (jnp.log(nf / exact) / math.log(REL_MAX_DIST / exact) * (NUM_BUCKETS - exact)).astype(jnp.int32)
    return jnp.where(n < exact, n, jnp.minimum(large, NUM_BUCKETS - 1))


def hgrn_scan(q, k, logf, v, s0):
    b_sz, t_len, n_h, _ = q.shape
    dv = v.shape[-1]
    c = math.gcd(t_len, HGRN_CHUNK)
    n_c = t_len // c

    def chunks(a):
        return a.reshape(b_sz, n_c, c, n_h, a.shape[-1]).transpose(1, 0, 3, 2, 4)

    causal = jnp.tril(jnp.ones((c, c), dtype=bool))[:, :, None]

    def step(s, xs):
        qc, kc, lc, vc = xs
        b = jnp.cumsum(lc, axis=2)
        diff = b[:, :, :, None, :] - b[:, :, None, :, :]
        decay = jnp.exp(jnp.where(causal, diff, -jnp.inf))
        attn = jnp.einsum('bhtk,bhsk,bhtsk->bhts', qc, kc, decay)
        o = jnp.einsum('bhts,bhsv->bhtv', attn, vc) + jnp.einsum('bhtk,bhkv->bhtv', qc * jnp.exp(b), s)
        b_last = b[:, :, -1:, :]
        s = jnp.exp(b_last[:, :, 0, :, None]) * s + jnp.einsum('bhsk,bhsv->bhkv', kc * jnp.exp(b_last - b), vc)
        return s, o

    s_fin, o = lax.scan(step, s0, (chunks(q), chunks(k), chunks(logf), chunks(v)))
    return o.transpose(1, 0, 3, 2, 4).reshape(b_sz, t_len, n_h, dv), s_fin


def hgrn_mix(q, z, i, g, s0, lb, onorm):
    q, z, i, g = (a.astype(jnp.float32) for a in (q, z, i, g))
    logf = jnp.log(lb + (1.0 - lb) * jax.nn.sigmoid(z))
    k = (1.0 - lb) * jax.nn.sigmoid(-z)
    o, s_fin = hgrn_scan(q, k, logf, i, s0.astype(jnp.float32))
    o = rmsnorm(o, onorm) * jax.nn.silu(g)
    return o.reshape(o.shape[0], o.shape[1], W_A), s_fin


def compress(x, w1, b1, w2):
    b_sz, t_len = x.shape[:2]
    n_half = -(-t_len // CMP_STRIDE)
    x = jnp.pad(x.astype(jnp.float32), ((0, 0), (0, n_half * CMP_STRIDE - t_len), (0, 0), (0, 0)))
    xh = x.reshape(b_sz, n_half, CMP_STRIDE, KV_B, HD_B).transpose(0, 1, 3, 2, 4).reshape(b_sz, n_half, KV_B, CMP_STRIDE * HD_B)
    w1h = w1.reshape(2, CMP_STRIDE * HD_B, CMP_HID)
    pre = (xh @ w1h[0])[:, :-1] + (xh @ w1h[1])[:, 1:] + b1
    return jax.nn.silu(pre) @ w2


def compressed_attention(qg, k_cmp, v_cmp, q_pos, rel_table):
    n_c = k_cmp.shape[1]
    t_q = q_pos.shape[0]
    end = jnp.arange(n_c) * CMP_STRIDE + CMP_BLOCK - 1
    rel = q_pos[:, None] - end[None, :]
    s = jnp.einsum('bqkgd,bckd->bkgqc', qg, k_cmp)
    bias = rel_table[rel_bucket(rel)].reshape(t_q, n_c, KV_B, G_B).transpose(2, 3, 0, 1)
    p = masked_softmax(s + bias, rel >= 0)
    return jnp.einsum('bkgqc,bckd->bqkgd', p, v_cmp), p


def select_blocks(p_cmp, q_pos, t_k):
    n_c = p_cmp.shape[-1]
    n_s = -(-t_k // SEL_BLOCK)
    imp = jnp.sum(p_cmp, axis=2)
    cs = jnp.arange(n_c) * CMP_STRIDE
    ss = jnp.arange(n_s) * SEL_BLOCK
    overlap = ((cs[:, None] < ss[None, :] + SEL_BLOCK) & (cs[:, None] + CMP_BLOCK > ss[None, :])).astype(jnp.float32)
    imp = imp @ overlap
    cur = (q_pos // SEL_BLOCK)[:, None]
    blk = jnp.arange(n_s)[None, :]
    forced = (blk == 0) | (blk == cur) | (blk == cur - 1)
    score = jnp.where(blk <= cur, imp + jnp.where(forced, FORCE_SCORE, 0.0), -jnp.inf)
    _, idx = lax.top_k(score, min(N_SEL, n_s))
    return idx


def selected_attention(qg, idx, q_pos, kblk, vblk, rel_table):
    b_sz, t_q = qg.shape[:2]
    n_k = idx.shape[-1]
    b_ix = jnp.arange(b_sz)[:, None, None, None]
    h_ix = jnp.arange(KV_B)[None, :, None, None]
    kg = kblk[b_ix, h_ix, idx]
    vg = vblk[b_ix, h_ix, idx].reshape(b_sz, KV_B, t_q, n_k * SEL_BLOCK, HD_B)
    pos = idx[..., None] * SEL_BLOCK + jnp.arange(SEL_BLOCK)
    rel = q_pos[:, None, None] - pos
    s = jnp.einsum('bqkgd,bkqnsd->bkgqns', qg, kg)
    tab = rel_table.reshape(NUM_BUCKETS, KV_B, G_B).transpose(1, 0, 2)
    bias = jnp.moveaxis(tab[h_ix[..., None], rel_bucket(rel)], -1, 2)
    s = (s + bias).reshape(b_sz, KV_B, G_B, t_q, n_k * SEL_BLOCK)
    mask = (rel >= 0).reshape(b_sz, KV_B, 1, t_q, n_k * SEL_BLOCK)
    p = masked_softmax(s, mask)
    return jnp.einsum('bkgqm,bkqmd->bqkgd', p, vg)


def selected_attention_blocked(qg, idx, q_pos, ks_full, vs_full, rel_table):
    b_sz, t_k = ks_full.shape[:2]
    t_q = qg.shape[1]
    n_s = -(-t_k // SEL_BLOCK)
    pad = n_s * SEL_BLOCK - t_k

    def blocks(a):
        a = jnp.pad(a.astype(jnp.float32), ((0, 0), (0, pad), (0, 0), (0, 0)))
        return a.reshape(b_sz, n_s, SEL_BLOCK, KV_B, HD_B).transpose(0, 3, 1, 2, 4)

    kblk, vblk = blocks(ks_full), blocks(vs_full)
    qb = math.gcd(t_q, Q_BLOCK)
    n_q = t_q // qb
    qs = jnp.moveaxis(qg.reshape(b_sz, n_q, qb, KV_B, G_B, HD_B), 1, 0)
    ids = jnp.moveaxis(idx.reshape(b_sz, KV_B, n_q, qb, idx.shape[-1]), 2, 0)
    ps = q_pos.reshape(n_q, qb)
    o = lax.map(lambda a: selected_attention(a[0], a[1], a[2], kblk, vblk, rel_table), (qs, ids, ps))
    return jnp.moveaxis(o, 0, 1).reshape(b_sz, t_q, KV_B, G_B, HD_B)


def window_attention(qg, qw_pos, kvw_band, kw_pos, rel_table):
    b_sz = qg.shape[0]
    n_b, qb = qw_pos.shape
    kb = kw_pos.shape[1]
    q_blk = qg.reshape(b_sz, n_b, qb, KV_B, G_B, HD_B)
    s = jnp.einsum('bnqkgd,bnskd->bnkgqs', q_blk, kvw_band[:, :, :, 0])
    rel = qw_pos[:, :, None] - kw_pos[:, None, :]
    bias = rel_table[rel_bucket(rel)].reshape(n_b, qb, kb, KV_B, G_B).transpose(0, 3, 4, 1, 2)
    mask = (rel >= 0) & (rel < WINDOW) & (kw_pos[:, None, :] >= 0)
    p = masked_softmax(s + bias, mask[:, None, None])
    o = jnp.einsum('bnkgqs,bnskd->bnqkgd', p, kvw_band[:, :, :, 1])
    return o.reshape(b_sz, n_b * qb, KV_B, G_B, HD_B)


def prompt_band(kv):
    b_sz, t_len = kv.shape[:2]
    qb = math.gcd(t_len, Q_BLOCK)
    n_b = t_len // qb
    n_w = WINDOW // qb
    kp = jnp.pad(kv, ((0, 0), (WINDOW, 0), (0, 0), (0, 0), (0, 0)))
    kb = kp.reshape((b_sz, n_w + n_b, qb) + kv.shape[2:])
    band = jnp.concatenate([kb[:, j:j + n_b] for j in range(n_w + 1)], axis=2)
    pos = jnp.arange(-WINDOW, t_len).reshape(n_w + n_b, qb)
    band_pos = jnp.concatenate([pos[j:j + n_b] for j in range(n_w + 1)], axis=1)
    return band, jnp.arange(t_len).reshape(n_b, qb), band_pos


def nsa_core(q, gates, kc_full, vc_full, ks_full, vs_full, kvw_band, qw_pos, kw_pos, q_pos, nsa_w):
    (q_norm, kc_norm, ck_w1, ck_b1, ck_w2, cv_w1, cv_b1, cv_w2, rel_table) = nsa_w
    b_sz, t_len = q.shape[:2]
    rel_table = rel_table.astype(jnp.float32)
    qg = (rmsnorm(q, q_norm) * HD_B ** -0.5).reshape(b_sz, t_len, KV_B, G_B, HD_B)
    k_cmp = rmsnorm(compress(kc_full, ck_w1, ck_b1, ck_w2), kc_norm)
    v_cmp = compress(vc_full, cv_w1, cv_b1, cv_w2)
    o_cmp, p_cmp = compressed_attention(qg, k_cmp, v_cmp, q_pos, rel_table)
    idx = select_blocks(p_cmp, q_pos, ks_full.shape[1])
    o_sel = selected_attention_blocked(qg, idx, q_pos, ks_full, vs_full, rel_table)
    o_win = window_attention(qg, qw_pos, kvw_band, kw_pos, rel_table)
    gt = jax.nn.sigmoid(gates.astype(jnp.float32)).reshape(b_sz, t_len, KV_B, G_B, 3)
    o = gt[..., 0:1] * o_cmp + gt[..., 1:2] * o_sel + gt[..., 2:3] * o_win
    return o.reshape(b_sz, t_len, W_B)


def nsa_prompt(q, kc, vc, ks, vs, kw, vw, gates, ks_norm, kw_norm, nsa_w):
    b_sz, t_len = q.shape[:2]
    ks = rmsnorm(ks, ks_norm)
    kw = rmsnorm(kw, kw_norm)
    kvw = jnp.stack([kw, vw.astype(jnp.float32)], axis=2)
    band, qw_pos, kw_pos = prompt_band(kvw)
    o = nsa_core(q, gates, kc, vc, ks, vs, band, qw_pos, kw_pos, jnp.arange(t_len), nsa_w)
    pages = (b_sz, t_len // PAGE_SIZE, PAGE_SIZE, 2, KV_B, HD_B)
    cmp_new = jnp.stack([kc, vc], axis=2).reshape(pages)
    sel_new = jnp.stack([ks, vs.astype(jnp.float32)], axis=2).reshape(pages)
    win_new = kvw[:, t_len - min(WINDOW, t_len):]
    return o, (cmp_new, sel_new, win_new)


def nsa_sample(q, kc, vc, ks, vs, kw, vw, gates, ks_norm, kw_norm, nsa_w, cache_cmp, cache_sel, cache_win, page_table):
    b_sz, t_len = q.shape[:2]
    past = page_table.shape[1] * PAGE_SIZE
    wb = cache_win.shape[1]
    ks = rmsnorm(ks, ks_norm)
    kw = rmsnorm(kw, kw_norm)
    cmp_new = jnp.stack([kc, vc], axis=2).astype(jnp.float32)
    sel_new = jnp.stack([ks, vs.astype(jnp.float32)], axis=2)
    win_rows = jnp.stack([kw, vw.astype(jnp.float32)], axis=2)

    def gather_past(cache):
        return cache[page_table].reshape((b_sz, past) + cache.shape[2:]).astype(jnp.float32)

    cmp_full = jnp.concatenate([gather_past(cache_cmp), cmp_new], axis=1)
    sel_full = jnp.concatenate([gather_past(cache_sel), sel_new], axis=1)
    win_full = jnp.concatenate([cache_win.astype(jnp.float32), win_rows], axis=1)
    q_pos = past + jnp.arange(t_len)
    kw_pos = (past - wb + jnp.arange(wb + t_len))[None]
    o = nsa_core(q, gates, cmp_full[:, :, 0], cmp_full[:, :, 1], sel_full[:, :, 0], sel_full[:, :, 1],
                 win_full[:, None], q_pos[None], kw_pos, q_pos, nsa_w)
    return o, (cmp_new, sel_new, win_full[:, t_len:])


def trunk_layer(x, c, lw, lb, s0, nsa_fn):
    (ada_w, ada_b, norm1, norm2, w_in, hgrn_onorm, w_ba, w_bb, w_out, mlp_w1, mlp_w2) = lw
    b_sz, t_len, _ = x.shape
    mods = jax.nn.silu(c.astype(jnp.float32)) @ ada_w + ada_b
    sh1, sc1, g1, sh2, sc2, g2 = jnp.split(mods[:, None, :], N_MOD, axis=-1)
    h = rmsnorm(x, norm1) * (1.0 + sc1) + sh1
    cuts = np.cumsum(np.array(IN_SPLITS))[:-1].tolist()
    (q_a, z_a, i_a, g_a, q_b, kc, vc, ks, vs, kw, vw, gate_b, mg_a, mg_b) = jnp.split(h @ w_in, cuts, axis=-1)

    def heads(a, n, d):
        return a.reshape(b_sz, t_len, n, d)

    o_a, s_fin = hgrn_mix(heads(q_a, H_A, DK_A), heads(z_a, H_A, DK_A), heads(i_a, H_A, DV_A),
                          heads(g_a, H_A, DV_A), s0, lb, hgrn_onorm)
    kvh = [heads(a, KV_B, HD_B) for a in (kc, vc, ks, vs, kw, vw)]
    o_b, nsa_state = nsa_fn(heads(q_b, H_B, HD_B), *kvh, gate_b)
    merged = jax.nn.sigmoid(mg_a) * (o_a @ w_ba) + jax.nn.sigmoid(mg_b) * (o_b @ w_bb)
    x1 = x.astype(jnp.float32) + g1 * (merged @ w_out)
    h2 = rmsnorm(x1, norm2) * (1.0 + sc2) + sh2
    u = jax.nn.relu(h2 @ mlp_w1)
    y = x1 + g2 * ((u * u) @ mlp_w2)
    return y.astype(x.dtype), s_fin, nsa_state


def setup_inputs(seed: int = 0) -> dict:
    key = jax.random.key(seed)
    ks = iter(jax.random.split(key, 32))
    n_pages = PAST_LEN // PAGE_SIZE
    n_used = DEC_BATCH * n_pages
    n_phys = n_used + max(1, n_used // 4)
    win_buf = min(WINDOW, PAST_LEN)
    f32 = jnp.float32

    def nrm(shape, scale):
        return jax.random.normal(next(ks), shape, f32) * scale

    def gain(shape):
        return 1.0 + nrm(shape, 0.1)

    page_table = jax.random.permutation(next(ks), n_phys)[:n_used].astype(jnp.int32).reshape(DEC_BATCH, n_pages)
    return {
        'x_prompt': nrm((BATCH, SEQ, D_MODEL), 1.0),
        'x_sample': nrm((DEC_BATCH, DEC_SEQ, D_MODEL), 1.0),
        'c_prompt': nrm((BATCH, D_MODEL), 1.0),
        'c_sample': nrm((DEC_BATCH, D_MODEL), 1.0),
        'cache_cmp_kv': nrm((DEPTH, n_phys, PAGE_SIZE, 2, KV_B, HD_B), 1.0),
        'cache_sel_kv': nrm((DEPTH, n_phys, PAGE_SIZE, 2, KV_B, HD_B), 1.0),
        'cache_win_kv': nrm((DEPTH, DEC_BATCH, win_buf, 2, KV_B, HD_B), 1.0),
        'state_hgrn': nrm((DEPTH, DEC_BATCH, H_A, DK_A, DV_A), 0.5),
        'page_table': page_table,
        'hgrn_lb_logits': nrm((DEPTH + 1, H_A * DK_A), 0.5),
        'rel_bias': nrm((NUM_BUCKETS, H_B), 0.5),
        'ada_w': nrm((DEPTH, D_MODEL, N_MOD * D_MODEL), D_MODEL ** -0.5),
        'ada_b': nrm((DEPTH, N_MOD * D_MODEL), 0.02),
        'norm1': gain((DEPTH, D_MODEL)),
        'norm2': gain((DEPTH, D_MODEL)),
        'w_in': nrm((DEPTH, D_MODEL, IN_COLS), D_MODEL ** -0.5),
        'hgrn_onorm': gain((DEPTH, H_A, DV_A)),
        'nsa_q_norm': gain((DEPTH, HD_B)),
        'nsa_kc_norm': gain((DEPTH, HD_B)),
        'nsa_ks_norm': gain((DEPTH, HD_B)),
        'nsa_kw_norm': gain((DEPTH, HD_B)),
        'cmp_k_w1': nrm((DEPTH, CMP_BLOCK * HD_B, CMP_HID), (CMP_BLOCK * HD_B) ** -0.5),
        'cmp_k_b1': nrm((DEPTH, CMP_HID), 0.02),
        'cmp_k_w2': nrm((DEPTH, CMP_HID, HD_B), CMP_HID ** -0.5),
        'cmp_v_w1': nrm((DEPTH, CMP_BLOCK * HD_B, CMP_HID), (CMP_BLOCK * HD_B) ** -0.5),
        'cmp_v_b1': nrm((DEPTH, CMP_HID), 0.02),
        'cmp_v_w2': nrm((DEPTH, CMP_HID, HD_B), CMP_HID ** -0.5),
        'w_branch_a': nrm((DEPTH, W_A, D_MODEL), W_A ** -0.5),
        'w_branch_b': nrm((DEPTH, W_B, D_MODEL), W_B ** -0.5),
        'w_out': nrm((DEPTH, D_MODEL, D_MODEL), D_MODEL ** -0.5),
        'mlp_w1': nrm((DEPTH, D_MODEL, D_FF), D_MODEL ** -0.5),
        'mlp_w2': nrm((DEPTH, D_FF, D_MODEL), D_FF ** -0.5),
    }


def reference(x_prompt, x_sample, c_prompt, c_sample, cache_cmp_kv, cache_sel_kv, cache_win_kv, state_hgrn,
              page_table, hgrn_lb_logits, rel_bias, ada_w, ada_b, norm1, norm2, w_in, hgrn_onorm,
              nsa_q_norm, nsa_kc_norm, nsa_ks_norm, nsa_kw_norm, cmp_k_w1, cmp_k_b1, cmp_k_w2,
              cmp_v_w1, cmp_v_b1, cmp_v_w2, w_branch_a, w_branch_b, w_out, mlp_w1, mlp_w2):
    lb_all = jnp.cumsum(jax.nn.softmax(hgrn_lb_logits.astype(jnp.float32), axis=0), axis=0)
    y_p, y_s = x_prompt, x_sample
    cmp_p, cmp_s, sel_p, sel_s, win_p, win_s, hg_p, hg_s = [], [], [], [], [], [], [], []
    for l in range(DEPTH):
        lw = (ada_w[l], ada_b[l], norm1[l], norm2[l], w_in[l], hgrn_onorm[l], w_branch_a[l], w_branch_b[l],
              w_out[l], mlp_w1[l], mlp_w2[l])
        nsa_w = (nsa_q_norm[l], nsa_kc_norm[l], cmp_k_w1[l], cmp_k_b1[l], cmp_k_w2[l],
                 cmp_v_w1[l], cmp_v_b1[l], cmp_v_w2[l], rel_bias)
        lb = lb_all[l].reshape(H_A, DK_A)
        s0_p = jnp.zeros((x_prompt.shape[0], H_A, DK_A, DV_A), jnp.float32)
        y_p, hp, (cp, sp, wp) = trunk_layer(
            y_p, c_prompt, lw, lb, s0_p,
            lambda *a: nsa_prompt(*a, nsa_ks_norm[l], nsa_kw_norm[l], nsa_w))
        y_s, hs, (cs, ss, wsm) = trunk_layer(
            y_s, c_sample, lw, lb, state_hgrn[l],
            lambda *a: nsa_sample(*a, nsa_ks_norm[l], nsa_kw_norm[l], nsa_w,
                                  cache_cmp_kv[l], cache_sel_kv[l], cache_win_kv[l], page_table))
        cmp_p.append(cp)
        cmp_s.append(cs)
        sel_p.append(sp)
        sel_s.append(ss)
        win_p.append(wp)
        win_s.append(wsm)
        hg_p.append(hp)
        hg_s.append(hs)
    dt = x_prompt.dtype
    return (y_p, y_s,
            jnp.stack(cmp_p).astype(dt), jnp.stack(cmp_s).astype(dt),
            jnp.stack(sel_p).astype(dt), jnp.stack(sel_s).astype(dt),
            jnp.stack(win_p).astype(dt), jnp.stack(win_s).astype(dt),
            jnp.stack(hg_p).astype(dt), jnp.stack(hg_s).astype(dt))
```

```python
import functools
import math

import numpy as np
import jax
import jax.numpy as jnp
from jax import lax
from jax.experimental import pallas as pl
from jax.experimental.pallas import tpu as pltpu

F32 = jnp.float32
BF16 = jnp.bfloat16

D_MODEL = 2048
N_HEADS_A = 8
HEAD_DIM = 128
HGRN_W = N_HEADS_A * HEAD_DIM
N_HEADS_B = 8
N_KV = 2
GROUP = N_HEADS_B // N_KV
NSA_W = N_HEADS_B * HEAD_DIM
KV_W = N_KV * HEAD_DIM
PAGE = 128
CMP_STRIDE = 16
CMP_BLOCK = 32
CMP_HID = 128
SEL_BLOCK = 64
N_SEL = 16
WINDOW = 512
FORCE_SCORE = 1.0e6
NUM_BUCKETS = 32
REL_MAX_DIST = 128
D_FF = 4 * D_MODEL
EPS = 1e-6
N_GATE = 3 * N_HEADS_B

LANES = 128
SUBLANES = 8
VMEM_LIMIT = 56 * 1024 * 1024

COL_HGRN = 0
COL_MGA = 4 * HGRN_W
COL_MGB = COL_MGA + D_MODEL
COL_QB = COL_MGB + D_MODEL
COL_CMP = COL_QB + NSA_W
COL_SEL = COL_CMP + 2 * KV_W
COL_WIN = COL_SEL + 2 * KV_W
COL_GATE = COL_WIN + 2 * KV_W
PROJ_TN = 512
PROJ_COLS = COL_GATE + PROJ_TN


def _cparams(sem):
    return pltpu.CompilerParams(dimension_semantics=sem, vmem_limit_bytes=VMEM_LIMIT)


def _sigmoid(x):
    return 1.0 / (1.0 + jnp.exp(-x))


def _silu(x):
    return x * _sigmoid(x)


def _rms(x, gain):
    return x * lax.rsqrt(jnp.mean(x * x, axis=-1, keepdims=True) + EPS) * gain


def _ada_kernel(c_ref, w_ref, b_ref, o_ref):
    a = _silu(c_ref[...]).astype(BF16)
    o_ref[...] = jnp.dot(a, w_ref[...].astype(BF16), preferred_element_type=F32) + b_ref[...]


def ada_mods(c, w, b):
    r, d = c.shape
    n = w.shape[1]
    tn = 1024
    return pl.pallas_call(
        _ada_kernel,
        grid=(n // tn,),
        in_specs=[pl.BlockSpec((r, d), lambda j: (0, 0)),
                  pl.BlockSpec((d, tn), lambda j: (0, j)),
                  pl.BlockSpec((1, tn), lambda j: (0, j))],
        out_specs=pl.BlockSpec((r, tn), lambda j: (0, j)),
        out_shape=jax.ShapeDtypeStruct((r, n), F32),
        compiler_params=_cparams(("parallel",)),
        name="ada_mods",
    )(c, w, b.reshape(1, n))


def _inproj_kernel(x_ref, gain_ref, sc_ref, sh_ref, w_ref, o_ref, h_ref):
    @pl.when(pl.program_id(1) == 0)
    def _():
        h = _rms(x_ref[...], gain_ref[...]) * (1.0 + sc_ref[0]) + sh_ref[0]
        h_ref[...] = h.astype(BF16)

    o_ref[...] = jnp.dot(h_ref[...], w_ref[...], preferred_element_type=F32)


def _mod_spec(mod, tm, rows_per_batch):
    d = mod.shape[-1]
    if mod.shape[1] == 1:
        return pl.BlockSpec((1, 1, d), lambda i, *_: ((i * tm) // rows_per_batch, 0, 0))
    return pl.BlockSpec((1, tm, d), lambda i, *_: (0, i, 0))


def in_proj(x2, gain, sc, sh, w, rows_per_batch, tm):
    m, d = x2.shape
    n = w.shape[1]
    tn = PROJ_TN
    return pl.pallas_call(
        _inproj_kernel,
        grid=(m // tm, n // tn),
        in_specs=[pl.BlockSpec((tm, d), lambda i, j: (i, 0)),
                  pl.BlockSpec((1, d), lambda i, j: (0, 0)),
                  _mod_spec(sc, tm, rows_per_batch),
                  _mod_spec(sh, tm, rows_per_batch),
                  pl.BlockSpec((d, tn), lambda i, j: (0, j))],
        out_specs=pl.BlockSpec((tm, tn), lambda i, j: (i, j)),
        out_shape=jax.ShapeDtypeStruct((m, n), F32),
        scratch_shapes=[pltpu.VMEM((tm, d), BF16)],
        compiler_params=_cparams(("parallel", "arbitrary")),
        name="in_proj",
    )(x2, gain.reshape(1, d), sc, sh, w)


def _post_kernel(x_ref, oa_ref, ob_ref, mga_ref, mgb_ref, wba_ref, wbb_ref, wout_ref,
                 g1_ref, gain2_ref, sc2_ref, sh2_ref, x1_ref, h2_ref):
    ya = jnp.dot(oa_ref[...], wba_ref[...], preferred_element_type=F32)
    yb = jnp.dot(ob_ref[...], wbb_ref[...], preferred_element_type=F32)
    merged = _sigmoid(mga_ref[...]) * ya + _sigmoid(mgb_ref[...]) * yb
    y = jnp.dot(merged.astype(BF16), wout_ref[...], preferred_element_type=F32)
    x1 = x_ref[...] + g1_ref[0] * y
    x1_ref[...] = x1
    h2_ref[...] = (_rms(x1, gain2_ref[...]) * (1.0 + sc2_ref[0]) + sh2_ref[0]).astype(BF16)


def post_attn(x2, oa, ob, proj, wba, wbb, wout, g1, gain2, sc2, sh2, rows_per_batch, tm):
    m, d = x2.shape
    const = lambda i: (0, 0)
    resident = lambda shape: pl.BlockSpec(shape, const, pipeline_mode=pl.Buffered(1))
    return pl.pallas_call(
        _post_kernel,
        grid=(m // tm,),
        in_specs=[pl.BlockSpec((tm, d), lambda i: (i, 0)),
                  pl.BlockSpec((tm, HGRN_W), lambda i: (i, 0)),
                  pl.BlockSpec((tm, NSA_W), lambda i: (i, 0)),
                  pl.BlockSpec((tm, d), lambda i: (i, COL_MGA // D_MODEL)),
                  pl.BlockSpec((tm, d), lambda i: (i, COL_MGB // D_MODEL)),
                  resident((HGRN_W, d)),
                  resident((NSA_W, d)),
                  resident((d, d)),
                  _mod_spec(g1, tm, rows_per_batch),
                  pl.BlockSpec((1, d), const),
                  _mod_spec(sc2, tm, rows_per_batch),
                  _mod_spec(sh2, tm, rows_per_batch)],
        out_specs=[pl.BlockSpec((tm, d), lambda i: (i, 0)),
                   pl.BlockSpec((tm, d), lambda i: (i, 0))],
        out_shape=[jax.ShapeDtypeStruct((m, d), F32), jax.ShapeDtypeStruct((m, d), BF16)],
        compiler_params=_cparams(("parallel",)),
        name="post_attn",
    )(x2, oa, ob, proj, proj, wba, wbb, wout, g1, gain2.reshape(1, d), sc2, sh2)


def _mlp_kernel(h_ref, x1_ref, w1_ref, w2_ref, g2_ref, y_ref, acc_ref):
    f = pl.program_id(1)
    u = jnp.maximum(jnp.dot(h_ref[...], w1_ref[...], preferred_element_type=F32), 0.0)
    part = jnp.dot((u * u).astype(BF16), w2_ref[...], preferred_element_type=F32)

    @pl.when(f == 0)
    def _():
        acc_ref[...] = part

    @pl.when(f > 0)
    def _():
        acc_ref[...] += part

    @pl.when(f == pl.num_programs(1) - 1)
    def _():
        y_ref[...] = x1_ref[...] + g2_ref[0] * acc_ref[...]


def mlp(h2, x1, w1, w2, g2, rows_per_batch, tm, tf):
    m, d = x1.shape
    ff = w1.shape[1]
    return pl.pallas_call(
        _mlp_kernel,
        grid=(m // tm, ff // tf),
        in_specs=[pl.BlockSpec((tm, d), lambda i, f: (i, 0)),
                  pl.BlockSpec((tm, d), lambda i, f: (i, 0)),
                  pl.BlockSpec((d, tf), lambda i, f: (0, f)),
                  pl.BlockSpec((tf, d), lambda i, f: (f, 0)),
                  _mod_spec(g2, tm, rows_per_batch)],
        out_specs=pl.BlockSpec((tm, d), lambda i, f: (i, 0)),
        out_shape=jax.ShapeDtypeStruct((m, d), F32),
        scratch_shapes=[pltpu.VMEM((tm, d), F32)],
        compiler_params=_cparams(("parallel", "arbitrary")),
        name="mlp",
    )(h2, x1, w1, w2, g2)


_NT = (((1,), (1,)), ((), ()))
_TN = (((0,), (0,)), ((), ()))


def _hgrn_kernel(q_ref, z_ref, v_ref, g_ref, lb_ref, on_ref, s0_ref, o_ref, sfin_ref, st_ref, b_ref, *, chunk):
    c = pl.program_id(2)

    @pl.when(c == 0)
    def _():
        st_ref[...] = s0_ref[0, 0].T

    q = q_ref[...]
    z = z_ref[...]
    v = v_ref[...]
    lb = lb_ref[0]
    e = jnp.exp(-jnp.abs(z))
    r = 1.0 / (1.0 + e)
    pos = z >= 0.0
    logf = jnp.log(lb + (1.0 - lb) * jnp.where(pos, r, e * r))
    k = (1.0 - lb) * jnp.where(pos, e * r, r)

    t = lax.broadcasted_iota(jnp.int32, (chunk, HEAD_DIM), 0)
    b = logf
    s = 1
    while s < chunk:
        b = b + jnp.where(t >= s, pltpu.roll(b, s, 0), 0.0)
        s *= 2
    b_ref[...] = b

    t8 = t & (SUBLANES - 1)
    o = jnp.zeros((chunk, HEAD_DIM), F32)
    for d in range(SUBLANES):
        kd, bd, vd = (k, b, v) if d == 0 else (pltpu.roll(k, d, 0), pltpu.roll(b, d, 0), pltpu.roll(v, d, 0))
        w = jnp.exp(jnp.where(t8 >= d, b - bd, -jnp.inf))
        o = o + jnp.sum(q * kd * w, axis=-1, keepdims=True) * vd

    if chunk > SUBLANES:
        row = lax.broadcasted_iota(jnp.int32, (chunk, chunk), 0)
        col = lax.broadcasted_iota(jnp.int32, (chunk, chunk), 1)
        apart = row ^ col
        att = jnp.zeros((chunk, chunk), F32)
        m = SUBLANES
        while m < chunk:
            refs = [jnp.broadcast_to(b_ref[pl.ds(blk * 2 * m + m - 1, 1), :], (2 * m, HEAD_DIM))
                    for blk in range(chunk // (2 * m))]
            ref_b = refs[0] if len(refs) == 1 else jnp.concatenate(refs, axis=0)
            second = (t & (2 * m - 1)) >= m
            w = jnp.exp(jnp.where(second, b - ref_b, ref_b - b))
            qs = jnp.where(second, q * w, 0.0).astype(BF16)
            ks = jnp.where(second, 0.0, k * w).astype(BF16)
            a_m = lax.dot_general(qs, ks, _NT, preferred_element_type=F32)
            att = att + jnp.where(apart < 2 * m, a_m, 0.0)
            m *= 2
        o = o + jnp.dot(att.astype(BF16), v.astype(BF16), preferred_element_type=F32)

    st = st_ref[...]
    b_last = b_ref[pl.ds(chunk - 1, 1), :]
    o = o + lax.dot_general((q * jnp.exp(b)).astype(BF16), st.astype(BF16), _NT, preferred_element_type=F32)
    kt = (k * jnp.exp(b_last - b)).astype(BF16)
    vb = v.astype(BF16)
    if chunk < 2 * SUBLANES:
        pad = jnp.zeros((2 * SUBLANES - chunk, HEAD_DIM), BF16)
        kt = jnp.concatenate([kt, pad], axis=0)
        vb = jnp.concatenate([vb, pad], axis=0)
    st_new = jnp.exp(b_last) * st + lax.dot_general(vb, kt, _TN, preferred_element_type=F32)
    st_ref[...] = st_new

    o_ref[...] = (_rms(o, on_ref[0]) * _silu(g_ref[...])).astype(o_ref.dtype)

    @pl.when(c == pl.num_programs(2) - 1)
    def _():
        sfin_ref[0, 0] = st_new.T


def hgrn(proj, lb, onorm, s0, n_batch, t_len, chunk):
    m = proj.shape[0]
    n_c = t_len // chunk
    hb = HGRN_W // HEAD_DIM

    def col(group):
        return pl.BlockSpec((chunk, HEAD_DIM), lambda bi, h, c: (bi * n_c + c, group * hb + h))

    vec = pl.BlockSpec((1, 1, HEAD_DIM), lambda bi, h, c: (h, 0, 0))
    state = pl.BlockSpec((1, 1, HEAD_DIM, HEAD_DIM), lambda bi, h, c: (bi, h, 0, 0))
    o_dtype = BF16 if chunk % (2 * SUBLANES) == 0 else F32
    o, s_fin = pl.pallas_call(
        functools.partial(_hgrn_kernel, chunk=chunk),
        grid=(n_batch, N_HEADS_A, n_c),
        in_specs=[col(0), col(1), col(2), col(3), vec, vec, state],
        out_specs=[pl.BlockSpec((chunk, HEAD_DIM), lambda bi, h, c: (bi * n_c + c, h)), state],
        out_shape=[jax.ShapeDtypeStruct((m, HGRN_W), o_dtype),
                   jax.ShapeDtypeStruct((n_batch, N_HEADS_A, HEAD_DIM, HEAD_DIM), F32)],
        scratch_shapes=[pltpu.VMEM((HEAD_DIM, HEAD_DIM), F32), pltpu.VMEM((chunk, HEAD_DIM), F32)],
        compiler_params=_cparams(("parallel", "parallel", "arbitrary")),
        name="hgrn",
    )(proj, proj, proj, proj, lb.reshape(N_HEADS_A, 1, HEAD_DIM), onorm.reshape(N_HEADS_A, 1, HEAD_DIM), s0)
    return o.astype(BF16), s_fin


def _bucket_steps():
    n = np.arange(REL_MAX_DIST)
    exact = NUM_BUCKETS // 2
    val = np.log(np.maximum(n, 1) / exact) / math.log(REL_MAX_DIST / exact) * (NUM_BUCKETS - exact)
    frac = np.abs(val - np.round(val))[exact + 1:]
    assert frac.min() > 1e-3, "a bucket edge sits on an integer distance"
    lut = np.where(n < exact, n, np.minimum(exact + np.floor(np.maximum(val, 0.0)).astype(np.int64), NUM_BUCKETS - 1))
    assert lut[-1] == NUM_BUCKETS - 1
    return int(lut[0]), [(int(i), int(lut[i])) for i in range(1, REL_MAX_DIST) if lut[i] != lut[i - 1]]


_BUCKET0, _BUCKET_EDGES = _bucket_steps()
FAR_DIST = _BUCKET_EDGES[-1][0]


def _bias(rel, tab_ref, head):
    val = jnp.full(rel.shape, tab_ref[_BUCKET0 * N_HEADS_B + head], F32)
    for edge, bucket in _BUCKET_EDGES:
        val = jnp.where(rel >= edge, tab_ref[bucket * N_HEADS_B + head], val)
    return val


def _stack_heads(x):
    return jnp.concatenate([x[:, g * HEAD_DIM:(g + 1) * HEAD_DIM] for g in range(GROUP)], axis=0)


def _tile_heads(x):
    return jnp.concatenate([x] * GROUP, axis=0)


def _masked_softmax(s, mask):
    s = jnp.where(mask, s, -jnp.inf)
    m = jnp.max(s, axis=-1, keepdims=True)
    m = jnp.where(m == -jnp.inf, 0.0, m)
    e = jnp.exp(s - m)
    return e / jnp.maximum(jnp.sum(e, axis=-1, keepdims=True), 1e-30)


def _flash_init(m_ref, l_ref, acc_ref):
    m_ref[...] = jnp.full(m_ref.shape, -jnp.inf, F32)
    l_ref[...] = jnp.zeros(l_ref.shape, F32)
    acc_ref[...] = jnp.zeros(acc_ref.shape, F32)


def _flash_update(s, mask, v, m_ref, l_ref, acc_ref):
    s = jnp.where(mask, s, -jnp.inf)
    m_prev = m_ref[...]
    m_new = jnp.maximum(m_prev, jnp.max(s, axis=-1, keepdims=True))
    m_safe = jnp.where(m_new == -jnp.inf, 0.0, m_new)
    alpha = jnp.exp(m_prev - m_safe)
    p = jnp.exp(s - m_safe)
    l_ref[...] = alpha * l_ref[...] + jnp.sum(p, axis=-1, keepdims=True)
    acc_ref[...] = alpha * acc_ref[...] + jnp.dot(p.astype(BF16), v, preferred_element_type=F32)
    m_ref[...] = m_new


def _flash_result(l_ref, acc_ref):
    return acc_ref[...] / jnp.maximum(l_ref[...], 1e-30)


def _block_importance(p, n_cmp, n_blocks, width):
    rows = p.shape[0] // GROUP
    imp = p[0:rows]
    for g in range(1, GROUP):
        imp = imp + p[g * rows:(g + 1) * rows]
    ci = lax.broadcasted_iota(jnp.int32, (n_cmp, width), 0) * CMP_STRIDE
    si = lax.broadcasted_iota(jnp.int32, (n_cmp, width), 1) * SEL_BLOCK
    overlap = (ci < si + SEL_BLOCK) & (ci + CMP_BLOCK > si) & (si < n_blocks * SEL_BLOCK)
    ov = jnp.where(overlap, 1.0, 0.0).astype(BF16)
    hi = imp.astype(BF16)
    lo = (imp - hi.astype(F32)).astype(BF16)
    return jnp.dot(hi, ov, preferred_element_type=F32) + jnp.dot(lo, ov, preferred_element_type=F32)


def _select_blocks(imp, cur):
    blk = lax.broadcasted_iota(jnp.int32, imp.shape, 1)
    forced = (blk == 0) | (blk == cur) | (blk == cur - 1)
    score = jnp.where(blk <= cur, imp + jnp.where(forced, FORCE_SCORE, 0.0), -jnp.inf)
    sel = jnp.zeros(imp.shape, F32)
    width = imp.shape[1]
    for _ in range(N_SEL):
        top = jnp.max(score, axis=-1, keepdims=True)
        first = jnp.min(jnp.where(score == top, blk, width), axis=-1, keepdims=True)
        pick = blk == first
        sel = jnp.where(pick, 1.0, sel)
        score = jnp.where(pick, -jnp.inf, score)
    return sel


def _expand_selection(sel, pos0, n_keys):
    width = sel.shape[1]
    blk = lax.broadcasted_iota(jnp.int32, (width, n_keys), 0)
    key_blk = (pos0 + lax.broadcasted_iota(jnp.int32, (width, n_keys), 1)) // SEL_BLOCK
    spread = jnp.where(blk == key_blk, 1.0, 0.0).astype(BF16)
    return _tile_heads(jnp.dot(sel.astype(BF16), spread, preferred_element_type=F32)) > 0.5


def _gate_mix(gate, parts):
    rows = gate.shape[0]
    gt = _sigmoid(gate)
    outs = []
    for g in range(GROUP):
        o = gt[:, 3 * g:3 * g + 1] * parts[0][g * rows:(g + 1) * rows]
        for j in (1, 2):
            o = o + gt[:, 3 * g + j:3 * g + j + 1] * parts[j][g * rows:(g + 1) * rows]
        outs.append(o)
    return jnp.concatenate(outs, axis=1)


def _nsa_prep_kernel(q_ref, cmp_ref, sel_ref, win_ref, qn_ref, ksn_ref, kwn_ref, qo_ref, co_ref, so_ref, wo_ref):
    q = q_ref[...]
    scale = HEAD_DIM ** -0.5
    for h in range(N_HEADS_B):
        sl = slice(h * HEAD_DIM, (h + 1) * HEAD_DIM)
        qo_ref[:, sl] = _rms(q[:, sl], qn_ref[...]) * scale
    co_ref[...] = cmp_ref[...]
    for src, gain, dst in ((sel_ref, ksn_ref, so_ref), (win_ref, kwn_ref, wo_ref)):
        x = src[...]
        for h in range(N_KV):
            sl = slice(h * HEAD_DIM, (h + 1) * HEAD_DIM)
            dst[:, sl] = _rms(x[:, sl], gain[...])
        dst[:, KV_W:] = x[:, KV_W:]


def nsa_prep(proj, q_norm, ks_norm, kw_norm, tm):
    m = proj.shape[0]
    kvw = 2 * KV_W
    vec = pl.BlockSpec((1, HEAD_DIM), lambda i: (0, 0))
    return pl.pallas_call(
        _nsa_prep_kernel,
        grid=(m // tm,),
        in_specs=[pl.BlockSpec((tm, NSA_W), lambda i: (i, COL_QB // NSA_W)),
                  pl.BlockSpec((tm, kvw), lambda i: (i, COL_CMP // kvw)),
                  pl.BlockSpec((tm, kvw), lambda i: (i, COL_SEL // kvw)),
                  pl.BlockSpec((tm, kvw), lambda i: (i, COL_WIN // kvw)),
                  vec, vec, vec],
        out_specs=[pl.BlockSpec((tm, NSA_W), lambda i: (i, 0))] + [pl.BlockSpec((tm, kvw), lambda i: (i, 0))] * 3,
        out_shape=[jax.ShapeDtypeStruct((m, NSA_W), F32)] + [jax.ShapeDtypeStruct((m, kvw), F32)] * 3,
        compiler_params=_cparams(("parallel",)),
        name="nsa_prep",
    )(proj, proj, proj, proj, q_norm.reshape(1, HEAD_DIM), ks_norm.reshape(1, HEAD_DIM), kw_norm.reshape(1, HEAD_DIM))


def _compress_kernel(x_ref, tail_ref, w1_ref, b1_ref, w2_ref, gain_ref, o_ref, x2_ref, a1_ref, *, n_half):
    for j in range(CMP_STRIDE):
        x2_ref[0:n_half, j * HEAD_DIM:(j + 1) * HEAD_DIM] = x_ref[pl.ds(j, n_half, stride=CMP_STRIDE), :]
        x2_ref[n_half:n_half + SUBLANES, j * HEAD_DIM:(j + 1) * HEAD_DIM] = jnp.broadcast_to(
            tail_ref[pl.ds(j, 1), :], (SUBLANES, HEAD_DIM))
    a = jnp.dot(x2_ref[...].astype(BF16), w1_ref[0], preferred_element_type=F32)
    a1_ref[...] = a[:, CMP_HID:]
    pre = a[0:n_half, :CMP_HID] + a1_ref[pl.ds(1, n_half), :] + b1_ref[0]
    out = jnp.dot(_silu(pre).astype(BF16), w2_ref[0], preferred_element_type=F32)
    is_k = pl.program_id(1) < N_KV
    o_ref[0, 0] = jnp.where(is_k, _rms(out, gain_ref[...]), out).astype(BF16)


def compress(raw, tail, w1, b1, w2, kc_norm, n_batch, n_half):
    t_len = n_half * CMP_STRIDE
    wsel = lambda b, c: (c // N_KV, 0, 0)
    return pl.pallas_call(
        functools.partial(_compress_kernel, n_half=n_half),
        grid=(n_batch, 2 * N_KV),
        in_specs=[pl.BlockSpec((t_len, HEAD_DIM), lambda b, c: (b, c)),
                  pl.BlockSpec((CMP_STRIDE, HEAD_DIM), lambda b, c: (b, c)),
                  pl.BlockSpec((1, CMP_STRIDE * HEAD_DIM, 2 * CMP_HID), wsel),
                  pl.BlockSpec((1, 1, CMP_HID), wsel),
                  pl.BlockSpec((1, CMP_HID, HEAD_DIM), wsel),
                  pl.BlockSpec((1, HEAD_DIM), lambda b, c: (0, 0))],
        out_specs=pl.BlockSpec((1, 1, n_half, HEAD_DIM), lambda b, c: (b, c, 0, 0)),
        out_shape=jax.ShapeDtypeStruct((n_batch, 2 * N_KV, n_half, HEAD_DIM), BF16),
        scratch_shapes=[pltpu.VMEM((n_half + SUBLANES, CMP_STRIDE * HEAD_DIM), F32),
                        pltpu.VMEM((n_half + SUBLANES, CMP_HID), F32)],
        compiler_params=_cparams(("parallel", "parallel")),
        name="compress",
    )(raw, tail, w1, b1, w2, kc_norm.reshape(1, HEAD_DIM))


TQ = 128
TK = 256
STRIP_W = TQ + 2 * TK
STRIP_ORIGIN = STRIP_W - TK
MASK_OFF = 1 << 20


def _far_bias(tab_ref, head0, rows):
    return jnp.concatenate(
        [jnp.full((rows, 1), tab_ref[(NUM_BUCKETS - 1) * N_HEADS_B + head0 + g], F32) for g in range(GROUP)], axis=0)


def _nsa_prompt_kernel(tab_ref, q_ref, kc_ref, vc_ref, ks_ref, vs_ref, kw_ref, vw_ref, gate_ref, o_ref,
                       strip_ref, m_ref, l_ref, acc_ref, *, n_half, n_blocks):
    kv = pl.program_id(1)
    n = pl.program_id(2)
    head0 = kv * GROUP
    q0 = n * TQ
    jd = n // 2
    odd = n - 2 * jd

    @pl.when(n == 0)
    def _():
        a = lax.broadcasted_iota(jnp.int32, (TQ, STRIP_W), 0)
        u = lax.broadcasted_iota(jnp.int32, (TQ, STRIP_W), 1)
        for g in range(GROUP):
            strip_ref[g] = _bias(a + STRIP_ORIGIN - u, tab_ref, head0 + g)

    far_bias = _far_bias(tab_ref, head0, TQ)
    qs = _stack_heads(q_ref[...]).astype(BF16)

    a_c = lax.broadcasted_iota(jnp.int32, (TQ, n_half), 0)
    i_c = lax.broadcasted_iota(jnp.int32, (TQ, n_half), 1)
    rel_c = q0 + a_c - (i_c * CMP_STRIDE + CMP_BLOCK - 1)
    bias_c = jnp.concatenate([_bias(rel_c, tab_ref, head0 + g) for g in range(GROUP)], axis=0)
    s = lax.dot_general(qs, kc_ref[0, 0], _NT, preferred_element_type=F32)
    p = _masked_softmax(s + bias_c, _tile_heads(rel_c) >= 0)
    o_cmp = jnp.dot(p.astype(BF16), vc_ref[0, 0], preferred_element_type=F32)

    imp = _block_importance(p, n_half, n_blocks, LANES)
    cur = (q0 + lax.broadcasted_iota(jnp.int32, (TQ, LANES), 0)) // SEL_BLOCK
    sel = _select_blocks(imp, cur)

    a_k = lax.broadcasted_iota(jnp.int32, (TQ, TK), 0)
    c_k = lax.broadcasted_iota(jnp.int32, (TQ, TK), 1)

    def tile(k_ref, v_ref, jj):
        start = pl.multiple_of(jj * TK, TK)
        return k_ref[pl.ds(start, TK), :].astype(BF16), v_ref[pl.ds(start, TK), :].astype(BF16)

    def near(k_ref, v_ref, d):
        jj = jd - d
        k, v = tile(k_ref, v_ref, jnp.maximum(jj, 0))
        u0 = pl.multiple_of(STRIP_ORIGIN - d * TK - odd * TQ, TQ)
        bias = jnp.concatenate([strip_ref[g, :, pl.ds(u0, TK)] for g in range(GROUP)], axis=0)
        rel = odd * TQ + d * TK + a_k - c_k - jnp.where(jj >= 0, 0, MASK_OFF)
        return lax.dot_general(qs, k, _NT, preferred_element_type=F32) + bias, _tile_heads(rel), v, jj

    _flash_init(m_ref, l_ref, acc_ref)
    for d in (0, 1):
        s, rel, v, jj = near(ks_ref, vs_ref, d)
        chosen = _expand_selection(sel, jnp.maximum(jj, 0) * TK, TK)
        _flash_update(s, chosen & (rel >= 0), v, m_ref, l_ref, acc_ref)

    def far_step(jj, carry):
        k, v = tile(ks_ref, vs_ref, jj)
        s = lax.dot_general(qs, k, _NT, preferred_element_type=F32) + far_bias
        _flash_update(s, _expand_selection(sel, jj * TK, TK), v, m_ref, l_ref, acc_ref)
        return carry

    lax.fori_loop(0, jnp.maximum(jd - 1, 0), far_step, 0)
    o_sel = _flash_result(l_ref, acc_ref)

    _flash_init(m_ref, l_ref, acc_ref)
    for d in (0, 1):
        s, rel, v, _ = near(kw_ref, vw_ref, d)
        _flash_update(s, (rel >= 0) & (rel < WINDOW), v, m_ref, l_ref, acc_ref)
    jj = jd - 2
    k, v = tile(kw_ref, vw_ref, jnp.maximum(jj, 0))
    rel = _tile_heads(odd * TQ + 2 * TK + a_k - c_k - jnp.where(jj >= 0, 0, MASK_OFF))
    s = lax.dot_general(qs, k, _NT, preferred_element_type=F32) + far_bias
    _flash_update(s, (rel >= 0) & (rel < WINDOW), v, m_ref, l_ref, acc_ref)
    o_win = _flash_result(l_ref, acc_ref)

    o_ref[...] = _gate_mix(gate_ref[...], (o_cmp, o_sel, o_win)).astype(o_ref.dtype)


def nsa_prompt_attention(tab, qn, cmp_tok, sel_new, win_new, proj, n_batch, t_len):
    m = qn.shape[0]
    n_q = t_len // TQ
    n_half = cmp_tok.shape[2]
    gw = GROUP * HEAD_DIM
    rows = GROUP * TQ
    seq = lambda off: pl.BlockSpec((t_len, HEAD_DIM), lambda b, kv, n: (b, off + kv))
    tok = lambda off: pl.BlockSpec((1, 1, n_half, HEAD_DIM), lambda b, kv, n: (b, off + kv, 0, 0))
    return pl.pallas_call(
        functools.partial(_nsa_prompt_kernel, n_half=n_half, n_blocks=t_len // SEL_BLOCK),
        grid=(n_batch, N_KV, n_q),
        in_specs=[pl.BlockSpec(memory_space=pltpu.SMEM),
                  pl.BlockSpec((TQ, gw), lambda b, kv, n: (b * n_q + n, kv)),
                  tok(0), tok(N_KV), seq(0), seq(N_KV), seq(0), seq(N_KV),
                  pl.BlockSpec((TQ, LANES), lambda b, kv, n: (b * n_q + n, COL_GATE // LANES + kv))],
        out_specs=pl.BlockSpec((TQ, gw), lambda b, kv, n: (b * n_q + n, kv)),
        out_shape=jax.ShapeDtypeStruct((m, NSA_W), BF16),
        scratch_shapes=[pltpu.VMEM((GROUP, TQ, STRIP_W), F32), pltpu.VMEM((rows, 1), F32),
                        pltpu.VMEM((rows, 1), F32), pltpu.VMEM((rows, HEAD_DIM), F32)],
        compiler_params=_cparams(("parallel", "parallel", "arbitrary")),
        name="nsa_prompt",
    )(tab, qn, cmp_tok, cmp_tok, sel_new, sel_new, win_new, win_new, proj)


def _nsa_sample_kernel(tab_ref, q_ref, kc_ref, vc_ref, ks_ref, vs_ref, kst_ref, vst_ref, kw_ref, vw_ref, gate_ref,
                       o_ref, m_ref, l_ref, acc_ref, *, past, t_new, n_blocks, sel_w, win_buf):
    kv = pl.program_id(1)
    head0 = kv * GROUP
    n_cmp = kc_ref.shape[2]
    tail_w = kst_ref.shape[0]
    win_w = kw_ref.shape[0]
    far_bias = _far_bias(tab_ref, head0, t_new)
    qs = _stack_heads(q_ref[...]).astype(BF16)

    def biased(s, rel):
        return s + jnp.concatenate([_bias(rel, tab_ref, head0 + g) for g in range(GROUP)], axis=0)

    def dist(pos0, width):
        u = lax.broadcasted_iota(jnp.int32, (t_new, width), 0)
        c = lax.broadcasted_iota(jnp.int32, (t_new, width), 1)
        return past + u - pos0 - c

    i_c = lax.broadcasted_iota(jnp.int32, (t_new, n_cmp), 1)
    rel_c = dist(0, n_cmp) - i_c * (CMP_STRIDE - 1) - (CMP_BLOCK - 1)
    s = lax.dot_general(qs, kc_ref[0, 0], _NT, preferred_element_type=F32)
    p = _masked_softmax(biased(s, rel_c), _tile_heads(rel_c) >= 0)
    o_cmp = jnp.dot(p.astype(BF16), vc_ref[0, 0], preferred_element_type=F32)

    imp = _block_importance(p, n_cmp, n_blocks, sel_w)
    cur = (past + lax.broadcasted_iota(jnp.int32, (t_new, sel_w), 0)) // SEL_BLOCK
    sel = _select_blocks(imp, cur)

    _flash_init(m_ref, l_ref, acc_ref)
    n_far = (past - FAR_DIST + 1) // TK

    def far_step(jj, carry):
        start = pl.multiple_of(jj * TK, TK)
        k = ks_ref[pl.ds(start, TK), :].astype(BF16)
        v = vs_ref[pl.ds(start, TK), :].astype(BF16)
        s = lax.dot_general(qs, k, _NT, preferred_element_type=F32) + far_bias
        _flash_update(s, _expand_selection(sel, jj * TK, TK), v, m_ref, l_ref, acc_ref)
        return carry

    lax.fori_loop(0, n_far, far_step, 0)
    for jj in range(n_far, past // TK):
        k = ks_ref[jj * TK:(jj + 1) * TK, :].astype(BF16)
        v = vs_ref[jj * TK:(jj + 1) * TK, :].astype(BF16)
        rel = dist(jj * TK, TK)
        s = biased(lax.dot_general(qs, k, _NT, preferred_element_type=F32), rel)
        chosen = _expand_selection(sel, jj * TK, TK)
        _flash_update(s, chosen & (_tile_heads(rel) >= 0), v, m_ref, l_ref, acc_ref)
    rel = dist(past, tail_w)
    s = biased(lax.dot_general(qs, kst_ref[...].astype(BF16), _NT, preferred_element_type=F32), rel)
    chosen = _expand_selection(sel, past, tail_w)
    _flash_update(s, chosen & (_tile_heads(rel) >= 0), vst_ref[...].astype(BF16), m_ref, l_ref, acc_ref)
    o_sel = _flash_result(l_ref, acc_ref)

    rel = dist(past - win_buf, win_w)
    s = biased(lax.dot_general(qs, kw_ref[...].astype(BF16), _NT, preferred_element_type=F32), rel)
    rel = _tile_heads(rel)
    p = _masked_softmax(s, (rel >= 0) & (rel < WINDOW))
    o_win = jnp.dot(p.astype(BF16), vw_ref[...].astype(BF16), preferred_element_type=F32)

    o_ref[...] = _gate_mix(gate_ref[...], (o_cmp, o_sel, o_win))


def nsa_sample_attention(tab, qn, cmp_tok, sel_past, sel_tail, win_full, proj, n_batch, t_new, past, win_buf):
    m = qn.shape[0]
    n_cmp = cmp_tok.shape[2]
    gw = GROUP * HEAD_DIM
    rows = GROUP * t_new
    tail_w = sel_tail.shape[0] // n_batch
    win_w = win_full.shape[0] // n_batch
    n_blocks = -(-(past + t_new) // SEL_BLOCK)
    sel_w = -(-n_blocks // LANES) * LANES
    seq = lambda length, off: pl.BlockSpec((length, HEAD_DIM), lambda b, kv: (b, off + kv))
    tok = lambda off: pl.BlockSpec((1, 1, n_cmp, HEAD_DIM), lambda b, kv: (b, off + kv, 0, 0))
    return pl.pallas_call(
        functools.partial(_nsa_sample_kernel, past=past, t_new=t_new, n_blocks=n_blocks, sel_w=sel_w, win_buf=win_buf),
        grid=(n_batch, N_KV),
        in_specs=[pl.BlockSpec(memory_space=pltpu.SMEM),
                  pl.BlockSpec((t_new, gw), lambda b, kv: (b, kv)),
                  tok(0), tok(N_KV), seq(past, 0), seq(past, N_KV), seq(tail_w, 0), seq(tail_w, N_KV),
                  seq(win_w, 0), seq(win_w, N_KV),
                  pl.BlockSpec((t_new, LANES), lambda b, kv: (b, COL_GATE // LANES + kv))],
        out_specs=pl.BlockSpec((t_new, gw), lambda b, kv: (b, kv)),
        out_shape=jax.ShapeDtypeStruct((m, NSA_W), F32),
        scratch_shapes=[pltpu.VMEM((rows, 1), F32), pltpu.VMEM((rows, 1), F32), pltpu.VMEM((rows, HEAD_DIM), F32)],
        compiler_params=_cparams(("parallel", "parallel")),
        name="nsa_sample",
    )(tab, qn, cmp_tok, cmp_tok, sel_past, sel_past, sel_tail, sel_tail, win_full, win_full, proj)


def _permute_in_proj(w_in):
    d = w_in.shape[0]
    a_end = 4 * HGRN_W
    b_end = a_end + NSA_W + 6 * KV_W
    g_end = b_end + N_GATE
    per_kv = GROUP * 3
    zeros = lambda n: jnp.zeros((d, n), w_in.dtype)
    gates = []
    for kv in range(N_KV):
        gates += [w_in[:, b_end + kv * per_kv:b_end + (kv + 1) * per_kv], zeros(LANES - per_kv)]
    parts = [w_in[:, :a_end], w_in[:, g_end:g_end + 2 * D_MODEL], w_in[:, a_end:b_end]] + gates
    parts.append(zeros(PROJ_TN - N_KV * LANES))
    return jnp.concatenate(parts, axis=1).astype(BF16)


def _compress_weights(k_w1, k_b1, k_w2, v_w1, v_b1, v_w2):
    half = CMP_STRIDE * HEAD_DIM
    cat = lambda w: jnp.concatenate([w[:half], w[half:]], axis=1)
    w1 = jnp.stack([cat(k_w1), cat(v_w1)]).astype(BF16)
    b1 = jnp.stack([k_b1, v_b1]).reshape(2, 1, CMP_HID)
    w2 = jnp.stack([k_w2, v_w2]).astype(BF16)
    return w1, b1, w2


def _trunk(x, mods, s0, w, chunk, tm, nsa_fn):
    nb, t_len, d = x.shape
    m = nb * t_len
    sh1, sc1, g1, sh2, sc2, g2 = mods
    x2 = x.reshape(m, d)
    proj = in_proj(x2, w["norm1"], sc1, sh1, w["w_in"], t_len, tm)
    o_a, s_fin = hgrn(proj, w["lb"], w["hgrn_onorm"], s0, nb, t_len, chunk)
    qn, cmp_new, sel_new, win_new = nsa_prep(proj, w["q_norm"], w["ks_norm"], w["kw_norm"], tm)
    o_b = nsa_fn(proj, qn, cmp_new, sel_new, win_new)
    x1, h2 = post_attn(x2, o_a, o_b, proj, w["w_ba"], w["w_bb"], w["w_out"], g1, w["norm2"], sc2, sh2, t_len,
                       min(tm, 256))
    y = mlp(h2, x1, w["mlp_w1"], w["mlp_w2"], g2, t_len, tm, 512)
    return y.reshape(nb, t_len, d), s_fin, (cmp_new, sel_new, win_new)


def kernel(x_prompt, x_sample, c_prompt, c_sample, cache_cmp_kv, cache_sel_kv, cache_win_kv, state_hgrn, page_table, hgrn_lb_logits, rel_bias, ada_w, ada_b, norm1, norm2, w_in, hgrn_onorm, nsa_q_norm, nsa_kc_norm, nsa_ks_norm, nsa_kw_norm, cmp_k_w1, cmp_k_b1, cmp_k_w2, cmp_v_w1, cmp_v_b1, cmp_v_w2, w_branch_a, w_branch_b, w_out, mlp_w1, mlp_w2):
    n_p, t_p, d = x_prompt.shape
    n_s, t_s, _ = x_sample.shape
    past = page_table.shape[1] * PAGE
    win_buf = cache_win_kv.shape[2]
    kvw = 2 * KV_W
    layer = 0

    lb_all = jnp.cumsum(jax.nn.softmax(hgrn_lb_logits.astype(F32), axis=0), axis=0)
    cw1, cb1, cw2 = _compress_weights(cmp_k_w1[layer], cmp_k_b1[layer], cmp_k_w2[layer],
                                      cmp_v_w1[layer], cmp_v_b1[layer], cmp_v_w2[layer])
    w = dict(norm1=norm1[layer], norm2=norm2[layer], w_in=_permute_in_proj(w_in[layer]),
             hgrn_onorm=hgrn_onorm[layer], lb=lb_all[layer], q_norm=nsa_q_norm[layer],
             ks_norm=nsa_ks_norm[layer], kw_norm=nsa_kw_norm[layer],
             w_ba=w_branch_a[layer].astype(BF16), w_bb=w_branch_b[layer].astype(BF16),
             w_out=w_out[layer].astype(BF16), mlp_w1=mlp_w1[layer].astype(BF16), mlp_w2=mlp_w2[layer].astype(BF16))
    tab = rel_bias.astype(F32).reshape(NUM_BUCKETS * N_HEADS_B)

    mods = ada_mods(jnp.concatenate([c_prompt, c_sample], axis=0).astype(F32), ada_w[layer], ada_b[layer])
    mods = jnp.split(mods, 6, axis=-1)
    mods_p = [a[:n_p].reshape(n_p, 1, d) for a in mods]
    mods_s = [jnp.repeat(a[n_p:], t_s, axis=0).reshape(1, n_s * t_s, d) for a in mods]

    def nsa_p(proj, qn, cmp_new, sel_new, win_new):
        tail = jnp.zeros((n_p * CMP_STRIDE, kvw), F32)
        cmp_tok = compress(cmp_new, tail, cw1, cb1, cw2, nsa_kc_norm[layer], n_p, t_p // CMP_STRIDE)
        return nsa_prompt_attention(tab, qn, cmp_tok, sel_new, win_new, proj, n_p, t_p)

    s0_p = jnp.zeros((n_p, N_HEADS_A, HEAD_DIM, HEAD_DIM), F32)
    y_p, hg_p, (cmp_p, sel_p, win_p) = _trunk(x_prompt, mods_p, s0_p, w, 256, 512, nsa_p)

    def nsa_s(proj, qn, cmp_new, sel_new, win_new):
        rows = lambda a, n: jnp.pad(a.reshape(n_s, t_s, kvw), ((0, 0), (0, n - t_s), (0, 0))).reshape(n_s * n, kvw)
        cmp_past = cache_cmp_kv[layer][page_table].reshape(n_s * past, kvw)
        sel_past = cache_sel_kv[layer][page_table].reshape(n_s * past, kvw)
        cmp_tok = compress(cmp_past, rows(cmp_new, CMP_STRIDE), cw1, cb1, cw2, nsa_kc_norm[layer], n_s,
                           past // CMP_STRIDE)
        win_w = -(-(win_buf + t_s) // LANES) * LANES
        win_full = jnp.concatenate([cache_win_kv[layer].reshape(n_s, win_buf, kvw), win_new.reshape(n_s, t_s, kvw),
                                    jnp.zeros((n_s, win_w - win_buf - t_s, kvw), F32)], axis=1)
        o = nsa_sample_attention(tab, qn, cmp_tok, sel_past, rows(sel_new, LANES), win_full.reshape(n_s * win_w, kvw),
                                 proj, n_s, t_s, past, win_buf)
        return o.astype(BF16)

    y_s, hg_s, (cmp_s, sel_s, win_s) = _trunk(x_sample, mods_s, state_hgrn[layer], w, t_s, n_s * t_s, nsa_s)

    dt = x_prompt.dtype
    pages = lambda a: a.reshape(1, n_p, t_p // PAGE, PAGE, 2, N_KV, HEAD_DIM).astype(dt)
    rows_s = lambda a: a.reshape(1, n_s, t_s, 2, N_KV, HEAD_DIM).astype(dt)
    win_keep = min(WINDOW, t_p)
    win_p_out = win_p.reshape(n_p, t_p, kvw)[:, t_p - win_keep:].reshape(1, n_p, win_keep, 2, N_KV, HEAD_DIM)
    win_s_out = jnp.concatenate([cache_win_kv[layer].reshape(n_s, win_buf, kvw).astype(F32),
                                 win_s.reshape(n_s, t_s, kvw)], axis=1)[:, t_s:]
    win_s_out = win_s_out.reshape(1, n_s, win_buf, 2, N_KV, HEAD_DIM)
    return (y_p, y_s, pages(cmp_p), rows_s(cmp_s), pages(sel_p), rows_s(sel_s),
            win_p_out.astype(dt), win_s_out.astype(dt), hg_p[None].astype(dt), hg_s[None].astype(dt))
```

```python
import functools
import math

import numpy as np
import jax
import jax.numpy as jnp
from jax import lax
from jax.experimental import pallas as pl
from jax.experimental.pallas import tpu as pltpu

F32 = jnp.float32
BF16 = jnp.bfloat16

D_MODEL = 2048
N_HEADS_A = 8
HEAD_DIM = 128
HGRN_W = N_HEADS_A * HEAD_DIM
N_HEADS_B = 8
N_KV = 2
GROUP = N_HEADS_B // N_KV
NSA_W = N_HEADS_B * HEAD_DIM
KV_W = N_KV * HEAD_DIM
PAGE = 128
CMP_STRIDE = 16
CMP_BLOCK = 32
CMP_HID = 128
SEL_BLOCK = 64
N_SEL = 16
WINDOW = 512
FORCE_SCORE = 1.0e6
NUM_BUCKETS = 32
REL_MAX_DIST = 128
D_FF = 4 * D_MODEL
EPS = 1e-6
N_GATE = 3 * N_HEADS_B

LANES = 128
SUBLANES = 8
VMEM_LIMIT = 56 * 1024 * 1024

COL_HGRN = 0
COL_MGA = 4 * HGRN_W
COL_MGB = COL_MGA + D_MODEL
COL_QB = COL_MGB + D_MODEL
COL_CMP = COL_QB + NSA_W
COL_SEL = COL_CMP + 2 * KV_W
COL_WIN = COL_SEL + 2 * KV_W
COL_GATE = COL_WIN + 2 * KV_W
PROJ_TN = 512
PROJ_COLS = COL_GATE + PROJ_TN


def _cparams(sem):
    return pltpu.CompilerParams(dimension_semantics=sem, vmem_limit_bytes=VMEM_LIMIT)


def _sigmoid(x):
    return 1.0 / (1.0 + jnp.exp(-x))


def _silu(x):
    return x * _sigmoid(x)


def _rms(x, gain):
    return x * lax.rsqrt(jnp.mean(x * x, axis=-1, keepdims=True) + EPS) * gain


def _ada_kernel(c_ref, w_ref, b_ref, o_ref):
    a = _silu(c_ref[...]).astype(BF16)
    o_ref[...] = jnp.dot(a, w_ref[...].astype(BF16), preferred_element_type=F32) + b_ref[...]


def ada_mods(c, w, b):
    r, d = c.shape
    n = w.shape[1]
    tn = 1024
    return pl.pallas_call(
        _ada_kernel,
        grid=(n // tn,),
        in_specs=[pl.BlockSpec((r, d), lambda j: (0, 0)),
                  pl.BlockSpec((d, tn), lambda j: (0, j)),
                  pl.BlockSpec((1, tn), lambda j: (0, j))],
        out_specs=pl.BlockSpec((r, tn), lambda j: (0, j)),
        out_shape=jax.ShapeDtypeStruct((r, n), F32),
        compiler_params=_cparams(("parallel",)),
        name="ada_mods",
    )(c, w, b.reshape(1, n))


def _inproj_kernel(x_ref, gain_ref, sc_ref, sh_ref, w_ref, o_ref, h_ref):
    @pl.when(pl.program_id(1) == 0)
    def _():
        h = _rms(x_ref[...], gain_ref[...]) * (1.0 + sc_ref[0]) + sh_ref[0]
        h_ref[...] = h.astype(BF16)

    o_ref[...] = jnp.dot(h_ref[...], w_ref[...], preferred_element_type=F32)


def _mod_spec(mod, tm, rows_per_batch):
    d = mod.shape[-1]
    if mod.shape[1] == 1:
        return pl.BlockSpec((1, 1, d), lambda i, *_: ((i * tm) // rows_per_batch, 0, 0))
    return pl.BlockSpec((1, tm, d), lambda i, *_: (0, i, 0))


def in_proj(x2, gain, sc, sh, w, rows_per_batch, tm):
    m, d = x2.shape
    n = w.shape[1]
    tn = PROJ_TN
    return pl.pallas_call(
        _inproj_kernel,
        grid=(m // tm, n // tn),
        in_specs=[pl.BlockSpec((tm, d), lambda i, j: (i, 0)),
                  pl.BlockSpec((1, d), lambda i, j: (0, 0)),
                  _mod_spec(sc, tm, rows_per_batch),
                  _mod_spec(sh, tm, rows_per_batch),
                  pl.BlockSpec((d, tn), lambda i, j: (0, j))],
        out_specs=pl.BlockSpec((tm, tn), lambda i, j: (i, j)),
        out_shape=jax.ShapeDtypeStruct((m, n), F32),
        scratch_shapes=[pltpu.VMEM((tm, d), BF16)],
        compiler_params=_cparams(("parallel", "arbitrary")),
        name="in_proj",
    )(x2, gain.reshape(1, d), sc, sh, w)


def _post_kernel(x_ref, oa_ref, ob_ref, mga_ref, mgb_ref, wba_ref, wbb_ref, wout_ref,
                 g1_ref, gain2_ref, sc2_ref, sh2_ref, x1_ref, h2_ref):
    ya = jnp.dot(oa_ref[...], wba_ref[...], preferred_element_type=F32)
    yb = jnp.dot(ob_ref[...], wbb_ref[...], preferred_element_type=F32)
    merged = _sigmoid(mga_ref[...]) * ya + _sigmoid(mgb_ref[...]) * yb
    y = jnp.dot(merged.astype(BF16), wout_ref[...], preferred_element_type=F32)
    x1 = x_ref[...] + g1_ref[0] * y
    x1_ref[...] = x1
    h2_ref[...] = (_rms(x1, gain2_ref[...]) * (1.0 + sc2_ref[0]) + sh2_ref[0]).astype(BF16)


def post_attn(x2, oa, ob, proj, wba, wbb, wout, g1, gain2, sc2, sh2, rows_per_batch, tm):
    m, d = x2.shape
    const = lambda i: (0, 0)
    resident = lambda shape: pl.BlockSpec(shape, const, pipeline_mode=pl.Buffered(1))
    return pl.pallas_call(
        _post_kernel,
        grid=(m // tm,),
        in_specs=[pl.BlockSpec((tm, d), lambda i: (i, 0)),
                  pl.BlockSpec((tm, HGRN_W), lambda i: (i, 0)),
                  pl.BlockSpec((tm, NSA_W), lambda i: (i, 0)),
                  pl.BlockSpec((tm, d), lambda i: (i, COL_MGA // D_MODEL)),
                  pl.BlockSpec((tm, d), lambda i: (i, COL_MGB // D_MODEL)),
                  resident((HGRN_W, d)),
                  resident((NSA_W, d)),
                  resident((d, d)),
                  _mod_spec(g1, tm, rows_per_batch),
                  pl.BlockSpec((1, d), const),
                  _mod_spec(sc2, tm, rows_per_batch),
                  _mod_spec(sh2, tm, rows_per_batch)],
        out_specs=[pl.BlockSpec((tm, d), lambda i: (i, 0)),
                   pl.BlockSpec((tm, d), lambda i: (i, 0))],
        out_shape=[jax.ShapeDtypeStruct((m, d), F32), jax.ShapeDtypeStruct((m, d), BF16)],
        compiler_params=_cparams(("parallel",)),
        name="post_attn",
    )(x2, oa, ob, proj, proj, wba, wbb, wout, g1, gain2.reshape(1, d), sc2, sh2)


def _mlp_kernel(h_ref, x1_ref, w1_ref, w2_ref, g2_ref, y_ref, acc_ref):
    f = pl.program_id(1)
    u = jnp.maximum(jnp.dot(h_ref[...], w1_ref[...], preferred_element_type=F32), 0.0)
    part = jnp.dot((u * u).astype(BF16), w2_ref[...], preferred_element_type=F32)

    @pl.when(f == 0)
    def _():
        acc_ref[...] = part

    @pl.when(f > 0)
    def _():
        acc_ref[...] += part

    @pl.when(f == pl.num_programs(1) - 1)
    def _():
        y_ref[...] = x1_ref[...] + g2_ref[0] * acc_ref[...]


def mlp(h2, x1, w1, w2, g2, rows_per_batch, tm, tf):
    m, d = x1.shape
    ff = w1.shape[1]
    return pl.pallas_call(
        _mlp_kernel,
        grid=(m // tm, ff // tf),
        in_specs=[pl.BlockSpec((tm, d), lambda i, f: (i, 0)),
                  pl.BlockSpec((tm, d), lambda i, f: (i, 0)),
                  pl.BlockSpec((d, tf), lambda i, f: (0, f)),
                  pl.BlockSpec((tf, d), lambda i, f: (f, 0)),
                  _mod_spec(g2, tm, rows_per_batch)],
        out_specs=pl.BlockSpec((tm, d), lambda i, f: (i, 0)),
        out_shape=jax.ShapeDtypeStruct((m, d), F32),
        scratch_shapes=[pltpu.VMEM((tm, d), F32)],
        compiler_params=_cparams(("parallel", "arbitrary")),
        name="mlp",
    )(h2, x1, w1, w2, g2)


_NT = (((1,), (1,)), ((), ()))
_TN = (((0,), (0,)), ((), ()))


def _hgrn_kernel(q_ref, z_ref, v_ref, g_ref, lb_ref, on_ref, s0_ref, o_ref, sfin_ref, st_ref, b_ref, *, chunk):
    c = pl.program_id(2)

    @pl.when(c == 0)
    def _():
        st_ref[...] = s0_ref[0, 0].T

    q = q_ref[...]
    z = z_ref[...]
    v = v_ref[...]
    lb = lb_ref[0]
    e = jnp.exp(-jnp.abs(z))
    r = 1.0 / (1.0 + e)
    pos = z >= 0.0
    logf = jnp.log(lb + (1.0 - lb) * jnp.where(pos, r, e * r))
    k = (1.0 - lb) * jnp.where(pos, e * r, r)

    t = lax.broadcasted_iota(jnp.int32, (chunk, HEAD_DIM), 0)
    b = logf
    s = 1
    while s < chunk:
        b = b + jnp.where(t >= s, pltpu.roll(b, s, 0), 0.0)
        s *= 2
    b_ref[...] = b

    t8 = t & (SUBLANES - 1)
    o = jnp.zeros((chunk, HEAD_DIM), F32)
    for d in range(SUBLANES):
        kd, bd, vd = (k, b, v) if d == 0 else (pltpu.roll(k, d, 0), pltpu.roll(b, d, 0), pltpu.roll(v, d, 0))
        w = jnp.exp(jnp.where(t8 >= d, b - bd, -jnp.inf))
        o = o + jnp.sum(q * kd * w, axis=-1, keepdims=True) * vd

    if chunk > SUBLANES:
        row = lax.broadcasted_iota(jnp.int32, (chunk, chunk), 0)
        col = lax.broadcasted_iota(jnp.int32, (chunk, chunk), 1)
        apart = row ^ col
        att = jnp.zeros((chunk, chunk), F32)
        m = SUBLANES
        while m < chunk:
            refs = [jnp.broadcast_to(b_ref[pl.ds(blk * 2 * m + m - 1, 1), :], (2 * m, HEAD_DIM))
                    for blk in range(chunk // (2 * m))]
            ref_b = refs[0] if len(refs) == 1 else jnp.concatenate(refs, axis=0)
            second = (t & (2 * m - 1)) >= m
            w = jnp.exp(jnp.where(second, b - ref_b, ref_b - b))
            qs = jnp.where(second, q * w, 0.0).astype(BF16)
            ks = jnp.where(second, 0.0, k * w).astype(BF16)
            a_m = lax.dot_general(qs, ks, _NT, preferred_element_type=F32)
            att = att + jnp.where(apart < 2 * m, a_m, 0.0)
            m *= 2
        o = o + jnp.dot(att.astype(BF16), v.astype(BF16), preferred_element_type=F32)

    st = st_ref[...]
    b_last = b_ref[pl.ds(chunk - 1, 1), :]
    o = o + lax.dot_general((q * jnp.exp(b)).astype(BF16), st.astype(BF16), _NT, preferred_element_type=F32)
    kt = (k * jnp.exp(b_last - b)).astype(BF16)
    vb = v.astype(BF16)
    if chunk < 2 * SUBLANES:
        pad = jnp.zeros((2 * SUBLANES - chunk, HEAD_DIM), BF16)
        kt = jnp.concatenate([kt, pad], axis=0)
        vb = jnp.concatenate([vb, pad], axis=0)
    st_new = jnp.exp(b_last) * st + lax.dot_general(vb, kt, _TN, preferred_element_type=F32)
    st_ref[...] = st_new

    o_ref[...] = (_rms(o, on_ref[0]) * _silu(g_ref[...])).astype(o_ref.dtype)

    @pl.when(c == pl.num_programs(2) - 1)
    def _():
        sfin_ref[0, 0] = st_new.T


def hgrn(proj, lb, onorm, s0, n_batch, t_len, chunk):
    m = proj.shape[0]
    n_c = t_len // chunk
    hb = HGRN_W // HEAD_DIM

    def col(group):
        return pl.BlockSpec((chunk, HEAD_DIM), lambda bi, h, c: (bi * n_c + c, group * hb + h))

    vec = pl.BlockSpec((1, 1, HEAD_DIM), lambda bi, h, c: (h, 0, 0))
    state = pl.BlockSpec((1, 1, HEAD_DIM, HEAD_DIM), lambda bi, h, c: (bi, h, 0, 0))
    o_dtype = BF16 if chunk % (2 * SUBLANES) == 0 else F32
    o, s_fin = pl.pallas_call(
        functools.partial(_hgrn_kernel, chunk=chunk),
        grid=(n_batch, N_HEADS_A, n_c),
        in_specs=[col(0), col(1), col(2), col(3), vec, vec, state],
        out_specs=[pl.BlockSpec((chunk, HEAD_DIM), lambda bi, h, c: (bi * n_c + c, h)), state],
        out_shape=[jax.ShapeDtypeStruct((m, HGRN_W), o_dtype),
                   jax.ShapeDtypeStruct((n_batch, N_HEADS_A, HEAD_DIM, HEAD_DIM), F32)],
        scratch_shapes=[pltpu.VMEM((HEAD_DIM, HEAD_DIM), F32), pltpu.VMEM((chunk, HEAD_DIM), F32)],
        compiler_params=_cparams(("parallel", "parallel", "arbitrary")),
        name="hgrn",
    )(proj, proj, proj, proj, lb.reshape(N_HEADS_A, 1, HEAD_DIM), onorm.reshape(N_HEADS_A, 1, HEAD_DIM), s0)
    return o.astype(BF16), s_fin


def _bucket_steps():
    n = np.arange(REL_MAX_DIST)
    exact = NUM_BUCKETS // 2
    val = np.log(np.maximum(n, 1) / exact) / math.log(REL_MAX_DIST / exact) * (NUM_BUCKETS - exact)
    frac = np.abs(val - np.round(val))[exact + 1:]
    assert frac.min() > 1e-3, "a bucket edge sits on an integer distance"
    lut = np.where(n < exact, n, np.minimum(exact + np.floor(np.maximum(val, 0.0)).astype(np.int64), NUM_BUCKETS - 1))
    assert lut[-1] == NUM_BUCKETS - 1
    return int(lut[0]), [(int(i), int(lut[i])) for i in range(1, REL_MAX_DIST) if lut[i] != lut[i - 1]]


_BUCKET0, _BUCKET_EDGES = _bucket_steps()
FAR_DIST = _BUCKET_EDGES[-1][0]


def _bias(rel, tab_ref, head):
    val = jnp.full(rel.shape, tab_ref[_BUCKET0 * N_HEADS_B + head], F32)
    for edge, bucket in _BUCKET_EDGES:
        val = jnp.where(rel >= edge, tab_ref[bucket * N_HEADS_B + head], val)
    return val


def _stack_heads(x):
    return jnp.concatenate([x[:, g * HEAD_DIM:(g + 1) * HEAD_DIM] for g in range(GROUP)], axis=0)


def _tile_heads(x):
    return jnp.concatenate([x] * GROUP, axis=0)


def _masked_softmax(s, mask):
    s = jnp.where(mask, s, -jnp.inf)
    m = jnp.max(s, axis=-1, keepdims=True)
    m = jnp.where(m == -jnp.inf, 0.0, m)
    e = jnp.exp(s - m)
    return e / jnp.maximum(jnp.sum(e, axis=-1, keepdims=True), 1e-30)


def _flash_init(m_ref, l_ref, acc_ref):
    m_ref[...] = jnp.full(m_ref.shape, -jnp.inf, F32)
    l_ref[...] = jnp.zeros(l_ref.shape, F32)
    acc_ref[...] = jnp.zeros(acc_ref.shape, F32)


def _flash_update(s, mask, v, m_ref, l_ref, acc_ref):
    s = jnp.where(mask, s, -jnp.inf)
    m_prev = m_ref[...]
    m_new = jnp.maximum(m_prev, jnp.max(s, axis=-1, keepdims=True))
    m_safe = jnp.where(m_new == -jnp.inf, 0.0, m_new)
    alpha = jnp.exp(m_prev - m_safe)
    p = jnp.exp(s - m_safe)
    l_ref[...] = alpha * l_ref[...] + jnp.sum(p, axis=-1, keepdims=True)
    acc_ref[...] = alpha * acc_ref[...] + jnp.dot(p.astype(BF16), v, preferred_element_type=F32)
    m_ref[...] = m_new


def _flash_result(l_ref, acc_ref):
    return acc_ref[...] / jnp.maximum(l_ref[...], 1e-30)


def _block_importance(p, n_cmp, n_blocks, width):
    rows = p.shape[0] // GROUP
    imp = p[0:rows]
    for g in range(1, GROUP):
        imp = imp + p[g * rows:(g + 1) * rows]
    ci = lax.broadcasted_iota(jnp.int32, (n_cmp, width), 0) * CMP_STRIDE
    si = lax.broadcasted_iota(jnp.int32, (n_cmp, width), 1) * SEL_BLOCK
    overlap = (ci < si + SEL_BLOCK) & (ci + CMP_BLOCK > si) & (si < n_blocks * SEL_BLOCK)
    ov = jnp.where(overlap, 1.0, 0.0).astype(BF16)
    hi = imp.astype(BF16)
    lo = (imp - hi.astype(F32)).astype(BF16)
    return jnp.dot(hi, ov, preferred_element_type=F32) + jnp.dot(lo, ov, preferred_element_type=F32)


def _select_blocks(imp, cur):
    blk = lax.broadcasted_iota(jnp.int32, imp.shape, 1)
    forced = (blk == 0) | (blk == cur) | (blk == cur - 1)
    score = jnp.where(blk <= cur, imp + jnp.where(forced, FORCE_SCORE, 0.0), -jnp.inf)
    sel = jnp.zeros(imp.shape, F32)
    width = imp.shape[1]
    for _ in range(N_SEL):
        top = jnp.max(score, axis=-1, keepdims=True)
        first = jnp.min(jnp.where(score == top, blk, width), axis=-1, keepdims=True)
        pick = blk == first
        sel = jnp.where(pick, 1.0, sel)
        score = jnp.where(pick, -jnp.inf, score)
    return sel


def _expand_selection(sel, pos0, n_keys):
    width = sel.shape[1]
    blk = lax.broadcasted_iota(jnp.int32, (width, n_keys), 0)
    key_blk = (pos0 + lax.broadcasted_iota(jnp.int32, (width, n_keys), 1)) // SEL_BLOCK
    spread = jnp.where(blk == key_blk, 1.0, 0.0).astype(BF16)
    return _tile_heads(jnp.dot(sel.astype(BF16), spread, preferred_element_type=F32)) > 0.5


def _gate_mix(gate, parts):
    rows = gate.shape[0]
    gt = _sigmoid(gate)
    outs = []
    for g in range(GROUP):
        o = gt[:, 3 * g:3 * g + 1] * parts[0][g * rows:(g + 1) * rows]
        for j in (1, 2):
            o = o + gt[:, 3 * g + j:3 * g + j + 1] * parts[j][g * rows:(g + 1) * rows]
        outs.append(o)
    return jnp.concatenate(outs, axis=1)


def _nsa_prep_kernel(q_ref, cmp_ref, sel_ref, win_ref, qn_ref, ksn_ref, kwn_ref, qo_ref, co_ref, so_ref, wo_ref):
    q = q_ref[...]
    scale = HEAD_DIM ** -0.5
    for h in range(N_HEADS_B):
        sl = slice(h * HEAD_DIM, (h + 1) * HEAD_DIM)
        qo_ref[:, sl] = _rms(q[:, sl], qn_ref[...]) * scale
    co_ref[...] = cmp_ref[...]
    for src, gain, dst in ((sel_ref, ksn_ref, so_ref), (win_ref, kwn_ref, wo_ref)):
        x = src[...]
        for h in range(N_KV):
            sl = slice(h * HEAD_DIM, (h + 1) * HEAD_DIM)
            dst[:, sl] = _rms(x[:, sl], gain[...])
        dst[:, KV_W:] = x[:, KV_W:]


def nsa_prep(proj, q_norm, ks_norm, kw_norm, tm):
    m = proj.shape[0]
    kvw = 2 * KV_W
    vec = pl.BlockSpec((1, HEAD_DIM), lambda i: (0, 0))
    return pl.pallas_call(
        _nsa_prep_kernel,
        grid=(m // tm,),
        in_specs=[pl.BlockSpec((tm, NSA_W), lambda i: (i, COL_QB // NSA_W)),
                  pl.BlockSpec((tm, kvw), lambda i: (i, COL_CMP // kvw)),
                  pl.BlockSpec((tm, kvw), lambda i: (i, COL_SEL // kvw)),
                  pl.BlockSpec((tm, kvw), lambda i: (i, COL_WIN // kvw)),
                  vec, vec, vec],
        out_specs=[pl.BlockSpec((tm, NSA_W), lambda i: (i, 0))] + [pl.BlockSpec((tm, kvw), lambda i: (i, 0))] * 3,
        out_shape=[jax.ShapeDtypeStruct((m, NSA_W), F32)] + [jax.ShapeDtypeStruct((m, kvw), F32)] * 3,
        compiler_params=_cparams(("parallel",)),
        name="nsa_prep",
    )(proj, proj, proj, proj, q_norm.reshape(1, HEAD_DIM), ks_norm.reshape(1, HEAD_DIM), kw_norm.reshape(1, HEAD_DIM))


def _compress_kernel(x_ref, tail_ref, w1_ref, b1_ref, w2_ref, gain_ref, o_ref, x2_ref, a1_ref, *, n_half):
    for j in range(CMP_STRIDE):
        x2_ref[0:n_half, j * HEAD_DIM:(j + 1) * HEAD_DIM] = x_ref[pl.ds(j, n_half, stride=CMP_STRIDE), :]
        x2_ref[n_half:n_half + SUBLANES, j * HEAD_DIM:(j + 1) * HEAD_DIM] = jnp.broadcast_to(
            tail_ref[pl.ds(j, 1), :], (SUBLANES, HEAD_DIM))
    a = jnp.dot(x2_ref[...].astype(BF16), w1_ref[0], preferred_element_type=F32)
    a1_ref[...] = a[:, CMP_HID:]
    pre = a[0:n_half, :CMP_HID] + a1_ref[pl.ds(1, n_half), :] + b1_ref[0]
    out = jnp.dot(_silu(pre).astype(BF16), w2_ref[0], preferred_element_type=F32)
    is_k = pl.program_id(1) < N_KV
    o_ref[0, 0] = jnp.where(is_k, _rms(out, gain_ref[...]), out).astype(BF16)


def compress(raw, tail, w1, b1, w2, kc_norm, n_batch, n_half):
    t_len = n_half * CMP_STRIDE
    wsel = lambda b, c: (c // N_KV, 0, 0)
    return pl.pallas_call(
        functools.partial(_compress_kernel, n_half=n_half),
        grid=(n_batch, 2 * N_KV),
        in_specs=[pl.BlockSpec((t_len, HEAD_DIM), lambda b, c: (b, c)),
                  pl.BlockSpec((CMP_STRIDE, HEAD_DIM), lambda b, c: (b, c)),
                  pl.BlockSpec((1, CMP_STRIDE * HEAD_DIM, 2 * CMP_HID), wsel),
                  pl.BlockSpec((1, 1, CMP_HID), wsel),
                  pl.BlockSpec((1, CMP_HID, HEAD_DIM), wsel),
                  pl.BlockSpec((1, HEAD_DIM), lambda b, c: (0, 0))],
        out_specs=pl.BlockSpec((1, 1, n_half, HEAD_DIM), lambda b, c: (b, c, 0, 0)),
        out_shape=jax.ShapeDtypeStruct((n_batch, 2 * N_KV, n_half, HEAD_DIM), BF16),
        scratch_shapes=[pltpu.VMEM((n_half + SUBLANES, CMP_STRIDE * HEAD_DIM), F32),
                        pltpu.VMEM((n_half + SUBLANES, CMP_HID), F32)],
        compiler_params=_cparams(("parallel", "parallel")),
        name="compress",
    )(raw, tail, w1, b1, w2, kc_norm.reshape(1, HEAD_DIM))


TQ = 128
TK = 256
STRIP_W = TQ + 2 * TK
STRIP_ORIGIN = STRIP_W - TK
MASK_OFF = 1 << 20


def _far_bias(tab_ref, head0, rows):
    return jnp.concatenate(
        [jnp.full((rows, 1), tab_ref[(NUM_BUCKETS - 1) * N_HEADS_B + head0 + g], F32) for g in range(GROUP)], axis=0)


def _nsa_prompt_kernel(tab_ref, q_ref, kc_ref, vc_ref, ks_ref, vs_ref, kw_ref, vw_ref, gate_ref, o_ref,
                       strip_ref, m_ref, l_ref, acc_ref, *, n_half, n_blocks):
    kv = pl.program_id(1)
    n = pl.program_id(2)
    head0 = kv * GROUP
    q0 = n * TQ
    jd = n // 2
    odd = n - 2 * jd

    @pl.when(n == 0)
    def _():
        a = lax.broadcasted_iota(jnp.int32, (TQ, STRIP_W), 0)
        u = lax.broadcasted_iota(jnp.int32, (TQ, STRIP_W), 1)
        for g in range(GROUP):
            strip_ref[g] = _bias(a + STRIP_ORIGIN - u, tab_ref, head0 + g)

    far_bias = _far_bias(tab_ref, head0, TQ)
    qs = _stack_heads(q_ref[...]).astype(BF16)

    a_c = lax.broadcasted_iota(jnp.int32, (TQ, n_half), 0)
    i_c = lax.broadcasted_iota(jnp.int32, (TQ, n_half), 1)
    rel_c = q0 + a_c - (i_c * CMP_STRIDE + CMP_BLOCK - 1)
    bias_c = jnp.concatenate([_bias(rel_c, tab_ref, head0 + g) for g in range(GROUP)], axis=0)
    s = lax.dot_general(qs, kc_ref[0, 0], _NT, preferred_element_type=F32)
    p = _masked_softmax(s + bias_c, _tile_heads(rel_c) >= 0)
    o_cmp = jnp.dot(p.astype(BF16), vc_ref[0, 0], preferred_element_type=F32)

    imp = _block_importance(p, n_half, n_blocks, LANES)
    cur = (q0 + lax.broadcasted_iota(jnp.int32, (TQ, LANES), 0)) // SEL_BLOCK
    sel = _select_blocks(imp, cur)

    a_k = lax.broadcasted_iota(jnp.int32, (TQ, TK), 0)
    c_k = lax.broadcasted_iota(jnp.int32, (TQ, TK), 1)

    def tile(k_ref, v_ref, jj):
        start = pl.multiple_of(jj * TK, TK)
        return k_ref[pl.ds(start, TK), :].astype(BF16), v_ref[pl.ds(start, TK), :].astype(BF16)

    def near(k_ref, v_ref, d):
        jj = jd - d
        k, v = tile(k_ref, v_ref, jnp.maximum(jj, 0))
        u0 = pl.multiple_of(STRIP_ORIGIN - d * TK - odd * TQ, TQ)
        bias = jnp.concatenate([strip_ref[g, :, pl.ds(u0, TK)] for g in range(GROUP)], axis=0)
        rel = odd * TQ + d * TK + a_k - c_k - jnp.where(jj >= 0, 0, MASK_OFF)
        return lax.dot_general(qs, k, _NT, preferred_element_type=F32) + bias, _tile_heads(rel), v, jj

    _flash_init(m_ref, l_ref, acc_ref)
    for d in (0, 1):
        s, rel, v, jj = near(ks_ref, vs_ref, d)
        chosen = _expand_selection(sel, jnp.maximum(jj, 0) * TK, TK)
        _flash_update(s, chosen & (rel >= 0), v, m_ref, l_ref, acc_ref)

    def far_step(jj, carry):
        k, v = tile(ks_ref, vs_ref, jj)
        s = lax.dot_general(qs, k, _NT, preferred_element_type=F32) + far_bias
        _flash_update(s, _expand_selection(sel, jj * TK, TK), v, m_ref, l_ref, acc_ref)
        return carry

    lax.fori_loop(0, jnp.maximum(jd - 1, 0), far_step, 0)
    o_sel = _flash_result(l_ref, acc_ref)

    _flash_init(m_ref, l_ref, acc_ref)
    for d in (0, 1):
        s, rel, v, _ = near(kw_ref, vw_ref, d)
        _flash_update(s, (rel >= 0) & (rel < WINDOW), v, m_ref, l_ref, acc_ref)
    jj = jd - 2
    k, v = tile(kw_ref, vw_ref, jnp.maximum(jj, 0))
    rel = _tile_heads(odd * TQ + 2 * TK + a_k - c_k - jnp.where(jj >= 0, 0, MASK_OFF))
    s = lax.dot_general(qs, k, _NT, preferred_element_type=F32) + far_bias
    _flash_update(s, (rel >= 0) & (rel < WINDOW), v, m_ref, l_ref, acc_ref)
    o_win = _flash_result(l_ref, acc_ref)

    o_ref[...] = _gate_mix(gate_ref[...], (o_cmp, o_sel, o_win)).astype(o_ref.dtype)


def nsa_prompt_attention(tab, qn, cmp_tok, sel_new, win_new, proj, n_batch, t_len):
    m = qn.shape[0]
    n_q = t_len // TQ
    n_half = cmp_tok.shape[2]
    gw = GROUP * HEAD_DIM
    rows = GROUP * TQ
    seq = lambda off: pl.BlockSpec((t_len, HEAD_DIM), lambda b, kv, n: (b, off + kv))
    tok = lambda off: pl.BlockSpec((1, 1, n_half, HEAD_DIM), lambda b, kv, n: (b, off + kv, 0, 0))
    return pl.pallas_call(
        functools.partial(_nsa_prompt_kernel, n_half=n_half, n_blocks=t_len // SEL_BLOCK),
        grid=(n_batch, N_KV, n_q),
        in_specs=[pl.BlockSpec(memory_space=pltpu.SMEM),
                  pl.BlockSpec((TQ, gw), lambda b, kv, n: (b * n_q + n, kv)),
                  tok(0), tok(N_KV), seq(0), seq(N_KV), seq(0), seq(N_KV),
                  pl.BlockSpec((TQ, LANES), lambda b, kv, n: (b * n_q + n, COL_GATE // LANES + kv))],
        out_specs=pl.BlockSpec((TQ, gw), lambda b, kv, n: (b * n_q + n, kv)),
        out_shape=jax.ShapeDtypeStruct((m, NSA_W), BF16),
        scratch_shapes=[pltpu.VMEM((GROUP, TQ, STRIP_W), F32), pltpu.VMEM((rows, 1), F32),
                        pltpu.VMEM((rows, 1), F32), pltpu.VMEM((rows, HEAD_DIM), F32)],
        compiler_params=_cparams(("parallel", "parallel", "arbitrary")),
        name="nsa_prompt",
    )(tab, qn, cmp_tok, cmp_tok, sel_new, sel_new, win_new, win_new, proj)


def _nsa_sample_kernel(tab_ref, q_ref, kc_ref, vc_ref, ks_ref, vs_ref, kst_ref, vst_ref, kw_ref, vw_ref, gate_ref,
                       o_ref, m_ref, l_ref, acc_ref, *, past, t_new, n_blocks, sel_w, win_buf):
    kv = pl.program_id(1)
    head0 = kv * GROUP
    n_cmp = kc_ref.shape[2]
    tail_w = kst_ref.shape[0]
    win_w = kw_ref.shape[0]
    far_bias = _far_bias(tab_ref, head0, t_new)
    qs = _stack_heads(q_ref[...]).astype(BF16)

    def biased(s, rel):
        return s + jnp.concatenate([_bias(rel, tab_ref, head0 + g) for g in range(GROUP)], axis=0)

    def dist(pos0, width):
        u = lax.broadcasted_iota(jnp.int32, (t_new, width), 0)
        c = lax.broadcasted_iota(jnp.int32, (t_new, width), 1)
        return past + u - pos0 - c

    i_c = lax.broadcasted_iota(jnp.int32, (t_new, n_cmp), 1)
    rel_c = dist(0, n_cmp) - i_c * (CMP_STRIDE - 1) - (CMP_BLOCK - 1)
    s = lax.dot_general(qs, kc_ref[0, 0], _NT, preferred_element_type=F32)
    p = _masked_softmax(biased(s, rel_c), _tile_heads(rel_c) >= 0)
    o_cmp = jnp.dot(p.astype(BF16), vc_ref[0, 0], preferred_element_type=F32)

    imp = _block_importance(p, n_cmp, n_blocks, sel_w)
    cur = (past + lax.broadcasted_iota(jnp.int32, (t_new, sel_w), 0)) // SEL_BLOCK
    sel = _select_blocks(imp, cur)

    _flash_init(m_ref, l_ref, acc_ref)
    n_far = (past - FAR_DIST + 1) // TK

    def far_step(jj, carry):
        start = pl.multiple_of(jj * TK, TK)
        k = ks_ref[pl.ds(start, TK), :].astype(BF16)
        v = vs_ref[pl.ds(start, TK), :].astype(BF16)
        s = lax.dot_general(qs, k, _NT, preferred_element_type=F32) + far_bias
        _flash_update(s, _expand_selection(sel, jj * TK, TK), v, m_ref, l_ref, acc_ref)
        return carry

    lax.fori_loop(0, n_far, far_step, 0)
    for jj in range(n_far, past // TK):
        k = ks_ref[jj * TK:(jj + 1) * TK, :].astype(BF16)
        v = vs_ref[jj * TK:(jj + 1) * TK, :].astype(BF16)
        rel = dist(jj * TK, TK)
        s = biased(lax.dot_general(qs, k, _NT, preferred_element_type=F32), rel)
        chosen = _expand_selection(sel, jj * TK, TK)
        _flash_update(s, chosen & (_tile_heads(rel) >= 0), v, m_ref, l_ref, acc_ref)
    rel = dist(past, tail_w)
    s = biased(lax.dot_general(qs, kst_ref[...].astype(BF16), _NT, preferred_element_type=F32), rel)
    chosen = _expand_selection(sel, past, tail_w)
    _flash_update(s, chosen & (_tile_heads(rel) >= 0), vst_ref[...].astype(BF16), m_ref, l_ref, acc_ref)
    o_sel = _flash_result(l_ref, acc_ref)

    rel = dist(past - win_buf, win_w)
    s = biased(lax.dot_general(qs, kw_ref[...].astype(BF16), _NT, preferred_element_type=F32), rel)
    rel = _tile_heads(rel)
    p = _masked_softmax(s, (rel >= 0) & (rel < WINDOW))
    o_win = jnp.dot(p.astype(BF16), vw_ref[...].astype(BF16), preferred_element_type=F32)

    o_ref[...] = _gate_mix(gate_ref[...], (o_cmp, o_sel, o_win))


def nsa_sample_attention(tab, qn, cmp_tok, sel_past, sel_tail, win_full, proj, n_batch, t_new, past, win_buf):
    m = qn.shape[0]
    n_cmp = cmp_tok.shape[2]
    gw = GROUP * HEAD_DIM
    rows = GROUP * t_new
    tail_w = sel_tail.shape[0] // n_batch
    win_w = win_full.shape[0] // n_batch
    n_blocks = -(-(past + t_new) // SEL_BLOCK)
    sel_w = -(-n_blocks // LANES) * LANES
    seq = lambda length, off: pl.BlockSpec((length, HEAD_DIM), lambda b, kv: (b, off + kv))
    tok = lambda off: pl.BlockSpec((1, 1, n_cmp, HEAD_DIM), lambda b, kv: (b, off + kv, 0, 0))
    return pl.pallas_call(
        functools.partial(_nsa_sample_kernel, past=past, t_new=t_new, n_blocks=n_blocks, sel_w=sel_w, win_buf=win_buf),
        grid=(n_batch, N_KV),
        in_specs=[pl.BlockSpec(memory_space=pltpu.SMEM),
                  pl.BlockSpec((t_new, gw), lambda b, kv: (b, kv)),
                  tok(0), tok(N_KV), seq(past, 0), seq(past, N_KV), seq(tail_w, 0), seq(tail_w, N_KV),
                  seq(win_w, 0), seq(win_w, N_KV),
                  pl.BlockSpec((t_new, LANES), lambda b, kv: (b, COL_GATE // LANES + kv))],
        out_specs=pl.BlockSpec((t_new, gw), lambda b, kv: (b, kv)),
        out_shape=jax.ShapeDtypeStruct((m, NSA_W), F32),
        scratch_shapes=[pltpu.VMEM((rows, 1), F32), pltpu.VMEM((rows, 1), F32), pltpu.VMEM((rows, HEAD_DIM), F32)],
        compiler_params=_cparams(("parallel", "parallel")),
        name="nsa_sample",
    )(tab, qn, cmp_tok, cmp_tok, sel_past, sel_past, sel_tail, sel_tail, win_full, win_full, proj)


PAGES_PER_STEP = 16
TOKEN_ROWS = 2 * N_KV
PAGE_ROWS = PAGE * TOKEN_ROWS


def _lane_iota(shape):
    return lax.broadcasted_iota(jnp.int32, shape, 1)


def _row_iota(shape):
    return lax.broadcasted_iota(jnp.int32, shape, 0)


def _lane_queries(q, t_new):
    per_kv = GROUP * t_new
    stacked = jnp.concatenate([q[:, h * HEAD_DIM:(h + 1) * HEAD_DIM] for h in range(N_HEADS_B)], axis=0)
    out = []
    for kv in range(N_KV):
        parts = []
        if kv:
            parts.append(jnp.zeros((kv * per_kv, HEAD_DIM), F32))
        parts.append(stacked[kv * per_kv:(kv + 1) * per_kv])
        parts.append(jnp.zeros((LANES - (kv + 1) * per_kv, HEAD_DIM), F32))
        out.append(jnp.concatenate(parts, axis=0).astype(BF16))
    return out


def _lane_scores(keys, qz):
    s = lax.dot_general(keys[0], qz[0], _NT, preferred_element_type=F32)
    for kv in range(1, N_KV):
        s = s + lax.dot_general(keys[kv], qz[kv], _NT, preferred_element_type=F32)
    return s


def _lane_values(vals, p, per_kv):
    pb = p.astype(BF16)
    lane = _lane_iota((HEAD_DIM, LANES))
    out = lax.dot_general(vals[N_KV - 1], pb, _TN, preferred_element_type=F32)
    for kv in range(N_KV - 2, -1, -1):
        out = jnp.where(lane < (kv + 1) * per_kv, lax.dot_general(vals[kv], pb, _TN, preferred_element_type=F32), out)
    return out


def _lane_bias(rel, tabl_ref):
    val = jnp.broadcast_to(tabl_ref[_BUCKET0:_BUCKET0 + 1, :], rel.shape)
    for edge, bucket in _BUCKET_EDGES:
        val = jnp.where(rel >= edge, tabl_ref[bucket:bucket + 1, :], val)
    return val


def _pad_rows(x, rows):
    return jnp.concatenate([x, jnp.zeros((rows - x.shape[0], x.shape[1]), x.dtype)], axis=0)


def _sample_cmp_kernel(pt_ref, *refs, past, t_new, n_blocks, sel_rows):
    pages = refs[:PAGES_PER_STEP]
    (new_ref, q_ref, w1_ref, b1_ref, w2_ref, gain_ref, tabl_ref,
     sel_ref, ocmp_ref, x2_ref, a_ref) = refs[PAGES_PER_STEP:]
    c = pl.program_id(1)
    n_half = past // CMP_STRIDE
    halves = PAGE // CMP_STRIDE
    step_halves = PAGES_PER_STEP * halves
    per_kv = GROUP * t_new

    for combo in range(TOKEN_ROWS):
        for p in range(PAGES_PER_STEP):
            for j in range(CMP_STRIDE):
                x2_ref[combo, p * halves:(p + 1) * halves, j * HEAD_DIM:(j + 1) * HEAD_DIM] = (
                    pages[p][pl.ds(j * TOKEN_ROWS + combo, halves, stride=CMP_STRIDE * TOKEN_ROWS), :])
        a = jnp.dot(x2_ref[combo].astype(BF16), w1_ref[combo // N_KV], preferred_element_type=F32)
        a_ref[combo, pl.ds(pl.multiple_of(c * step_halves, step_halves), step_halves), :] = a

    @pl.when(c == pl.num_programs(1) - 1)
    def _():
        tok = []
        for combo in range(TOKEN_ROWS):
            kind = combo // N_KV
            sl = slice(combo * HEAD_DIM, (combo + 1) * HEAD_DIM)
            row = jnp.concatenate([new_ref[j:j + 1, sl] for j in range(t_new)]
                                  + [jnp.zeros((1, (CMP_STRIDE - t_new) * HEAD_DIM), F32)], axis=1)
            tail = jnp.broadcast_to(row, (SUBLANES, CMP_STRIDE * HEAD_DIM)).astype(BF16)
            a_ref[combo, n_half:n_half + SUBLANES, :] = jnp.dot(tail, w1_ref[kind], preferred_element_type=F32)
            pre = a_ref[combo, 0:n_half, 0:CMP_HID] + a_ref[combo, pl.ds(1, n_half), CMP_HID:] + b1_ref[kind]
            out = jnp.dot(_silu(pre).astype(BF16), w2_ref[kind], preferred_element_type=F32)
            if kind == 0:
                out = _rms(out, gain_ref[...])
            tok.append(out.astype(BF16))

        qz = _lane_queries(q_ref[...], t_new)
        shape = (n_half, LANES)
        u = _lane_iota(shape) & (t_new - 1)
        rel = past + u - (_row_iota(shape) * CMP_STRIDE + CMP_BLOCK - 1)
        s = jnp.where(rel >= 0, _lane_scores(tok[:N_KV], qz) + _lane_bias(rel, tabl_ref), -jnp.inf)
        m = jnp.max(s, axis=0, keepdims=True)
        e = jnp.exp(s - jnp.where(m == -jnp.inf, 0.0, m))
        p = e / jnp.maximum(jnp.sum(e, axis=0, keepdims=True), 1e-30)
        ocmp_ref[...] = _lane_values(tok[N_KV:], p, per_kv)

        src = _row_iota((LANES, LANES))
        dst = _lane_iota((LANES, LANES))
        pooled = (src // per_kv) * t_new + (src & (t_new - 1))
        pool = jnp.where((src < N_KV * per_kv) & (dst == pooled), 1.0, 0.0).astype(BF16)
        hi = p.astype(BF16)
        lo = (p - hi.astype(F32)).astype(BF16)
        imp = jnp.dot(hi, pool, preferred_element_type=F32) + jnp.dot(lo, pool, preferred_element_type=F32)
        si = _row_iota((sel_rows, n_half)) * SEL_BLOCK
        ci = _lane_iota((sel_rows, n_half)) * CMP_STRIDE
        overlap = (ci < si + SEL_BLOCK) & (ci + CMP_BLOCK > si) & (si < n_blocks * SEL_BLOCK)
        ov = jnp.where(overlap, 1.0, 0.0).astype(BF16)
        hi = imp.astype(BF16)
        lo = (imp - hi.astype(F32)).astype(BF16)
        imp = jnp.dot(ov, hi, preferred_element_type=F32) + jnp.dot(ov, lo, preferred_element_type=F32)

        shape = (sel_rows, LANES)
        blk = _row_iota(shape)
        cur = (past + (_lane_iota(shape) & (t_new - 1))) // SEL_BLOCK
        forced = (blk == 0) | (blk == cur) | (blk == cur - 1)
        score = jnp.where(blk <= cur, imp + jnp.where(forced, FORCE_SCORE, 0.0), -jnp.inf)
        sel = jnp.zeros(shape, F32)
        for _ in range(N_SEL):
            top = jnp.max(score, axis=0, keepdims=True)
            first = jnp.min(jnp.where(score == top, blk, sel_rows), axis=0, keepdims=True)
            pick = blk == first
            sel = jnp.where(pick, 1.0, sel)
            score = jnp.where(pick, -jnp.inf, score)
        unpool = jnp.where((dst < N_KV * per_kv) & (src == (dst // per_kv) * t_new + (dst & (t_new - 1))), 1.0, 0.0)
        sel_ref[...] = jnp.dot(sel.astype(BF16), unpool.astype(BF16), preferred_element_type=F32)


def _page_specs(n_chunk_pages):
    def spec(j):
        return pl.BlockSpec((None, PAGE_ROWS, HEAD_DIM), lambda b, c, pt: (pt[b, c * n_chunk_pages + j], 0, 0))
    return [spec(j) for j in range(n_chunk_pages)]


def sample_compressed(page_table, cache3, cmp_new, qn, w1, b1, w2, kc_norm, tabl, t_new):
    n_seq, n_pages = page_table.shape
    past = n_pages * PAGE
    n_half = past // CMP_STRIDE
    n_blocks = -(-(past + t_new) // SEL_BLOCK)
    sel_rows = -(-n_blocks // LANES) * LANES
    const = lambda *shape: pl.BlockSpec(shape, lambda b, c, pt: (0,) * len(shape))
    row = lambda width: pl.BlockSpec((t_new, width), lambda b, c, pt: (b, 0))
    out = lambda rows: pl.BlockSpec((None, rows, LANES), lambda b, c, pt: (b, 0, 0))
    grid_spec = pltpu.PrefetchScalarGridSpec(
        num_scalar_prefetch=1,
        grid=(n_seq, n_pages // PAGES_PER_STEP),
        in_specs=_page_specs(PAGES_PER_STEP) + [
            row(2 * KV_W), row(NSA_W), const(2, CMP_STRIDE * HEAD_DIM, 2 * CMP_HID), const(2, 1, CMP_HID),
            const(2, CMP_HID, HEAD_DIM), const(1, HEAD_DIM), const(NUM_BUCKETS, LANES)],
        out_specs=[out(sel_rows), out(HEAD_DIM)],
        scratch_shapes=[pltpu.VMEM((TOKEN_ROWS, PAGES_PER_STEP * PAGE // CMP_STRIDE, CMP_STRIDE * HEAD_DIM), F32),
                        pltpu.VMEM((TOKEN_ROWS, n_half + SUBLANES, 2 * CMP_HID), F32)])
    return pl.pallas_call(
        functools.partial(_sample_cmp_kernel, past=past, t_new=t_new, n_blocks=n_blocks, sel_rows=sel_rows),
        grid_spec=grid_spec,
        out_shape=[jax.ShapeDtypeStruct((n_seq, sel_rows, LANES), F32),
                   jax.ShapeDtypeStruct((n_seq, HEAD_DIM, LANES), F32)],
        compiler_params=_cparams(("parallel", "arbitrary")),
        name="sample_cmp",
    )(page_table, *([cache3] * PAGES_PER_STEP), cmp_new, qn, w1, b1, w2, kc_norm.reshape(1, HEAD_DIM), tabl)


def _sample_sel_kernel(pt_ref, *refs, past, t_new, win_buf):
    pages = refs[:PAGES_PER_STEP]
    (new_ref, q_ref, sel_ref, ocmp_ref, win_ref, wnew_ref, gate_ref, tabl_ref,
     o_ref, m_ref, l_ref, acc_ref, owin_ref, s_ref) = refs[PAGES_PER_STEP:]
    c = pl.program_id(1)
    per_kv = GROUP * t_new
    qz = _lane_queries(q_ref[...], t_new)
    far = tabl_ref[NUM_BUCKETS - 1:NUM_BUCKETS, :]
    new_rows = 2 * SUBLANES

    def new_kv(ref, kind):
        return [_pad_rows(ref[:, (kind * N_KV + kv) * HEAD_DIM:(kind * N_KV + kv + 1) * HEAD_DIM], new_rows).astype(BF16)
                for kv in range(N_KV)]

    def dist(pos0, rows):
        shape = (rows, LANES)
        return past + (_lane_iota(shape) & (t_new - 1)) - pos0 - _row_iota(shape)

    def update(s, values):
        m_prev = m_ref[...]
        m_new = jnp.maximum(m_prev, jnp.max(s, axis=0, keepdims=True))
        m_safe = jnp.where(m_new == -jnp.inf, 0.0, m_new)
        alpha = jnp.exp(m_prev - m_safe)
        p = jnp.exp(s - m_safe)
        l_ref[...] = alpha * l_ref[...] + jnp.sum(p, axis=0, keepdims=True)
        acc = alpha * acc_ref[...]
        for vals, lo, hi_ in values:
            acc = acc + _lane_values(vals, p[lo:hi_], per_kv)
        acc_ref[...] = acc
        m_ref[...] = m_new

    @pl.when(c == 0)
    def _():
        m_ref[...] = jnp.full(m_ref.shape, -jnp.inf, F32)
        l_ref[...] = jnp.zeros(l_ref.shape, F32)
        acc_ref[...] = jnp.zeros(acc_ref.shape, F32)

        kw = [win_ref[pl.ds(kv, win_buf, stride=TOKEN_ROWS), :].astype(BF16) for kv in range(N_KV)]
        vw = [win_ref[pl.ds(N_KV + kv, win_buf, stride=TOKEN_ROWS), :].astype(BF16) for kv in range(N_KV)]
        pieces = []
        for keys, rel in ((kw, dist(past - win_buf, win_buf)), (new_kv(wnew_ref, 0), dist(past, new_rows))):
            s = _lane_scores(keys, qz) + _lane_bias(rel, tabl_ref)
            pieces.append(jnp.where((rel >= 0) & (rel < WINDOW), s, -jnp.inf))
        m = jnp.maximum(jnp.max(pieces[0], axis=0, keepdims=True), jnp.max(pieces[1], axis=0, keepdims=True))
        m = jnp.where(m == -jnp.inf, 0.0, m)
        e0 = jnp.exp(pieces[0] - m)
        e1 = jnp.exp(pieces[1] - m)
        tot = jnp.sum(e0, axis=0, keepdims=True) + jnp.sum(e1, axis=0, keepdims=True)
        num = _lane_values(vw, e0, per_kv) + _lane_values(new_kv(wnew_ref, 1), e1, per_kv)
        owin_ref[...] = num / jnp.maximum(tot, 1e-30)

        rel = dist(past, new_rows)
        s = _lane_scores(new_kv(new_ref, 0), qz) + _lane_bias(rel, tabl_ref)
        chosen = sel_ref[past // SEL_BLOCK:past // SEL_BLOCK + 1, :] > 0.5
        update(jnp.where(chosen & (rel >= 0), s, -jnp.inf), [(new_kv(new_ref, 1), 0, new_rows)])

    values = []
    for p in range(PAGES_PER_STEP):
        page = c * PAGES_PER_STEP + p
        keys = [pages[p][pl.ds(kv, PAGE, stride=TOKEN_ROWS), :].astype(BF16) for kv in range(N_KV)]
        s = _lane_scores(keys, qz)
        if p == PAGES_PER_STEP - 1:
            s = s + _lane_bias(dist(page * PAGE, PAGE), tabl_ref)
        else:
            s = s + far
        per_block = PAGE // SEL_BLOCK
        chosen = jnp.concatenate(
            [jnp.broadcast_to(sel_ref[pl.ds(page * per_block + i, 1), :], (SEL_BLOCK, LANES)) for i in range(per_block)],
            axis=0)
        s_ref[p * PAGE:(p + 1) * PAGE, :] = jnp.where(chosen > 0.5, s, -jnp.inf)
        vals = [pages[p][pl.ds(N_KV + kv, PAGE, stride=TOKEN_ROWS), :].astype(BF16) for kv in range(N_KV)]
        values.append((vals, p * PAGE, (p + 1) * PAGE))
    update(s_ref[...], values)

    @pl.when(c == pl.num_programs(1) - 1)
    def _():
        gt = _sigmoid(gate_ref[...])
        osel = acc_ref[...] / jnp.maximum(l_ref[...], 1e-30)
        o_ref[...] = gt[0:1] * ocmp_ref[...] + gt[1:2] * osel + gt[2:3] * owin_ref[...]


def sample_selected(page_table, cache3, sel_new, qn, sel_t, ocmp_t, win3, win_new, gate_l, tabl, t_new):
    n_seq, n_pages = page_table.shape
    past = n_pages * PAGE
    win_buf = win3.shape[1] // TOKEN_ROWS
    assert (past - FAR_DIST) // PAGE >= n_pages - 1, "only the last page may hold near keys"
    sel_rows = sel_t.shape[1]
    const = lambda *shape: pl.BlockSpec(shape, lambda b, c, pt: (0,) * len(shape))
    row = lambda width: pl.BlockSpec((t_new, width), lambda b, c, pt: (b, 0))
    per_seq = lambda rows, width: pl.BlockSpec((None, rows, width), lambda b, c, pt: (b, 0, 0))
    grid_spec = pltpu.PrefetchScalarGridSpec(
        num_scalar_prefetch=1,
        grid=(n_seq, n_pages // PAGES_PER_STEP),
        in_specs=_page_specs(PAGES_PER_STEP) + [
            row(2 * KV_W), row(NSA_W), per_seq(sel_rows, LANES), per_seq(HEAD_DIM, LANES),
            per_seq(win_buf * TOKEN_ROWS, HEAD_DIM), row(2 * KV_W), per_seq(SUBLANES, LANES),
            const(NUM_BUCKETS, LANES)],
        out_specs=per_seq(HEAD_DIM, LANES),
        scratch_shapes=[pltpu.VMEM((1, LANES), F32), pltpu.VMEM((1, LANES), F32), pltpu.VMEM((HEAD_DIM, LANES), F32),
                        pltpu.VMEM((HEAD_DIM, LANES), F32), pltpu.VMEM((PAGES_PER_STEP * PAGE, LANES), F32)])
    return pl.pallas_call(
        functools.partial(_sample_sel_kernel, past=past, t_new=t_new, win_buf=win_buf),
        grid_spec=grid_spec,
        out_shape=jax.ShapeDtypeStruct((n_seq, HEAD_DIM, LANES), F32),
        compiler_params=_cparams(("parallel", "arbitrary")),
        name="sample_sel",
    )(page_table, *([cache3] * PAGES_PER_STEP), sel_new, qn, sel_t, ocmp_t, win3, win_new, gate_l, tabl)


def _permute_in_proj(w_in):
    d = w_in.shape[0]
    a_end = 4 * HGRN_W
    b_end = a_end + NSA_W + 6 * KV_W
    g_end = b_end + N_GATE
    per_kv = GROUP * 3
    zeros = lambda n: jnp.zeros((d, n), w_in.dtype)
    gates = []
    for kv in range(N_KV):
        gates += [w_in[:, b_end + kv * per_kv:b_end + (kv + 1) * per_kv], zeros(LANES - per_kv)]
    parts = [w_in[:, :a_end], w_in[:, g_end:g_end + 2 * D_MODEL], w_in[:, a_end:b_end]] + gates
    parts.append(zeros(PROJ_TN - N_KV * LANES))
    return jnp.concatenate(parts, axis=1).astype(BF16)


def _compress_weights(k_w1, k_b1, k_w2, v_w1, v_b1, v_w2):
    half = CMP_STRIDE * HEAD_DIM
    cat = lambda w: jnp.concatenate([w[:half], w[half:]], axis=1)
    w1 = jnp.stack([cat(k_w1), cat(v_w1)]).astype(BF16)
    b1 = jnp.stack([k_b1, v_b1]).reshape(2, 1, CMP_HID)
    w2 = jnp.stack([k_w2, v_w2]).astype(BF16)
    return w1, b1, w2


def _lane_table(rel_bias, t_new):
    tab = jnp.repeat(rel_bias, t_new, axis=1)
    return jnp.pad(tab, ((0, 0), (0, LANES - tab.shape[1])))


def _gate_lanes(proj, n_seq, t_new):
    per_kv = GROUP * 3
    g = jnp.stack([proj[:, COL_GATE + kv * LANES:COL_GATE + kv * LANES + per_kv] for kv in range(N_KV)], axis=1)
    g = g.reshape(n_seq, t_new, N_KV, GROUP, 3).transpose(0, 4, 2, 3, 1).reshape(n_seq, 3, N_HEADS_B * t_new)
    return jnp.pad(g, ((0, 0), (0, SUBLANES - 3), (0, LANES - N_HEADS_B * t_new)))


def _trunk(x, mods, s0, w, chunk, tm, nsa_fn):
    nb, t_len, d = x.shape
    m = nb * t_len
    sh1, sc1, g1, sh2, sc2, g2 = mods
    x2 = x.reshape(m, d)
    proj = in_proj(x2, w["norm1"], sc1, sh1, w["w_in"], t_len, tm)
    o_a, s_fin = hgrn(proj, w["lb"], w["hgrn_onorm"], s0, nb, t_len, chunk)
    qn, cmp_new, sel_new, win_new = nsa_prep(proj, w["q_norm"], w["ks_norm"], w["kw_norm"], tm)
    o_b = nsa_fn(proj, qn, cmp_new, sel_new, win_new)
    x1, h2 = post_attn(x2, o_a, o_b, proj, w["w_ba"], w["w_bb"], w["w_out"], g1, w["norm2"], sc2, sh2, t_len,
                       min(tm, 256))
    y = mlp(h2, x1, w["mlp_w1"], w["mlp_w2"], g2, t_len, tm, 512)
    return y.reshape(nb, t_len, d), s_fin, (cmp_new, sel_new, win_new)


def kernel(x_prompt, x_sample, c_prompt, c_sample, cache_cmp_kv, cache_sel_kv, cache_win_kv, state_hgrn, page_table, hgrn_lb_logits, rel_bias, ada_w, ada_b, norm1, norm2, w_in, hgrn_onorm, nsa_q_norm, nsa_kc_norm, nsa_ks_norm, nsa_kw_norm, cmp_k_w1, cmp_k_b1, cmp_k_w2, cmp_v_w1, cmp_v_b1, cmp_v_w2, w_branch_a, w_branch_b, w_out, mlp_w1, mlp_w2):
    n_p, t_p, d = x_prompt.shape
    n_s, t_s, _ = x_sample.shape
    past = page_table.shape[1] * PAGE
    win_buf = cache_win_kv.shape[2]
    kvw = 2 * KV_W
    layer = 0

    lb_all = jnp.cumsum(jax.nn.softmax(hgrn_lb_logits.astype(F32), axis=0), axis=0)
    cw1, cb1, cw2 = _compress_weights(cmp_k_w1[layer], cmp_k_b1[layer], cmp_k_w2[layer],
                                      cmp_v_w1[layer], cmp_v_b1[layer], cmp_v_w2[layer])
    w = dict(norm1=norm1[layer], norm2=norm2[layer], w_in=_permute_in_proj(w_in[layer]),
             hgrn_onorm=hgrn_onorm[layer], lb=lb_all[layer], q_norm=nsa_q_norm[layer],
             ks_norm=nsa_ks_norm[layer], kw_norm=nsa_kw_norm[layer],
             w_ba=w_branch_a[layer].astype(BF16), w_bb=w_branch_b[layer].astype(BF16),
             w_out=w_out[layer].astype(BF16), mlp_w1=mlp_w1[layer].astype(BF16), mlp_w2=mlp_w2[layer].astype(BF16))
    tab = rel_bias.astype(F32).reshape(NUM_BUCKETS * N_HEADS_B)

    mods = ada_mods(jnp.concatenate([c_prompt, c_sample], axis=0).astype(F32), ada_w[layer], ada_b[layer])
    mods = jnp.split(mods, 6, axis=-1)
    mods_p = [a[:n_p].reshape(n_p, 1, d) for a in mods]
    mods_s = [jnp.repeat(a[n_p:], t_s, axis=0).reshape(1, n_s * t_s, d) for a in mods]

    def nsa_p(proj, qn, cmp_new, sel_new, win_new):
        tail = jnp.zeros((n_p * CMP_STRIDE, kvw), F32)
        cmp_tok = compress(cmp_new, tail, cw1, cb1, cw2, nsa_kc_norm[layer], n_p, t_p // CMP_STRIDE)
        return nsa_prompt_attention(tab, qn, cmp_tok, sel_new, win_new, proj, n_p, t_p)

    s0_p = jnp.zeros((n_p, N_HEADS_A, HEAD_DIM, HEAD_DIM), F32)
    y_p, hg_p, (cmp_p, sel_p, win_p) = _trunk(x_prompt, mods_p, s0_p, w, 256, 512, nsa_p)

    def nsa_s(proj, qn, cmp_new, sel_new, win_new):
        n_phys = cache_cmp_kv.shape[1]
        cmp3 = cache_cmp_kv[layer].reshape(n_phys, PAGE_ROWS, HEAD_DIM)
        sel3 = cache_sel_kv[layer].reshape(n_phys, PAGE_ROWS, HEAD_DIM)
        win3 = cache_win_kv[layer].reshape(n_s, win_buf * TOKEN_ROWS, HEAD_DIM)
        tabl = _lane_table(rel_bias.astype(F32), t_s)
        sel_t, ocmp_t = sample_compressed(page_table, cmp3, cmp_new, qn, cw1, cb1, cw2, nsa_kc_norm[layer], tabl, t_s)
        o_t = sample_selected(page_table, sel3, sel_new, qn, sel_t, ocmp_t, win3, win_new,
                              _gate_lanes(proj, n_s, t_s), tabl, t_s)
        o = o_t[:, :, :N_HEADS_B * t_s].reshape(n_s, HEAD_DIM, N_HEADS_B, t_s)
        return o.transpose(0, 3, 2, 1).reshape(n_s * t_s, NSA_W).astype(BF16)

    y_s, hg_s, (cmp_s, sel_s, win_s) = _trunk(x_sample, mods_s, state_hgrn[layer], w, t_s, n_s * t_s, nsa_s)

    dt = x_prompt.dtype
    pages = lambda a: a.reshape(1, n_p, t_p // PAGE, PAGE, 2, N_KV, HEAD_DIM).astype(dt)
    rows_s = lambda a: a.reshape(1, n_s, t_s, 2, N_KV, HEAD_DIM).astype(dt)
    win_keep = min(WINDOW, t_p)
    win_p_out = win_p.reshape(n_p, t_p, kvw)[:, t_p - win_keep:].reshape(1, n_p, win_keep, 2, N_KV, HEAD_DIM)
    win_s_out = jnp.concatenate([cache_win_kv[layer].reshape(n_s, win_buf, kvw).astype(F32),
                                 win_s.reshape(n_s, t_s, kvw)], axis=1)[:, t_s:]
    win_s_out = win_s_out.reshape(1, n_s, win_buf, 2, N_KV, HEAD_DIM)
    return (y_p, y_s, pages(cmp_p), rows_s(cmp_s), pages(sel_p), rows_s(sel_s),
            win_p_out.astype(dt), win_s_out.astype(dt), hg_p[None].astype(dt), hg_s[None].astype(dt))
```

```python
import functools
import math

import numpy as np
import jax
import jax.numpy as jnp
from jax import lax
from jax.experimental import pallas as pl
from jax.experimental.pallas import tpu as pltpu

F32 = jnp.float32
BF16 = jnp.bfloat16

D_MODEL = 2048
N_HEADS_A = 8
HEAD_DIM = 128
HGRN_W = N_HEADS_A * HEAD_DIM
N_HEADS_B = 8
N_KV = 2
GROUP = N_HEADS_B // N_KV
NSA_W = N_HEADS_B * HEAD_DIM
KV_W = N_KV * HEAD_DIM
PAGE = 128
CMP_STRIDE = 16
CMP_BLOCK = 32
CMP_HID = 128
SEL_BLOCK = 64
N_SEL = 16
WINDOW = 512
FORCE_SCORE = 1.0e6
NUM_BUCKETS = 32
REL_MAX_DIST = 128
D_FF = 4 * D_MODEL
EPS = 1e-6
N_GATE = 3 * N_HEADS_B

LANES = 128
SUBLANES = 8
VMEM_LIMIT = 56 * 1024 * 1024

COL_HGRN = 0
COL_MGA = 4 * HGRN_W
COL_MGB = COL_MGA + D_MODEL
COL_QB = COL_MGB + D_MODEL
COL_CMP = COL_QB + NSA_W
COL_SEL = COL_CMP + 2 * KV_W
COL_WIN = COL_SEL + 2 * KV_W
COL_GATE = COL_WIN + 2 * KV_W
PROJ_TN = 512
PROJ_COLS = COL_GATE + PROJ_TN


def _cparams(sem):
    return pltpu.CompilerParams(dimension_semantics=sem, vmem_limit_bytes=VMEM_LIMIT)


def _sigmoid(x):
    return 1.0 / (1.0 + jnp.exp(-x))


def _silu(x):
    return x * _sigmoid(x)


def _rms(x, gain):
    return x * lax.rsqrt(jnp.mean(x * x, axis=-1, keepdims=True) + EPS) * gain


def _lane_iota(shape):
    return lax.broadcasted_iota(jnp.int32, shape, 1)


def _row_iota(shape):
    return lax.broadcasted_iota(jnp.int32, shape, 0)


def _ada_kernel(c_ref, w_ref, b_ref, o_ref):
    a = _silu(c_ref[...]).astype(BF16)
    o_ref[...] = jnp.dot(a, w_ref[...].astype(BF16), preferred_element_type=F32) + b_ref[...]


def ada_mods(c, w, b):
    r, d = c.shape
    n = w.shape[1]
    tn = 1024
    return pl.pallas_call(
        _ada_kernel,
        grid=(n // tn,),
        in_specs=[pl.BlockSpec((r, d), lambda j: (0, 0)),
                  pl.BlockSpec((d, tn), lambda j: (0, j)),
                  pl.BlockSpec((1, tn), lambda j: (0, j))],
        out_specs=pl.BlockSpec((r, tn), lambda j: (0, j)),
        out_shape=jax.ShapeDtypeStruct((r, n), F32),
        compiler_params=_cparams(("parallel",)),
        name="ada_mods",
    )(c, w, b.reshape(1, n))


def _inproj_kernel(x_ref, gain_ref, sc_ref, sh_ref, w_ref, o_ref, h_ref):
    @pl.when(pl.program_id(1) == 0)
    def _():
        h = _rms(x_ref[...], gain_ref[...]) * (1.0 + sc_ref[0]) + sh_ref[0]
        h_ref[...] = h.astype(BF16)

    o_ref[...] = jnp.dot(h_ref[...], w_ref[...], preferred_element_type=F32)


def _mod_spec(mod, tm, rows_per_batch):
    d = mod.shape[-1]
    if mod.shape[1] == 1:
        return pl.BlockSpec((1, 1, d), lambda i, *_: ((i * tm) // rows_per_batch, 0, 0))
    return pl.BlockSpec((1, tm, d), lambda i, *_: (0, i, 0))


def in_proj(x2, gain, sc, sh, w, rows_per_batch, tm):
    m, d = x2.shape
    n = w.shape[1]
    tn = PROJ_TN
    return pl.pallas_call(
        _inproj_kernel,
        grid=(m // tm, n // tn),
        in_specs=[pl.BlockSpec((tm, d), lambda i, j: (i, 0)),
                  pl.BlockSpec((1, d), lambda i, j: (0, 0)),
                  _mod_spec(sc, tm, rows_per_batch),
                  _mod_spec(sh, tm, rows_per_batch),
                  pl.BlockSpec((d, tn), lambda i, j: (0, j))],
        out_specs=pl.BlockSpec((tm, tn), lambda i, j: (i, j)),
        out_shape=jax.ShapeDtypeStruct((m, n), F32),
        scratch_shapes=[pltpu.VMEM((tm, d), BF16)],
        compiler_params=_cparams(("parallel", "arbitrary")),
        name="in_proj",
    )(x2, gain.reshape(1, d), sc, sh, w)


def _post_kernel(x_ref, oa_ref, ob_ref, mga_ref, mgb_ref, wba_ref, wbb_ref, wout_ref,
                 g1_ref, gain2_ref, sc2_ref, sh2_ref, x1_ref, h2_ref):
    ya = jnp.dot(oa_ref[...], wba_ref[...], preferred_element_type=F32)
    yb = jnp.dot(ob_ref[...], wbb_ref[...], preferred_element_type=F32)
    merged = _sigmoid(mga_ref[...]) * ya + _sigmoid(mgb_ref[...]) * yb
    y = jnp.dot(merged.astype(BF16), wout_ref[...], preferred_element_type=F32)
    x1 = x_ref[...] + g1_ref[0] * y
    x1_ref[...] = x1
    h2_ref[...] = (_rms(x1, gain2_ref[...]) * (1.0 + sc2_ref[0]) + sh2_ref[0]).astype(BF16)


def post_attn(x2, oa, ob, proj, wba, wbb, wout, g1, gain2, sc2, sh2, rows_per_batch, tm):
    m, d = x2.shape
    const = lambda i: (0, 0)
    resident = lambda shape: pl.BlockSpec(shape, const, pipeline_mode=pl.Buffered(1))
    return pl.pallas_call(
        _post_kernel,
        grid=(m // tm,),
        in_specs=[pl.BlockSpec((tm, d), lambda i: (i, 0)),
                  pl.BlockSpec((tm, HGRN_W), lambda i: (i, 0)),
                  pl.BlockSpec((tm, NSA_W), lambda i: (i, 0)),
                  pl.BlockSpec((tm, d), lambda i: (i, COL_MGA // D_MODEL)),
                  pl.BlockSpec((tm, d), lambda i: (i, COL_MGB // D_MODEL)),
                  resident((HGRN_W, d)),
                  resident((NSA_W, d)),
                  resident((d, d)),
                  _mod_spec(g1, tm, rows_per_batch),
                  pl.BlockSpec((1, d), const),
                  _mod_spec(sc2, tm, rows_per_batch),
                  _mod_spec(sh2, tm, rows_per_batch)],
        out_specs=[pl.BlockSpec((tm, d), lambda i: (i, 0)),
                   pl.BlockSpec((tm, d), lambda i: (i, 0))],
        out_shape=[jax.ShapeDtypeStruct((m, d), F32), jax.ShapeDtypeStruct((m, d), BF16)],
        compiler_params=_cparams(("parallel",)),
        name="post_attn",
    )(x2, oa, ob, proj, proj, wba, wbb, wout, g1, gain2.reshape(1, d), sc2, sh2)


def _mlp_kernel(h_ref, x1_ref, w1_ref, w2_ref, g2_ref, y_ref, acc_ref):
    f = pl.program_id(1)
    u = jnp.maximum(jnp.dot(h_ref[...], w1_ref[...], preferred_element_type=F32), 0.0)
    part = jnp.dot((u * u).astype(BF16), w2_ref[...], preferred_element_type=F32)

    @pl.when(f == 0)
    def _():
        acc_ref[...] = part

    @pl.when(f > 0)
    def _():
        acc_ref[...] += part

    @pl.when(f == pl.num_programs(1) - 1)
    def _():
        y_ref[...] = x1_ref[...] + g2_ref[0] * acc_ref[...]


def mlp(h2, x1, w1, w2, g2, rows_per_batch, tm, tf):
    m, d = x1.shape
    ff = w1.shape[1]
    return pl.pallas_call(
        _mlp_kernel,
        grid=(m // tm, ff // tf),
        in_specs=[pl.BlockSpec((tm, d), lambda i, f: (i, 0)),
                  pl.BlockSpec((tm, d), lambda i, f: (i, 0)),
                  pl.BlockSpec((d, tf), lambda i, f: (0, f)),
                  pl.BlockSpec((tf, d), lambda i, f: (f, 0)),
                  _mod_spec(g2, tm, rows_per_batch)],
        out_specs=pl.BlockSpec((tm, d), lambda i, f: (i, 0)),
        out_shape=jax.ShapeDtypeStruct((m, d), F32),
        scratch_shapes=[pltpu.VMEM((tm, d), F32)],
        compiler_params=_cparams(("parallel", "arbitrary")),
        name="mlp",
    )(h2, x1, w1, w2, g2)


_NT = (((1,), (1,)), ((), ()))
_TN = (((0,), (0,)), ((), ()))


def _hgrn_kernel(q_ref, z_ref, v_ref, g_ref, lb_ref, on_ref, s0_ref, o_ref, sfin_ref, st_ref, b_ref, *, chunk, heads):
    for h in range(heads):
        _hgrn_head(h, q_ref, z_ref, v_ref, g_ref, lb_ref, on_ref, s0_ref, o_ref, sfin_ref, st_ref, b_ref, chunk)


def _hgrn_head(h, q_ref, z_ref, v_ref, g_ref, lb_ref, on_ref, s0_ref, o_ref, sfin_ref, st_ref, b_ref, chunk):
    c = pl.program_id(2)
    cols = slice(h * HEAD_DIM, (h + 1) * HEAD_DIM)

    @pl.when(c == 0)
    def _():
        st_ref[h] = s0_ref[0, h].T

    q = q_ref[:, cols]
    z = z_ref[:, cols]
    v = v_ref[:, cols]
    lb = lb_ref[h]
    e = jnp.exp(-jnp.abs(z))
    r = 1.0 / (1.0 + e)
    pos = z >= 0.0
    logf = jnp.log(lb + (1.0 - lb) * jnp.where(pos, r, e * r))
    k = (1.0 - lb) * jnp.where(pos, e * r, r)

    t = lax.broadcasted_iota(jnp.int32, (chunk, HEAD_DIM), 0)
    b = logf
    s = 1
    while s < chunk:
        b = b + jnp.where(t >= s, pltpu.roll(b, s, 0), 0.0)
        s *= 2
    b_ref[h] = b

    t8 = t & (SUBLANES - 1)
    o = jnp.zeros((chunk, HEAD_DIM), F32)
    for d in range(SUBLANES):
        kd, bd, vd = (k, b, v) if d == 0 else (pltpu.roll(k, d, 0), pltpu.roll(b, d, 0), pltpu.roll(v, d, 0))
        w = jnp.exp(jnp.where(t8 >= d, b - bd, -jnp.inf))
        o = o + jnp.sum(q * kd * w, axis=-1, keepdims=True) * vd

    if chunk > SUBLANES:
        row = lax.broadcasted_iota(jnp.int32, (chunk, chunk), 0)
        col = lax.broadcasted_iota(jnp.int32, (chunk, chunk), 1)
        apart = row ^ col
        att = jnp.zeros((chunk, chunk), F32)
        m = SUBLANES
        while m < chunk:
            refs = [jnp.broadcast_to(b_ref[h, pl.ds(blk * 2 * m + m - 1, 1), :], (2 * m, HEAD_DIM))
                    for blk in range(chunk // (2 * m))]
            ref_b = refs[0] if len(refs) == 1 else jnp.concatenate(refs, axis=0)
            second = (t & (2 * m - 1)) >= m
            w = jnp.exp(jnp.where(second, b - ref_b, ref_b - b))
            qs = jnp.where(second, q * w, 0.0).astype(BF16)
            ks = jnp.where(second, 0.0, k * w).astype(BF16)
            a_m = lax.dot_general(qs, ks, _NT, preferred_element_type=F32)
            att = att + jnp.where(apart < 2 * m, a_m, 0.0)
            m *= 2
        o = o + jnp.dot(att.astype(BF16), v.astype(BF16), preferred_element_type=F32)

    st = st_ref[h]
    b_last = b_ref[h, pl.ds(chunk - 1, 1), :]
    o = o + lax.dot_general((q * jnp.exp(b)).astype(BF16), st.astype(BF16), _NT, preferred_element_type=F32)
    kt = (k * jnp.exp(b_last - b)).astype(BF16)
    vb = v.astype(BF16)
    if chunk < 2 * SUBLANES:
        pad = jnp.zeros((2 * SUBLANES - chunk, HEAD_DIM), BF16)
        kt = jnp.concatenate([kt, pad], axis=0)
        vb = jnp.concatenate([vb, pad], axis=0)
    st_new = jnp.exp(b_last) * st + lax.dot_general(vb, kt, _TN, preferred_element_type=F32)
    st_ref[h] = st_new

    o_ref[:, cols] = (_rms(o, on_ref[h]) * _silu(g_ref[:, cols])).astype(o_ref.dtype)

    @pl.when(c == pl.num_programs(2) - 1)
    def _():
        sfin_ref[0, h] = st_new.T


def hgrn(proj, lb, onorm, s0, n_batch, t_len, chunk, heads):
    m = proj.shape[0]
    n_c = t_len // chunk
    hb = N_HEADS_A // heads
    width = heads * HEAD_DIM

    def col(group):
        return pl.BlockSpec((chunk, width), lambda bi, h, c: (bi * n_c + c, group * hb + h))

    vec = pl.BlockSpec((heads, 1, HEAD_DIM), lambda bi, h, c: (h, 0, 0))
    state = pl.BlockSpec((1, heads, HEAD_DIM, HEAD_DIM), lambda bi, h, c: (bi, h, 0, 0))
    o_dtype = BF16 if chunk % (2 * SUBLANES) == 0 else F32
    o, s_fin = pl.pallas_call(
        functools.partial(_hgrn_kernel, chunk=chunk, heads=heads),
        grid=(n_batch, hb, n_c),
        in_specs=[col(0), col(1), col(2), col(3), vec, vec, state],
        out_specs=[pl.BlockSpec((chunk, width), lambda bi, h, c: (bi * n_c + c, h)), state],
        out_shape=[jax.ShapeDtypeStruct((m, HGRN_W), o_dtype),
                   jax.ShapeDtypeStruct((n_batch, N_HEADS_A, HEAD_DIM, HEAD_DIM), F32)],
        scratch_shapes=[pltpu.VMEM((heads, HEAD_DIM, HEAD_DIM), F32), pltpu.VMEM((heads, chunk, HEAD_DIM), F32)],
        compiler_params=_cparams(("parallel", "parallel", "arbitrary")),
        name="hgrn",
    )(proj, proj, proj, proj, lb.reshape(N_HEADS_A, 1, HEAD_DIM), onorm.reshape(N_HEADS_A, 1, HEAD_DIM), s0)
    return o.astype(BF16), s_fin


def _bucket_steps():
    n = np.arange(REL_MAX_DIST)
    exact = NUM_BUCKETS // 2
    val = np.log(np.maximum(n, 1) / exact) / math.log(REL_MAX_DIST / exact) * (NUM_BUCKETS - exact)
    frac = np.abs(val - np.round(val))[exact + 1:]
    assert frac.min() > 1e-3, "a bucket edge sits on an integer distance"
    lut = np.where(n < exact, n, np.minimum(exact + np.floor(np.maximum(val, 0.0)).astype(np.int64), NUM_BUCKETS - 1))
    assert lut[-1] == NUM_BUCKETS - 1
    return int(lut[0]), [(int(i), int(lut[i])) for i in range(1, REL_MAX_DIST) if lut[i] != lut[i - 1]]


_BUCKET0, _BUCKET_EDGES = _bucket_steps()
FAR_DIST = _BUCKET_EDGES[-1][0]


def _bias(rel, tab_ref, head):
    val = jnp.full(rel.shape, tab_ref[_BUCKET0 * N_HEADS_B + head], F32)
    for edge, bucket in _BUCKET_EDGES:
        val = jnp.where(rel >= edge, tab_ref[bucket * N_HEADS_B + head], val)
    return val


def _stack_heads(x):
    return jnp.concatenate([x[:, g * HEAD_DIM:(g + 1) * HEAD_DIM] for g in range(GROUP)], axis=0)


def _tile_heads(x):
    return jnp.concatenate([x] * GROUP, axis=0)


def _masked_softmax(s, mask):
    s = jnp.where(mask, s, -jnp.inf)
    m = jnp.max(s, axis=-1, keepdims=True)
    m = jnp.where(m == -jnp.inf, 0.0, m)
    e = jnp.exp(s - m)
    return e / jnp.maximum(jnp.sum(e, axis=-1, keepdims=True), 1e-30)


def _block_importance(p, n_cmp, n_blocks, width):
    rows = p.shape[0] // GROUP
    imp = p[0:rows]
    for g in range(1, GROUP):
        imp = imp + p[g * rows:(g + 1) * rows]
    ci = lax.broadcasted_iota(jnp.int32, (n_cmp, width), 0) * CMP_STRIDE
    si = lax.broadcasted_iota(jnp.int32, (n_cmp, width), 1) * SEL_BLOCK
    overlap = (ci < si + SEL_BLOCK) & (ci + CMP_BLOCK > si) & (si < n_blocks * SEL_BLOCK)
    ov = jnp.where(overlap, 1.0, 0.0).astype(BF16)
    hi = imp.astype(BF16)
    lo = (imp - hi.astype(F32)).astype(BF16)
    return jnp.dot(hi, ov, preferred_element_type=F32) + jnp.dot(lo, ov, preferred_element_type=F32)


def _select_blocks_t(imp_t, cur):
    blk = _row_iota(imp_t.shape)
    forced = (blk == 0) | (blk == cur) | (blk == cur - 1)
    score = jnp.where(blk <= cur, imp_t + jnp.where(forced, FORCE_SCORE, 0.0), -jnp.inf)
    sel = jnp.zeros(imp_t.shape, F32)
    n_rows = imp_t.shape[0]
    for _ in range(N_SEL):
        top = jnp.max(score, axis=0, keepdims=True)
        first = jnp.min(jnp.where(score == top, blk, n_rows), axis=0, keepdims=True)
        pick = blk == first
        sel = jnp.where(pick, 1.0, sel)
        score = jnp.where(pick, -jnp.inf, score)
    return sel


def _gate_mix(gate, parts):
    rows = gate.shape[0]
    gt = _sigmoid(gate)
    outs = []
    for g in range(GROUP):
        o = gt[:, 3 * g:3 * g + 1] * parts[0][g * rows:(g + 1) * rows]
        for j in (1, 2):
            o = o + gt[:, 3 * g + j:3 * g + j + 1] * parts[j][g * rows:(g + 1) * rows]
        outs.append(o)
    return jnp.concatenate(outs, axis=1)


ATT_KSA = 0
ATT_VS = N_KV * 2 * HEAD_DIM
ATT_KW = ATT_VS + KV_W
ATT_VW = ATT_KW + KV_W
ATT_COLS = ATT_VW + KV_W


def _nsa_prep_kernel(q_ref, cmp_ref, sel_ref, win_ref, qn_ref, ksn_ref, kwn_ref, qo_ref, co_ref, so_ref, wo_ref,
                     *att_ref, t_len):
    q = q_ref[...]
    scale = HEAD_DIM ** -0.5
    for h in range(N_HEADS_B):
        sl = slice(h * HEAD_DIM, (h + 1) * HEAD_DIM)
        qo_ref[:, sl] = _rms(q[:, sl], qn_ref[...]) * scale
    co_ref[...] = cmp_ref[...]
    for src, gain, dst in ((sel_ref, ksn_ref, so_ref), (win_ref, kwn_ref, wo_ref)):
        x = src[...]
        for h in range(N_KV):
            sl = slice(h * HEAD_DIM, (h + 1) * HEAD_DIM)
            dst[:, sl] = _rms(x[:, sl], gain[...])
        dst[:, KV_W:] = x[:, KV_W:]
    if att_ref:
        (att_ref,) = att_ref
        tm = q.shape[0]
        pos = (pl.program_id(0) * tm + _row_iota((tm, LANES))) % t_len
        onehot = jnp.where(_lane_iota((tm, LANES)) == pos // SEL_BLOCK, 1.0, 0.0).astype(BF16)
        for h in range(N_KV):
            att_ref[:, ATT_KSA + 2 * h * HEAD_DIM:ATT_KSA + (2 * h + 1) * HEAD_DIM] = (
                so_ref[:, h * HEAD_DIM:(h + 1) * HEAD_DIM].astype(BF16))
            att_ref[:, ATT_KSA + (2 * h + 1) * HEAD_DIM:ATT_KSA + (2 * h + 2) * HEAD_DIM] = onehot
        att_ref[:, ATT_VS:ATT_KW] = so_ref[:, KV_W:].astype(BF16)
        att_ref[:, ATT_KW:ATT_VW] = wo_ref[:, :KV_W].astype(BF16)
        att_ref[:, ATT_VW:ATT_COLS] = wo_ref[:, KV_W:].astype(BF16)


def nsa_prep(proj, q_norm, ks_norm, kw_norm, tm, t_len, attention_operands):
    m = proj.shape[0]
    kvw = 2 * KV_W
    vec = pl.BlockSpec((1, HEAD_DIM), lambda i: (0, 0))
    widths = [NSA_W, kvw, kvw, kvw] + ([ATT_COLS] if attention_operands else [])
    dtypes = [F32] * 4 + ([BF16] if attention_operands else [])
    if attention_operands:
        assert t_len // SEL_BLOCK <= LANES
    return pl.pallas_call(
        functools.partial(_nsa_prep_kernel, t_len=t_len),
        grid=(m // tm,),
        in_specs=[pl.BlockSpec((tm, NSA_W), lambda i: (i, COL_QB // NSA_W)),
                  pl.BlockSpec((tm, kvw), lambda i: (i, COL_CMP // kvw)),
                  pl.BlockSpec((tm, kvw), lambda i: (i, COL_SEL // kvw)),
                  pl.BlockSpec((tm, kvw), lambda i: (i, COL_WIN // kvw)),
                  vec, vec, vec],
        out_specs=[pl.BlockSpec((tm, wd), lambda i: (i, 0)) for wd in widths],
        out_shape=[jax.ShapeDtypeStruct((m, wd), dt) for wd, dt in zip(widths, dtypes)],
        compiler_params=_cparams(("parallel",)),
        name="nsa_prep",
    )(proj, proj, proj, proj, q_norm.reshape(1, HEAD_DIM), ks_norm.reshape(1, HEAD_DIM), kw_norm.reshape(1, HEAD_DIM))


def _compress_kernel(x_ref, tail_ref, w1_ref, b1_ref, w2_ref, gain_ref, o_ref, x2_ref, a1_ref, *, n_half):
    for j in range(CMP_STRIDE):
        x2_ref[0:n_half, j * HEAD_DIM:(j + 1) * HEAD_DIM] = x_ref[pl.ds(j, n_half, stride=CMP_STRIDE), :]
        x2_ref[n_half:n_half + SUBLANES, j * HEAD_DIM:(j + 1) * HEAD_DIM] = jnp.broadcast_to(
            tail_ref[pl.ds(j, 1), :], (SUBLANES, HEAD_DIM))
    a = jnp.dot(x2_ref[...].astype(BF16), w1_ref[0], preferred_element_type=F32)
    a1_ref[...] = a[:, CMP_HID:]
    pre = a[0:n_half, :CMP_HID] + a1_ref[pl.ds(1, n_half), :] + b1_ref[0]
    out = jnp.dot(_silu(pre).astype(BF16), w2_ref[0], preferred_element_type=F32)
    is_k = pl.program_id(1) < N_KV
    o_ref[0, 0] = jnp.where(is_k, _rms(out, gain_ref[...]), out).astype(BF16)


def compress(raw, tail, w1, b1, w2, kc_norm, n_batch, n_half):
    t_len = n_half * CMP_STRIDE
    wsel = lambda b, c: (c // N_KV, 0, 0)
    return pl.pallas_call(
        functools.partial(_compress_kernel, n_half=n_half),
        grid=(n_batch, 2 * N_KV),
        in_specs=[pl.BlockSpec((t_len, HEAD_DIM), lambda b, c: (b, c)),
                  pl.BlockSpec((CMP_STRIDE, HEAD_DIM), lambda b, c: (b, c)),
                  pl.BlockSpec((1, CMP_STRIDE * HEAD_DIM, 2 * CMP_HID), wsel),
                  pl.BlockSpec((1, 1, CMP_HID), wsel),
                  pl.BlockSpec((1, CMP_HID, HEAD_DIM), wsel),
                  pl.BlockSpec((1, HEAD_DIM), lambda b, c: (0, 0))],
        out_specs=pl.BlockSpec((1, 1, n_half, HEAD_DIM), lambda b, c: (b, c, 0, 0)),
        out_shape=jax.ShapeDtypeStruct((n_batch, 2 * N_KV, n_half, HEAD_DIM), BF16),
        scratch_shapes=[pltpu.VMEM((n_half + SUBLANES, CMP_STRIDE * HEAD_DIM), F32),
                        pltpu.VMEM((n_half + SUBLANES, CMP_HID), F32)],
        compiler_params=_cparams(("parallel", "parallel")),
        name="compress",
    )(raw, tail, w1, b1, w2, kc_norm.reshape(1, HEAD_DIM))


TQ = 128
TK = 256
STRIP_W = TQ + 2 * TK
STRIP_ORIGIN = STRIP_W - TK
MASK_OFF = 1 << 20


UNSELECTED = -2.0 ** 30


def _nsa_prompt_kernel(tab_ref, q_ref, kc_ref, vc_ref, ksa_ref, vs_ref, kw_ref, vw_ref, gate_ref, o_ref,
                       strip_ref, s_ref, wide_ref, acc_ref, *, n_half, n_blocks):
    kv = pl.program_id(1)
    n = pl.program_id(2)
    head0 = kv * GROUP
    q0 = n * TQ
    jd = n // 2
    odd = n - 2 * jd
    rows = GROUP * TQ
    far = [tab_ref[(NUM_BUCKETS - 1) * N_HEADS_B + head0 + g] for g in range(GROUP)]

    @pl.when(n == 0)
    def _():
        a = _row_iota((TQ, STRIP_W))
        u = _lane_iota((TQ, STRIP_W))
        for g in range(GROUP):
            strip_ref[g] = _bias(a + STRIP_ORIGIN - u, tab_ref, head0 + g) - far[g]

    q = _stack_heads(q_ref[...])
    qs = q.astype(BF16)
    a_k = _row_iota((TQ, TK))
    c_k = _lane_iota((TQ, TK))

    def tile_start(jj):
        return pl.multiple_of(jnp.maximum(jj, 0) * TK, TK)

    def near_bias(d):
        u0 = pl.multiple_of(STRIP_ORIGIN - d * TK - odd * TQ, TQ)
        return jnp.concatenate([strip_ref[g, :, pl.ds(u0, TK)] for g in range(GROUP)], axis=0)

    def distance(d):
        return _tile_heads(odd * TQ + d * TK + a_k - c_k - jnp.where(jd - d >= 0, 0, MASK_OFF))

    def finish(n_tiles, v_ref):
        m = jnp.max(wide_ref[...], axis=-1, keepdims=True)
        m = jnp.where(m == -jnp.inf, 0.0, m)
        wide_ref[...] = jnp.zeros(wide_ref.shape, F32)
        acc_ref[...] = jnp.zeros(acc_ref.shape, F32)

        def body(jj, carry):
            p = jnp.exp(s_ref[jj] - m)
            wide_ref[...] += p
            acc_ref[...] += jnp.dot(p.astype(BF16), v_ref[pl.ds(tile_start(jj), TK), :], preferred_element_type=F32)
            return carry

        lax.fori_loop(0, n_tiles, body, 0)
        return acc_ref[...] / jnp.maximum(jnp.sum(wide_ref[...], axis=-1, keepdims=True), 1e-30)

    a_c = _row_iota((TQ, n_half))
    rel_c = q0 + a_c - (_lane_iota((TQ, n_half)) * CMP_STRIDE + CMP_BLOCK - 1)
    bias_c = jnp.concatenate([_bias(rel_c, tab_ref, head0 + g) for g in range(GROUP)], axis=0)
    s = lax.dot_general(qs, kc_ref[0, 0], _NT, preferred_element_type=F32)
    p = _masked_softmax(s + bias_c, _tile_heads(rel_c) >= 0)
    o_cmp = jnp.dot(p.astype(BF16), vc_ref[0, 0], preferred_element_type=F32)
    imp = _block_importance(p, n_half, n_blocks, LANES)

    for d in range(3):
        k = kw_ref[pl.ds(tile_start(jd - d), TK), :]
        s = lax.dot_general(qs, k, _NT, preferred_element_type=F32)
        if d < 2:
            s = s + near_bias(d)
        rel = distance(d)
        s = jnp.where((rel >= 0) & (rel < WINDOW), s, -jnp.inf)
        s_ref[d] = s
        wide_ref[...] = s if d == 0 else jnp.maximum(wide_ref[...], s)
    m = jnp.max(wide_ref[...], axis=-1, keepdims=True)
    m = jnp.where(m == -jnp.inf, 0.0, m)
    num = jnp.zeros((rows, HEAD_DIM), F32)
    den = jnp.zeros((rows, TK), F32)
    for d in range(3):
        e = jnp.exp(s_ref[d] - m)
        den = den + e
        num = num + jnp.dot(e.astype(BF16), vw_ref[pl.ds(tile_start(jd - d), TK), :], preferred_element_type=F32)
    o_win = num / jnp.maximum(jnp.sum(den, axis=-1, keepdims=True), 1e-30)

    cur_t = (q0 + _lane_iota((LANES, TQ))) // SEL_BLOCK
    unsel = ((_select_blocks_t(imp.T, cur_t) - 1.0) * -UNSELECTED).T
    qa = jnp.concatenate([q, _tile_heads(unsel)], axis=1).astype(BF16)

    def scores(jj):
        return lax.dot_general(qa, ksa_ref[pl.ds(tile_start(jj), TK), :], _NT, preferred_element_type=F32)

    s = jnp.where(distance(0) >= 0, scores(jd) + near_bias(0), -jnp.inf)
    s_ref[jd] = s
    wide_ref[...] = s

    @pl.when(jd >= 1)
    def _():
        s = scores(jd - 1) + near_bias(1)
        s_ref[jd - 1] = s
        wide_ref[...] = jnp.maximum(wide_ref[...], s)

    def far_step(jj, carry):
        s = scores(jj)
        s_ref[jj] = s
        wide_ref[...] = jnp.maximum(wide_ref[...], s)
        return carry

    lax.fori_loop(0, jnp.maximum(jd - 1, 0), far_step, 0)
    o_sel = finish(jd + 1, vs_ref)

    o_ref[...] = _gate_mix(gate_ref[...], (o_cmp, o_sel, o_win)).astype(o_ref.dtype)


def nsa_prompt_attention(tab, qn, cmp_tok, att, proj, n_batch, t_len):
    m = qn.shape[0]
    n_q = t_len // TQ
    n_half = cmp_tok.shape[2]
    gw = GROUP * HEAD_DIM
    rows = GROUP * TQ
    seq = lambda col, width: pl.BlockSpec((t_len, width), lambda b, kv, n: (b, col // width + kv))
    tok = lambda off: pl.BlockSpec((1, 1, n_half, HEAD_DIM), lambda b, kv, n: (b, off + kv, 0, 0))
    return pl.pallas_call(
        functools.partial(_nsa_prompt_kernel, n_half=n_half, n_blocks=t_len // SEL_BLOCK),
        grid=(n_batch, N_KV, n_q),
        in_specs=[pl.BlockSpec(memory_space=pltpu.SMEM),
                  pl.BlockSpec((TQ, gw), lambda b, kv, n: (b * n_q + n, kv)),
                  tok(0), tok(N_KV), seq(ATT_KSA, 2 * HEAD_DIM), seq(ATT_VS, HEAD_DIM), seq(ATT_KW, HEAD_DIM),
                  seq(ATT_VW, HEAD_DIM),
                  pl.BlockSpec((TQ, LANES), lambda b, kv, n: (b * n_q + n, COL_GATE // LANES + kv))],
        out_specs=pl.BlockSpec((TQ, gw), lambda b, kv, n: (b * n_q + n, kv)),
        out_shape=jax.ShapeDtypeStruct((m, NSA_W), BF16),
        scratch_shapes=[pltpu.VMEM((GROUP, TQ, STRIP_W), F32), pltpu.VMEM((t_len // TK, rows, TK), F32),
                        pltpu.VMEM((rows, TK), F32), pltpu.VMEM((rows, HEAD_DIM), F32)],
        compiler_params=_cparams(("parallel", "parallel", "arbitrary")),
        name="nsa_prompt",
    )(tab, qn, cmp_tok, cmp_tok, att, att, att, att, proj)


PAGES_PER_STEP = 16
TOKEN_ROWS = 2 * N_KV
PAGE_ROWS = PAGE * TOKEN_ROWS


def _lane_queries(q, t_new):
    per_kv = GROUP * t_new
    stacked = jnp.concatenate([q[:, h * HEAD_DIM:(h + 1) * HEAD_DIM] for h in range(N_HEADS_B)], axis=0)
    out = []
    for kv in range(N_KV):
        parts = []
        if kv:
            parts.append(jnp.zeros((kv * per_kv, HEAD_DIM), F32))
        parts.append(stacked[kv * per_kv:(kv + 1) * per_kv])
        parts.append(jnp.zeros((LANES - (kv + 1) * per_kv, HEAD_DIM), F32))
        out.append(jnp.concatenate(parts, axis=0).astype(BF16))
    return out


def _lane_scores(keys, qz):
    s = lax.dot_general(keys[0], qz[0], _NT, preferred_element_type=F32)
    for kv in range(1, N_KV):
        s = s + lax.dot_general(keys[kv], qz[kv], _NT, preferred_element_type=F32)
    return s


def _lane_values(vals, p, per_kv):
    pb = p.astype(BF16)
    lane = _lane_iota((HEAD_DIM, LANES))
    out = lax.dot_general(vals[N_KV - 1], pb, _TN, preferred_element_type=F32)
    for kv in range(N_KV - 2, -1, -1):
        out = jnp.where(lane < (kv + 1) * per_kv, lax.dot_general(vals[kv], pb, _TN, preferred_element_type=F32), out)
    return out


def _lane_bias(rel, tabl_ref):
    val = jnp.broadcast_to(tabl_ref[_BUCKET0:_BUCKET0 + 1, :], rel.shape)
    for edge, bucket in _BUCKET_EDGES:
        val = jnp.where(rel >= edge, tabl_ref[bucket:bucket + 1, :], val)
    return val


def _pad_rows(x, rows):
    return jnp.concatenate([x, jnp.zeros((rows - x.shape[0], x.shape[1]), x.dtype)], axis=0)


def _sample_cmp_kernel(pt_ref, *refs, past, t_new, n_blocks, sel_rows):
    pages = refs[:PAGES_PER_STEP]
    (new_ref, q_ref, w1_ref, b1_ref, w2_ref, gain_ref, tabl_ref,
     sel_ref, ocmp_ref, x2_ref, a_ref) = refs[PAGES_PER_STEP:]
    c = pl.program_id(1)
    n_half = past // CMP_STRIDE
    halves = PAGE // CMP_STRIDE
    step_halves = PAGES_PER_STEP * halves
    per_kv = GROUP * t_new

    for combo in range(TOKEN_ROWS):
        for p in range(PAGES_PER_STEP):
            for j in range(CMP_STRIDE):
                x2_ref[combo, p * halves:(p + 1) * halves, j * HEAD_DIM:(j + 1) * HEAD_DIM] = (
                    pages[p][pl.ds(j * TOKEN_ROWS + combo, halves, stride=CMP_STRIDE * TOKEN_ROWS), :])
        a = jnp.dot(x2_ref[combo].astype(BF16), w1_ref[combo // N_KV], preferred_element_type=F32)
        a_ref[combo, pl.ds(pl.multiple_of(c * step_halves, step_halves), step_halves), :] = a

    @pl.when(c == pl.num_programs(1) - 1)
    def _():
        tok = []
        for combo in range(TOKEN_ROWS):
            kind = combo // N_KV
            sl = slice(combo * HEAD_DIM, (combo + 1) * HEAD_DIM)
            row = jnp.concatenate([new_ref[j:j + 1, sl] for j in range(t_new)]
                                  + [jnp.zeros((1, (CMP_STRIDE - t_new) * HEAD_DIM), F32)], axis=1)
            tail = jnp.broadcast_to(row, (SUBLANES, CMP_STRIDE * HEAD_DIM)).astype(BF16)
            a_ref[combo, n_half:n_half + SUBLANES, :] = jnp.dot(tail, w1_ref[kind], preferred_element_type=F32)
            pre = a_ref[combo, 0:n_half, 0:CMP_HID] + a_ref[combo, pl.ds(1, n_half), CMP_HID:] + b1_ref[kind]
            out = jnp.dot(_silu(pre).astype(BF16), w2_ref[kind], preferred_element_type=F32)
            if kind == 0:
                out = _rms(out, gain_ref[...])
            tok.append(out.astype(BF16))

        qz = _lane_queries(q_ref[...], t_new)
        shape = (n_half, LANES)
        u = _lane_iota(shape) & (t_new - 1)
        rel = past + u - (_row_iota(shape) * CMP_STRIDE + CMP_BLOCK - 1)
        s = jnp.where(rel >= 0, _lane_scores(tok[:N_KV], qz) + _lane_bias(rel, tabl_ref), -jnp.inf)
        m = jnp.max(s, axis=0, keepdims=True)
        e = jnp.exp(s - jnp.where(m == -jnp.inf, 0.0, m))
        p = e / jnp.maximum(jnp.sum(e, axis=0, keepdims=True), 1e-30)
        ocmp_ref[...] = _lane_values(tok[N_KV:], p, per_kv)

        src = _row_iota((LANES, LANES))
        dst = _lane_iota((LANES, LANES))
        pooled = (src // per_kv) * t_new + (src & (t_new - 1))
        pool = jnp.where((src < N_KV * per_kv) & (dst == pooled), 1.0, 0.0).astype(BF16)
        hi = p.astype(BF16)
        lo = (p - hi.astype(F32)).astype(BF16)
        imp = jnp.dot(hi, pool, preferred_element_type=F32) + jnp.dot(lo, pool, preferred_element_type=F32)
        si = _row_iota((sel_rows, n_half)) * SEL_BLOCK
        ci = _lane_iota((sel_rows, n_half)) * CMP_STRIDE
        overlap = (ci < si + SEL_BLOCK) & (ci + CMP_BLOCK > si) & (si < n_blocks * SEL_BLOCK)
        ov = jnp.where(overlap, 1.0, 0.0).astype(BF16)
        hi = imp.astype(BF16)
        lo = (imp - hi.astype(F32)).astype(BF16)
        imp = jnp.dot(ov, hi, preferred_element_type=F32) + jnp.dot(ov, lo, preferred_element_type=F32)

        cur = (past + (_lane_iota((sel_rows, LANES)) & (t_new - 1))) // SEL_BLOCK
        sel = _select_blocks_t(imp, cur)
        unpool = jnp.where((dst < N_KV * per_kv) & (src == (dst // per_kv) * t_new + (dst & (t_new - 1))), 1.0, 0.0)
        sel_ref[...] = jnp.dot(sel.astype(BF16), unpool.astype(BF16), preferred_element_type=F32)


def _page_specs(n_chunk_pages):
    def spec(j):
        return pl.BlockSpec((None, PAGE_ROWS, HEAD_DIM), lambda b, c, pt: (pt[b, c * n_chunk_pages + j], 0, 0))
    return [spec(j) for j in range(n_chunk_pages)]


def sample_compressed(page_table, cache3, cmp_new, qn, w1, b1, w2, kc_norm, tabl, t_new):
    n_seq, n_pages = page_table.shape
    past = n_pages * PAGE
    n_half = past // CMP_STRIDE
    n_blocks = -(-(past + t_new) // SEL_BLOCK)
    sel_rows = -(-n_blocks // LANES) * LANES
    const = lambda *shape: pl.BlockSpec(shape, lambda b, c, pt: (0,) * len(shape))
    row = lambda width: pl.BlockSpec((t_new, width), lambda b, c, pt: (b, 0))
    out = lambda rows: pl.BlockSpec((None, rows, LANES), lambda b, c, pt: (b, 0, 0))
    grid_spec = pltpu.PrefetchScalarGridSpec(
        num_scalar_prefetch=1,
        grid=(n_seq, n_pages // PAGES_PER_STEP),
        in_specs=_page_specs(PAGES_PER_STEP) + [
            row(2 * KV_W), row(NSA_W), const(2, CMP_STRIDE * HEAD_DIM, 2 * CMP_HID), const(2, 1, CMP_HID),
            const(2, CMP_HID, HEAD_DIM), const(1, HEAD_DIM), const(NUM_BUCKETS, LANES)],
        out_specs=[out(sel_rows), out(HEAD_DIM)],
        scratch_shapes=[pltpu.VMEM((TOKEN_ROWS, PAGES_PER_STEP * PAGE // CMP_STRIDE, CMP_STRIDE * HEAD_DIM), F32),
                        pltpu.VMEM((TOKEN_ROWS, n_half + SUBLANES, 2 * CMP_HID), F32)])
    return pl.pallas_call(
        functools.partial(_sample_cmp_kernel, past=past, t_new=t_new, n_blocks=n_blocks, sel_rows=sel_rows),
        grid_spec=grid_spec,
        out_shape=[jax.ShapeDtypeStruct((n_seq, sel_rows, LANES), F32),
                   jax.ShapeDtypeStruct((n_seq, HEAD_DIM, LANES), F32)],
        compiler_params=_cparams(("parallel", "arbitrary")),
        name="sample_cmp",
    )(page_table, *([cache3] * PAGES_PER_STEP), cmp_new, qn, w1, b1, w2, kc_norm.reshape(1, HEAD_DIM), tabl)


def _sample_sel_kernel(pt_ref, *refs, past, t_new, win_buf):
    pages = refs[:PAGES_PER_STEP]
    (new_ref, q_ref, sel_ref, ocmp_ref, win_ref, wnew_ref, gate_ref, tabl_ref,
     o_ref, m_ref, l_ref, acc_ref, owin_ref, s_ref) = refs[PAGES_PER_STEP:]
    c = pl.program_id(1)
    per_kv = GROUP * t_new
    qz = _lane_queries(q_ref[...], t_new)
    far = tabl_ref[NUM_BUCKETS - 1:NUM_BUCKETS, :]
    new_rows = 2 * SUBLANES

    def new_kv(ref, kind):
        return [_pad_rows(ref[:, (kind * N_KV + kv) * HEAD_DIM:(kind * N_KV + kv + 1) * HEAD_DIM], new_rows).astype(BF16)
                for kv in range(N_KV)]

    def dist(pos0, rows):
        shape = (rows, LANES)
        return past + (_lane_iota(shape) & (t_new - 1)) - pos0 - _row_iota(shape)

    def update(s, values):
        m_prev = m_ref[...]
        m_new = jnp.maximum(m_prev, jnp.max(s, axis=0, keepdims=True))
        m_safe = jnp.where(m_new == -jnp.inf, 0.0, m_new)
        alpha = jnp.exp(m_prev - m_safe)
        p = jnp.exp(s - m_safe)
        l_ref[...] = alpha * l_ref[...] + jnp.sum(p, axis=0, keepdims=True)
        acc = alpha * acc_ref[...]
        for vals, lo, hi_ in values:
            acc = acc + _lane_values(vals, p[lo:hi_], per_kv)
        acc_ref[...] = acc
        m_ref[...] = m_new

    @pl.when(c == 0)
    def _():
        m_ref[...] = jnp.full(m_ref.shape, -jnp.inf, F32)
        l_ref[...] = jnp.zeros(l_ref.shape, F32)
        acc_ref[...] = jnp.zeros(acc_ref.shape, F32)

        kw = [win_ref[pl.ds(kv, win_buf, stride=TOKEN_ROWS), :].astype(BF16) for kv in range(N_KV)]
        vw = [win_ref[pl.ds(N_KV + kv, win_buf, stride=TOKEN_ROWS), :].astype(BF16) for kv in range(N_KV)]
        pieces = []
        for keys, rel in ((kw, dist(past - win_buf, win_buf)), (new_kv(wnew_ref, 0), dist(past, new_rows))):
            s = _lane_scores(keys, qz) + _lane_bias(rel, tabl_ref)
            pieces.append(jnp.where((rel >= 0) & (rel < WINDOW), s, -jnp.inf))
        m = jnp.maximum(jnp.max(pieces[0], axis=0, keepdims=True), jnp.max(pieces[1], axis=0, keepdims=True))
        m = jnp.where(m == -jnp.inf, 0.0, m)
        e0 = jnp.exp(pieces[0] - m)
        e1 = jnp.exp(pieces[1] - m)
        tot = jnp.sum(e0, axis=0, keepdims=True) + jnp.sum(e1, axis=0, keepdims=True)
        num = _lane_values(vw, e0, per_kv) + _lane_values(new_kv(wnew_ref, 1), e1, per_kv)
        owin_ref[...] = num / jnp.maximum(tot, 1e-30)

        rel = dist(past, new_rows)
        s = _lane_scores(new_kv(new_ref, 0), qz) + _lane_bias(rel, tabl_ref)
        chosen = sel_ref[past // SEL_BLOCK:past // SEL_BLOCK + 1, :] > 0.5
        update(jnp.where(chosen & (rel >= 0), s, -jnp.inf), [(new_kv(new_ref, 1), 0, new_rows)])

    values = []
    for p in range(PAGES_PER_STEP):
        page = c * PAGES_PER_STEP + p
        keys = [pages[p][pl.ds(kv, PAGE, stride=TOKEN_ROWS), :].astype(BF16) for kv in range(N_KV)]
        s = _lane_scores(keys, qz)
        if p == PAGES_PER_STEP - 1:
            s = s + _lane_bias(dist(page * PAGE, PAGE), tabl_ref)
        else:
            s = s + far
        per_block = PAGE // SEL_BLOCK
        chosen = jnp.concatenate(
            [jnp.broadcast_to(sel_ref[pl.ds(page * per_block + i, 1), :], (SEL_BLOCK, LANES)) for i in range(per_block)],
            axis=0)
        s_ref[p * PAGE:(p + 1) * PAGE, :] = jnp.where(chosen > 0.5, s, -jnp.inf)
        vals = [pages[p][pl.ds(N_KV + kv, PAGE, stride=TOKEN_ROWS), :].astype(BF16) for kv in range(N_KV)]
        values.append((vals, p * PAGE, (p + 1) * PAGE))
    update(s_ref[...], values)

    @pl.when(c == pl.num_programs(1) - 1)
    def _():
        gt = _sigmoid(gate_ref[...])
        osel = acc_ref[...] / jnp.maximum(l_ref[...], 1e-30)
        o_ref[...] = gt[0:1] * ocmp_ref[...] + gt[1:2] * osel + gt[2:3] * owin_ref[...]


def sample_selected(page_table, cache3, sel_new, qn, sel_t, ocmp_t, win3, win_new, gate_l, tabl, t_new):
    n_seq, n_pages = page_table.shape
    past = n_pages * PAGE
    win_buf = win3.shape[1] // TOKEN_ROWS
    assert (past - FAR_DIST) // PAGE >= n_pages - 1, "only the last page may hold near keys"
    sel_rows = sel_t.shape[1]
    const = lambda *shape: pl.BlockSpec(shape, lambda b, c, pt: (0,) * len(shape))
    row = lambda width: pl.BlockSpec((t_new, width), lambda b, c, pt: (b, 0))
    per_seq = lambda rows, width: pl.BlockSpec((None, rows, width), lambda b, c, pt: (b, 0, 0))
    grid_spec = pltpu.PrefetchScalarGridSpec(
        num_scalar_prefetch=1,
        grid=(n_seq, n_pages // PAGES_PER_STEP),
        in_specs=_page_specs(PAGES_PER_STEP) + [
            row(2 * KV_W), row(NSA_W), per_seq(sel_rows, LANES), per_seq(HEAD_DIM, LANES),
            per_seq(win_buf * TOKEN_ROWS, HEAD_DIM), row(2 * KV_W), per_seq(SUBLANES, LANES),
            const(NUM_BUCKETS, LANES)],
        out_specs=per_seq(HEAD_DIM, LANES),
        scratch_shapes=[pltpu.VMEM((1, LANES), F32), pltpu.VMEM((1, LANES), F32), pltpu.VMEM((HEAD_DIM, LANES), F32),
                        pltpu.VMEM((HEAD_DIM, LANES), F32), pltpu.VMEM((PAGES_PER_STEP * PAGE, LANES), F32)])
    return pl.pallas_call(
        functools.partial(_sample_sel_kernel, past=past, t_new=t_new, win_buf=win_buf),
        grid_spec=grid_spec,
        out_shape=jax.ShapeDtypeStruct((n_seq, HEAD_DIM, LANES), F32),
        compiler_params=_cparams(("parallel", "arbitrary")),
        name="sample_sel",
    )(page_table, *([cache3] * PAGES_PER_STEP), sel_new, qn, sel_t, ocmp_t, win3, win_new, gate_l, tabl)


def _permute_in_proj(w_in):
    d = w_in.shape[0]
    a_end = 4 * HGRN_W
    b_end = a_end + NSA_W + 6 * KV_W
    g_end = b_end + N_GATE
    per_kv = GROUP * 3
    zeros = lambda n: jnp.zeros((d, n), w_in.dtype)
    gates = []
    for kv in range(N_KV):
        gates += [w_in[:, b_end + kv * per_kv:b_end + (kv + 1) * per_kv], zeros(LANES - per_kv)]
    parts = [w_in[:, :a_end], w_in[:, g_end:g_end + 2 * D_MODEL], w_in[:, a_end:b_end]] + gates
    parts.append(zeros(PROJ_TN - N_KV * LANES))
    return jnp.concatenate(parts, axis=1).astype(BF16)


def _compress_weights(k_w1, k_b1, k_w2, v_w1, v_b1, v_w2):
    half = CMP_STRIDE * HEAD_DIM
    cat = lambda w: jnp.concatenate([w[:half], w[half:]], axis=1)
    w1 = jnp.stack([cat(k_w1), cat(v_w1)]).astype(BF16)
    b1 = jnp.stack([k_b1, v_b1]).reshape(2, 1, CMP_HID)
    w2 = jnp.stack([k_w2, v_w2]).astype(BF16)
    return w1, b1, w2


def _lane_table(rel_bias, t_new):
    tab = jnp.repeat(rel_bias, t_new, axis=1)
    return jnp.pad(tab, ((0, 0), (0, LANES - tab.shape[1])))


def _gate_lanes(proj, n_seq, t_new):
    per_kv = GROUP * 3
    g = jnp.stack([proj[:, COL_GATE + kv * LANES:COL_GATE + kv * LANES + per_kv] for kv in range(N_KV)], axis=1)
    g = g.reshape(n_seq, t_new, N_KV, GROUP, 3).transpose(0, 4, 2, 3, 1).reshape(n_seq, 3, N_HEADS_B * t_new)
    return jnp.pad(g, ((0, 0), (0, SUBLANES - 3), (0, LANES - N_HEADS_B * t_new)))


def _tiles(m, t_len):
    big = m >= 1024
    return dict(proj_tm=1024 if big else m, prep_tm=512 if big else m, post_tm=256, mlp_tm=512 if big else m,
                mlp_tf=1024, hgrn_chunk=min(t_len, 256), hgrn_heads=2 if big else N_HEADS_A)


def _trunk(x, mods, s0, w, nsa_fn, attention_operands):
    nb, t_len, d = x.shape
    m = nb * t_len
    tl = _tiles(m, t_len)
    sh1, sc1, g1, sh2, sc2, g2 = mods
    x2 = x.reshape(m, d)
    proj = in_proj(x2, w["norm1"], sc1, sh1, w["w_in"], t_len, tl["proj_tm"])
    o_a, s_fin = hgrn(proj, w["lb"], w["hgrn_onorm"], s0, nb, t_len, tl["hgrn_chunk"], tl["hgrn_heads"])
    qn, cmp_new, sel_new, win_new, *att = nsa_prep(proj, w["q_norm"], w["ks_norm"], w["kw_norm"], tl["prep_tm"], t_len,
                                                   attention_operands)
    o_b = nsa_fn(proj, qn, cmp_new, sel_new, win_new, *att)
    x1, h2 = post_attn(x2, o_a, o_b, proj, w["w_ba"], w["w_bb"], w["w_out"], g1, w["norm2"], sc2, sh2, t_len,
                       tl["post_tm"])
    y = mlp(h2, x1, w["mlp_w1"], w["mlp_w2"], g2, t_len, tl["mlp_tm"], tl["mlp_tf"])
    return y.reshape(nb, t_len, d), s_fin, (cmp_new, sel_new, win_new)


def kernel(x_prompt, x_sample, c_prompt, c_sample, cache_cmp_kv, cache_sel_kv, cache_win_kv, state_hgrn, page_table, hgrn_lb_logits, rel_bias, ada_w, ada_b, norm1, norm2, w_in, hgrn_onorm, nsa_q_norm, nsa_kc_norm, nsa_ks_norm, nsa_kw_norm, cmp_k_w1, cmp_k_b1, cmp_k_w2, cmp_v_w1, cmp_v_b1, cmp_v_w2, w_branch_a, w_branch_b, w_out, mlp_w1, mlp_w2):
    n_p, t_p, d = x_prompt.shape
    n_s, t_s, _ = x_sample.shape
    past = page_table.shape[1] * PAGE
    win_buf = cache_win_kv.shape[2]
    kvw = 2 * KV_W
    layer = 0

    lb_all = jnp.cumsum(jax.nn.softmax(hgrn_lb_logits.astype(F32), axis=0), axis=0)
    cw1, cb1, cw2 = _compress_weights(cmp_k_w1[layer], cmp_k_b1[layer], cmp_k_w2[layer],
                                      cmp_v_w1[layer], cmp_v_b1[layer], cmp_v_w2[layer])
    w = dict(norm1=norm1[layer], norm2=norm2[layer], w_in=_permute_in_proj(w_in[layer]),
             hgrn_onorm=hgrn_onorm[layer], lb=lb_all[layer], q_norm=nsa_q_norm[layer],
             ks_norm=nsa_ks_norm[layer], kw_norm=nsa_kw_norm[layer],
             w_ba=w_branch_a[layer].astype(BF16), w_bb=w_branch_b[layer].astype(BF16),
             w_out=w_out[layer].astype(BF16), mlp_w1=mlp_w1[layer].astype(BF16), mlp_w2=mlp_w2[layer].astype(BF16))
    tab = rel_bias.astype(F32).reshape(NUM_BUCKETS * N_HEADS_B)

    mods = ada_mods(jnp.concatenate([c_prompt, c_sample], axis=0).astype(F32), ada_w[layer], ada_b[layer])
    mods = jnp.split(mods, 6, axis=-1)
    mods_p = [a[:n_p].reshape(n_p, 1, d) for a in mods]
    mods_s = [jnp.repeat(a[n_p:], t_s, axis=0).reshape(1, n_s * t_s, d) for a in mods]

    def nsa_p(proj, qn, cmp_new, sel_new, win_new, att):
        tail = jnp.zeros((n_p * CMP_STRIDE, kvw), F32)
        cmp_tok = compress(cmp_new, tail, cw1, cb1, cw2, nsa_kc_norm[layer], n_p, t_p // CMP_STRIDE)
        return nsa_prompt_attention(tab, qn, cmp_tok, att, proj, n_p, t_p)

    s0_p = jnp.zeros((n_p, N_HEADS_A, HEAD_DIM, HEAD_DIM), F32)
    y_p, hg_p, (cmp_p, sel_p, win_p) = _trunk(x_prompt, mods_p, s0_p, w, nsa_p, True)

    def nsa_s(proj, qn, cmp_new, sel_new, win_new):
        n_phys = cache_cmp_kv.shape[1]
        cmp3 = cache_cmp_kv[layer].reshape(n_phys, PAGE_ROWS, HEAD_DIM)
        sel3 = cache_sel_kv[layer].reshape(n_phys, PAGE_ROWS, HEAD_DIM)
        win3 = cache_win_kv[layer].reshape(n_s, win_buf * TOKEN_ROWS, HEAD_DIM)
        tabl = _lane_table(rel_bias.astype(F32), t_s)
        sel_t, ocmp_t = sample_compressed(page_table, cmp3, cmp_new, qn, cw1, cb1, cw2, nsa_kc_norm[layer], tabl, t_s)
        o_t = sample_selected(page_table, sel3, sel_new, qn, sel_t, ocmp_t, win3, win_new,
                              _gate_lanes(proj, n_s, t_s), tabl, t_s)
        o = o_t[:, :, :N_HEADS_B * t_s].reshape(n_s, HEAD_DIM, N_HEADS_B, t_s)
        return o.transpose(0, 3, 2, 1).reshape(n_s * t_s, NSA_W).astype(BF16)

    y_s, hg_s, (cmp_s, sel_s, win_s) = _trunk(x_sample, mods_s, state_hgrn[layer], w, nsa_s, False)

    dt = x_prompt.dtype
    pages = lambda a: a.reshape(1, n_p, t_p // PAGE, PAGE, 2, N_KV, HEAD_DIM).astype(dt)
    rows_s = lambda a: a.reshape(1, n_s, t_s, 2, N_KV, HEAD_DIM).astype(dt)
    win_keep = min(WINDOW, t_p)
    win_p_out = win_p.reshape(n_p, t_p, kvw)[:, t_p - win_keep:].reshape(1, n_p, win_keep, 2, N_KV, HEAD_DIM)
    win_s_out = jnp.concatenate([cache_win_kv[layer].reshape(n_s, win_buf, kvw).astype(F32),
                                 win_s.reshape(n_s, t_s, kvw)], axis=1)[:, t_s:]
    win_s_out = win_s_out.reshape(1, n_s, win_buf, 2, N_KV, HEAD_DIM)
    return (y_p, y_s, pages(cmp_p), rows_s(cmp_s), pages(sel_p), rows_s(sel_s),
            win_p_out.astype(dt), win_s_out.astype(dt), hg_p[None].astype(dt), hg_s[None].astype(dt))
```

```python
import functools
import math

import numpy as np
import jax
import jax.numpy as jnp
from jax import lax
from jax.experimental import pallas as pl
from jax.experimental.pallas import tpu as pltpu

F32 = jnp.float32
BF16 = jnp.bfloat16

D_MODEL = 2048
N_HEADS_A = 8
HEAD_DIM = 128
HGRN_W = N_HEADS_A * HEAD_DIM
N_HEADS_B = 8
N_KV = 2
GROUP = N_HEADS_B // N_KV
NSA_W = N_HEADS_B * HEAD_DIM
KV_W = N_KV * HEAD_DIM
PAGE = 128
CMP_STRIDE = 16
CMP_BLOCK = 32
CMP_HID = 128
SEL_BLOCK = 64
N_SEL = 16
WINDOW = 512
FORCE_SCORE = 1.0e6
NUM_BUCKETS = 32
REL_MAX_DIST = 128
D_FF = 4 * D_MODEL
EPS = 1e-6
N_GATE = 3 * N_HEADS_B

LANES = 128
SUBLANES = 8
VMEM_LIMIT = 56 * 1024 * 1024

COL_HGRN = 0
COL_MGA = 4 * HGRN_W
COL_MGB = COL_MGA + D_MODEL
COL_QB = COL_MGB + D_MODEL
COL_CMP = COL_QB + NSA_W
COL_SEL = COL_CMP + 2 * KV_W
COL_WIN = COL_SEL + 2 * KV_W
COL_GATE = COL_WIN + 2 * KV_W
PROJ_TN = 512
PROJ_COLS = COL_GATE + PROJ_TN


def _cparams(sem):
    return pltpu.CompilerParams(dimension_semantics=sem, vmem_limit_bytes=VMEM_LIMIT)


def _sigmoid(x):
    return 1.0 / (1.0 + jnp.exp(-x))


def _silu(x):
    return x * _sigmoid(x)


def _rms(x, gain):
    return x * lax.rsqrt(jnp.mean(x * x, axis=-1, keepdims=True) + EPS) * gain


def _lane_iota(shape):
    return lax.broadcasted_iota(jnp.int32, shape, 1)


def _row_iota(shape):
    return lax.broadcasted_iota(jnp.int32, shape, 0)


def _ada_kernel(c_ref, w_ref, b_ref, o_ref):
    a = _silu(c_ref[...]).astype(BF16)
    o_ref[...] = jnp.dot(a, w_ref[...].astype(BF16), preferred_element_type=F32) + b_ref[...]


def ada_mods(c, w, b):
    r, d = c.shape
    n = w.shape[1]
    tn = 1024
    return pl.pallas_call(
        _ada_kernel,
        grid=(n // tn,),
        in_specs=[pl.BlockSpec((r, d), lambda j: (0, 0)),
                  pl.BlockSpec((d, tn), lambda j: (0, j)),
                  pl.BlockSpec((1, tn), lambda j: (0, j))],
        out_specs=pl.BlockSpec((r, tn), lambda j: (0, j)),
        out_shape=jax.ShapeDtypeStruct((r, n), F32),
        compiler_params=_cparams(("parallel",)),
        name="ada_mods",
    )(c, w, b.reshape(1, n))


def _inproj_kernel(x_ref, gain_ref, sc_ref, sh_ref, w_ref, o_ref, h_ref):
    @pl.when(pl.program_id(1) == 0)
    def _():
        h = _rms(x_ref[...], gain_ref[...]) * (1.0 + sc_ref[0]) + sh_ref[0]
        h_ref[...] = h.astype(BF16)

    o_ref[...] = jnp.dot(h_ref[...], w_ref[...], preferred_element_type=F32)


def _mod_spec(mod, tm, rows_per_batch):
    d = mod.shape[-1]
    if mod.shape[1] == 1:
        return pl.BlockSpec((1, 1, d), lambda i, *_: ((i * tm) // rows_per_batch, 0, 0))
    return pl.BlockSpec((1, tm, d), lambda i, *_: (0, i, 0))


def in_proj(x2, gain, sc, sh, w, rows_per_batch, tm):
    m, d = x2.shape
    n = w.shape[1]
    tn = PROJ_TN
    return pl.pallas_call(
        _inproj_kernel,
        grid=(m // tm, n // tn),
        in_specs=[pl.BlockSpec((tm, d), lambda i, j: (i, 0)),
                  pl.BlockSpec((1, d), lambda i, j: (0, 0)),
                  _mod_spec(sc, tm, rows_per_batch),
                  _mod_spec(sh, tm, rows_per_batch),
                  pl.BlockSpec((d, tn), lambda i, j: (0, j))],
        out_specs=pl.BlockSpec((tm, tn), lambda i, j: (i, j)),
        out_shape=jax.ShapeDtypeStruct((m, n), F32),
        scratch_shapes=[pltpu.VMEM((tm, d), BF16)],
        compiler_params=_cparams(("parallel", "arbitrary")),
        name="in_proj",
    )(x2, gain.reshape(1, d), sc, sh, w)


def _post_kernel(x_ref, oa_ref, ob_ref, mga_ref, mgb_ref, wba_ref, wbb_ref, wout_ref,
                 g1_ref, gain2_ref, sc2_ref, sh2_ref, x1_ref, h2_ref):
    ya = jnp.dot(oa_ref[...], wba_ref[...], preferred_element_type=F32)
    yb = jnp.dot(ob_ref[...], wbb_ref[...], preferred_element_type=F32)
    merged = _sigmoid(mga_ref[...]) * ya + _sigmoid(mgb_ref[...]) * yb
    y = jnp.dot(merged.astype(BF16), wout_ref[...], preferred_element_type=F32)
    x1 = x_ref[...] + g1_ref[0] * y
    x1_ref[...] = x1
    h2_ref[...] = (_rms(x1, gain2_ref[...]) * (1.0 + sc2_ref[0]) + sh2_ref[0]).astype(BF16)


def post_attn(x2, oa, ob, proj, wba, wbb, wout, g1, gain2, sc2, sh2, rows_per_batch, tm):
    m, d = x2.shape
    const = lambda i: (0, 0)
    resident = lambda shape: pl.BlockSpec(shape, const, pipeline_mode=pl.Buffered(1))
    return pl.pallas_call(
        _post_kernel,
        grid=(m // tm,),
        in_specs=[pl.BlockSpec((tm, d), lambda i: (i, 0)),
                  pl.BlockSpec((tm, HGRN_W), lambda i: (i, 0)),
                  pl.BlockSpec((tm, NSA_W), lambda i: (i, 0)),
                  pl.BlockSpec((tm, d), lambda i: (i, COL_MGA // D_MODEL)),
                  pl.BlockSpec((tm, d), lambda i: (i, COL_MGB // D_MODEL)),
                  resident((HGRN_W, d)),
                  resident((NSA_W, d)),
                  resident((d, d)),
                  _mod_spec(g1, tm, rows_per_batch),
                  pl.BlockSpec((1, d), const),
                  _mod_spec(sc2, tm, rows_per_batch),
                  _mod_spec(sh2, tm, rows_per_batch)],
        out_specs=[pl.BlockSpec((tm, d), lambda i: (i, 0)),
                   pl.BlockSpec((tm, d), lambda i: (i, 0))],
        out_shape=[jax.ShapeDtypeStruct((m, d), F32), jax.ShapeDtypeStruct((m, d), BF16)],
        compiler_params=_cparams(("parallel",)),
        name="post_attn",
    )(x2, oa, ob, proj, proj, wba, wbb, wout, g1, gain2.reshape(1, d), sc2, sh2)


def _mlp_kernel(h_ref, x1_ref, w1_ref, w2_ref, g2_ref, y_ref, acc_ref):
    f = pl.program_id(1)
    u = jnp.maximum(jnp.dot(h_ref[...], w1_ref[...], preferred_element_type=F32), 0.0)
    part = jnp.dot((u * u).astype(BF16), w2_ref[...], preferred_element_type=F32)

    @pl.when(f == 0)
    def _():
        acc_ref[...] = part

    @pl.when(f > 0)
    def _():
        acc_ref[...] += part

    @pl.when(f == pl.num_programs(1) - 1)
    def _():
        y_ref[...] = x1_ref[...] + g2_ref[0] * acc_ref[...]


def mlp(h2, x1, w1, w2, g2, rows_per_batch, tm, tf):
    m, d = x1.shape
    ff = w1.shape[1]
    return pl.pallas_call(
        _mlp_kernel,
        grid=(m // tm, ff // tf),
        in_specs=[pl.BlockSpec((tm, d), lambda i, f: (i, 0)),
                  pl.BlockSpec((tm, d), lambda i, f: (i, 0)),
                  pl.BlockSpec((d, tf), lambda i, f: (0, f)),
                  pl.BlockSpec((tf, d), lambda i, f: (f, 0)),
                  _mod_spec(g2, tm, rows_per_batch)],
        out_specs=pl.BlockSpec((tm, d), lambda i, f: (i, 0)),
        out_shape=jax.ShapeDtypeStruct((m, d), F32),
        scratch_shapes=[pltpu.VMEM((tm, d), F32)],
        compiler_params=_cparams(("parallel", "arbitrary")),
        name="mlp",
    )(h2, x1, w1, w2, g2)


_NT = (((1,), (1,)), ((), ()))
_TN = (((0,), (0,)), ((), ()))


def _hgrn_kernel(q_ref, z_ref, v_ref, g_ref, lb_ref, on_ref, s0_ref, o_ref, sfin_ref, st_ref, b_ref, *, chunk, heads):
    for h in range(heads):
        _hgrn_head(h, q_ref, z_ref, v_ref, g_ref, lb_ref, on_ref, s0_ref, o_ref, sfin_ref, st_ref, b_ref, chunk)


def _hgrn_head(h, q_ref, z_ref, v_ref, g_ref, lb_ref, on_ref, s0_ref, o_ref, sfin_ref, st_ref, b_ref, chunk):
    c = pl.program_id(2)
    cols = slice(h * HEAD_DIM, (h + 1) * HEAD_DIM)

    @pl.when(c == 0)
    def _():
        st_ref[h] = s0_ref[0, h].T

    q = q_ref[:, cols]
    z = z_ref[:, cols]
    v = v_ref[:, cols]
    lb = lb_ref[h]
    e = jnp.exp(-jnp.abs(z))
    r = 1.0 / (1.0 + e)
    pos = z >= 0.0
    logf = jnp.log(lb + (1.0 - lb) * jnp.where(pos, r, e * r))
    k = (1.0 - lb) * jnp.where(pos, e * r, r)

    t = lax.broadcasted_iota(jnp.int32, (chunk, HEAD_DIM), 0)
    b = logf
    s = 1
    while s < chunk:
        b = b + jnp.where(t >= s, pltpu.roll(b, s, 0), 0.0)
        s *= 2
    b_ref[h] = b

    t8 = t & (SUBLANES - 1)
    o = jnp.zeros((chunk, HEAD_DIM), F32)
    for d in range(SUBLANES):
        kd, bd, vd = (k, b, v) if d == 0 else (pltpu.roll(k, d, 0), pltpu.roll(b, d, 0), pltpu.roll(v, d, 0))
        w = jnp.exp(jnp.where(t8 >= d, b - bd, -jnp.inf))
        o = o + jnp.sum(q * kd * w, axis=-1, keepdims=True) * vd

    if chunk > SUBLANES:
        row = lax.broadcasted_iota(jnp.int32, (chunk, chunk), 0)
        col = lax.broadcasted_iota(jnp.int32, (chunk, chunk), 1)
        apart = row ^ col
        att = jnp.zeros((chunk, chunk), F32)
        m = SUBLANES
        while m < chunk:
            refs = [jnp.broadcast_to(b_ref[h, pl.ds(blk * 2 * m + m - 1, 1), :], (2 * m, HEAD_DIM))
                    for blk in range(chunk // (2 * m))]
            ref_b = refs[0] if len(refs) == 1 else jnp.concatenate(refs, axis=0)
            second = (t & (2 * m - 1)) >= m
            w = jnp.exp(jnp.where(second, b - ref_b, ref_b - b))
            qs = jnp.where(second, q * w, 0.0).astype(BF16)
            ks = jnp.where(second, 0.0, k * w).astype(BF16)
            a_m = lax.dot_general(qs, ks, _NT, preferred_element_type=F32)
            att = att + jnp.where(apart < 2 * m, a_m, 0.0)
            m *= 2
        o = o + jnp.dot(att.astype(BF16), v.astype(BF16), preferred_element_type=F32)

    st = st_ref[h]
    b_last = b_ref[h, pl.ds(chunk - 1, 1), :]
    o = o + lax.dot_general((q * jnp.exp(b)).astype(BF16), st.astype(BF16), _NT, preferred_element_type=F32)
    kt = (k * jnp.exp(b_last - b)).astype(BF16)
    vb = v.astype(BF16)
    if chunk < 2 * SUBLANES:
        pad = jnp.zeros((2 * SUBLANES - chunk, HEAD_DIM), BF16)
        kt = jnp.concatenate([kt, pad], axis=0)
        vb = jnp.concatenate([vb, pad], axis=0)
    st_new = jnp.exp(b_last) * st + lax.dot_general(vb, kt, _TN, preferred_element_type=F32)
    st_ref[h] = st_new

    o_ref[:, cols] = (_rms(o, on_ref[h]) * _silu(g_ref[:, cols])).astype(o_ref.dtype)

    @pl.when(c == pl.num_programs(2) - 1)
    def _():
        sfin_ref[0, h] = st_new.T


def hgrn(proj, lb, onorm, s0, n_batch, t_len, chunk, heads):
    m = proj.shape[0]
    n_c = t_len // chunk
    hb = N_HEADS_A // heads
    width = heads * HEAD_DIM

    def col(group):
        return pl.BlockSpec((chunk, width), lambda bi, h, c: (bi * n_c + c, group * hb + h))

    vec = pl.BlockSpec((heads, 1, HEAD_DIM), lambda bi, h, c: (h, 0, 0))
    state = pl.BlockSpec((1, heads, HEAD_DIM, HEAD_DIM), lambda bi, h, c: (bi, h, 0, 0))
    o_dtype = BF16 if chunk % (2 * SUBLANES) == 0 else F32
    o, s_fin = pl.pallas_call(
        functools.partial(_hgrn_kernel, chunk=chunk, heads=heads),
        grid=(n_batch, hb, n_c),
        in_specs=[col(0), col(1), col(2), col(3), vec, vec, state],
        out_specs=[pl.BlockSpec((chunk, width), lambda bi, h, c: (bi * n_c + c, h)), state],
        out_shape=[jax.ShapeDtypeStruct((m, HGRN_W), o_dtype),
                   jax.ShapeDtypeStruct((n_batch, N_HEADS_A, HEAD_DIM, HEAD_DIM), F32)],
        scratch_shapes=[pltpu.VMEM((heads, HEAD_DIM, HEAD_DIM), F32), pltpu.VMEM((heads, chunk, HEAD_DIM), F32)],
        compiler_params=_cparams(("parallel", "parallel", "arbitrary")),
        name="hgrn",
    )(proj, proj, proj, proj, lb.reshape(N_HEADS_A, 1, HEAD_DIM), onorm.reshape(N_HEADS_A, 1, HEAD_DIM), s0)
    return o.astype(BF16), s_fin


def _bucket_steps():
    n = np.arange(REL_MAX_DIST)
    exact = NUM_BUCKETS // 2
    val = np.log(np.maximum(n, 1) / exact) / math.log(REL_MAX_DIST / exact) * (NUM_BUCKETS - exact)
    frac = np.abs(val - np.round(val))[exact + 1:]
    assert frac.min() > 1e-3, "a bucket edge sits on an integer distance"
    lut = np.where(n < exact, n, np.minimum(exact + np.floor(np.maximum(val, 0.0)).astype(np.int64), NUM_BUCKETS - 1))
    assert lut[-1] == NUM_BUCKETS - 1
    return int(lut[0]), [(int(i), int(lut[i])) for i in range(1, REL_MAX_DIST) if lut[i] != lut[i - 1]]


_BUCKET0, _BUCKET_EDGES = _bucket_steps()
FAR_DIST = _BUCKET_EDGES[-1][0]


def _bias(rel, tab_ref, head):
    val = jnp.full(rel.shape, tab_ref[_BUCKET0 * N_HEADS_B + head], F32)
    for edge, bucket in _BUCKET_EDGES:
        val = jnp.where(rel >= edge, tab_ref[bucket * N_HEADS_B + head], val)
    return val


def _stack_heads(x):
    return jnp.concatenate([x[:, g * HEAD_DIM:(g + 1) * HEAD_DIM] for g in range(GROUP)], axis=0)


def _tile_heads(x):
    return jnp.concatenate([x] * GROUP, axis=0)


def _masked_softmax(s, mask):
    s = jnp.where(mask, s, -jnp.inf)
    m = jnp.max(s, axis=-1, keepdims=True)
    m = jnp.where(m == -jnp.inf, 0.0, m)
    e = jnp.exp(s - m)
    return e / jnp.maximum(jnp.sum(e, axis=-1, keepdims=True), 1e-30)


def _block_importance(p, n_cmp, n_blocks, width):
    rows = p.shape[0] // GROUP
    imp = p[0:rows]
    for g in range(1, GROUP):
        imp = imp + p[g * rows:(g + 1) * rows]
    ci = lax.broadcasted_iota(jnp.int32, (n_cmp, width), 0) * CMP_STRIDE
    si = lax.broadcasted_iota(jnp.int32, (n_cmp, width), 1) * SEL_BLOCK
    overlap = (ci < si + SEL_BLOCK) & (ci + CMP_BLOCK > si) & (si < n_blocks * SEL_BLOCK)
    ov = jnp.where(overlap, 1.0, 0.0).astype(BF16)
    hi = imp.astype(BF16)
    lo = (imp - hi.astype(F32)).astype(BF16)
    return jnp.dot(hi, ov, preferred_element_type=F32) + jnp.dot(lo, ov, preferred_element_type=F32)


def _select_blocks_t(imp_t, cur):
    blk = _row_iota(imp_t.shape)
    forced = (blk == 0) | (blk == cur) | (blk == cur - 1)
    score = jnp.where(blk <= cur, imp_t + jnp.where(forced, FORCE_SCORE, 0.0), -jnp.inf)
    sel = jnp.zeros(imp_t.shape, F32)
    n_rows = imp_t.shape[0]
    for _ in range(N_SEL):
        top = jnp.max(score, axis=0, keepdims=True)
        first = jnp.min(jnp.where(score == top, blk, n_rows), axis=0, keepdims=True)
        pick = blk == first
        sel = jnp.where(pick, 1.0, sel)
        score = jnp.where(pick, -jnp.inf, score)
    return sel


def _gate_mix(gate, parts):
    rows = gate.shape[0]
    gt = _sigmoid(gate)
    outs = []
    for g in range(GROUP):
        o = gt[:, 3 * g:3 * g + 1] * parts[0][g * rows:(g + 1) * rows]
        for j in (1, 2):
            o = o + gt[:, 3 * g + j:3 * g + j + 1] * parts[j][g * rows:(g + 1) * rows]
        outs.append(o)
    return jnp.concatenate(outs, axis=1)


ATT_KSA = 0
ATT_VS = N_KV * 2 * HEAD_DIM
ATT_KW = ATT_VS + KV_W
ATT_VW = ATT_KW + KV_W
ATT_COLS = ATT_VW + KV_W


def _nsa_prep_kernel(q_ref, cmp_ref, sel_ref, win_ref, qn_ref, ksn_ref, kwn_ref, qo_ref, co_ref, so_ref, wo_ref,
                     *att_ref, t_len):
    q = q_ref[...]
    scale = HEAD_DIM ** -0.5
    for h in range(N_HEADS_B):
        sl = slice(h * HEAD_DIM, (h + 1) * HEAD_DIM)
        qo_ref[:, sl] = _rms(q[:, sl], qn_ref[...]) * scale
    co_ref[...] = cmp_ref[...]
    for src, gain, dst in ((sel_ref, ksn_ref, so_ref), (win_ref, kwn_ref, wo_ref)):
        x = src[...]
        for h in range(N_KV):
            sl = slice(h * HEAD_DIM, (h + 1) * HEAD_DIM)
            dst[:, sl] = _rms(x[:, sl], gain[...])
        dst[:, KV_W:] = x[:, KV_W:]
    if att_ref:
        (att_ref,) = att_ref
        tm = q.shape[0]
        pos = (pl.program_id(0) * tm + _row_iota((tm, LANES))) % t_len
        onehot = jnp.where(_lane_iota((tm, LANES)) == pos // SEL_BLOCK, 1.0, 0.0).astype(BF16)
        for h in range(N_KV):
            att_ref[:, ATT_KSA + 2 * h * HEAD_DIM:ATT_KSA + (2 * h + 1) * HEAD_DIM] = (
                so_ref[:, h * HEAD_DIM:(h + 1) * HEAD_DIM].astype(BF16))
            att_ref[:, ATT_KSA + (2 * h + 1) * HEAD_DIM:ATT_KSA + (2 * h + 2) * HEAD_DIM] = onehot
        att_ref[:, ATT_VS:ATT_KW] = so_ref[:, KV_W:].astype(BF16)
        att_ref[:, ATT_KW:ATT_VW] = wo_ref[:, :KV_W].astype(BF16)
        att_ref[:, ATT_VW:ATT_COLS] = wo_ref[:, KV_W:].astype(BF16)


def nsa_prep(proj, q_norm, ks_norm, kw_norm, tm, t_len, attention_operands):
    m = proj.shape[0]
    kvw = 2 * KV_W
    vec = pl.BlockSpec((1, HEAD_DIM), lambda i: (0, 0))
    widths = [NSA_W, kvw, kvw, kvw] + ([ATT_COLS] if attention_operands else [])
    dtypes = [F32] * 4 + ([BF16] if attention_operands else [])
    if attention_operands:
        assert t_len // SEL_BLOCK <= LANES
    return pl.pallas_call(
        functools.partial(_nsa_prep_kernel, t_len=t_len),
        grid=(m // tm,),
        in_specs=[pl.BlockSpec((tm, NSA_W), lambda i: (i, COL_QB // NSA_W)),
                  pl.BlockSpec((tm, kvw), lambda i: (i, COL_CMP // kvw)),
                  pl.BlockSpec((tm, kvw), lambda i: (i, COL_SEL // kvw)),
                  pl.BlockSpec((tm, kvw), lambda i: (i, COL_WIN // kvw)),
                  vec, vec, vec],
        out_specs=[pl.BlockSpec((tm, wd), lambda i: (i, 0)) for wd in widths],
        out_shape=[jax.ShapeDtypeStruct((m, wd), dt) for wd, dt in zip(widths, dtypes)],
        compiler_params=_cparams(("parallel",)),
        name="nsa_prep",
    )(proj, proj, proj, proj, q_norm.reshape(1, HEAD_DIM), ks_norm.reshape(1, HEAD_DIM), kw_norm.reshape(1, HEAD_DIM))


def _compress_kernel(x_ref, tail_ref, w1_ref, b1_ref, w2_ref, gain_ref, o_ref, x2_ref, a1_ref, *, n_half):
    for j in range(CMP_STRIDE):
        x2_ref[0:n_half, j * HEAD_DIM:(j + 1) * HEAD_DIM] = x_ref[pl.ds(j, n_half, stride=CMP_STRIDE), :]
        x2_ref[n_half:n_half + SUBLANES, j * HEAD_DIM:(j + 1) * HEAD_DIM] = jnp.broadcast_to(
            tail_ref[pl.ds(j, 1), :], (SUBLANES, HEAD_DIM))
    a = jnp.dot(x2_ref[...].astype(BF16), w1_ref[0], preferred_element_type=F32)
    a1_ref[...] = a[:, CMP_HID:]
    pre = a[0:n_half, :CMP_HID] + a1_ref[pl.ds(1, n_half), :] + b1_ref[0]
    out = jnp.dot(_silu(pre).astype(BF16), w2_ref[0], preferred_element_type=F32)
    is_k = pl.program_id(1) < N_KV
    o_ref[0, 0] = jnp.where(is_k, _rms(out, gain_ref[...]), out).astype(BF16)


def compress(raw, tail, w1, b1, w2, kc_norm, n_batch, n_half):
    t_len = n_half * CMP_STRIDE
    wsel = lambda b, c: (c // N_KV, 0, 0)
    return pl.pallas_call(
        functools.partial(_compress_kernel, n_half=n_half),
        grid=(n_batch, 2 * N_KV),
        in_specs=[pl.BlockSpec((t_len, HEAD_DIM), lambda b, c: (b, c)),
                  pl.BlockSpec((CMP_STRIDE, HEAD_DIM), lambda b, c: (b, c)),
                  pl.BlockSpec((1, CMP_STRIDE * HEAD_DIM, 2 * CMP_HID), wsel),
                  pl.BlockSpec((1, 1, CMP_HID), wsel),
                  pl.BlockSpec((1, CMP_HID, HEAD_DIM), wsel),
                  pl.BlockSpec((1, HEAD_DIM), lambda b, c: (0, 0))],
        out_specs=pl.BlockSpec((1, 1, n_half, HEAD_DIM), lambda b, c: (b, c, 0, 0)),
        out_shape=jax.ShapeDtypeStruct((n_batch, 2 * N_KV, n_half, HEAD_DIM), BF16),
        scratch_shapes=[pltpu.VMEM((n_half + SUBLANES, CMP_STRIDE * HEAD_DIM), F32),
                        pltpu.VMEM((n_half + SUBLANES, CMP_HID), F32)],
        compiler_params=_cparams(("parallel", "parallel")),
        name="compress",
    )(raw, tail, w1, b1, w2, kc_norm.reshape(1, HEAD_DIM))


TQ = 128
TK = 256
STRIP_W = TQ + 2 * TK
STRIP_ORIGIN = STRIP_W - TK
MASK_OFF = 1 << 20
BAND_ORIGIN = 64


UNSELECTED = -2.0 ** 30


def _nsa_prompt_kernel(tab_ref, q_ref, kc_ref, vc_ref, ksa_ref, vs_ref, kw_ref, vw_ref, gate_ref, o_ref,
                       strip_ref, band_ref, s_ref, wide_ref, acc_ref, *, n_half, n_blocks):
    kv = pl.program_id(1)
    n = pl.program_id(2)
    head0 = kv * GROUP
    q0 = n * TQ
    jd = n // 2
    odd = n - 2 * jd
    rows = GROUP * TQ
    far = [tab_ref[(NUM_BUCKETS - 1) * N_HEADS_B + head0 + g] for g in range(GROUP)]

    @pl.when(n == 0)
    def _():
        a = _row_iota((TQ, STRIP_W))
        u = _lane_iota((TQ, STRIP_W))
        for g in range(GROUP):
            strip_ref[g] = _bias(a + STRIP_ORIGIN - u, tab_ref, head0 + g) - far[g]
        rel = _row_iota((TQ, n_half)) - ((_lane_iota((TQ, n_half)) - BAND_ORIGIN) * CMP_STRIDE + CMP_BLOCK - 1)
        for g in range(GROUP):
            band_ref[g] = jnp.where(rel >= 0, _bias(rel, tab_ref, head0 + g) - far[g], 0.0)

    q = _stack_heads(q_ref[...])
    qs = q.astype(BF16)
    a_k = _row_iota((TQ, TK))
    c_k = _lane_iota((TQ, TK))

    def tile_start(jj):
        return pl.multiple_of(jnp.maximum(jj, 0) * TK, TK)

    def near_bias(d):
        u0 = pl.multiple_of(STRIP_ORIGIN - d * TK - odd * TQ, TQ)
        return jnp.concatenate([strip_ref[g, :, pl.ds(u0, TK)] for g in range(GROUP)], axis=0)

    def distance(d):
        return _tile_heads(odd * TQ + d * TK + a_k - c_k - jnp.where(jd - d >= 0, 0, MASK_OFF))

    def finish(n_tiles, v_ref):
        m = jnp.max(wide_ref[...], axis=-1, keepdims=True)
        m = jnp.where(m == -jnp.inf, 0.0, m)
        wide_ref[...] = jnp.zeros(wide_ref.shape, F32)
        acc_ref[...] = jnp.zeros(acc_ref.shape, F32)

        def one(jj):
            p = jnp.exp(s_ref[jj] - m)
            wide_ref[...] += p
            acc_ref[...] += jnp.dot(p.astype(BF16), v_ref[pl.ds(tile_start(jj), TK), :], preferred_element_type=F32)

        def pair(t, carry):
            one(2 * t)
            one(2 * t + 1)
            return carry

        lax.fori_loop(0, n_tiles // 2, pair, 0)

        @pl.when(n_tiles % 2 == 1)
        def _():
            one(n_tiles - 1)

        return acc_ref[...] / jnp.maximum(jnp.sum(wide_ref[...], axis=-1, keepdims=True), 1e-30)

    a_c = _row_iota((TQ, n_half))
    rel_c = q0 + a_c - (_lane_iota((TQ, n_half)) * CMP_STRIDE + CMP_BLOCK - 1)
    shift = (n * (TQ // CMP_STRIDE) - BAND_ORIGIN) % n_half
    bias_c = jnp.concatenate([pltpu.roll(band_ref[g], shift, 1) for g in range(GROUP)], axis=0)
    s = lax.dot_general(qs, kc_ref[0, 0], _NT, preferred_element_type=F32)
    p = _masked_softmax(s + bias_c, _tile_heads(rel_c) >= 0)
    o_cmp = jnp.dot(p.astype(BF16), vc_ref[0, 0], preferred_element_type=F32)
    imp = _block_importance(p, n_half, n_blocks, LANES)

    for d in range(3):
        k = kw_ref[pl.ds(tile_start(jd - d), TK), :]
        s = lax.dot_general(qs, k, _NT, preferred_element_type=F32)
        if d < 2:
            s = s + near_bias(d)
        rel = distance(d)
        s = jnp.where((rel >= 0) & (rel < WINDOW), s, -jnp.inf)
        s_ref[d] = s
        wide_ref[...] = s if d == 0 else jnp.maximum(wide_ref[...], s)
    m = jnp.max(wide_ref[...], axis=-1, keepdims=True)
    m = jnp.where(m == -jnp.inf, 0.0, m)
    num = jnp.zeros((rows, HEAD_DIM), F32)
    den = jnp.zeros((rows, TK), F32)
    for d in range(3):
        e = jnp.exp(s_ref[d] - m)
        den = den + e
        num = num + jnp.dot(e.astype(BF16), vw_ref[pl.ds(tile_start(jd - d), TK), :], preferred_element_type=F32)
    o_win = num / jnp.maximum(jnp.sum(den, axis=-1, keepdims=True), 1e-30)

    cur_t = (q0 + _lane_iota((LANES, TQ))) // SEL_BLOCK
    unsel = ((_select_blocks_t(imp.T, cur_t) - 1.0) * -UNSELECTED).T
    qa = jnp.concatenate([q, _tile_heads(unsel)], axis=1).astype(BF16)

    def scores(jj):
        return lax.dot_general(qa, ksa_ref[pl.ds(tile_start(jj), TK), :], _NT, preferred_element_type=F32)

    s = jnp.where(distance(0) >= 0, scores(jd) + near_bias(0), -jnp.inf)
    s_ref[jd] = s
    wide_ref[...] = s

    @pl.when(jd >= 1)
    def _():
        s = scores(jd - 1) + near_bias(1)
        s_ref[jd - 1] = s
        wide_ref[...] = jnp.maximum(wide_ref[...], s)

    n_far = jnp.maximum(jd - 1, 0)

    def far_pair(t, carry):
        for jj in (2 * t, jnp.minimum(2 * t + 1, n_far - 1)):
            s = scores(jj)
            s_ref[jj] = s
            wide_ref[...] = jnp.maximum(wide_ref[...], s)
        return carry

    lax.fori_loop(0, (n_far + 1) // 2, far_pair, 0)
    o_sel = finish(jd + 1, vs_ref)

    o_ref[...] = _gate_mix(gate_ref[...], (o_cmp, o_sel, o_win)).astype(o_ref.dtype)


def nsa_prompt_attention(tab, qn, cmp_tok, att, proj, n_batch, t_len):
    m = qn.shape[0]
    n_q = t_len // TQ
    n_half = cmp_tok.shape[2]
    gw = GROUP * HEAD_DIM
    rows = GROUP * TQ
    seq = lambda col, width: pl.BlockSpec((t_len, width), lambda b, kv, n: (b, col // width + kv))
    tok = lambda off: pl.BlockSpec((1, 1, n_half, HEAD_DIM), lambda b, kv, n: (b, off + kv, 0, 0))
    return pl.pallas_call(
        functools.partial(_nsa_prompt_kernel, n_half=n_half, n_blocks=t_len // SEL_BLOCK),
        grid=(n_batch, N_KV, n_q),
        in_specs=[pl.BlockSpec(memory_space=pltpu.SMEM),
                  pl.BlockSpec((TQ, gw), lambda b, kv, n: (b * n_q + n, kv)),
                  tok(0), tok(N_KV), seq(ATT_KSA, 2 * HEAD_DIM), seq(ATT_VS, HEAD_DIM), seq(ATT_KW, HEAD_DIM),
                  seq(ATT_VW, HEAD_DIM),
                  pl.BlockSpec((TQ, LANES), lambda b, kv, n: (b * n_q + n, COL_GATE // LANES + kv))],
        out_specs=pl.BlockSpec((TQ, gw), lambda b, kv, n: (b * n_q + n, kv)),
        out_shape=jax.ShapeDtypeStruct((m, NSA_W), BF16),
        scratch_shapes=[pltpu.VMEM((GROUP, TQ, STRIP_W), F32), pltpu.VMEM((GROUP, TQ, n_half), F32),
                        pltpu.VMEM((t_len // TK, rows, TK), F32),
                        pltpu.VMEM((rows, TK), F32), pltpu.VMEM((rows, HEAD_DIM), F32)],
        compiler_params=_cparams(("parallel", "parallel", "arbitrary")),
        name="nsa_prompt",
    )(tab, qn, cmp_tok, cmp_tok, att, att, att, att, proj)


PAGES_PER_STEP = 16
TOKEN_ROWS = 2 * N_KV
PAGE_ROWS = PAGE * TOKEN_ROWS


def _lane_queries(q, t_new):
    per_kv = GROUP * t_new
    stacked = jnp.concatenate([q[:, h * HEAD_DIM:(h + 1) * HEAD_DIM] for h in range(N_HEADS_B)], axis=0)
    out = []
    for kv in range(N_KV):
        parts = []
        if kv:
            parts.append(jnp.zeros((kv * per_kv, HEAD_DIM), F32))
        parts.append(stacked[kv * per_kv:(kv + 1) * per_kv])
        parts.append(jnp.zeros((LANES - (kv + 1) * per_kv, HEAD_DIM), F32))
        out.append(jnp.concatenate(parts, axis=0).astype(BF16))
    return out


def _lane_scores(keys, qz):
    s = lax.dot_general(keys[0], qz[0], _NT, preferred_element_type=F32)
    for kv in range(1, N_KV):
        s = s + lax.dot_general(keys[kv], qz[kv], _NT, preferred_element_type=F32)
    return s


def _lane_values(vals, p, per_kv):
    pb = p.astype(BF16)
    lane = _lane_iota((HEAD_DIM, LANES))
    out = lax.dot_general(vals[N_KV - 1], pb, _TN, preferred_element_type=F32)
    for kv in range(N_KV - 2, -1, -1):
        out = jnp.where(lane < (kv + 1) * per_kv, lax.dot_general(vals[kv], pb, _TN, preferred_element_type=F32), out)
    return out


def _lane_bias(rel, tabl_ref):
    val = jnp.broadcast_to(tabl_ref[_BUCKET0:_BUCKET0 + 1, :], rel.shape)
    for edge, bucket in _BUCKET_EDGES:
        val = jnp.where(rel >= edge, tabl_ref[bucket:bucket + 1, :], val)
    return val


def _pad_rows(x, rows):
    return jnp.concatenate([x, jnp.zeros((rows - x.shape[0], x.shape[1]), x.dtype)], axis=0)


PAGE_HALVES = PAGE // CMP_STRIDE
SLAB_ROWS = CMP_STRIDE * TOKEN_ROWS
SLAB_PITCH = SLAB_ROWS + SUBLANES
STAGE_ROWS = PAGES_PER_STEP * PAGE_HALVES * SLAB_PITCH


def _stage_copies(pt_ref, cache_ref, stage_ref, sem_ref, step, slot, n_chunks):
    b = step // n_chunks
    c = step - b * n_chunks
    copies = []
    for p in range(PAGES_PER_STEP):
        page = pt_ref[b, c * PAGES_PER_STEP + p]
        for n in range(PAGE_HALVES):
            dst = pl.multiple_of(slot * STAGE_ROWS + (p * PAGE_HALVES + n) * SLAB_PITCH, SUBLANES)
            copies.append(pltpu.make_async_copy(cache_ref.at[page, pl.ds(n * SLAB_ROWS, SLAB_ROWS), :],
                                                stage_ref.at[pl.ds(dst, SLAB_ROWS), :], sem_ref.at[slot]))
    return copies


def _sample_cmp_kernel(pt_ref, cache_ref, new_ref, q_ref, w1_ref, b1_ref, w2_ref, gain_ref, tabl_ref,
                       sel_ref, ocmp_ref, x2_ref, a_ref, stage_ref, sem_ref, *, past, t_new, n_blocks, sel_rows):
    c = pl.program_id(1)
    n_chunks = pl.num_programs(1)
    step = pl.program_id(0) * n_chunks + c
    slot = step % 2
    n_half = past // CMP_STRIDE
    step_halves = PAGES_PER_STEP * PAGE_HALVES
    per_kv = GROUP * t_new

    @pl.when(step == 0)
    def _():
        for cp in _stage_copies(pt_ref, cache_ref, stage_ref, sem_ref, step, slot, n_chunks):
            cp.start()

    @pl.when(step + 1 < pl.num_programs(0) * n_chunks)
    def _():
        for cp in _stage_copies(pt_ref, cache_ref, stage_ref, sem_ref, step + 1, 1 - slot, n_chunks):
            cp.start()

    for cp in _stage_copies(pt_ref, cache_ref, stage_ref, sem_ref, step, slot, n_chunks):
        cp.wait()

    base = slot * STAGE_ROWS
    for combo in range(TOKEN_ROWS):
        for p in range(PAGES_PER_STEP):
            for j in range(CMP_STRIDE):
                start = base + p * PAGE_HALVES * SLAB_PITCH + j * TOKEN_ROWS + combo
                x2_ref[combo, p * PAGE_HALVES:(p + 1) * PAGE_HALVES, j * HEAD_DIM:(j + 1) * HEAD_DIM] = (
                    stage_ref[pl.ds(start, PAGE_HALVES, stride=SLAB_PITCH), :])
        a = jnp.dot(x2_ref[combo].astype(BF16), w1_ref[combo // N_KV], preferred_element_type=F32)
        a_ref[combo, pl.ds(pl.multiple_of(c * step_halves, step_halves), step_halves), :] = a

    @pl.when(c == pl.num_programs(1) - 1)
    def _():
        tok = []
        for combo in range(TOKEN_ROWS):
            kind = combo // N_KV
            sl = slice(combo * HEAD_DIM, (combo + 1) * HEAD_DIM)
            row = jnp.concatenate([new_ref[j:j + 1, sl] for j in range(t_new)]
                                  + [jnp.zeros((1, (CMP_STRIDE - t_new) * HEAD_DIM), F32)], axis=1)
            tail = jnp.broadcast_to(row, (SUBLANES, CMP_STRIDE * HEAD_DIM)).astype(BF16)
            a_ref[combo, n_half:n_half + SUBLANES, :] = jnp.dot(tail, w1_ref[kind], preferred_element_type=F32)
            pre = a_ref[combo, 0:n_half, 0:CMP_HID] + a_ref[combo, pl.ds(1, n_half), CMP_HID:] + b1_ref[kind]
            out = jnp.dot(_silu(pre).astype(BF16), w2_ref[kind], preferred_element_type=F32)
            if kind == 0:
                out = _rms(out, gain_ref[...])
            tok.append(out.astype(BF16))

        qz = _lane_queries(q_ref[...], t_new)
        shape = (n_half, LANES)
        u = _lane_iota(shape) & (t_new - 1)
        rel = past + u - (_row_iota(shape) * CMP_STRIDE + CMP_BLOCK - 1)
        s = jnp.where(rel >= 0, _lane_scores(tok[:N_KV], qz) + _lane_bias(rel, tabl_ref), -jnp.inf)
        m = jnp.max(s, axis=0, keepdims=True)
        e = jnp.exp(s - jnp.where(m == -jnp.inf, 0.0, m))
        p = e / jnp.maximum(jnp.sum(e, axis=0, keepdims=True), 1e-30)
        ocmp_ref[...] = _lane_values(tok[N_KV:], p, per_kv)

        src = _row_iota((LANES, LANES))
        dst = _lane_iota((LANES, LANES))
        pooled = (src // per_kv) * t_new + (src & (t_new - 1))
        pool = jnp.where((src < N_KV * per_kv) & (dst == pooled), 1.0, 0.0).astype(BF16)
        hi = p.astype(BF16)
        lo = (p - hi.astype(F32)).astype(BF16)
        imp = jnp.dot(hi, pool, preferred_element_type=F32) + jnp.dot(lo, pool, preferred_element_type=F32)
        si = _row_iota((sel_rows, n_half)) * SEL_BLOCK
        ci = _lane_iota((sel_rows, n_half)) * CMP_STRIDE
        overlap = (ci < si + SEL_BLOCK) & (ci + CMP_BLOCK > si) & (si < n_blocks * SEL_BLOCK)
        ov = jnp.where(overlap, 1.0, 0.0).astype(BF16)
        hi = imp.astype(BF16)
        lo = (imp - hi.astype(F32)).astype(BF16)
        imp = jnp.dot(ov, hi, preferred_element_type=F32) + jnp.dot(ov, lo, preferred_element_type=F32)

        cur = (past + (_lane_iota((sel_rows, LANES)) & (t_new - 1))) // SEL_BLOCK
        sel = _select_blocks_t(imp, cur)
        unpool = jnp.where((dst < N_KV * per_kv) & (src == (dst // per_kv) * t_new + (dst & (t_new - 1))), 1.0, 0.0)
        sel_ref[...] = jnp.dot(sel.astype(BF16), unpool.astype(BF16), preferred_element_type=F32)


def _page_specs(n_chunk_pages):
    def spec(j):
        return pl.BlockSpec((None, PAGE_ROWS, HEAD_DIM), lambda b, c, pt: (pt[b, c * n_chunk_pages + j], 0, 0))
    return [spec(j) for j in range(n_chunk_pages)]


def sample_compressed(page_table, cache3, cmp_new, qn, w1, b1, w2, kc_norm, tabl, t_new):
    n_seq, n_pages = page_table.shape
    past = n_pages * PAGE
    n_half = past // CMP_STRIDE
    n_blocks = -(-(past + t_new) // SEL_BLOCK)
    sel_rows = -(-n_blocks // LANES) * LANES
    const = lambda *shape: pl.BlockSpec(shape, lambda b, c, pt: (0,) * len(shape))
    row = lambda width: pl.BlockSpec((t_new, width), lambda b, c, pt: (b, 0))
    out = lambda rows: pl.BlockSpec((None, rows, LANES), lambda b, c, pt: (b, 0, 0))
    grid_spec = pltpu.PrefetchScalarGridSpec(
        num_scalar_prefetch=1,
        grid=(n_seq, n_pages // PAGES_PER_STEP),
        in_specs=[pl.BlockSpec(memory_space=pl.ANY),
                  row(2 * KV_W), row(NSA_W), const(2, CMP_STRIDE * HEAD_DIM, 2 * CMP_HID), const(2, 1, CMP_HID),
                  const(2, CMP_HID, HEAD_DIM), const(1, HEAD_DIM), const(NUM_BUCKETS, LANES)],
        out_specs=[out(sel_rows), out(HEAD_DIM)],
        scratch_shapes=[pltpu.VMEM((TOKEN_ROWS, PAGES_PER_STEP * PAGE_HALVES, CMP_STRIDE * HEAD_DIM), F32),
                        pltpu.VMEM((TOKEN_ROWS, n_half + SUBLANES, 2 * CMP_HID), F32),
                        pltpu.VMEM((2 * STAGE_ROWS, HEAD_DIM), F32),
                        pltpu.SemaphoreType.DMA((2,))])
    return pl.pallas_call(
        functools.partial(_sample_cmp_kernel, past=past, t_new=t_new, n_blocks=n_blocks, sel_rows=sel_rows),
        grid_spec=grid_spec,
        out_shape=[jax.ShapeDtypeStruct((n_seq, sel_rows, LANES), F32),
                   jax.ShapeDtypeStruct((n_seq, HEAD_DIM, LANES), F32)],
        compiler_params=_cparams(("arbitrary", "arbitrary")),
        name="sample_cmp",
    )(page_table, cache3, cmp_new, qn, w1, b1, w2, kc_norm.reshape(1, HEAD_DIM), tabl)


def _sample_sel_kernel(pt_ref, *refs, past, t_new, win_buf):
    pages = refs[:PAGES_PER_STEP]
    (new_ref, q_ref, sel_ref, ocmp_ref, win_ref, wnew_ref, gate_ref, tabl_ref,
     o_ref, m_ref, l_ref, acc_ref, owin_ref, s_ref) = refs[PAGES_PER_STEP:]
    c = pl.program_id(1)
    per_kv = GROUP * t_new
    qz = _lane_queries(q_ref[...], t_new)
    far = tabl_ref[NUM_BUCKETS - 1:NUM_BUCKETS, :]
    new_rows = 2 * SUBLANES

    def new_kv(ref, kind):
        return [_pad_rows(ref[:, (kind * N_KV + kv) * HEAD_DIM:(kind * N_KV + kv + 1) * HEAD_DIM], new_rows).astype(BF16)
                for kv in range(N_KV)]

    def dist(pos0, rows):
        shape = (rows, LANES)
        return past + (_lane_iota(shape) & (t_new - 1)) - pos0 - _row_iota(shape)

    def update(s, values):
        m_prev = m_ref[...]
        m_new = jnp.maximum(m_prev, jnp.max(s, axis=0, keepdims=True))
        m_safe = jnp.where(m_new == -jnp.inf, 0.0, m_new)
        alpha = jnp.exp(m_prev - m_safe)
        p = jnp.exp(s - m_safe)
        l_ref[...] = alpha * l_ref[...] + jnp.sum(p, axis=0, keepdims=True)
        acc = alpha * acc_ref[...]
        for vals, lo, hi_ in values:
            acc = acc + _lane_values(vals, p[lo:hi_], per_kv)
        acc_ref[...] = acc
        m_ref[...] = m_new

    @pl.when(c == 0)
    def _():
        m_ref[...] = jnp.full(m_ref.shape, -jnp.inf, F32)
        l_ref[...] = jnp.zeros(l_ref.shape, F32)
        acc_ref[...] = jnp.zeros(acc_ref.shape, F32)

        kw = [win_ref[pl.ds(kv, win_buf, stride=TOKEN_ROWS), :].astype(BF16) for kv in range(N_KV)]
        vw = [win_ref[pl.ds(N_KV + kv, win_buf, stride=TOKEN_ROWS), :].astype(BF16) for kv in range(N_KV)]
        pieces = []
        for keys, rel in ((kw, dist(past - win_buf, win_buf)), (new_kv(wnew_ref, 0), dist(past, new_rows))):
            s = _lane_scores(keys, qz) + _lane_bias(rel, tabl_ref)
            pieces.append(jnp.where((rel >= 0) & (rel < WINDOW), s, -jnp.inf))
        m = jnp.maximum(jnp.max(pieces[0], axis=0, keepdims=True), jnp.max(pieces[1], axis=0, keepdims=True))
        m = jnp.where(m == -jnp.inf, 0.0, m)
        e0 = jnp.exp(pieces[0] - m)
        e1 = jnp.exp(pieces[1] - m)
        tot = jnp.sum(e0, axis=0, keepdims=True) + jnp.sum(e1, axis=0, keepdims=True)
        num = _lane_values(vw, e0, per_kv) + _lane_values(new_kv(wnew_ref, 1), e1, per_kv)
        owin_ref[...] = num / jnp.maximum(tot, 1e-30)

        rel = dist(past, new_rows)
        s = _lane_scores(new_kv(new_ref, 0), qz) + _lane_bias(rel, tabl_ref)
        chosen = sel_ref[past // SEL_BLOCK:past // SEL_BLOCK + 1, :] > 0.5
        update(jnp.where(chosen & (rel >= 0), s, -jnp.inf), [(new_kv(new_ref, 1), 0, new_rows)])

    values = []
    for p in range(PAGES_PER_STEP):
        page = c * PAGES_PER_STEP + p
        keys = [pages[p][pl.ds(kv, PAGE, stride=TOKEN_ROWS), :].astype(BF16) for kv in range(N_KV)]
        s = _lane_scores(keys, qz)
        if p == PAGES_PER_STEP - 1:
            s = s + _lane_bias(dist(page * PAGE, PAGE), tabl_ref)
        else:
            s = s + far
        per_block = PAGE // SEL_BLOCK
        chosen = jnp.concatenate(
            [jnp.broadcast_to(sel_ref[pl.ds(page * per_block + i, 1), :], (SEL_BLOCK, LANES)) for i in range(per_block)],
            axis=0)
        s_ref[p * PAGE:(p + 1) * PAGE, :] = jnp.where(chosen > 0.5, s, -jnp.inf)
        vals = [pages[p][pl.ds(N_KV + kv, PAGE, stride=TOKEN_ROWS), :].astype(BF16) for kv in range(N_KV)]
        values.append((vals, p * PAGE, (p + 1) * PAGE))
    update(s_ref[...], values)

    @pl.when(c == pl.num_programs(1) - 1)
    def _():
        gt = _sigmoid(gate_ref[...])
        osel = acc_ref[...] / jnp.maximum(l_ref[...], 1e-30)
        o_ref[...] = gt[0:1] * ocmp_ref[...] + gt[1:2] * osel + gt[2:3] * owin_ref[...]


def sample_selected(page_table, cache3, sel_new, qn, sel_t, ocmp_t, win3, win_new, gate_l, tabl, t_new):
    n_seq, n_pages = page_table.shape
    past = n_pages * PAGE
    win_buf = win3.shape[1] // TOKEN_ROWS
    assert (past - FAR_DIST) // PAGE >= n_pages - 1, "only the last page may hold near keys"
    sel_rows = sel_t.shape[1]
    const = lambda *shape: pl.BlockSpec(shape, lambda b, c, pt: (0,) * len(shape))
    row = lambda width: pl.BlockSpec((t_new, width), lambda b, c, pt: (b, 0))
    per_seq = lambda rows, width: pl.BlockSpec((None, rows, width), lambda b, c, pt: (b, 0, 0))
    grid_spec = pltpu.PrefetchScalarGridSpec(
        num_scalar_prefetch=1,
        grid=(n_seq, n_pages // PAGES_PER_STEP),
        in_specs=_page_specs(PAGES_PER_STEP) + [
            row(2 * KV_W), row(NSA_W), per_seq(sel_rows, LANES), per_seq(HEAD_DIM, LANES),
            per_seq(win_buf * TOKEN_ROWS, HEAD_DIM), row(2 * KV_W), per_seq(SUBLANES, LANES),
            const(NUM_BUCKETS, LANES)],
        out_specs=per_seq(HEAD_DIM, LANES),
        scratch_shapes=[pltpu.VMEM((1, LANES), F32), pltpu.VMEM((1, LANES), F32), pltpu.VMEM((HEAD_DIM, LANES), F32),
                        pltpu.VMEM((HEAD_DIM, LANES), F32), pltpu.VMEM((PAGES_PER_STEP * PAGE, LANES), F32)])
    return pl.pallas_call(
        functools.partial(_sample_sel_kernel, past=past, t_new=t_new, win_buf=win_buf),
        grid_spec=grid_spec,
        out_shape=jax.ShapeDtypeStruct((n_seq, HEAD_DIM, LANES), F32),
        compiler_params=_cparams(("parallel", "arbitrary")),
        name="sample_sel",
    )(page_table, *([cache3] * PAGES_PER_STEP), sel_new, qn, sel_t, ocmp_t, win3, win_new, gate_l, tabl)


def _permute_in_proj(w_in):
    d = w_in.shape[0]
    a_end = 4 * HGRN_W
    b_end = a_end + NSA_W + 6 * KV_W
    g_end = b_end + N_GATE
    per_kv = GROUP * 3
    zeros = lambda n: jnp.zeros((d, n), w_in.dtype)
    gates = []
    for kv in range(N_KV):
        gates += [w_in[:, b_end + kv * per_kv:b_end + (kv + 1) * per_kv], zeros(LANES - per_kv)]
    parts = [w_in[:, :a_end], w_in[:, g_end:g_end + 2 * D_MODEL], w_in[:, a_end:b_end]] + gates
    parts.append(zeros(PROJ_TN - N_KV * LANES))
    return jnp.concatenate(parts, axis=1).astype(BF16)


def _compress_weights(k_w1, k_b1, k_w2, v_w1, v_b1, v_w2):
    half = CMP_STRIDE * HEAD_DIM
    cat = lambda w: jnp.concatenate([w[:half], w[half:]], axis=1)
    w1 = jnp.stack([cat(k_w1), cat(v_w1)]).astype(BF16)
    b1 = jnp.stack([k_b1, v_b1]).reshape(2, 1, CMP_HID)
    w2 = jnp.stack([k_w2, v_w2]).astype(BF16)
    return w1, b1, w2


def _lane_table(rel_bias, t_new):
    tab = jnp.repeat(rel_bias, t_new, axis=1)
    return jnp.pad(tab, ((0, 0), (0, LANES - tab.shape[1])))


def _gate_lanes(proj, n_seq, t_new):
    per_kv = GROUP * 3
    g = jnp.stack([proj[:, COL_GATE + kv * LANES:COL_GATE + kv * LANES + per_kv] for kv in range(N_KV)], axis=1)
    g = g.reshape(n_seq, t_new, N_KV, GROUP, 3).transpose(0, 4, 2, 3, 1).reshape(n_seq, 3, N_HEADS_B * t_new)
    return jnp.pad(g, ((0, 0), (0, SUBLANES - 3), (0, LANES - N_HEADS_B * t_new)))


def _tiles(m, t_len):
    big = m >= 1024
    return dict(proj_tm=1024 if big else m, prep_tm=512 if big else m, post_tm=256, mlp_tm=512 if big else m,
                mlp_tf=1024, hgrn_chunk=min(t_len, 256), hgrn_heads=2 if big else N_HEADS_A)


def _trunk(x, mods, s0, w, nsa_fn, attention_operands):
    nb, t_len, d = x.shape
    m = nb * t_len
    tl = _tiles(m, t_len)
    sh1, sc1, g1, sh2, sc2, g2 = mods
    x2 = x.reshape(m, d)
    proj = in_proj(x2, w["norm1"], sc1, sh1, w["w_in"], t_len, tl["proj_tm"])
    o_a, s_fin = hgrn(proj, w["lb"], w["hgrn_onorm"], s0, nb, t_len, tl["hgrn_chunk"], tl["hgrn_heads"])
    qn, cmp_new, sel_new, win_new, *att = nsa_prep(proj, w["q_norm"], w["ks_norm"], w["kw_norm"], tl["prep_tm"], t_len,
                                                   attention_operands)
    o_b = nsa_fn(proj, qn, cmp_new, sel_new, win_new, *att)
    x1, h2 = post_attn(x2, o_a, o_b, proj, w["w_ba"], w["w_bb"], w["w_out"], g1, w["norm2"], sc2, sh2, t_len,
                       tl["post_tm"])
    y = mlp(h2, x1, w["mlp_w1"], w["mlp_w2"], g2, t_len, tl["mlp_tm"], tl["mlp_tf"])
    return y.reshape(nb, t_len, d), s_fin, (cmp_new, sel_new, win_new)


def kernel(x_prompt, x_sample, c_prompt, c_sample, cache_cmp_kv, cache_sel_kv, cache_win_kv, state_hgrn, page_table, hgrn_lb_logits, rel_bias, ada_w, ada_b, norm1, norm2, w_in, hgrn_onorm, nsa_q_norm, nsa_kc_norm, nsa_ks_norm, nsa_kw_norm, cmp_k_w1, cmp_k_b1, cmp_k_w2, cmp_v_w1, cmp_v_b1, cmp_v_w2, w_branch_a, w_branch_b, w_out, mlp_w1, mlp_w2):
    n_p, t_p, d = x_prompt.shape
    n_s, t_s, _ = x_sample.shape
    past = page_table.shape[1] * PAGE
    win_buf = cache_win_kv.shape[2]
    kvw = 2 * KV_W
    layer = 0

    lb_all = jnp.cumsum(jax.nn.softmax(hgrn_lb_logits.astype(F32), axis=0), axis=0)
    cw1, cb1, cw2 = _compress_weights(cmp_k_w1[layer], cmp_k_b1[layer], cmp_k_w2[layer],
                                      cmp_v_w1[layer], cmp_v_b1[layer], cmp_v_w2[layer])
    w = dict(norm1=norm1[layer], norm2=norm2[layer], w_in=_permute_in_proj(w_in[layer]),
             hgrn_onorm=hgrn_onorm[layer], lb=lb_all[layer], q_norm=nsa_q_norm[layer],
             ks_norm=nsa_ks_norm[layer], kw_norm=nsa_kw_norm[layer],
             w_ba=w_branch_a[layer].astype(BF16), w_bb=w_branch_b[layer].astype(BF16),
             w_out=w_out[layer].astype(BF16), mlp_w1=mlp_w1[layer].astype(BF16), mlp_w2=mlp_w2[layer].astype(BF16))
    tab = rel_bias.astype(F32).reshape(NUM_BUCKETS * N_HEADS_B)

    mods = ada_mods(jnp.concatenate([c_prompt, c_sample], axis=0).astype(F32), ada_w[layer], ada_b[layer])
    mods = jnp.split(mods, 6, axis=-1)
    mods_p = [a[:n_p].reshape(n_p, 1, d) for a in mods]
    mods_s = [jnp.repeat(a[n_p:], t_s, axis=0).reshape(1, n_s * t_s, d) for a in mods]

    def nsa_p(proj, qn, cmp_new, sel_new, win_new, att):
        tail = jnp.zeros((n_p * CMP_STRIDE, kvw), F32)
        cmp_tok = compress(cmp_new, tail, cw1, cb1, cw2, nsa_kc_norm[layer], n_p, t_p // CMP_STRIDE)
        return nsa_prompt_attention(tab, qn, cmp_tok, att, proj, n_p, t_p)

    s0_p = jnp.zeros((n_p, N_HEADS_A, HEAD_DIM, HEAD_DIM), F32)
    y_p, hg_p, (cmp_p, sel_p, win_p) = _trunk(x_prompt, mods_p, s0_p, w, nsa_p, True)

    def nsa_s(proj, qn, cmp_new, sel_new, win_new):
        n_phys = cache_cmp_kv.shape[1]
        cmp3 = cache_cmp_kv[layer].reshape(n_phys, PAGE_ROWS, HEAD_DIM)
        sel3 = cache_sel_kv[layer].reshape(n_phys, PAGE_ROWS, HEAD_DIM)
        win3 = cache_win_kv[layer].reshape(n_s, win_buf * TOKEN_ROWS, HEAD_DIM)
        tabl = _lane_table(rel_bias.astype(F32), t_s)
        sel_t, ocmp_t = sample_compressed(page_table, cmp3, cmp_new, qn, cw1, cb1, cw2, nsa_kc_norm[layer], tabl, t_s)
        o_t = sample_selected(page_table, sel3, sel_new, qn, sel_t, ocmp_t, win3, win_new,
                              _gate_lanes(proj, n_s, t_s), tabl, t_s)
        o = o_t[:, :, :N_HEADS_B * t_s].reshape(n_s, HEAD_DIM, N_HEADS_B, t_s)
        return o.transpose(0, 3, 2, 1).reshape(n_s * t_s, NSA_W).astype(BF16)

    y_s, hg_s, (cmp_s, sel_s, win_s) = _trunk(x_sample, mods_s, state_hgrn[layer], w, nsa_s, False)

    dt = x_prompt.dtype
    pages = lambda a: a.reshape(1, n_p, t_p // PAGE, PAGE, 2, N_KV, HEAD_DIM).astype(dt)
    rows_s = lambda a: a.reshape(1, n_s, t_s, 2, N_KV, HEAD_DIM).astype(dt)
    win_keep = min(WINDOW, t_p)
    win_p_out = win_p.reshape(n_p, t_p, kvw)[:, t_p - win_keep:].reshape(1, n_p, win_keep, 2, N_KV, HEAD_DIM)
    win_rows = cache_win_kv[layer].reshape(n_s, win_buf * TOKEN_ROWS, HEAD_DIM).astype(F32)
    win_s_out = jnp.concatenate([win_rows, win_s.reshape(n_s, t_s * TOKEN_ROWS, HEAD_DIM)], axis=1)
    win_s_out = win_s_out[:, t_s * TOKEN_ROWS:].reshape(1, n_s, win_buf, 2, N_KV, HEAD_DIM)
    return (y_p, y_s, pages(cmp_p), rows_s(cmp_s), pages(sel_p), rows_s(sel_s),
            win_p_out.astype(dt), win_s_out.astype(dt), hg_p[None].astype(dt), hg_s[None].astype(dt))
```

```python
import functools
import math

import numpy as np
import jax
import jax.numpy as jnp
from jax import lax
from jax.experimental import pallas as pl
from jax.experimental.pallas import tpu as pltpu

F32 = jnp.float32
BF16 = jnp.bfloat16

D_MODEL = 2048
N_HEADS_A = 8
HEAD_DIM = 128
HGRN_W = N_HEADS_A * HEAD_DIM
N_HEADS_B = 8
N_KV = 2
GROUP = N_HEADS_B // N_KV
NSA_W = N_HEADS_B * HEAD_DIM
KV_W = N_KV * HEAD_DIM
PAGE = 128
CMP_STRIDE = 16
CMP_BLOCK = 32
CMP_HID = 128
SEL_BLOCK = 64
N_SEL = 16
WINDOW = 512
FORCE_SCORE = 1.0e6
NUM_BUCKETS = 32
REL_MAX_DIST = 128
D_FF = 4 * D_MODEL
EPS = 1e-6
N_GATE = 3 * N_HEADS_B

LANES = 128
SUBLANES = 8
VMEM_LIMIT = 56 * 1024 * 1024

COL_HGRN = 0
COL_MGA = 4 * HGRN_W
COL_MGB = COL_MGA + D_MODEL
COL_QB = COL_MGB + D_MODEL
COL_CMP = COL_QB + NSA_W
COL_SEL = COL_CMP + 2 * KV_W
COL_WIN = COL_SEL + 2 * KV_W
COL_GATE = COL_WIN + 2 * KV_W
PROJ_TN = 512
PROJ_COLS = COL_GATE + PROJ_TN


def _cparams(sem):
    return pltpu.CompilerParams(dimension_semantics=sem, vmem_limit_bytes=VMEM_LIMIT)


def _sigmoid(x):
    return 1.0 / (1.0 + jnp.exp(-x))


def _silu(x):
    return x * _sigmoid(x)


def _rms(x, gain):
    return x * lax.rsqrt(jnp.mean(x * x, axis=-1, keepdims=True) + EPS) * gain


def _lane_iota(shape):
    return lax.broadcasted_iota(jnp.int32, shape, 1)


def _row_iota(shape):
    return lax.broadcasted_iota(jnp.int32, shape, 0)


def _ada_kernel(c_ref, w_ref, b_ref, o_ref):
    a = _silu(c_ref[...]).astype(BF16)
    o_ref[...] = jnp.dot(a, w_ref[...].astype(BF16), preferred_element_type=F32) + b_ref[...]


def ada_mods(c, w, b):
    r, d = c.shape
    n = w.shape[1]
    tn = 1024
    return pl.pallas_call(
        _ada_kernel,
        grid=(n // tn,),
        in_specs=[pl.BlockSpec((r, d), lambda j: (0, 0)),
                  pl.BlockSpec((d, tn), lambda j: (0, j)),
                  pl.BlockSpec((1, tn), lambda j: (0, j))],
        out_specs=pl.BlockSpec((r, tn), lambda j: (0, j)),
        out_shape=jax.ShapeDtypeStruct((r, n), F32),
        compiler_params=_cparams(("parallel",)),
        name="ada_mods",
    )(c, w, b.reshape(1, n))


def _inproj_kernel(x_ref, gain_ref, sc_ref, sh_ref, w_ref, o_ref, h_ref):
    @pl.when(pl.program_id(1) == 0)
    def _():
        h = _rms(x_ref[...], gain_ref[...]) * (1.0 + sc_ref[0]) + sh_ref[0]
        h_ref[...] = h.astype(BF16)

    o_ref[...] = jnp.dot(h_ref[...], w_ref[...], preferred_element_type=F32)


def _mod_spec(mod, tm, rows_per_batch):
    d = mod.shape[-1]
    if mod.shape[1] == 1:
        return pl.BlockSpec((1, 1, d), lambda i, *_: ((i * tm) // rows_per_batch, 0, 0))
    return pl.BlockSpec((1, tm, d), lambda i, *_: (0, i, 0))


def in_proj(x2, gain, sc, sh, w, rows_per_batch, tm, tn):
    m, d = x2.shape
    n = w.shape[1]
    return pl.pallas_call(
        _inproj_kernel,
        grid=(m // tm, n // tn),
        in_specs=[pl.BlockSpec((tm, d), lambda i, j: (i, 0)),
                  pl.BlockSpec((1, d), lambda i, j: (0, 0)),
                  _mod_spec(sc, tm, rows_per_batch),
                  _mod_spec(sh, tm, rows_per_batch),
                  pl.BlockSpec((d, tn), lambda i, j: (0, j))],
        out_specs=pl.BlockSpec((tm, tn), lambda i, j: (i, j)),
        out_shape=jax.ShapeDtypeStruct((m, n), F32),
        scratch_shapes=[pltpu.VMEM((tm, d), BF16)],
        compiler_params=_cparams(("parallel", "arbitrary")),
        name="in_proj",
    )(x2, gain.reshape(1, d), sc, sh, w)


def _post_kernel(x_ref, oa_ref, ob_ref, mga_ref, mgb_ref, wba_ref, wbb_ref, wout_ref,
                 g1_ref, gain2_ref, sc2_ref, sh2_ref, x1_ref, h2_ref):
    ya = jnp.dot(oa_ref[...], wba_ref[...], preferred_element_type=F32)
    yb = jnp.dot(ob_ref[...], wbb_ref[...], preferred_element_type=F32)
    merged = _sigmoid(mga_ref[...]) * ya + _sigmoid(mgb_ref[...]) * yb
    y = jnp.dot(merged.astype(BF16), wout_ref[...], preferred_element_type=F32)
    x1 = x_ref[...] + g1_ref[0] * y
    x1_ref[...] = x1
    h2_ref[...] = (_rms(x1, gain2_ref[...]) * (1.0 + sc2_ref[0]) + sh2_ref[0]).astype(BF16)


def post_attn(x2, oa, ob, proj, wba, wbb, wout, g1, gain2, sc2, sh2, rows_per_batch, tm):
    m, d = x2.shape
    const = lambda i: (0, 0)
    resident = lambda shape: pl.BlockSpec(shape, const, pipeline_mode=pl.Buffered(1))
    return pl.pallas_call(
        _post_kernel,
        grid=(m // tm,),
        in_specs=[pl.BlockSpec((tm, d), lambda i: (i, 0)),
                  pl.BlockSpec((tm, HGRN_W), lambda i: (i, 0)),
                  pl.BlockSpec((tm, NSA_W), lambda i: (i, 0)),
                  pl.BlockSpec((tm, d), lambda i: (i, COL_MGA // D_MODEL)),
                  pl.BlockSpec((tm, d), lambda i: (i, COL_MGB // D_MODEL)),
                  resident((HGRN_W, d)),
                  resident((NSA_W, d)),
                  resident((d, d)),
                  _mod_spec(g1, tm, rows_per_batch),
                  pl.BlockSpec((1, d), const),
                  _mod_spec(sc2, tm, rows_per_batch),
                  _mod_spec(sh2, tm, rows_per_batch)],
        out_specs=[pl.BlockSpec((tm, d), lambda i: (i, 0)),
                   pl.BlockSpec((tm, d), lambda i: (i, 0))],
        out_shape=[jax.ShapeDtypeStruct((m, d), F32), jax.ShapeDtypeStruct((m, d), BF16)],
        compiler_params=_cparams(("parallel",)),
        name="post_attn",
    )(x2, oa, ob, proj, proj, wba, wbb, wout, g1, gain2.reshape(1, d), sc2, sh2)


def _mlp_kernel(h_ref, x1_ref, w1_ref, w2_ref, g2_ref, y_ref):
    f = pl.program_id(1)
    u = jnp.maximum(jnp.dot(h_ref[...], w1_ref[...], preferred_element_type=F32), 0.0)
    part = jnp.dot((u * u).astype(BF16), w2_ref[...], preferred_element_type=F32)

    @pl.when(f == 0)
    def _():
        y_ref[...] = part

    @pl.when(f > 0)
    def _():
        y_ref[...] += part

    @pl.when(f == pl.num_programs(1) - 1)
    def _():
        y_ref[...] = x1_ref[...] + g2_ref[0] * y_ref[...]


def mlp(h2, x1, w1, w2, g2, rows_per_batch, tm, tf):
    m, d = x1.shape
    ff = w1.shape[1]
    return pl.pallas_call(
        _mlp_kernel,
        grid=(m // tm, ff // tf),
        in_specs=[pl.BlockSpec((tm, d), lambda i, f: (i, 0)),
                  pl.BlockSpec((tm, d), lambda i, f: (i, 0), pipeline_mode=pl.Buffered(1)),
                  pl.BlockSpec((d, tf), lambda i, f: (0, f)),
                  pl.BlockSpec((tf, d), lambda i, f: (f, 0)),
                  _mod_spec(g2, tm, rows_per_batch)],
        out_specs=pl.BlockSpec((tm, d), lambda i, f: (i, 0)),
        out_shape=jax.ShapeDtypeStruct((m, d), F32),
        compiler_params=_cparams(("parallel", "arbitrary")),
        name="mlp",
    )(h2, x1, w1, w2, g2)


_NT = (((1,), (1,)), ((), ()))
_TN = (((0,), (0,)), ((), ()))


def _hgrn_kernel(q_ref, z_ref, v_ref, g_ref, lb_ref, on_ref, s0_ref, o_ref, sfin_ref, st_ref, b_ref, *, chunk, heads):
    for h in range(heads):
        _hgrn_head(h, q_ref, z_ref, v_ref, g_ref, lb_ref, on_ref, s0_ref, o_ref, sfin_ref, st_ref, b_ref, chunk)


def _hgrn_head(h, q_ref, z_ref, v_ref, g_ref, lb_ref, on_ref, s0_ref, o_ref, sfin_ref, st_ref, b_ref, chunk):
    c = pl.program_id(2)
    cols = slice(h * HEAD_DIM, (h + 1) * HEAD_DIM)

    @pl.when(c == 0)
    def _():
        st_ref[h] = s0_ref[0, h].T

    q = q_ref[:, cols]
    z = z_ref[:, cols]
    v = v_ref[:, cols]
    lb = lb_ref[h]
    e = jnp.exp(-jnp.abs(z))
    r = 1.0 / (1.0 + e)
    pos = z >= 0.0
    logf = jnp.log(lb + (1.0 - lb) * jnp.where(pos, r, e * r))
    k = (1.0 - lb) * jnp.where(pos, e * r, r)

    t = lax.broadcasted_iota(jnp.int32, (chunk, HEAD_DIM), 0)
    b = logf
    s = 1
    while s < chunk:
        b = b + jnp.where(t >= s, pltpu.roll(b, s, 0), 0.0)
        s *= 2
    b_ref[h] = b

    t8 = t & (SUBLANES - 1)
    if chunk <= SUBLANES:
        o = jnp.zeros((chunk, HEAD_DIM), F32)
        for d in range(SUBLANES):
            kd, bd, vd = (k, b, v) if d == 0 else (pltpu.roll(k, d, 0), pltpu.roll(b, d, 0), pltpu.roll(v, d, 0))
            w = jnp.exp(jnp.where(t8 >= d, b - bd, -jnp.inf))
            o = o + jnp.sum(q * kd * w, axis=-1, keepdims=True) * vd
    else:
        o = jnp.sum(q * k, axis=-1, keepdims=True) * v
        row = lax.broadcasted_iota(jnp.int32, (chunk, chunk), 0)
        col = lax.broadcasted_iota(jnp.int32, (chunk, chunk), 1)
        apart = row ^ col
        att = jnp.zeros((chunk, chunk), F32)
        sub = _row_iota((SUBLANES, HEAD_DIM))
        m = 1
        while m < chunk:
            def ref_row(r, rows):
                return jnp.broadcast_to(b_ref[h, pl.ds(r, 1), :], (rows, HEAD_DIM))

            if 2 * m >= SUBLANES:
                refs = [ref_row(blk * 2 * m + m - 1, 2 * m) for blk in range(chunk // (2 * m))]
            else:
                refs = []
                for tile in range(chunk // SUBLANES):
                    ref_t = ref_row(tile * SUBLANES + m - 1, SUBLANES)
                    for j in range(1, SUBLANES // (2 * m)):
                        ref_t = jnp.where(sub >= j * 2 * m, ref_row(tile * SUBLANES + j * 2 * m + m - 1, SUBLANES), ref_t)
                    refs.append(ref_t)
            ref_b = refs[0] if len(refs) == 1 else jnp.concatenate(refs, axis=0)
            second = (t & (2 * m - 1)) >= m
            w = jnp.exp(jnp.where(second, b - ref_b, ref_b - b))
            qs = jnp.where(second, q * w, 0.0).astype(BF16)
            ks = jnp.where(second, 0.0, k * w).astype(BF16)
            a_m = lax.dot_general(qs, ks, _NT, preferred_element_type=F32)
            att = att + jnp.where(apart < 2 * m, a_m, 0.0)
            m *= 2
        o = o + jnp.dot(att.astype(BF16), v.astype(BF16), preferred_element_type=F32)

    st = st_ref[h]
    b_last = b_ref[h, pl.ds(chunk - 1, 1), :]
    o = o + lax.dot_general((q * jnp.exp(b)).astype(BF16), st.astype(BF16), _NT, preferred_element_type=F32)
    kt = (k * jnp.exp(b_last - b)).astype(BF16)
    vb = v.astype(BF16)
    if chunk < 2 * SUBLANES:
        pad = jnp.zeros((2 * SUBLANES - chunk, HEAD_DIM), BF16)
        kt = jnp.concatenate([kt, pad], axis=0)
        vb = jnp.concatenate([vb, pad], axis=0)
    st_new = jnp.exp(b_last) * st + lax.dot_general(vb, kt, _TN, preferred_element_type=F32)
    st_ref[h] = st_new

    o_ref[:, cols] = (_rms(o, on_ref[h]) * _silu(g_ref[:, cols])).astype(o_ref.dtype)

    @pl.when(c == pl.num_programs(2) - 1)
    def _():
        sfin_ref[0, h] = st_new.T


def hgrn(proj, lb, onorm, s0, n_batch, t_len, chunk, heads):
    m = proj.shape[0]
    n_c = t_len // chunk
    hb = N_HEADS_A // heads
    width = heads * HEAD_DIM

    def col(group):
        return pl.BlockSpec((chunk, width), lambda bi, h, c: (bi * n_c + c, group * hb + h))

    vec = pl.BlockSpec((heads, 1, HEAD_DIM), lambda bi, h, c: (h, 0, 0))
    state = pl.BlockSpec((1, heads, HEAD_DIM, HEAD_DIM), lambda bi, h, c: (bi, h, 0, 0))
    o_dtype = BF16 if chunk % (2 * SUBLANES) == 0 else F32
    o, s_fin = pl.pallas_call(
        functools.partial(_hgrn_kernel, chunk=chunk, heads=heads),
        grid=(n_batch, hb, n_c),
        in_specs=[col(0), col(1), col(2), col(3), vec, vec, state],
        out_specs=[pl.BlockSpec((chunk, width), lambda bi, h, c: (bi * n_c + c, h)), state],
        out_shape=[jax.ShapeDtypeStruct((m, HGRN_W), o_dtype),
                   jax.ShapeDtypeStruct((n_batch, N_HEADS_A, HEAD_DIM, HEAD_DIM), F32)],
        scratch_shapes=[pltpu.VMEM((heads, HEAD_DIM, HEAD_DIM), F32), pltpu.VMEM((heads, chunk, HEAD_DIM), F32)],
        compiler_params=_cparams(("parallel", "parallel", "arbitrary")),
        name="hgrn",
    )(proj, proj, proj, proj, lb.reshape(N_HEADS_A, 1, HEAD_DIM), onorm.reshape(N_HEADS_A, 1, HEAD_DIM), s0)
    return o.astype(BF16), s_fin


def _bucket_steps():
    n = np.arange(REL_MAX_DIST)
    exact = NUM_BUCKETS // 2
    val = np.log(np.maximum(n, 1) / exact) / math.log(REL_MAX_DIST / exact) * (NUM_BUCKETS - exact)
    frac = np.abs(val - np.round(val))[exact + 1:]
    assert frac.min() > 1e-3, "a bucket edge sits on an integer distance"
    lut = np.where(n < exact, n, np.minimum(exact + np.floor(np.maximum(val, 0.0)).astype(np.int64), NUM_BUCKETS - 1))
    assert lut[-1] == NUM_BUCKETS - 1
    return int(lut[0]), [(int(i), int(lut[i])) for i in range(1, REL_MAX_DIST) if lut[i] != lut[i - 1]]


_BUCKET0, _BUCKET_EDGES = _bucket_steps()
FAR_DIST = _BUCKET_EDGES[-1][0]


def _bias(rel, tab_ref, head):
    val = jnp.full(rel.shape, tab_ref[_BUCKET0 * N_HEADS_B + head], F32)
    for edge, bucket in _BUCKET_EDGES:
        val = jnp.where(rel >= edge, tab_ref[bucket * N_HEADS_B + head], val)
    return val


def _stack_heads(x):
    return jnp.concatenate([x[:, g * HEAD_DIM:(g + 1) * HEAD_DIM] for g in range(GROUP)], axis=0)


def _tile_heads(x):
    return jnp.concatenate([x] * GROUP, axis=0)


def _masked_softmax(s, mask):
    s = jnp.where(mask, s, -jnp.inf)
    m = jnp.max(s, axis=-1, keepdims=True)
    m = jnp.where(m == -jnp.inf, 0.0, m)
    e = jnp.exp(s - m)
    return e / jnp.maximum(jnp.sum(e, axis=-1, keepdims=True), 1e-30)


def _block_importance(p, n_cmp, n_blocks, width):
    rows = p.shape[0] // GROUP
    imp = p[0:rows]
    for g in range(1, GROUP):
        imp = imp + p[g * rows:(g + 1) * rows]
    ci = lax.broadcasted_iota(jnp.int32, (n_cmp, width), 0) * CMP_STRIDE
    si = lax.broadcasted_iota(jnp.int32, (n_cmp, width), 1) * SEL_BLOCK
    overlap = (ci < si + SEL_BLOCK) & (ci + CMP_BLOCK > si) & (si < n_blocks * SEL_BLOCK)
    ov = jnp.where(overlap, 1.0, 0.0).astype(BF16)
    hi = imp.astype(BF16)
    lo = (imp - hi.astype(F32)).astype(BF16)
    return jnp.dot(hi, ov, preferred_element_type=F32) + jnp.dot(lo, ov, preferred_element_type=F32)


def _select_blocks_t(imp_t, cur):
    blk = _row_iota(imp_t.shape)
    forced = (blk == 0) | (blk == cur) | (blk == cur - 1)
    score = jnp.where(blk <= cur, imp_t + jnp.where(forced, FORCE_SCORE, 0.0), -jnp.inf)
    sel = jnp.zeros(imp_t.shape, F32)
    n_rows = imp_t.shape[0]
    for _ in range(N_SEL):
        top = jnp.max(score, axis=0, keepdims=True)
        first = jnp.min(jnp.where(score == top, blk, n_rows), axis=0, keepdims=True)
        pick = blk == first
        sel = jnp.where(pick, 1.0, sel)
        score = jnp.where(pick, -jnp.inf, score)
    return sel


def _gate_mix(gate, parts):
    rows = gate.shape[0]
    gt = _sigmoid(gate)
    outs = []
    for g in range(GROUP):
        o = gt[:, 3 * g:3 * g + 1] * parts[0][g * rows:(g + 1) * rows]
        for j in (1, 2):
            o = o + gt[:, 3 * g + j:3 * g + j + 1] * parts[j][g * rows:(g + 1) * rows]
        outs.append(o)
    return jnp.concatenate(outs, axis=1)


ATT_KSA = 0
ATT_VS = N_KV * 2 * HEAD_DIM
ATT_KW = ATT_VS + KV_W
ATT_VW = ATT_KW + KV_W
ATT_COLS = ATT_VW + KV_W


def _nsa_prep_kernel(q_ref, cmp_ref, sel_ref, win_ref, qn_ref, ksn_ref, kwn_ref, qo_ref, co_ref, so_ref, wo_ref,
                     *att_ref, t_len):
    q = q_ref[...]
    scale = HEAD_DIM ** -0.5
    for h in range(N_HEADS_B):
        sl = slice(h * HEAD_DIM, (h + 1) * HEAD_DIM)
        qo_ref[:, sl] = _rms(q[:, sl], qn_ref[...]) * scale
    co_ref[...] = cmp_ref[...]
    for src, gain, dst in ((sel_ref, ksn_ref, so_ref), (win_ref, kwn_ref, wo_ref)):
        x = src[...]
        for h in range(N_KV):
            sl = slice(h * HEAD_DIM, (h + 1) * HEAD_DIM)
            dst[:, sl] = _rms(x[:, sl], gain[...])
        dst[:, KV_W:] = x[:, KV_W:]
    if att_ref:
        (att_ref,) = att_ref
        tm = q.shape[0]
        pos = (pl.program_id(0) * tm + _row_iota((tm, LANES))) % t_len
        onehot = jnp.where(_lane_iota((tm, LANES)) == pos // SEL_BLOCK, 1.0, 0.0).astype(BF16)
        for h in range(N_KV):
            att_ref[:, ATT_KSA + 2 * h * HEAD_DIM:ATT_KSA + (2 * h + 1) * HEAD_DIM] = (
                so_ref[:, h * HEAD_DIM:(h + 1) * HEAD_DIM].astype(BF16))
            att_ref[:, ATT_KSA + (2 * h + 1) * HEAD_DIM:ATT_KSA + (2 * h + 2) * HEAD_DIM] = onehot
        att_ref[:, ATT_VS:ATT_KW] = so_ref[:, KV_W:].astype(BF16)
        att_ref[:, ATT_KW:ATT_VW] = wo_ref[:, :KV_W].astype(BF16)
        att_ref[:, ATT_VW:ATT_COLS] = wo_ref[:, KV_W:].astype(BF16)


def nsa_prep(proj, q_norm, ks_norm, kw_norm, tm, t_len, attention_operands):
    m = proj.shape[0]
    kvw = 2 * KV_W
    vec = pl.BlockSpec((1, HEAD_DIM), lambda i: (0, 0))
    widths = [NSA_W, kvw, kvw, kvw] + ([ATT_COLS] if attention_operands else [])
    dtypes = [F32] * 4 + ([BF16] if attention_operands else [])
    if attention_operands:
        assert t_len // SEL_BLOCK <= LANES
    return pl.pallas_call(
        functools.partial(_nsa_prep_kernel, t_len=t_len),
        grid=(m // tm,),
        in_specs=[pl.BlockSpec((tm, NSA_W), lambda i: (i, COL_QB // NSA_W)),
                  pl.BlockSpec((tm, kvw), lambda i: (i, COL_CMP // kvw)),
                  pl.BlockSpec((tm, kvw), lambda i: (i, COL_SEL // kvw)),
                  pl.BlockSpec((tm, kvw), lambda i: (i, COL_WIN // kvw)),
                  vec, vec, vec],
        out_specs=[pl.BlockSpec((tm, wd), lambda i: (i, 0)) for wd in widths],
        out_shape=[jax.ShapeDtypeStruct((m, wd), dt) for wd, dt in zip(widths, dtypes)],
        compiler_params=_cparams(("parallel",)),
        name="nsa_prep",
    )(proj, proj, proj, proj, q_norm.reshape(1, HEAD_DIM), ks_norm.reshape(1, HEAD_DIM), kw_norm.reshape(1, HEAD_DIM))


def _compress_kernel(x_ref, tail_ref, w1_ref, b1_ref, w2_ref, gain_ref, o_ref, x2_ref, a1_ref, *, n_half):
    for j in range(CMP_STRIDE):
        x2_ref[0:n_half, j * HEAD_DIM:(j + 1) * HEAD_DIM] = x_ref[pl.ds(j, n_half, stride=CMP_STRIDE), :]
        x2_ref[n_half:n_half + SUBLANES, j * HEAD_DIM:(j + 1) * HEAD_DIM] = jnp.broadcast_to(
            tail_ref[pl.ds(j, 1), :], (SUBLANES, HEAD_DIM))
    a = jnp.dot(x2_ref[...].astype(BF16), w1_ref[0], preferred_element_type=F32)
    a1_ref[...] = a[:, CMP_HID:]
    pre = a[0:n_half, :CMP_HID] + a1_ref[pl.ds(1, n_half), :] + b1_ref[0]
    out = jnp.dot(_silu(pre).astype(BF16), w2_ref[0], preferred_element_type=F32)
    is_k = pl.program_id(1) < N_KV
    o_ref[0, 0] = jnp.where(is_k, _rms(out, gain_ref[...]), out).astype(BF16)


def compress(raw, tail, w1, b1, w2, kc_norm, n_batch, n_half):
    t_len = n_half * CMP_STRIDE
    wsel = lambda b, c: (c // N_KV, 0, 0)
    return pl.pallas_call(
        functools.partial(_compress_kernel, n_half=n_half),
        grid=(n_batch, 2 * N_KV),
        in_specs=[pl.BlockSpec((t_len, HEAD_DIM), lambda b, c: (b, c)),
                  pl.BlockSpec((CMP_STRIDE, HEAD_DIM), lambda b, c: (b, c)),
                  pl.BlockSpec((1, CMP_STRIDE * HEAD_DIM, 2 * CMP_HID), wsel),
                  pl.BlockSpec((1, 1, CMP_HID), wsel),
                  pl.BlockSpec((1, CMP_HID, HEAD_DIM), wsel),
                  pl.BlockSpec((1, HEAD_DIM), lambda b, c: (0, 0))],
        out_specs=pl.BlockSpec((1, 1, n_half, HEAD_DIM), lambda b, c: (b, c, 0, 0)),
        out_shape=jax.ShapeDtypeStruct((n_batch, 2 * N_KV, n_half, HEAD_DIM), BF16),
        scratch_shapes=[pltpu.VMEM((n_half + SUBLANES, CMP_STRIDE * HEAD_DIM), F32),
                        pltpu.VMEM((n_half + SUBLANES, CMP_HID), F32)],
        compiler_params=_cparams(("parallel", "parallel")),
        name="compress",
    )(raw, tail, w1, b1, w2, kc_norm.reshape(1, HEAD_DIM))


TQ = 128
TK = 256
STRIP_W = TQ + 2 * TK
STRIP_ORIGIN = STRIP_W - TK
MASK_OFF = 1 << 20
BAND_ORIGIN = 64


UNSELECTED = -2.0 ** 30


def _nsa_prompt_kernel(tab_ref, q_ref, kc_ref, vc_ref, ksa_ref, vs_ref, kw_ref, vw_ref, gate_ref, o_ref,
                       strip_ref, band_ref, s_ref, wide_ref, acc_ref, *, n_half, n_blocks):
    kv = pl.program_id(1)
    n = pl.program_id(2)
    head0 = kv * GROUP
    q0 = n * TQ
    jd = n // 2
    odd = n - 2 * jd
    rows = GROUP * TQ
    far = [tab_ref[(NUM_BUCKETS - 1) * N_HEADS_B + head0 + g] for g in range(GROUP)]

    @pl.when(n == 0)
    def _():
        a = _row_iota((TQ, STRIP_W))
        u = _lane_iota((TQ, STRIP_W))
        for g in range(GROUP):
            strip_ref[g] = _bias(a + STRIP_ORIGIN - u, tab_ref, head0 + g) - far[g]
        rel = _row_iota((TQ, n_half)) - ((_lane_iota((TQ, n_half)) - BAND_ORIGIN) * CMP_STRIDE + CMP_BLOCK - 1)
        for g in range(GROUP):
            band_ref[g] = jnp.where(rel >= 0, _bias(rel, tab_ref, head0 + g) - far[g], 0.0)

    q = _stack_heads(q_ref[...])
    qs = q.astype(BF16)
    a_k = _row_iota((TQ, TK))
    c_k = _lane_iota((TQ, TK))

    def tile_start(jj):
        return pl.multiple_of(jnp.maximum(jj, 0) * TK, TK)

    def near_bias(d):
        u0 = pl.multiple_of(STRIP_ORIGIN - d * TK - odd * TQ, TQ)
        return jnp.concatenate([strip_ref[g, :, pl.ds(u0, TK)] for g in range(GROUP)], axis=0)

    def distance(d):
        return _tile_heads(odd * TQ + d * TK + a_k - c_k - jnp.where(jd - d >= 0, 0, MASK_OFF))

    def finish(n_tiles, v_ref):
        m = jnp.max(wide_ref[...], axis=-1, keepdims=True)
        m = jnp.where(m == -jnp.inf, 0.0, m)
        wide_ref[...] = jnp.zeros(wide_ref.shape, F32)
        acc_ref[...] = jnp.zeros(acc_ref.shape, F32)

        def one(jj):
            p = jnp.exp(s_ref[jj] - m)
            wide_ref[...] += p
            acc_ref[...] += jnp.dot(p.astype(BF16), v_ref[pl.ds(tile_start(jj), TK), :], preferred_element_type=F32)

        def pair(t, carry):
            one(2 * t)
            one(2 * t + 1)
            return carry

        lax.fori_loop(0, n_tiles // 2, pair, 0)

        @pl.when(n_tiles % 2 == 1)
        def _():
            one(n_tiles - 1)

        return acc_ref[...] / jnp.maximum(jnp.sum(wide_ref[...], axis=-1, keepdims=True), 1e-30)

    a_c = _row_iota((TQ, n_half))
    rel_c = q0 + a_c - (_lane_iota((TQ, n_half)) * CMP_STRIDE + CMP_BLOCK - 1)
    shift = (n * (TQ // CMP_STRIDE) - BAND_ORIGIN) % n_half
    bias_c = jnp.concatenate([pltpu.roll(band_ref[g], shift, 1) for g in range(GROUP)], axis=0)
    s = lax.dot_general(qs, kc_ref[0, 0], _NT, preferred_element_type=F32)
    p = _masked_softmax(s + bias_c, _tile_heads(rel_c) >= 0)
    o_cmp = jnp.dot(p.astype(BF16), vc_ref[0, 0], preferred_element_type=F32)
    imp = _block_importance(p, n_half, n_blocks, LANES)

    for d in range(3):
        k = kw_ref[pl.ds(tile_start(jd - d), TK), :]
        s = lax.dot_general(qs, k, _NT, preferred_element_type=F32)
        if d < 2:
            s = s + near_bias(d)
        rel = distance(d)
        s = jnp.where((rel >= 0) & (rel < WINDOW), s, -jnp.inf)
        s_ref[d] = s
        wide_ref[...] = s if d == 0 else jnp.maximum(wide_ref[...], s)
    m = jnp.max(wide_ref[...], axis=-1, keepdims=True)
    m = jnp.where(m == -jnp.inf, 0.0, m)
    num = jnp.zeros((rows, HEAD_DIM), F32)
    den = jnp.zeros((rows, TK), F32)
    for d in range(3):
        e = jnp.exp(s_ref[d] - m)
        den = den + e
        num = num + jnp.dot(e.astype(BF16), vw_ref[pl.ds(tile_start(jd - d), TK), :], preferred_element_type=F32)
    o_win = num / jnp.maximum(jnp.sum(den, axis=-1, keepdims=True), 1e-30)

    cur_t = (q0 + _lane_iota((LANES, TQ))) // SEL_BLOCK
    unsel = ((_select_blocks_t(imp.T, cur_t) - 1.0) * -UNSELECTED).T
    qa = jnp.concatenate([q, _tile_heads(unsel)], axis=1).astype(BF16)

    def scores(jj):
        return lax.dot_general(qa, ksa_ref[pl.ds(tile_start(jj), TK), :], _NT, preferred_element_type=F32)

    s = jnp.where(distance(0) >= 0, scores(jd) + near_bias(0), -jnp.inf)
    s_ref[jd] = s
    wide_ref[...] = s

    @pl.when(jd >= 1)
    def _():
        s = scores(jd - 1) + near_bias(1)
        s_ref[jd - 1] = s
        wide_ref[...] = jnp.maximum(wide_ref[...], s)

    n_far = jnp.maximum(jd - 1, 0)

    def far_pair(t, carry):
        for jj in (2 * t, jnp.minimum(2 * t + 1, n_far - 1)):
            s = scores(jj)
            s_ref[jj] = s
            wide_ref[...] = jnp.maximum(wide_ref[...], s)
        return carry

    lax.fori_loop(0, (n_far + 1) // 2, far_pair, 0)
    o_sel = finish(jd + 1, vs_ref)

    o_ref[...] = _gate_mix(gate_ref[...], (o_cmp, o_sel, o_win)).astype(o_ref.dtype)


def nsa_prompt_attention(tab, qn, cmp_tok, att, proj, n_batch, t_len):
    m = qn.shape[0]
    n_q = t_len // TQ
    n_half = cmp_tok.shape[2]
    gw = GROUP * HEAD_DIM
    rows = GROUP * TQ
    seq = lambda col, width: pl.BlockSpec((t_len, width), lambda b, kv, n: (b, col // width + kv))
    tok = lambda off: pl.BlockSpec((1, 1, n_half, HEAD_DIM), lambda b, kv, n: (b, off + kv, 0, 0))
    return pl.pallas_call(
        functools.partial(_nsa_prompt_kernel, n_half=n_half, n_blocks=t_len // SEL_BLOCK),
        grid=(n_batch, N_KV, n_q),
        in_specs=[pl.BlockSpec(memory_space=pltpu.SMEM),
                  pl.BlockSpec((TQ, gw), lambda b, kv, n: (b * n_q + n, kv)),
                  tok(0), tok(N_KV), seq(ATT_KSA, 2 * HEAD_DIM), seq(ATT_VS, HEAD_DIM), seq(ATT_KW, HEAD_DIM),
                  seq(ATT_VW, HEAD_DIM),
                  pl.BlockSpec((TQ, LANES), lambda b, kv, n: (b * n_q + n, COL_GATE // LANES + kv))],
        out_specs=pl.BlockSpec((TQ, gw), lambda b, kv, n: (b * n_q + n, kv)),
        out_shape=jax.ShapeDtypeStruct((m, NSA_W), BF16),
        scratch_shapes=[pltpu.VMEM((GROUP, TQ, STRIP_W), F32), pltpu.VMEM((GROUP, TQ, n_half), F32),
                        pltpu.VMEM((t_len // TK, rows, TK), F32),
                        pltpu.VMEM((rows, TK), F32), pltpu.VMEM((rows, HEAD_DIM), F32)],
        compiler_params=_cparams(("parallel", "parallel", "arbitrary")),
        name="nsa_prompt",
    )(tab, qn, cmp_tok, cmp_tok, att, att, att, att, proj)


PAGES_PER_STEP = 16
TOKEN_ROWS = 2 * N_KV
PAGE_ROWS = PAGE * TOKEN_ROWS


def _lane_queries(q, t_new):
    per_kv = GROUP * t_new
    stacked = jnp.concatenate([q[:, h * HEAD_DIM:(h + 1) * HEAD_DIM] for h in range(N_HEADS_B)], axis=0)
    out = []
    for kv in range(N_KV):
        parts = []
        if kv:
            parts.append(jnp.zeros((kv * per_kv, HEAD_DIM), F32))
        parts.append(stacked[kv * per_kv:(kv + 1) * per_kv])
        parts.append(jnp.zeros((LANES - (kv + 1) * per_kv, HEAD_DIM), F32))
        out.append(jnp.concatenate(parts, axis=0).astype(BF16))
    return out


def _lane_scores(keys, qz):
    s = lax.dot_general(keys[0], qz[0], _NT, preferred_element_type=F32)
    for kv in range(1, N_KV):
        s = s + lax.dot_general(keys[kv], qz[kv], _NT, preferred_element_type=F32)
    return s


def _lane_values(vals, p, per_kv):
    pb = p.astype(BF16)
    lane = _lane_iota((HEAD_DIM, LANES))
    out = lax.dot_general(vals[N_KV - 1], pb, _TN, preferred_element_type=F32)
    for kv in range(N_KV - 2, -1, -1):
        out = jnp.where(lane < (kv + 1) * per_kv, lax.dot_general(vals[kv], pb, _TN, preferred_element_type=F32), out)
    return out


def _lane_bias(rel, tabl_ref):
    val = jnp.broadcast_to(tabl_ref[_BUCKET0:_BUCKET0 + 1, :], rel.shape)
    for edge, bucket in _BUCKET_EDGES:
        val = jnp.where(rel >= edge, tabl_ref[bucket:bucket + 1, :], val)
    return val


def _pad_rows(x, rows):
    return jnp.concatenate([x, jnp.zeros((rows - x.shape[0], x.shape[1]), x.dtype)], axis=0)


PAGE_HALVES = PAGE // CMP_STRIDE
SLAB_ROWS = CMP_STRIDE * TOKEN_ROWS
SLAB_PITCH = SLAB_ROWS + SUBLANES
STAGE_ROWS = PAGES_PER_STEP * PAGE_HALVES * SLAB_PITCH


def _stage_copies(pt_ref, cache_ref, stage_ref, sem_ref, step, slot, n_chunks):
    b = step // n_chunks
    c = step - b * n_chunks
    copies = []
    for p in range(PAGES_PER_STEP):
        page = pt_ref[b, c * PAGES_PER_STEP + p]
        for n in range(PAGE_HALVES):
            dst = pl.multiple_of(slot * STAGE_ROWS + (p * PAGE_HALVES + n) * SLAB_PITCH, SUBLANES)
            copies.append(pltpu.make_async_copy(cache_ref.at[page, pl.ds(n * SLAB_ROWS, SLAB_ROWS), :],
                                                stage_ref.at[pl.ds(dst, SLAB_ROWS), :], sem_ref.at[slot]))
    return copies


def _sample_cmp_kernel(pt_ref, cache_ref, new_ref, q_ref, w1_ref, b1_ref, w2_ref, gain_ref, tabl_ref,
                       sel_ref, ocmp_ref, x2_ref, a_ref, stage_ref, sem_ref, *, past, t_new, n_blocks, sel_rows):
    c = pl.program_id(1)
    n_chunks = pl.num_programs(1)
    step = pl.program_id(0) * n_chunks + c
    slot = step % 2
    n_half = past // CMP_STRIDE
    step_halves = PAGES_PER_STEP * PAGE_HALVES
    per_kv = GROUP * t_new

    @pl.when(step == 0)
    def _():
        for cp in _stage_copies(pt_ref, cache_ref, stage_ref, sem_ref, step, slot, n_chunks):
            cp.start()

    @pl.when(step + 1 < pl.num_programs(0) * n_chunks)
    def _():
        for cp in _stage_copies(pt_ref, cache_ref, stage_ref, sem_ref, step + 1, 1 - slot, n_chunks):
            cp.start()

    for cp in _stage_copies(pt_ref, cache_ref, stage_ref, sem_ref, step, slot, n_chunks):
        cp.wait()

    base = slot * STAGE_ROWS
    for combo in range(TOKEN_ROWS):
        for p in range(PAGES_PER_STEP):
            for j in range(CMP_STRIDE):
                start = base + p * PAGE_HALVES * SLAB_PITCH + j * TOKEN_ROWS + combo
                x2_ref[combo, p * PAGE_HALVES:(p + 1) * PAGE_HALVES, j * HEAD_DIM:(j + 1) * HEAD_DIM] = (
                    stage_ref[pl.ds(start, PAGE_HALVES, stride=SLAB_PITCH), :])
        a = jnp.dot(x2_ref[combo].astype(BF16), w1_ref[combo // N_KV], preferred_element_type=F32)
        a_ref[combo, pl.ds(pl.multiple_of(c * step_halves, step_halves), step_halves), :] = a

    @pl.when(c == pl.num_programs(1) - 1)
    def _():
        tok = []
        for combo in range(TOKEN_ROWS):
            kind = combo // N_KV
            sl = slice(combo * HEAD_DIM, (combo + 1) * HEAD_DIM)
            row = jnp.concatenate([new_ref[j:j + 1, sl] for j in range(t_new)]
                                  + [jnp.zeros((1, (CMP_STRIDE - t_new) * HEAD_DIM), F32)], axis=1)
            tail = jnp.broadcast_to(row, (SUBLANES, CMP_STRIDE * HEAD_DIM)).astype(BF16)
            a_ref[combo, n_half:n_half + SUBLANES, :] = jnp.dot(tail, w1_ref[kind], preferred_element_type=F32)
            pre = a_ref[combo, 0:n_half, 0:CMP_HID] + a_ref[combo, pl.ds(1, n_half), CMP_HID:] + b1_ref[kind]
            out = jnp.dot(_silu(pre).astype(BF16), w2_ref[kind], preferred_element_type=F32)
            if kind == 0:
                out = _rms(out, gain_ref[...])
            tok.append(out.astype(BF16))

        qz = _lane_queries(q_ref[...], t_new)
        shape = (n_half, LANES)
        u = _lane_iota(shape) & (t_new - 1)
        rel = past + u - (_row_iota(shape) * CMP_STRIDE + CMP_BLOCK - 1)
        near0 = ((past - (CMP_BLOCK - 1) - FAR_DIST) // CMP_STRIDE + 1) // SUBLANES * SUBLANES
        bias = jnp.concatenate([jnp.broadcast_to(tabl_ref[NUM_BUCKETS - 1:NUM_BUCKETS, :], (near0, LANES)),
                                _lane_bias(rel[near0:], tabl_ref)], axis=0)
        s = jnp.where(rel >= 0, _lane_scores(tok[:N_KV], qz) + bias, -jnp.inf)
        m = jnp.max(s, axis=0, keepdims=True)
        e = jnp.exp(s - jnp.where(m == -jnp.inf, 0.0, m))
        p = e / jnp.maximum(jnp.sum(e, axis=0, keepdims=True), 1e-30)
        ocmp_ref[...] = _lane_values(tok[N_KV:], p, per_kv)

        src = _row_iota((LANES, LANES))
        dst = _lane_iota((LANES, LANES))
        pooled = (src // per_kv) * t_new + (src & (t_new - 1))
        pool = jnp.where((src < N_KV * per_kv) & (dst == pooled), 1.0, 0.0).astype(BF16)
        hi = p.astype(BF16)
        lo = (p - hi.astype(F32)).astype(BF16)
        imp = jnp.dot(hi, pool, preferred_element_type=F32) + jnp.dot(lo, pool, preferred_element_type=F32)
        si = _row_iota((sel_rows, n_half)) * SEL_BLOCK
        ci = _lane_iota((sel_rows, n_half)) * CMP_STRIDE
        overlap = (ci < si + SEL_BLOCK) & (ci + CMP_BLOCK > si) & (si < n_blocks * SEL_BLOCK)
        ov = jnp.where(overlap, 1.0, 0.0).astype(BF16)
        hi = imp.astype(BF16)
        lo = (imp - hi.astype(F32)).astype(BF16)
        imp = jnp.dot(ov, hi, preferred_element_type=F32) + jnp.dot(ov, lo, preferred_element_type=F32)

        cur = (past + (_lane_iota((sel_rows, LANES)) & (t_new - 1))) // SEL_BLOCK
        sel = _select_blocks_t(imp, cur)
        unpool = jnp.where((dst < N_KV * per_kv) & (src == (dst // per_kv) * t_new + (dst & (t_new - 1))), 1.0, 0.0)
        sel_ref[...] = jnp.dot(sel.astype(BF16), unpool.astype(BF16), preferred_element_type=F32)


def _page_specs(n_chunk_pages):
    def spec(j):
        return pl.BlockSpec((None, PAGE_ROWS, HEAD_DIM), lambda b, c, pt: (pt[b, c * n_chunk_pages + j], 0, 0))
    return [spec(j) for j in range(n_chunk_pages)]


def sample_compressed(page_table, cache3, cmp_new, qn, w1, b1, w2, kc_norm, tabl, t_new):
    n_seq, n_pages = page_table.shape
    past = n_pages * PAGE
    n_half = past // CMP_STRIDE
    n_blocks = -(-(past + t_new) // SEL_BLOCK)
    sel_rows = -(-n_blocks // LANES) * LANES
    const = lambda *shape: pl.BlockSpec(shape, lambda b, c, pt: (0,) * len(shape))
    row = lambda width: pl.BlockSpec((t_new, width), lambda b, c, pt: (b, 0))
    out = lambda rows: pl.BlockSpec((None, rows, LANES), lambda b, c, pt: (b, 0, 0))
    grid_spec = pltpu.PrefetchScalarGridSpec(
        num_scalar_prefetch=1,
        grid=(n_seq, n_pages // PAGES_PER_STEP),
        in_specs=[pl.BlockSpec(memory_space=pl.ANY),
                  row(2 * KV_W), row(NSA_W), const(2, CMP_STRIDE * HEAD_DIM, 2 * CMP_HID), const(2, 1, CMP_HID),
                  const(2, CMP_HID, HEAD_DIM), const(1, HEAD_DIM), const(NUM_BUCKETS, LANES)],
        out_specs=[out(sel_rows), out(HEAD_DIM)],
        scratch_shapes=[pltpu.VMEM((TOKEN_ROWS, PAGES_PER_STEP * PAGE_HALVES, CMP_STRIDE * HEAD_DIM), F32),
                        pltpu.VMEM((TOKEN_ROWS, n_half + SUBLANES, 2 * CMP_HID), F32),
                        pltpu.VMEM((2 * STAGE_ROWS, HEAD_DIM), F32),
                        pltpu.SemaphoreType.DMA((2,))])
    return pl.pallas_call(
        functools.partial(_sample_cmp_kernel, past=past, t_new=t_new, n_blocks=n_blocks, sel_rows=sel_rows),
        grid_spec=grid_spec,
        out_shape=[jax.ShapeDtypeStruct((n_seq, sel_rows, LANES), F32),
                   jax.ShapeDtypeStruct((n_seq, HEAD_DIM, LANES), F32)],
        compiler_params=_cparams(("arbitrary", "arbitrary")),
        name="sample_cmp",
    )(page_table, cache3, cmp_new, qn, w1, b1, w2, kc_norm.reshape(1, HEAD_DIM), tabl)


def _sample_sel_kernel(pt_ref, *refs, past, t_new, win_buf):
    pages = refs[:PAGES_PER_STEP]
    (new_ref, q_ref, sel_ref, ocmp_ref, win_ref, wnew_ref, gate_ref, tabl_ref,
     o_ref, m_ref, l_ref, acc_ref, owin_ref, s_ref) = refs[PAGES_PER_STEP:]
    c = pl.program_id(1)
    per_kv = GROUP * t_new
    qz = _lane_queries(q_ref[...], t_new)
    far = tabl_ref[NUM_BUCKETS - 1:NUM_BUCKETS, :]
    new_rows = 2 * SUBLANES

    def new_kv(ref, kind):
        return [_pad_rows(ref[:, (kind * N_KV + kv) * HEAD_DIM:(kind * N_KV + kv + 1) * HEAD_DIM], new_rows).astype(BF16)
                for kv in range(N_KV)]

    def dist(pos0, rows):
        shape = (rows, LANES)
        return past + (_lane_iota(shape) & (t_new - 1)) - pos0 - _row_iota(shape)

    def update(s, values):
        m_prev = m_ref[...]
        m_new = jnp.maximum(m_prev, jnp.max(s, axis=0, keepdims=True))
        m_safe = jnp.where(m_new == -jnp.inf, 0.0, m_new)
        alpha = jnp.exp(m_prev - m_safe)
        p = jnp.exp(s - m_safe)
        l_ref[...] = alpha * l_ref[...] + jnp.sum(p, axis=0, keepdims=True)
        acc = alpha * acc_ref[...]
        for vals, lo, hi_ in values:
            acc = acc + _lane_values(vals, p[lo:hi_], per_kv)
        acc_ref[...] = acc
        m_ref[...] = m_new

    @pl.when(c == 0)
    def _():
        m_ref[...] = jnp.full(m_ref.shape, -jnp.inf, F32)
        l_ref[...] = jnp.zeros(l_ref.shape, F32)
        acc_ref[...] = jnp.zeros(acc_ref.shape, F32)

        kw = [win_ref[pl.ds(kv, win_buf, stride=TOKEN_ROWS), :].astype(BF16) for kv in range(N_KV)]
        vw = [win_ref[pl.ds(N_KV + kv, win_buf, stride=TOKEN_ROWS), :].astype(BF16) for kv in range(N_KV)]
        pieces = []
        near0 = (win_buf - FAR_DIST + 1) // SUBLANES * SUBLANES
        for keys, rel, near in ((kw, dist(past - win_buf, win_buf), near0), (new_kv(wnew_ref, 0), dist(past, new_rows), 0)):
            bias = _lane_bias(rel[near:], tabl_ref)
            if near:
                bias = jnp.concatenate([jnp.broadcast_to(far, (near, LANES)), bias], axis=0)
            s = _lane_scores(keys, qz) + bias
            pieces.append(jnp.where((rel >= 0) & (rel < WINDOW), s, -jnp.inf))
        m = jnp.maximum(jnp.max(pieces[0], axis=0, keepdims=True), jnp.max(pieces[1], axis=0, keepdims=True))
        m = jnp.where(m == -jnp.inf, 0.0, m)
        e0 = jnp.exp(pieces[0] - m)
        e1 = jnp.exp(pieces[1] - m)
        tot = jnp.sum(e0, axis=0, keepdims=True) + jnp.sum(e1, axis=0, keepdims=True)
        num = _lane_values(vw, e0, per_kv) + _lane_values(new_kv(wnew_ref, 1), e1, per_kv)
        owin_ref[...] = num / jnp.maximum(tot, 1e-30)

        rel = dist(past, new_rows)
        s = _lane_scores(new_kv(new_ref, 0), qz) + _lane_bias(rel, tabl_ref)
        chosen = sel_ref[past // SEL_BLOCK:past // SEL_BLOCK + 1, :] > 0.5
        update(jnp.where(chosen & (rel >= 0), s, -jnp.inf), [(new_kv(new_ref, 1), 0, new_rows)])

    values = []
    for p in range(PAGES_PER_STEP):
        page = c * PAGES_PER_STEP + p
        keys = [pages[p][pl.ds(kv, PAGE, stride=TOKEN_ROWS), :].astype(BF16) for kv in range(N_KV)]
        s = _lane_scores(keys, qz) + far
        per_block = PAGE // SEL_BLOCK
        chosen = jnp.concatenate(
            [jnp.broadcast_to(sel_ref[pl.ds(page * per_block + i, 1), :], (SEL_BLOCK, LANES)) for i in range(per_block)],
            axis=0)
        s_ref[p * PAGE:(p + 1) * PAGE, :] = jnp.where(chosen > 0.5, s, -jnp.inf)
        vals = [pages[p][pl.ds(N_KV + kv, PAGE, stride=TOKEN_ROWS), :].astype(BF16) for kv in range(N_KV)]
        values.append((vals, p * PAGE, (p + 1) * PAGE))

    @pl.when(c == pl.num_programs(1) - 1)
    def _():
        rows = slice((PAGES_PER_STEP - 1) * PAGE, PAGES_PER_STEP * PAGE)
        s_ref[rows, :] = s_ref[rows, :] + (_lane_bias(dist(past - PAGE, PAGE), tabl_ref) - far)

    update(s_ref[...], values)

    @pl.when(c == pl.num_programs(1) - 1)
    def _():
        gt = _sigmoid(gate_ref[...])
        osel = acc_ref[...] / jnp.maximum(l_ref[...], 1e-30)
        o_ref[...] = gt[0:1] * ocmp_ref[...] + gt[1:2] * osel + gt[2:3] * owin_ref[...]


def sample_selected(page_table, cache3, sel_new, qn, sel_t, ocmp_t, win3, win_new, gate_l, tabl, t_new):
    n_seq, n_pages = page_table.shape
    past = n_pages * PAGE
    win_buf = win3.shape[1] // TOKEN_ROWS
    assert (past - FAR_DIST) // PAGE >= n_pages - 1, "only the last page may hold near keys"
    sel_rows = sel_t.shape[1]
    const = lambda *shape: pl.BlockSpec(shape, lambda b, c, pt: (0,) * len(shape))
    row = lambda width: pl.BlockSpec((t_new, width), lambda b, c, pt: (b, 0))
    per_seq = lambda rows, width: pl.BlockSpec((None, rows, width), lambda b, c, pt: (b, 0, 0))
    grid_spec = pltpu.PrefetchScalarGridSpec(
        num_scalar_prefetch=1,
        grid=(n_seq, n_pages // PAGES_PER_STEP),
        in_specs=_page_specs(PAGES_PER_STEP) + [
            row(2 * KV_W), row(NSA_W), per_seq(sel_rows, LANES), per_seq(HEAD_DIM, LANES),
            per_seq(win_buf * TOKEN_ROWS, HEAD_DIM), row(2 * KV_W), per_seq(SUBLANES, LANES),
            const(NUM_BUCKETS, LANES)],
        out_specs=per_seq(HEAD_DIM, LANES),
        scratch_shapes=[pltpu.VMEM((1, LANES), F32), pltpu.VMEM((1, LANES), F32), pltpu.VMEM((HEAD_DIM, LANES), F32),
                        pltpu.VMEM((HEAD_DIM, LANES), F32), pltpu.VMEM((PAGES_PER_STEP * PAGE, LANES), F32)])
    return pl.pallas_call(
        functools.partial(_sample_sel_kernel, past=past, t_new=t_new, win_buf=win_buf),
        grid_spec=grid_spec,
        out_shape=jax.ShapeDtypeStruct((n_seq, HEAD_DIM, LANES), F32),
        compiler_params=_cparams(("parallel", "arbitrary")),
        name="sample_sel",
    )(page_table, *([cache3] * PAGES_PER_STEP), sel_new, qn, sel_t, ocmp_t, win3, win_new, gate_l, tabl)


def _permute_in_proj(w_in):
    d = w_in.shape[0]
    a_end = 4 * HGRN_W
    b_end = a_end + NSA_W + 6 * KV_W
    g_end = b_end + N_GATE
    per_kv = GROUP * 3
    zeros = lambda n: jnp.zeros((d, n), w_in.dtype)
    gates = []
    for kv in range(N_KV):
        gates += [w_in[:, b_end + kv * per_kv:b_end + (kv + 1) * per_kv], zeros(LANES - per_kv)]
    parts = [w_in[:, :a_end], w_in[:, g_end:g_end + 2 * D_MODEL], w_in[:, a_end:b_end]] + gates
    parts.append(zeros(PROJ_TN - N_KV * LANES))
    return jnp.concatenate(parts, axis=1).astype(BF16)


def _compress_weights(k_w1, k_b1, k_w2, v_w1, v_b1, v_w2):
    half = CMP_STRIDE * HEAD_DIM
    cat = lambda w: jnp.concatenate([w[:half], w[half:]], axis=1)
    w1 = jnp.stack([cat(k_w1), cat(v_w1)]).astype(BF16)
    b1 = jnp.stack([k_b1, v_b1]).reshape(2, 1, CMP_HID)
    w2 = jnp.stack([k_w2, v_w2]).astype(BF16)
    return w1, b1, w2


def _lane_table(rel_bias, t_new):
    tab = jnp.repeat(rel_bias, t_new, axis=1)
    return jnp.pad(tab, ((0, 0), (0, LANES - tab.shape[1])))


def _gate_lanes(proj, n_seq, t_new):
    per_kv = GROUP * 3
    g = jnp.stack([proj[:, COL_GATE + kv * LANES:COL_GATE + kv * LANES + per_kv] for kv in range(N_KV)], axis=1)
    g = g.reshape(n_seq, t_new, N_KV, GROUP, 3).transpose(0, 4, 2, 3, 1).reshape(n_seq, 3, N_HEADS_B * t_new)
    return jnp.pad(g, ((0, 0), (0, SUBLANES - 3), (0, LANES - N_HEADS_B * t_new)))


def _tiles(m, t_len):
    big = m >= 1024
    return dict(proj_tm=1024 if big else m, proj_tn=1024, prep_tm=512 if big else m, post_tm=256,
                mlp_tm=1024 if big else m, mlp_tf=512 if big else 1024,
                hgrn_chunk=min(t_len, 256), hgrn_heads=2 if big else N_HEADS_A)


def _trunk(x, mods, s0, w, nsa_fn, attention_operands):
    nb, t_len, d = x.shape
    m = nb * t_len
    tl = _tiles(m, t_len)
    sh1, sc1, g1, sh2, sc2, g2 = mods
    x2 = x.reshape(m, d)
    proj = in_proj(x2, w["norm1"], sc1, sh1, w["w_in"], t_len, tl["proj_tm"], tl["proj_tn"])
    o_a, s_fin = hgrn(proj, w["lb"], w["hgrn_onorm"], s0, nb, t_len, tl["hgrn_chunk"], tl["hgrn_heads"])
    qn, cmp_new, sel_new, win_new, *att = nsa_prep(proj, w["q_norm"], w["ks_norm"], w["kw_norm"], tl["prep_tm"], t_len,
                                                   attention_operands)
    o_b = nsa_fn(proj, qn, cmp_new, sel_new, win_new, *att)
    x1, h2 = post_attn(x2, o_a, o_b, proj, w["w_ba"], w["w_bb"], w["w_out"], g1, w["norm2"], sc2, sh2, t_len,
                       tl["post_tm"])
    y = mlp(h2, x1, w["mlp_w1"], w["mlp_w2"], g2, t_len, tl["mlp_tm"], tl["mlp_tf"])
    return y.reshape(nb, t_len, d), s_fin, (cmp_new, sel_new, win_new)


def kernel(x_prompt, x_sample, c_prompt, c_sample, cache_cmp_kv, cache_sel_kv, cache_win_kv, state_hgrn, page_table, hgrn_lb_logits, rel_bias, ada_w, ada_b, norm1, norm2, w_in, hgrn_onorm, nsa_q_norm, nsa_kc_norm, nsa_ks_norm, nsa_kw_norm, cmp_k_w1, cmp_k_b1, cmp_k_w2, cmp_v_w1, cmp_v_b1, cmp_v_w2, w_branch_a, w_branch_b, w_out, mlp_w1, mlp_w2):
    n_p, t_p, d = x_prompt.shape
    n_s, t_s, _ = x_sample.shape
    past = page_table.shape[1] * PAGE
    win_buf = cache_win_kv.shape[2]
    kvw = 2 * KV_W
    layer = 0

    lb_all = jnp.cumsum(jax.nn.softmax(hgrn_lb_logits.astype(F32), axis=0), axis=0)
    cw1, cb1, cw2 = _compress_weights(cmp_k_w1[layer], cmp_k_b1[layer], cmp_k_w2[layer],
                                      cmp_v_w1[layer], cmp_v_b1[layer], cmp_v_w2[layer])
    w = dict(norm1=norm1[layer], norm2=norm2[layer], w_in=_permute_in_proj(w_in[layer]),
             hgrn_onorm=hgrn_onorm[layer], lb=lb_all[layer], q_norm=nsa_q_norm[layer],
             ks_norm=nsa_ks_norm[layer], kw_norm=nsa_kw_norm[layer],
             w_ba=w_branch_a[layer].astype(BF16), w_bb=w_branch_b[layer].astype(BF16),
             w_out=w_out[layer].astype(BF16), mlp_w1=mlp_w1[layer].astype(BF16), mlp_w2=mlp_w2[layer].astype(BF16))
    tab = rel_bias.astype(F32).reshape(NUM_BUCKETS * N_HEADS_B)

    mods = ada_mods(jnp.concatenate([c_prompt, c_sample], axis=0).astype(F32), ada_w[layer], ada_b[layer])
    mods = jnp.split(mods, 6, axis=-1)
    mods_p = [a[:n_p].reshape(n_p, 1, d) for a in mods]
    mods_s = [jnp.repeat(a[n_p:], t_s, axis=0).reshape(1, n_s * t_s, d) for a in mods]

    def nsa_p(proj, qn, cmp_new, sel_new, win_new, att):
        tail = jnp.zeros((n_p * CMP_STRIDE, kvw), F32)
        cmp_tok = compress(cmp_new, tail, cw1, cb1, cw2, nsa_kc_norm[layer], n_p, t_p // CMP_STRIDE)
        return nsa_prompt_attention(tab, qn, cmp_tok, att, proj, n_p, t_p)

    s0_p = jnp.zeros((n_p, N_HEADS_A, HEAD_DIM, HEAD_DIM), F32)
    y_p, hg_p, (cmp_p, sel_p, win_p) = _trunk(x_prompt, mods_p, s0_p, w, nsa_p, True)

    def nsa_s(proj, qn, cmp_new, sel_new, win_new):
        n_phys = cache_cmp_kv.shape[1]
        cmp3 = cache_cmp_kv[layer].reshape(n_phys, PAGE_ROWS, HEAD_DIM)
        sel3 = cache_sel_kv[layer].reshape(n_phys, PAGE_ROWS, HEAD_DIM)
        win3 = cache_win_kv[layer].reshape(n_s, win_buf * TOKEN_ROWS, HEAD_DIM)
        tabl = _lane_table(rel_bias.astype(F32), t_s)
        sel_t, ocmp_t = sample_compressed(page_table, cmp3, cmp_new, qn, cw1, cb1, cw2, nsa_kc_norm[layer], tabl, t_s)
        o_t = sample_selected(page_table, sel3, sel_new, qn, sel_t, ocmp_t, win3, win_new,
                              _gate_lanes(proj, n_s, t_s), tabl, t_s)
        o = o_t[:, :, :N_HEADS_B * t_s].reshape(n_s, HEAD_DIM, N_HEADS_B, t_s)
        return o.transpose(0, 3, 2, 1).reshape(n_s * t_s, NSA_W).astype(BF16)

    y_s, hg_s, (cmp_s, sel_s, win_s) = _trunk(x_sample, mods_s, state_hgrn[layer], w, nsa_s, False)

    dt = x_prompt.dtype
    pages = lambda a: a.reshape(1, n_p, t_p // PAGE, PAGE, 2, N_KV, HEAD_DIM).astype(dt)
    rows_s = lambda a: a.reshape(1, n_s, t_s, 2, N_KV, HEAD_DIM).astype(dt)
    win_keep = min(WINDOW, t_p)
    win_p_out = win_p.reshape(n_p, t_p, kvw)[:, t_p - win_keep:].reshape(1, n_p, win_keep, 2, N_KV, HEAD_DIM)
    win_rows = cache_win_kv[layer].reshape(n_s, win_buf * TOKEN_ROWS, HEAD_DIM).astype(F32)
    win_s_out = jnp.concatenate([win_rows, win_s.reshape(n_s, t_s * TOKEN_ROWS, HEAD_DIM)], axis=1)
    win_s_out = win_s_out[:, t_s * TOKEN_ROWS:].reshape(1, n_s, win_buf, 2, N_KV, HEAD_DIM)
    return (y_p, y_s, pages(cmp_p), rows_s(cmp_s), pages(sel_p), rows_s(sel_s),
            win_p_out.astype(dt), win_s_out.astype(dt), hg_p[None].astype(dt), hg_s[None].astype(dt))
```

```python
import functools
import math

import numpy as np
import jax
import jax.numpy as jnp
from jax import lax
from jax.experimental import pallas as pl
from jax.experimental.pallas import tpu as pltpu

F32 = jnp.float32
BF16 = jnp.bfloat16

D_MODEL = 2048
N_HEADS_A = 8
HEAD_DIM = 128
HGRN_W = N_HEADS_A * HEAD_DIM
N_HEADS_B = 8
N_KV = 2
GROUP = N_HEADS_B // N_KV
NSA_W = N_HEADS_B * HEAD_DIM
KV_W = N_KV * HEAD_DIM
PAGE = 128
CMP_STRIDE = 16
CMP_BLOCK = 32
CMP_HID = 128
SEL_BLOCK = 64
N_SEL = 16
WINDOW = 512
FORCE_SCORE = 1.0e6
NUM_BUCKETS = 32
REL_MAX_DIST = 128
D_FF = 4 * D_MODEL
EPS = 1e-6
N_GATE = 3 * N_HEADS_B

LANES = 128
SUBLANES = 8
VMEM_LIMIT = 56 * 1024 * 1024

COL_HGRN = 0
COL_QB = 4 * HGRN_W
COL_CMP = COL_QB + NSA_W
COL_SEL = COL_CMP + 2 * KV_W
COL_WIN = COL_SEL + 2 * KV_W
COL_GATE = COL_WIN + 2 * KV_W
GATE_COLS = 512
PROJ_COLS = COL_GATE + GATE_COLS
MERGE_COLS = 2 * D_MODEL


def _cparams(sem):
    return pltpu.CompilerParams(dimension_semantics=sem, vmem_limit_bytes=VMEM_LIMIT)


def _sigmoid(x):
    return 1.0 / (1.0 + jnp.exp(-x))


def _silu(x):
    return x * _sigmoid(x)


def _rms(x, gain):
    return x * lax.rsqrt(jnp.mean(x * x, axis=-1, keepdims=True) + EPS) * gain


def _lane_iota(shape):
    return lax.broadcasted_iota(jnp.int32, shape, 1)


def _row_iota(shape):
    return lax.broadcasted_iota(jnp.int32, shape, 0)


def _ada_kernel(c_ref, w_ref, b_ref, o_ref):
    a = _silu(c_ref[...]).astype(BF16)
    o_ref[...] = jnp.dot(a, w_ref[...].astype(BF16), preferred_element_type=F32) + b_ref[...]


def ada_mods(c, w, b):
    r, d = c.shape
    n = w.shape[1]
    tn = 1024
    return pl.pallas_call(
        _ada_kernel,
        grid=(n // tn,),
        in_specs=[pl.BlockSpec((r, d), lambda j: (0, 0)),
                  pl.BlockSpec((d, tn), lambda j: (0, j)),
                  pl.BlockSpec((1, tn), lambda j: (0, j))],
        out_specs=pl.BlockSpec((r, tn), lambda j: (0, j)),
        out_shape=jax.ShapeDtypeStruct((r, n), F32),
        compiler_params=_cparams(("parallel",)),
        name="ada_mods",
    )(c, w, b.reshape(1, n))


def _inproj_kernel(x_ref, gain_ref, sc_ref, sh_ref, wa_ref, wb_ref, oa_ref, ob_ref, h_ref, *, n_a):
    j = pl.program_id(1)

    @pl.when(j == 0)
    def _():
        h = _rms(x_ref[...], gain_ref[...]) * (1.0 + sc_ref[0]) + sh_ref[0]
        h_ref[...] = h.astype(BF16)

    @pl.when(j < n_a)
    def _():
        oa_ref[...] = jnp.dot(h_ref[...], wa_ref[...], preferred_element_type=F32)

    @pl.when(j >= n_a)
    def _():
        ob_ref[...] = jnp.dot(h_ref[...], wb_ref[...], preferred_element_type=F32)


def _mod_spec(mod, tm, rows_per_batch):
    d = mod.shape[-1]
    if mod.shape[1] == 1:
        return pl.BlockSpec((1, 1, d), lambda i, *_: ((i * tm) // rows_per_batch, 0, 0))
    return pl.BlockSpec((1, tm, d), lambda i, *_: (0, i, 0))


def in_proj(x2, gain, sc, sh, w_a, w_b, rows_per_batch, tm, tn):
    m, d = x2.shape
    n_a = w_a.shape[1] // tn
    n_b = w_b.shape[1] // tn
    col_a = lambda i, j: jnp.minimum(j, n_a - 1)
    col_b = lambda i, j: jnp.maximum(j - n_a, 0)
    return pl.pallas_call(
        functools.partial(_inproj_kernel, n_a=n_a),
        grid=(m // tm, n_a + n_b),
        in_specs=[pl.BlockSpec((tm, d), lambda i, j: (i, 0), pipeline_mode=pl.Buffered(1)),
                  pl.BlockSpec((1, d), lambda i, j: (0, 0)),
                  _mod_spec(sc, tm, rows_per_batch),
                  _mod_spec(sh, tm, rows_per_batch),
                  pl.BlockSpec((d, tn), lambda i, j: (0, col_a(i, j))),
                  pl.BlockSpec((d, tn), lambda i, j: (0, col_b(i, j)))],
        out_specs=[pl.BlockSpec((tm, tn), lambda i, j: (i, col_a(i, j))),
                   pl.BlockSpec((tm, tn), lambda i, j: (i, col_b(i, j)))],
        out_shape=[jax.ShapeDtypeStruct((m, w_a.shape[1]), F32), jax.ShapeDtypeStruct((m, w_b.shape[1]), F32)],
        scratch_shapes=[pltpu.VMEM((tm, d), BF16)],
        compiler_params=_cparams(("parallel", "arbitrary")),
        name="in_proj",
    )(x2, gain.reshape(1, d), sc, sh, w_a, w_b)


def _post_kernel(x_ref, oa_ref, ob_ref, mga_ref, mgb_ref, wba_ref, wbb_ref, wout_ref,
                 g1_ref, gain2_ref, sc2_ref, sh2_ref, x1_ref, h2_ref):
    ya = jnp.dot(oa_ref[...], wba_ref[...], preferred_element_type=F32)
    yb = jnp.dot(ob_ref[...], wbb_ref[...], preferred_element_type=F32)
    merged = _sigmoid(mga_ref[...]) * ya + _sigmoid(mgb_ref[...]) * yb
    y = jnp.dot(merged.astype(BF16), wout_ref[...], preferred_element_type=F32)
    x1 = x_ref[...] + g1_ref[0] * y
    x1_ref[...] = x1
    h2_ref[...] = (_rms(x1, gain2_ref[...]) * (1.0 + sc2_ref[0]) + sh2_ref[0]).astype(BF16)


def post_attn(x2, oa, ob, proj, wba, wbb, wout, g1, gain2, sc2, sh2, rows_per_batch, tm):
    m, d = x2.shape
    const = lambda i: (0, 0)
    resident = lambda shape: pl.BlockSpec(shape, const, pipeline_mode=pl.Buffered(1))
    return pl.pallas_call(
        _post_kernel,
        grid=(m // tm,),
        in_specs=[pl.BlockSpec((tm, d), lambda i: (i, 0)),
                  pl.BlockSpec((tm, HGRN_W), lambda i: (i, 0)),
                  pl.BlockSpec((tm, NSA_W), lambda i: (i, 0)),
                  pl.BlockSpec((tm, d), lambda i: (i, 0)),
                  pl.BlockSpec((tm, d), lambda i: (i, 1)),
                  resident((HGRN_W, d)),
                  resident((NSA_W, d)),
                  resident((d, d)),
                  _mod_spec(g1, tm, rows_per_batch),
                  pl.BlockSpec((1, d), const),
                  _mod_spec(sc2, tm, rows_per_batch),
                  _mod_spec(sh2, tm, rows_per_batch)],
        out_specs=[pl.BlockSpec((tm, d), lambda i: (i, 0)),
                   pl.BlockSpec((tm, d), lambda i: (i, 0))],
        out_shape=[jax.ShapeDtypeStruct((m, d), F32), jax.ShapeDtypeStruct((m, d), BF16)],
        compiler_params=_cparams(("parallel",)),
        name="post_attn",
    )(x2, oa, ob, proj, proj, wba, wbb, wout, g1, gain2.reshape(1, d), sc2, sh2)


def _mlp_kernel(h_ref, x1_ref, w1_ref, w2_ref, g2_ref, y_ref, acc_ref):
    f = pl.program_id(1)
    u = jnp.maximum(jnp.dot(h_ref[...], w1_ref[...], preferred_element_type=F32), 0.0)
    part = jnp.dot((u * u).astype(BF16), w2_ref[...], preferred_element_type=F32)

    @pl.when(f == 0)
    def _():
        acc_ref[...] = part

    @pl.when(f > 0)
    def _():
        acc_ref[...] += part

    @pl.when(f == pl.num_programs(1) - 1)
    def _():
        y_ref[...] = x1_ref[...] + g2_ref[0] * acc_ref[...]


def mlp(h2, x1, w1, w2, g2, rows_per_batch, tm, tf):
    m, d = x1.shape
    ff = w1.shape[1]
    return pl.pallas_call(
        _mlp_kernel,
        grid=(m // tm, ff // tf),
        in_specs=[pl.BlockSpec((tm, d), lambda i, f: (i, 0)),
                  pl.BlockSpec((tm, d), lambda i, f: (i, 0)),
                  pl.BlockSpec((d, tf), lambda i, f: (0, f)),
                  pl.BlockSpec((tf, d), lambda i, f: (f, 0)),
                  _mod_spec(g2, tm, rows_per_batch)],
        out_specs=pl.BlockSpec((tm, d), lambda i, f: (i, 0)),
        out_shape=jax.ShapeDtypeStruct((m, d), F32),
        scratch_shapes=[pltpu.VMEM((tm, d), F32)],
        compiler_params=_cparams(("parallel", "arbitrary")),
        name="mlp",
    )(h2, x1, w1, w2, g2)


_NT = (((1,), (1,)), ((), ()))
_TN = (((0,), (0,)), ((), ()))


def _hgrn_kernel(q_ref, z_ref, v_ref, g_ref, lb_ref, on_ref, s0_ref, o_ref, sfin_ref, st_ref, b_ref, *, chunk, heads):
    for h in range(heads):
        _hgrn_head(h, q_ref, z_ref, v_ref, g_ref, lb_ref, on_ref, s0_ref, o_ref, sfin_ref, st_ref, b_ref, chunk)


def _hgrn_head(h, q_ref, z_ref, v_ref, g_ref, lb_ref, on_ref, s0_ref, o_ref, sfin_ref, st_ref, b_ref, chunk):
    c = pl.program_id(2)
    cols = slice(h * HEAD_DIM, (h + 1) * HEAD_DIM)

    @pl.when(c == 0)
    def _():
        st_ref[h] = s0_ref[0, h].T

    q = q_ref[:, cols]
    z = z_ref[:, cols]
    v = v_ref[:, cols]
    lb = lb_ref[h]
    e = jnp.exp(-jnp.abs(z))
    r = 1.0 / (1.0 + e)
    pos = z >= 0.0
    logf = jnp.log(lb + (1.0 - lb) * jnp.where(pos, r, e * r))
    k = (1.0 - lb) * jnp.where(pos, e * r, r)

    t = lax.broadcasted_iota(jnp.int32, (chunk, HEAD_DIM), 0)
    b = logf
    s = 1
    while s < chunk:
        b = b + jnp.where(t >= s, pltpu.roll(b, s, 0), 0.0)
        s *= 2
    b_ref[h] = b

    t8 = t & (SUBLANES - 1)
    if chunk <= SUBLANES:
        o = jnp.zeros((chunk, HEAD_DIM), F32)
        for d in range(SUBLANES):
            kd, bd, vd = (k, b, v) if d == 0 else (pltpu.roll(k, d, 0), pltpu.roll(b, d, 0), pltpu.roll(v, d, 0))
            w = jnp.exp(jnp.where(t8 >= d, b - bd, -jnp.inf))
            o = o + jnp.sum(q * kd * w, axis=-1, keepdims=True) * vd
    else:
        o = jnp.sum(q * k, axis=-1, keepdims=True) * v
        row = lax.broadcasted_iota(jnp.int32, (chunk, chunk), 0)
        col = lax.broadcasted_iota(jnp.int32, (chunk, chunk), 1)
        apart = row ^ col
        att = jnp.zeros((chunk, chunk), F32)
        sub = _row_iota((SUBLANES, HEAD_DIM))
        m = 1
        while m < chunk:
            def ref_row(r, rows):
                return jnp.broadcast_to(b_ref[h, pl.ds(r, 1), :], (rows, HEAD_DIM))

            if 2 * m >= SUBLANES:
                refs = [ref_row(blk * 2 * m + m - 1, 2 * m) for blk in range(chunk // (2 * m))]
            else:
                refs = []
                for tile in range(chunk // SUBLANES):
                    ref_t = ref_row(tile * SUBLANES + m - 1, SUBLANES)
                    for j in range(1, SUBLANES // (2 * m)):
                        ref_t = jnp.where(sub >= j * 2 * m, ref_row(tile * SUBLANES + j * 2 * m + m - 1, SUBLANES), ref_t)
                    refs.append(ref_t)
            ref_b = refs[0] if len(refs) == 1 else jnp.concatenate(refs, axis=0)
            second = (t & (2 * m - 1)) >= m
            w = jnp.exp(jnp.where(second, b - ref_b, ref_b - b))
            qs = jnp.where(second, q * w, 0.0).astype(BF16)
            ks = jnp.where(second, 0.0, k * w).astype(BF16)
            a_m = lax.dot_general(qs, ks, _NT, preferred_element_type=F32)
            att = att + jnp.where(apart < 2 * m, a_m, 0.0)
            m *= 2
        o = o + jnp.dot(att.astype(BF16), v.astype(BF16), preferred_element_type=F32)

    st = st_ref[h]
    b_last = b_ref[h, pl.ds(chunk - 1, 1), :]
    o = o + lax.dot_general((q * jnp.exp(b)).astype(BF16), st.astype(BF16), _NT, preferred_element_type=F32)
    kt = (k * jnp.exp(b_last - b)).astype(BF16)
    vb = v.astype(BF16)
    if chunk < 2 * SUBLANES:
        pad = jnp.zeros((2 * SUBLANES - chunk, HEAD_DIM), BF16)
        kt = jnp.concatenate([kt, pad], axis=0)
        vb = jnp.concatenate([vb, pad], axis=0)
    st_new = jnp.exp(b_last) * st + lax.dot_general(vb, kt, _TN, preferred_element_type=F32)
    st_ref[h] = st_new

    o_ref[:, cols] = (_rms(o, on_ref[h]) * _silu(g_ref[:, cols])).astype(o_ref.dtype)

    @pl.when(c == pl.num_programs(2) - 1)
    def _():
        sfin_ref[0, h] = st_new.T


def hgrn(proj, lb, onorm, s0, n_batch, t_len, chunk, heads):
    m = proj.shape[0]
    n_c = t_len // chunk
    hb = N_HEADS_A // heads
    width = heads * HEAD_DIM

    def col(group):
        return pl.BlockSpec((chunk, width), lambda bi, h, c: (bi * n_c + c, group * hb + h))

    vec = pl.BlockSpec((heads, 1, HEAD_DIM), lambda bi, h, c: (h, 0, 0))
    state = pl.BlockSpec((1, heads, HEAD_DIM, HEAD_DIM), lambda bi, h, c: (bi, h, 0, 0))
    o_dtype = BF16 if chunk % (2 * SUBLANES) == 0 else F32
    o, s_fin = pl.pallas_call(
        functools.partial(_hgrn_kernel, chunk=chunk, heads=heads),
        grid=(n_batch, hb, n_c),
        in_specs=[col(0), col(1), col(2), col(3), vec, vec, state],
        out_specs=[pl.BlockSpec((chunk, width), lambda bi, h, c: (bi * n_c + c, h)), state],
        out_shape=[jax.ShapeDtypeStruct((m, HGRN_W), o_dtype),
                   jax.ShapeDtypeStruct((n_batch, N_HEADS_A, HEAD_DIM, HEAD_DIM), F32)],
        scratch_shapes=[pltpu.VMEM((heads, HEAD_DIM, HEAD_DIM), F32), pltpu.VMEM((heads, chunk, HEAD_DIM), F32)],
        compiler_params=_cparams(("parallel", "parallel", "arbitrary")),
        name="hgrn",
    )(proj, proj, proj, proj, lb.reshape(N_HEADS_A, 1, HEAD_DIM), onorm.reshape(N_HEADS_A, 1, HEAD_DIM), s0)
    return o.astype(BF16), s_fin


def _bucket_steps():
    n = np.arange(REL_MAX_DIST)
    exact = NUM_BUCKETS // 2
    val = np.log(np.maximum(n, 1) / exact) / math.log(REL_MAX_DIST / exact) * (NUM_BUCKETS - exact)
    frac = np.abs(val - np.round(val))[exact + 1:]
    assert frac.min() > 1e-3, "a bucket edge sits on an integer distance"
    lut = np.where(n < exact, n, np.minimum(exact + np.floor(np.maximum(val, 0.0)).astype(np.int64), NUM_BUCKETS - 1))
    assert lut[-1] == NUM_BUCKETS - 1
    return int(lut[0]), [(int(i), int(lut[i])) for i in range(1, REL_MAX_DIST) if lut[i] != lut[i - 1]]


_BUCKET0, _BUCKET_EDGES = _bucket_steps()
FAR_DIST = _BUCKET_EDGES[-1][0]


def _bias(rel, tab_ref, head):
    val = jnp.full(rel.shape, tab_ref[_BUCKET0 * N_HEADS_B + head], F32)
    for edge, bucket in _BUCKET_EDGES:
        val = jnp.where(rel >= edge, tab_ref[bucket * N_HEADS_B + head], val)
    return val


def _stack_heads(x):
    return jnp.concatenate([x[:, g * HEAD_DIM:(g + 1) * HEAD_DIM] for g in range(GROUP)], axis=0)


def _tile_heads(x):
    return jnp.concatenate([x] * GROUP, axis=0)


def _masked_softmax(s, mask):
    s = jnp.where(mask, s, -jnp.inf)
    m = jnp.max(s, axis=-1, keepdims=True)
    m = jnp.where(m == -jnp.inf, 0.0, m)
    e = jnp.exp(s - m)
    return e / jnp.maximum(jnp.sum(e, axis=-1, keepdims=True), 1e-30)


def _block_importance(p, n_cmp, n_blocks, width):
    rows = p.shape[0] // GROUP
    imp = p[0:rows]
    for g in range(1, GROUP):
        imp = imp + p[g * rows:(g + 1) * rows]
    ci = lax.broadcasted_iota(jnp.int32, (n_cmp, width), 0) * CMP_STRIDE
    si = lax.broadcasted_iota(jnp.int32, (n_cmp, width), 1) * SEL_BLOCK
    overlap = (ci < si + SEL_BLOCK) & (ci + CMP_BLOCK > si) & (si < n_blocks * SEL_BLOCK)
    ov = jnp.where(overlap, 1.0, 0.0).astype(BF16)
    hi = imp.astype(BF16)
    lo = (imp - hi.astype(F32)).astype(BF16)
    return jnp.dot(hi, ov, preferred_element_type=F32) + jnp.dot(lo, ov, preferred_element_type=F32)


def _select_blocks_t(imp_t, cur):
    blk = _row_iota(imp_t.shape)
    forced = (blk == 0) | (blk == cur) | (blk == cur - 1)
    score = jnp.where(blk <= cur, imp_t + jnp.where(forced, FORCE_SCORE, 0.0), -jnp.inf)
    sel = jnp.zeros(imp_t.shape, F32)
    n_rows = imp_t.shape[0]
    for _ in range(N_SEL):
        top = jnp.max(score, axis=0, keepdims=True)
        first = jnp.min(jnp.where(score == top, blk, n_rows), axis=0, keepdims=True)
        pick = blk == first
        sel = jnp.where(pick, 1.0, sel)
        score = jnp.where(pick, -jnp.inf, score)
    return sel


def _gate_mix(gate, parts):
    rows = gate.shape[0]
    gt = _sigmoid(gate)
    outs = []
    for g in range(GROUP):
        o = gt[:, 3 * g:3 * g + 1] * parts[0][g * rows:(g + 1) * rows]
        for j in (1, 2):
            o = o + gt[:, 3 * g + j:3 * g + j + 1] * parts[j][g * rows:(g + 1) * rows]
        outs.append(o)
    return jnp.concatenate(outs, axis=1)


ATT_KSA = 0
ATT_VS = N_KV * 2 * HEAD_DIM
ATT_KW = ATT_VS + KV_W
ATT_VW = ATT_KW + KV_W
ATT_COLS = ATT_VW + KV_W


def _nsa_prep_kernel(q_ref, cmp_ref, sel_ref, win_ref, qn_ref, ksn_ref, kwn_ref, qo_ref, co_ref, so_ref, wo_ref,
                     *att_ref, t_len):
    q = q_ref[...]
    scale = HEAD_DIM ** -0.5
    for h in range(N_HEADS_B):
        sl = slice(h * HEAD_DIM, (h + 1) * HEAD_DIM)
        qo_ref[:, sl] = _rms(q[:, sl], qn_ref[...]) * scale
    co_ref[...] = cmp_ref[...]
    for src, gain, dst in ((sel_ref, ksn_ref, so_ref), (win_ref, kwn_ref, wo_ref)):
        x = src[...]
        for h in range(N_KV):
            sl = slice(h * HEAD_DIM, (h + 1) * HEAD_DIM)
            dst[:, sl] = _rms(x[:, sl], gain[...])
        dst[:, KV_W:] = x[:, KV_W:]
    if att_ref:
        (att_ref,) = att_ref
        tm = q.shape[0]
        pos = (pl.program_id(0) * tm + _row_iota((tm, LANES))) % t_len
        onehot = jnp.where(_lane_iota((tm, LANES)) == pos // SEL_BLOCK, 1.0, 0.0).astype(BF16)
        for h in range(N_KV):
            att_ref[:, ATT_KSA + 2 * h * HEAD_DIM:ATT_KSA + (2 * h + 1) * HEAD_DIM] = (
                so_ref[:, h * HEAD_DIM:(h + 1) * HEAD_DIM].astype(BF16))
            att_ref[:, ATT_KSA + (2 * h + 1) * HEAD_DIM:ATT_KSA + (2 * h + 2) * HEAD_DIM] = onehot
        att_ref[:, ATT_VS:ATT_KW] = so_ref[:, KV_W:].astype(BF16)
        att_ref[:, ATT_KW:ATT_VW] = wo_ref[:, :KV_W].astype(BF16)
        att_ref[:, ATT_VW:ATT_COLS] = wo_ref[:, KV_W:].astype(BF16)


def nsa_prep(proj, q_norm, ks_norm, kw_norm, tm, t_len, attention_operands):
    m = proj.shape[0]
    kvw = 2 * KV_W
    vec = pl.BlockSpec((1, HEAD_DIM), lambda i: (0, 0))
    widths = [NSA_W, kvw, kvw, kvw] + ([ATT_COLS] if attention_operands else [])
    dtypes = [F32] * 4 + ([BF16] if attention_operands else [])
    if attention_operands:
        assert t_len // SEL_BLOCK <= LANES
    return pl.pallas_call(
        functools.partial(_nsa_prep_kernel, t_len=t_len),
        grid=(m // tm,),
        in_specs=[pl.BlockSpec((tm, NSA_W), lambda i: (i, COL_QB // NSA_W)),
                  pl.BlockSpec((tm, kvw), lambda i: (i, COL_CMP // kvw)),
                  pl.BlockSpec((tm, kvw), lambda i: (i, COL_SEL // kvw)),
                  pl.BlockSpec((tm, kvw), lambda i: (i, COL_WIN // kvw)),
                  vec, vec, vec],
        out_specs=[pl.BlockSpec((tm, wd), lambda i: (i, 0)) for wd in widths],
        out_shape=[jax.ShapeDtypeStruct((m, wd), dt) for wd, dt in zip(widths, dtypes)],
        compiler_params=_cparams(("parallel",)),
        name="nsa_prep",
    )(proj, proj, proj, proj, q_norm.reshape(1, HEAD_DIM), ks_norm.reshape(1, HEAD_DIM), kw_norm.reshape(1, HEAD_DIM))


def _compress_kernel(x_ref, tail_ref, w1_ref, b1_ref, w2_ref, gain_ref, o_ref, x2_ref, a1_ref, *, n_half):
    for j in range(CMP_STRIDE):
        x2_ref[0:n_half, j * HEAD_DIM:(j + 1) * HEAD_DIM] = x_ref[pl.ds(j, n_half, stride=CMP_STRIDE), :]
        x2_ref[n_half:n_half + SUBLANES, j * HEAD_DIM:(j + 1) * HEAD_DIM] = jnp.broadcast_to(
            tail_ref[pl.ds(j, 1), :], (SUBLANES, HEAD_DIM))
    a = jnp.dot(x2_ref[...].astype(BF16), w1_ref[0], preferred_element_type=F32)
    a1_ref[...] = a[:, CMP_HID:]
    pre = a[0:n_half, :CMP_HID] + a1_ref[pl.ds(1, n_half), :] + b1_ref[0]
    out = jnp.dot(_silu(pre).astype(BF16), w2_ref[0], preferred_element_type=F32)
    is_k = pl.program_id(1) < N_KV
    o_ref[0, 0] = jnp.where(is_k, _rms(out, gain_ref[...]), out).astype(BF16)


def compress(raw, tail, w1, b1, w2, kc_norm, n_batch, n_half):
    t_len = n_half * CMP_STRIDE
    wsel = lambda b, c: (c // N_KV, 0, 0)
    return pl.pallas_call(
        functools.partial(_compress_kernel, n_half=n_half),
        grid=(n_batch, 2 * N_KV),
        in_specs=[pl.BlockSpec((t_len, HEAD_DIM), lambda b, c: (b, c)),
                  pl.BlockSpec((CMP_STRIDE, HEAD_DIM), lambda b, c: (b, c)),
                  pl.BlockSpec((1, CMP_STRIDE * HEAD_DIM, 2 * CMP_HID), wsel),
                  pl.BlockSpec((1, 1, CMP_HID), wsel),
                  pl.BlockSpec((1, CMP_HID, HEAD_DIM), wsel),
                  pl.BlockSpec((1, HEAD_DIM), lambda b, c: (0, 0))],
        out_specs=pl.BlockSpec((1, 1, n_half, HEAD_DIM), lambda b, c: (b, c, 0, 0)),
        out_shape=jax.ShapeDtypeStruct((n_batch, 2 * N_KV, n_half, HEAD_DIM), BF16),
        scratch_shapes=[pltpu.VMEM((n_half + SUBLANES, CMP_STRIDE * HEAD_DIM), F32),
                        pltpu.VMEM((n_half + SUBLANES, CMP_HID), F32)],
        compiler_params=_cparams(("parallel", "parallel")),
        name="compress",
    )(raw, tail, w1, b1, w2, kc_norm.reshape(1, HEAD_DIM))


TQ = 128
TK = 256
STRIP_W = TQ + 2 * TK
STRIP_ORIGIN = STRIP_W - TK
MASK_OFF = 1 << 20
BAND_ORIGIN = 64


UNSELECTED = -2.0 ** 30


def _nsa_prompt_kernel(tab_ref, q_ref, kc_ref, vc_ref, ksa_ref, vs_ref, kw_ref, vw_ref, gate_ref, o_ref,
                       strip_ref, band_ref, s_ref, wide_ref, acc_ref, *, n_half, n_blocks):
    kv = pl.program_id(1)
    n = pl.program_id(2)
    head0 = kv * GROUP
    q0 = n * TQ
    jd = n // 2
    odd = n - 2 * jd
    rows = GROUP * TQ
    far = [tab_ref[(NUM_BUCKETS - 1) * N_HEADS_B + head0 + g] for g in range(GROUP)]

    @pl.when(n == 0)
    def _():
        a = _row_iota((TQ, STRIP_W))
        u = _lane_iota((TQ, STRIP_W))
        for g in range(GROUP):
            strip_ref[g] = _bias(a + STRIP_ORIGIN - u, tab_ref, head0 + g) - far[g]
        rel = _row_iota((TQ, n_half)) - ((_lane_iota((TQ, n_half)) - BAND_ORIGIN) * CMP_STRIDE + CMP_BLOCK - 1)
        for g in range(GROUP):
            band_ref[g] = jnp.where(rel >= 0, _bias(rel, tab_ref, head0 + g) - far[g], 0.0)

    q = _stack_heads(q_ref[...])
    qs = q.astype(BF16)
    a_k = _row_iota((TQ, TK))
    c_k = _lane_iota((TQ, TK))

    def tile_start(jj):
        return pl.multiple_of(jnp.maximum(jj, 0) * TK, TK)

    def near_bias(d):
        u0 = pl.multiple_of(STRIP_ORIGIN - d * TK - odd * TQ, TQ)
        return jnp.concatenate([strip_ref[g, :, pl.ds(u0, TK)] for g in range(GROUP)], axis=0)

    def finish(n_tiles, v_ref):
        m = jnp.max(wide_ref[...], axis=-1, keepdims=True)
        m = jnp.where(m == -jnp.inf, 0.0, m)
        wide_ref[...] = jnp.zeros(wide_ref.shape, F32)
        acc_ref[...] = jnp.zeros(acc_ref.shape, F32)

        def one(jj):
            p = jnp.exp(s_ref[jj] - m)
            wide_ref[...] += p
            acc_ref[...] += jnp.dot(p.astype(BF16), v_ref[pl.ds(tile_start(jj), TK), :], preferred_element_type=F32)

        def pair(t, carry):
            one(2 * t)
            one(2 * t + 1)
            return carry

        lax.fori_loop(0, n_tiles // 2, pair, 0)

        @pl.when(n_tiles % 2 == 1)
        def _():
            one(n_tiles - 1)

        return acc_ref[...] / jnp.maximum(jnp.sum(wide_ref[...], axis=-1, keepdims=True), 1e-30)

    a_c = _row_iota((TQ, n_half))
    rel_c = q0 + a_c - (_lane_iota((TQ, n_half)) * CMP_STRIDE + CMP_BLOCK - 1)
    shift = (n * (TQ // CMP_STRIDE) - BAND_ORIGIN) % n_half
    bias_c = jnp.concatenate([pltpu.roll(band_ref[g], shift, 1) for g in range(GROUP)], axis=0)
    s = lax.dot_general(qs, kc_ref[0, 0], _NT, preferred_element_type=F32)
    p = _masked_softmax(s + bias_c, _tile_heads(rel_c) >= 0)
    o_cmp = jnp.dot(p.astype(BF16), vc_ref[0, 0], preferred_element_type=F32)
    imp = _block_importance(p, n_half, n_blocks, LANES)

    lead = _tile_heads(a_k - c_k) + odd * TQ
    causal = lead >= 0
    for d in range(3):
        k = kw_ref[pl.ds(tile_start(jd - d), TK), :]
        s = lax.dot_general(qs, k, _NT, preferred_element_type=F32)
        if d == 0:
            s = jnp.where(causal, s + near_bias(0), -jnp.inf)
        elif d == 1:
            s = s + near_bias(1) + jnp.where(jd >= 1, 0.0, -jnp.inf)
        else:
            s = jnp.where(lead < jnp.where(jd >= 2, 0, -MASK_OFF), s, -jnp.inf)
        s_ref[d] = s
        wide_ref[...] = s if d == 0 else jnp.maximum(wide_ref[...], s)
    m = jnp.max(wide_ref[...], axis=-1, keepdims=True)
    m = jnp.where(m == -jnp.inf, 0.0, m)
    num = jnp.zeros((rows, HEAD_DIM), F32)
    den = jnp.zeros((rows, TK), F32)
    for d in range(3):
        e = jnp.exp(s_ref[d] - m)
        den = den + e
        num = num + jnp.dot(e.astype(BF16), vw_ref[pl.ds(tile_start(jd - d), TK), :], preferred_element_type=F32)
    o_win = num / jnp.maximum(jnp.sum(den, axis=-1, keepdims=True), 1e-30)

    cur_t = (q0 + _lane_iota((LANES, TQ))) // SEL_BLOCK
    unsel = ((_select_blocks_t(imp.T, cur_t) - 1.0) * -UNSELECTED).T
    qa = jnp.concatenate([q, _tile_heads(unsel)], axis=1).astype(BF16)

    def scores(jj):
        return lax.dot_general(qa, ksa_ref[pl.ds(tile_start(jj), TK), :], _NT, preferred_element_type=F32)

    s = jnp.where(causal, scores(jd) + near_bias(0), -jnp.inf)
    s_ref[jd] = s
    wide_ref[...] = s

    @pl.when(jd >= 1)
    def _():
        s = scores(jd - 1) + near_bias(1)
        s_ref[jd - 1] = s
        wide_ref[...] = jnp.maximum(wide_ref[...], s)

    n_far = jnp.maximum(jd - 1, 0)

    def far_pair(t, carry):
        for jj in (2 * t, jnp.minimum(2 * t + 1, n_far - 1)):
            s = scores(jj)
            s_ref[jj] = s
            wide_ref[...] = jnp.maximum(wide_ref[...], s)
        return carry

    lax.fori_loop(0, (n_far + 1) // 2, far_pair, 0)
    o_sel = finish(jd + 1, vs_ref)

    o_ref[...] = _gate_mix(gate_ref[...], (o_cmp, o_sel, o_win)).astype(o_ref.dtype)


def nsa_prompt_attention(tab, qn, cmp_tok, att, proj, n_batch, t_len):
    m = qn.shape[0]
    n_q = t_len // TQ
    n_half = cmp_tok.shape[2]
    gw = GROUP * HEAD_DIM
    rows = GROUP * TQ
    seq = lambda col, width: pl.BlockSpec((t_len, width), lambda b, kv, n: (b, col // width + kv))
    tok = lambda off: pl.BlockSpec((1, 1, n_half, HEAD_DIM), lambda b, kv, n: (b, off + kv, 0, 0))
    return pl.pallas_call(
        functools.partial(_nsa_prompt_kernel, n_half=n_half, n_blocks=t_len // SEL_BLOCK),
        grid=(n_batch, N_KV, n_q),
        in_specs=[pl.BlockSpec(memory_space=pltpu.SMEM),
                  pl.BlockSpec((TQ, gw), lambda b, kv, n: (b * n_q + n, kv)),
                  tok(0), tok(N_KV), seq(ATT_KSA, 2 * HEAD_DIM), seq(ATT_VS, HEAD_DIM), seq(ATT_KW, HEAD_DIM),
                  seq(ATT_VW, HEAD_DIM),
                  pl.BlockSpec((TQ, LANES), lambda b, kv, n: (b * n_q + n, COL_GATE // LANES + kv))],
        out_specs=pl.BlockSpec((TQ, gw), lambda b, kv, n: (b * n_q + n, kv)),
        out_shape=jax.ShapeDtypeStruct((m, NSA_W), BF16),
        scratch_shapes=[pltpu.VMEM((GROUP, TQ, STRIP_W), F32), pltpu.VMEM((GROUP, TQ, n_half), F32),
                        pltpu.VMEM((t_len // TK, rows, TK), F32),
                        pltpu.VMEM((rows, TK), F32), pltpu.VMEM((rows, HEAD_DIM), F32)],
        compiler_params=_cparams(("parallel", "parallel", "arbitrary")),
        name="nsa_prompt",
    )(tab, qn, cmp_tok, cmp_tok, att, att, att, att, proj)


PAGES_PER_STEP = 32
TOKEN_ROWS = 2 * N_KV
PAGE_ROWS = PAGE * TOKEN_ROWS


def _lane_queries(q, t_new):
    per_kv = GROUP * t_new
    stacked = jnp.concatenate([q[:, h * HEAD_DIM:(h + 1) * HEAD_DIM] for h in range(N_HEADS_B)], axis=0)
    out = []
    for kv in range(N_KV):
        parts = []
        if kv:
            parts.append(jnp.zeros((kv * per_kv, HEAD_DIM), F32))
        parts.append(stacked[kv * per_kv:(kv + 1) * per_kv])
        parts.append(jnp.zeros((LANES - (kv + 1) * per_kv, HEAD_DIM), F32))
        out.append(jnp.concatenate(parts, axis=0).astype(BF16))
    return out


def _lane_scores(keys, qz):
    s = lax.dot_general(keys[0], qz[0], _NT, preferred_element_type=F32)
    for kv in range(1, N_KV):
        s = s + lax.dot_general(keys[kv], qz[kv], _NT, preferred_element_type=F32)
    return s


def _lane_values(vals, p, per_kv):
    pb = p.astype(BF16)
    lane = _lane_iota((HEAD_DIM, LANES))
    out = lax.dot_general(vals[N_KV - 1], pb, _TN, preferred_element_type=F32)
    for kv in range(N_KV - 2, -1, -1):
        out = jnp.where(lane < (kv + 1) * per_kv, lax.dot_general(vals[kv], pb, _TN, preferred_element_type=F32), out)
    return out


def _lane_bias(rel, tabl_ref):
    val = jnp.broadcast_to(tabl_ref[_BUCKET0:_BUCKET0 + 1, :], rel.shape)
    for edge, bucket in _BUCKET_EDGES:
        val = jnp.where(rel >= edge, tabl_ref[bucket:bucket + 1, :], val)
    return val


def _pad_rows(x, rows):
    return jnp.concatenate([x, jnp.zeros((rows - x.shape[0], x.shape[1]), x.dtype)], axis=0)


PAGE_HALVES = PAGE // CMP_STRIDE
SLAB_ROWS = CMP_STRIDE * TOKEN_ROWS
SLAB_PITCH = SLAB_ROWS + SUBLANES
STAGE_ROWS = PAGES_PER_STEP * PAGE_HALVES * SLAB_PITCH


def _stage_copies(pt_ref, cache_ref, stage_ref, sem_ref, step, slot, n_chunks):
    b = step // n_chunks
    c = step - b * n_chunks
    copies = []
    for p in range(PAGES_PER_STEP):
        page = pt_ref[b, c * PAGES_PER_STEP + p]
        for n in range(PAGE_HALVES):
            dst = pl.multiple_of(slot * STAGE_ROWS + (p * PAGE_HALVES + n) * SLAB_PITCH, SUBLANES)
            copies.append(pltpu.make_async_copy(cache_ref.at[page, pl.ds(n * SLAB_ROWS, SLAB_ROWS), :],
                                                stage_ref.at[pl.ds(dst, SLAB_ROWS), :], sem_ref.at[slot]))
    return copies


def _sample_cmp_kernel(pt_ref, cache_ref, new_ref, q_ref, w1_ref, b1_ref, w2_ref, gain_ref, tabl_ref,
                       sel_ref, ocmp_ref, x2_ref, a_ref, stage_ref, sem_ref, *, past, t_new, n_blocks, sel_rows):
    c = pl.program_id(1)
    n_chunks = pl.num_programs(1)
    step = pl.program_id(0) * n_chunks + c
    slot = step % 2
    n_half = past // CMP_STRIDE
    step_halves = PAGES_PER_STEP * PAGE_HALVES
    per_kv = GROUP * t_new

    @pl.when(step == 0)
    def _():
        for cp in _stage_copies(pt_ref, cache_ref, stage_ref, sem_ref, step, slot, n_chunks):
            cp.start()

    @pl.when(step + 1 < pl.num_programs(0) * n_chunks)
    def _():
        for cp in _stage_copies(pt_ref, cache_ref, stage_ref, sem_ref, step + 1, 1 - slot, n_chunks):
            cp.start()

    for cp in _stage_copies(pt_ref, cache_ref, stage_ref, sem_ref, step, slot, n_chunks):
        cp.wait()

    base = slot * STAGE_ROWS
    for combo in range(TOKEN_ROWS):
        for p in range(PAGES_PER_STEP):
            for j in range(CMP_STRIDE):
                start = base + p * PAGE_HALVES * SLAB_PITCH + j * TOKEN_ROWS + combo
                x2_ref[combo, p * PAGE_HALVES:(p + 1) * PAGE_HALVES, j * HEAD_DIM:(j + 1) * HEAD_DIM] = (
                    stage_ref[pl.ds(start, PAGE_HALVES, stride=SLAB_PITCH), :])
        a = jnp.dot(x2_ref[combo].astype(BF16), w1_ref[combo // N_KV], preferred_element_type=F32)
        a_ref[combo, pl.ds(pl.multiple_of(c * step_halves, step_halves), step_halves), :] = a

    @pl.when(c == pl.num_programs(1) - 1)
    def _():
        tok = []
        for combo in range(TOKEN_ROWS):
            kind = combo // N_KV
            sl = slice(combo * HEAD_DIM, (combo + 1) * HEAD_DIM)
            row = jnp.concatenate([new_ref[j:j + 1, sl] for j in range(t_new)]
                                  + [jnp.zeros((1, (CMP_STRIDE - t_new) * HEAD_DIM), F32)], axis=1)
            tail = jnp.broadcast_to(row, (SUBLANES, CMP_STRIDE * HEAD_DIM)).astype(BF16)
            a_ref[combo, n_half:n_half + SUBLANES, :] = jnp.dot(tail, w1_ref[kind], preferred_element_type=F32)
            pre = a_ref[combo, 0:n_half, 0:CMP_HID] + a_ref[combo, pl.ds(1, n_half), CMP_HID:] + b1_ref[kind]
            out = jnp.dot(_silu(pre).astype(BF16), w2_ref[kind], preferred_element_type=F32)
            if kind == 0:
                out = _rms(out, gain_ref[...])
            tok.append(out.astype(BF16))

        qz = _lane_queries(q_ref[...], t_new)
        shape = (n_half, LANES)
        u = _lane_iota(shape) & (t_new - 1)
        rel = past + u - (_row_iota(shape) * CMP_STRIDE + CMP_BLOCK - 1)
        near0 = ((past - (CMP_BLOCK - 1) - FAR_DIST) // CMP_STRIDE + 1) // SUBLANES * SUBLANES
        bias = jnp.concatenate([jnp.broadcast_to(tabl_ref[NUM_BUCKETS - 1:NUM_BUCKETS, :], (near0, LANES)),
                                _lane_bias(rel[near0:], tabl_ref)], axis=0)
        s = jnp.where(rel >= 0, _lane_scores(tok[:N_KV], qz) + bias, -jnp.inf)
        m = jnp.max(s, axis=0, keepdims=True)
        e = jnp.exp(s - jnp.where(m == -jnp.inf, 0.0, m))
        p = e / jnp.maximum(jnp.sum(e, axis=0, keepdims=True), 1e-30)
        ocmp_ref[...] = _lane_values(tok[N_KV:], p, per_kv)

        src = _row_iota((LANES, LANES))
        dst = _lane_iota((LANES, LANES))
        pooled = (src // per_kv) * t_new + (src & (t_new - 1))
        pool = jnp.where((src < N_KV * per_kv) & (dst == pooled), 1.0, 0.0).astype(BF16)
        hi = p.astype(BF16)
        lo = (p - hi.astype(F32)).astype(BF16)
        imp = jnp.dot(hi, pool, preferred_element_type=F32) + jnp.dot(lo, pool, preferred_element_type=F32)
        si = _row_iota((sel_rows, n_half)) * SEL_BLOCK
        ci = _lane_iota((sel_rows, n_half)) * CMP_STRIDE
        overlap = (ci < si + SEL_BLOCK) & (ci + CMP_BLOCK > si) & (si < n_blocks * SEL_BLOCK)
        ov = jnp.where(overlap, 1.0, 0.0).astype(BF16)
        hi = imp.astype(BF16)
        lo = (imp - hi.astype(F32)).astype(BF16)
        imp = jnp.dot(ov, hi, preferred_element_type=F32) + jnp.dot(ov, lo, preferred_element_type=F32)

        cur = (past + (_lane_iota((sel_rows, LANES)) & (t_new - 1))) // SEL_BLOCK
        sel = _select_blocks_t(imp, cur)
        unpool = jnp.where((dst < N_KV * per_kv) & (src == (dst // per_kv) * t_new + (dst & (t_new - 1))), 1.0, 0.0)
        sel_ref[...] = jnp.dot(sel.astype(BF16), unpool.astype(BF16), preferred_element_type=F32)


def _page_specs(n_chunk_pages):
    def spec(j):
        return pl.BlockSpec((None, PAGE_ROWS, HEAD_DIM), lambda b, c, pt: (pt[b, c * n_chunk_pages + j], 0, 0))
    return [spec(j) for j in range(n_chunk_pages)]


def sample_compressed(page_table, cache3, cmp_new, qn, w1, b1, w2, kc_norm, tabl, t_new):
    n_seq, n_pages = page_table.shape
    assert n_pages % PAGES_PER_STEP == 0
    past = n_pages * PAGE
    n_half = past // CMP_STRIDE
    n_blocks = -(-(past + t_new) // SEL_BLOCK)
    sel_rows = -(-n_blocks // SUBLANES) * SUBLANES
    const = lambda *shape: pl.BlockSpec(shape, lambda b, c, pt: (0,) * len(shape))
    row = lambda width: pl.BlockSpec((t_new, width), lambda b, c, pt: (b, 0))
    out = lambda rows: pl.BlockSpec((None, rows, LANES), lambda b, c, pt: (b, 0, 0))
    grid_spec = pltpu.PrefetchScalarGridSpec(
        num_scalar_prefetch=1,
        grid=(n_seq, n_pages // PAGES_PER_STEP),
        in_specs=[pl.BlockSpec(memory_space=pl.ANY),
                  row(2 * KV_W), row(NSA_W), const(2, CMP_STRIDE * HEAD_DIM, 2 * CMP_HID), const(2, 1, CMP_HID),
                  const(2, CMP_HID, HEAD_DIM), const(1, HEAD_DIM), const(NUM_BUCKETS, LANES)],
        out_specs=[out(sel_rows), out(HEAD_DIM)],
        scratch_shapes=[pltpu.VMEM((TOKEN_ROWS, PAGES_PER_STEP * PAGE_HALVES, CMP_STRIDE * HEAD_DIM), F32),
                        pltpu.VMEM((TOKEN_ROWS, n_half + SUBLANES, 2 * CMP_HID), F32),
                        pltpu.VMEM((2 * STAGE_ROWS, HEAD_DIM), F32),
                        pltpu.SemaphoreType.DMA((2,))])
    return pl.pallas_call(
        functools.partial(_sample_cmp_kernel, past=past, t_new=t_new, n_blocks=n_blocks, sel_rows=sel_rows),
        grid_spec=grid_spec,
        out_shape=[jax.ShapeDtypeStruct((n_seq, sel_rows, LANES), F32),
                   jax.ShapeDtypeStruct((n_seq, HEAD_DIM, LANES), F32)],
        compiler_params=_cparams(("arbitrary", "arbitrary")),
        name="sample_cmp",
    )(page_table, cache3, cmp_new, qn, w1, b1, w2, kc_norm.reshape(1, HEAD_DIM), tabl)


def _sample_sel_kernel(pt_ref, *refs, past, t_new, win_buf):
    pages = refs[:PAGES_PER_STEP]
    (new_ref, q_ref, sel_ref, ocmp_ref, win_ref, wnew_ref, gate_ref, tabl_ref,
     o_ref, m_ref, l_ref, acc_ref, owin_ref, s_ref) = refs[PAGES_PER_STEP:]
    c = pl.program_id(1)
    per_kv = GROUP * t_new
    qz = _lane_queries(q_ref[...], t_new)
    far = tabl_ref[NUM_BUCKETS - 1:NUM_BUCKETS, :]
    new_rows = 2 * SUBLANES

    def new_kv(ref, kind):
        return [_pad_rows(ref[:, (kind * N_KV + kv) * HEAD_DIM:(kind * N_KV + kv + 1) * HEAD_DIM], new_rows).astype(BF16)
                for kv in range(N_KV)]

    def dist(pos0, rows):
        shape = (rows, LANES)
        return past + (_lane_iota(shape) & (t_new - 1)) - pos0 - _row_iota(shape)

    def update(s, values):
        m_prev = m_ref[...]
        m_new = jnp.maximum(m_prev, jnp.max(s, axis=0, keepdims=True))
        m_safe = jnp.where(m_new == -jnp.inf, 0.0, m_new)
        alpha = jnp.exp(m_prev - m_safe)
        p = jnp.exp(s - m_safe)
        l_ref[...] = alpha * l_ref[...] + jnp.sum(p, axis=0, keepdims=True)
        acc = alpha * acc_ref[...]
        for vals, lo, hi_ in values:
            acc = acc + _lane_values(vals, p[lo:hi_], per_kv)
        acc_ref[...] = acc
        m_ref[...] = m_new

    @pl.when(c == 0)
    def _():
        m_ref[...] = jnp.full(m_ref.shape, -jnp.inf, F32)
        l_ref[...] = jnp.zeros(l_ref.shape, F32)
        acc_ref[...] = jnp.zeros(acc_ref.shape, F32)

        kw = [win_ref[pl.ds(kv, win_buf, stride=TOKEN_ROWS), :].astype(BF16) for kv in range(N_KV)]
        vw = [win_ref[pl.ds(N_KV + kv, win_buf, stride=TOKEN_ROWS), :].astype(BF16) for kv in range(N_KV)]
        pieces = []
        near0 = (win_buf - FAR_DIST + 1) // SUBLANES * SUBLANES
        for keys, rel, near in ((kw, dist(past - win_buf, win_buf), near0), (new_kv(wnew_ref, 0), dist(past, new_rows), 0)):
            bias = _lane_bias(rel[near:], tabl_ref)
            if near:
                bias = jnp.concatenate([jnp.broadcast_to(far, (near, LANES)), bias], axis=0)
            s = _lane_scores(keys, qz) + bias
            pieces.append(jnp.where((rel >= 0) & (rel < WINDOW), s, -jnp.inf))
        m = jnp.maximum(jnp.max(pieces[0], axis=0, keepdims=True), jnp.max(pieces[1], axis=0, keepdims=True))
        m = jnp.where(m == -jnp.inf, 0.0, m)
        e0 = jnp.exp(pieces[0] - m)
        e1 = jnp.exp(pieces[1] - m)
        tot = jnp.sum(e0, axis=0, keepdims=True) + jnp.sum(e1, axis=0, keepdims=True)
        num = _lane_values(vw, e0, per_kv) + _lane_values(new_kv(wnew_ref, 1), e1, per_kv)
        owin_ref[...] = num / jnp.maximum(tot, 1e-30)

        rel = dist(past, new_rows)
        s = _lane_scores(new_kv(new_ref, 0), qz) + _lane_bias(rel, tabl_ref)
        chosen = sel_ref[past // SEL_BLOCK:past // SEL_BLOCK + 1, :] > 0.5
        update(jnp.where(chosen & (rel >= 0), s, -jnp.inf), [(new_kv(new_ref, 1), 0, new_rows)])

    values = []
    for p in range(PAGES_PER_STEP):
        page = c * PAGES_PER_STEP + p
        keys = [pages[p][pl.ds(kv, PAGE, stride=TOKEN_ROWS), :].astype(BF16) for kv in range(N_KV)]
        s = _lane_scores(keys, qz) + far
        per_block = PAGE // SEL_BLOCK
        chosen = jnp.concatenate(
            [jnp.broadcast_to(sel_ref[pl.ds(page * per_block + i, 1), :], (SEL_BLOCK, LANES)) for i in range(per_block)],
            axis=0)
        s_ref[p * PAGE:(p + 1) * PAGE, :] = jnp.where(chosen > 0.5, s, -jnp.inf)
        vals = [pages[p][pl.ds(N_KV + kv, PAGE, stride=TOKEN_ROWS), :].astype(BF16) for kv in range(N_KV)]
        values.append((vals, p * PAGE, (p + 1) * PAGE))

    @pl.when(c == pl.num_programs(1) - 1)
    def _():
        rows = slice((PAGES_PER_STEP - 1) * PAGE, PAGES_PER_STEP * PAGE)
        s_ref[rows, :] = s_ref[rows, :] + (_lane_bias(dist(past - PAGE, PAGE), tabl_ref) - far)

    update(s_ref[...], values)

    @pl.when(c == pl.num_programs(1) - 1)
    def _():
        gt = _sigmoid(gate_ref[...])
        osel = acc_ref[...] / jnp.maximum(l_ref[...], 1e-30)
        o_ref[...] = gt[0:1] * ocmp_ref[...] + gt[1:2] * osel + gt[2:3] * owin_ref[...]


def sample_selected(page_table, cache3, sel_new, qn, sel_t, ocmp_t, win3, win_new, gate_l, tabl, t_new):
    n_seq, n_pages = page_table.shape
    past = n_pages * PAGE
    win_buf = win3.shape[1] // TOKEN_ROWS
    assert (past - FAR_DIST) // PAGE >= n_pages - 1, "only the last page may hold near keys"
    sel_rows = sel_t.shape[1]
    const = lambda *shape: pl.BlockSpec(shape, lambda b, c, pt: (0,) * len(shape))
    row = lambda width: pl.BlockSpec((t_new, width), lambda b, c, pt: (b, 0))
    per_seq = lambda rows, width: pl.BlockSpec((None, rows, width), lambda b, c, pt: (b, 0, 0))
    grid_spec = pltpu.PrefetchScalarGridSpec(
        num_scalar_prefetch=1,
        grid=(n_seq, n_pages // PAGES_PER_STEP),
        in_specs=_page_specs(PAGES_PER_STEP) + [
            row(2 * KV_W), row(NSA_W), per_seq(sel_rows, LANES), per_seq(HEAD_DIM, LANES),
            per_seq(win_buf * TOKEN_ROWS, HEAD_DIM), row(2 * KV_W), per_seq(SUBLANES, LANES),
            const(NUM_BUCKETS, LANES)],
        out_specs=per_seq(HEAD_DIM, LANES),
        scratch_shapes=[pltpu.VMEM((1, LANES), F32), pltpu.VMEM((1, LANES), F32), pltpu.VMEM((HEAD_DIM, LANES), F32),
                        pltpu.VMEM((HEAD_DIM, LANES), F32), pltpu.VMEM((PAGES_PER_STEP * PAGE, LANES), F32)])
    return pl.pallas_call(
        functools.partial(_sample_sel_kernel, past=past, t_new=t_new, win_buf=win_buf),
        grid_spec=grid_spec,
        out_shape=jax.ShapeDtypeStruct((n_seq, HEAD_DIM, LANES), F32),
        compiler_params=_cparams(("parallel", "arbitrary")),
        name="sample_sel",
    )(page_table, *([cache3] * PAGES_PER_STEP), sel_new, qn, sel_t, ocmp_t, win3, win_new, gate_l, tabl)


def _split_in_proj(w_in):
    d = w_in.shape[0]
    assert w_in.shape[1] == COL_GATE + N_GATE + MERGE_COLS
    per_kv = GROUP * 3
    zeros = lambda n: jnp.zeros((d, n), BF16)
    parts = [w_in[:, :COL_GATE].astype(BF16)]
    for kv in range(N_KV):
        parts += [w_in[:, COL_GATE + kv * per_kv:COL_GATE + (kv + 1) * per_kv].astype(BF16), zeros(LANES - per_kv)]
    parts.append(zeros(GATE_COLS - N_KV * LANES))
    return jnp.concatenate(parts, axis=1), w_in[:, COL_GATE + N_GATE:].astype(BF16)


def _compress_weights(k_w1, k_b1, k_w2, v_w1, v_b1, v_w2):
    half = CMP_STRIDE * HEAD_DIM
    cat = lambda w: jnp.concatenate([w[:half], w[half:]], axis=1)
    w1 = jnp.stack([cat(k_w1), cat(v_w1)]).astype(BF16)
    b1 = jnp.stack([k_b1, v_b1]).reshape(2, 1, CMP_HID)
    w2 = jnp.stack([k_w2, v_w2]).astype(BF16)
    return w1, b1, w2


def _lane_table(rel_bias, t_new):
    tab = jnp.repeat(rel_bias, t_new, axis=1)
    return jnp.pad(tab, ((0, 0), (0, LANES - tab.shape[1])))


def _gate_lanes(proj, n_seq, t_new):
    per_kv = GROUP * 3
    g = jnp.stack([proj[:, COL_GATE + kv * LANES:COL_GATE + kv * LANES + per_kv] for kv in range(N_KV)], axis=1)
    g = g.reshape(n_seq, t_new, N_KV, GROUP, 3).transpose(0, 4, 2, 3, 1).reshape(n_seq, 3, N_HEADS_B * t_new)
    return jnp.pad(g, ((0, 0), (0, SUBLANES - 3), (0, LANES - N_HEADS_B * t_new)))


def _tiles(m, t_len):
    big = m >= 1024
    return dict(proj_tm=1024 if big else m, proj_tn=1024, prep_tm=512 if big else m, post_tm=256,
                mlp_tm=512 if big else m, mlp_tf=1024,
                hgrn_chunk=min(t_len, 256), hgrn_heads=2 if big else N_HEADS_A)


def _trunk(x, mods, s0, w, nsa_fn, attention_operands):
    nb, t_len, d = x.shape
    m = nb * t_len
    tl = _tiles(m, t_len)
    sh1, sc1, g1, sh2, sc2, g2 = mods
    x2 = x.reshape(m, d)
    proj, merge = in_proj(x2, w["norm1"], sc1, sh1, *w["w_in"], t_len, tl["proj_tm"], tl["proj_tn"])
    o_a, s_fin = hgrn(proj, w["lb"], w["hgrn_onorm"], s0, nb, t_len, tl["hgrn_chunk"], tl["hgrn_heads"])
    qn, cmp_new, sel_new, win_new, *att = nsa_prep(proj, w["q_norm"], w["ks_norm"], w["kw_norm"], tl["prep_tm"], t_len,
                                                   attention_operands)
    o_b = nsa_fn(proj, qn, cmp_new, sel_new, win_new, *att)
    x1, h2 = post_attn(x2, o_a, o_b, merge, w["w_ba"], w["w_bb"], w["w_out"], g1, w["norm2"], sc2, sh2, t_len,
                       tl["post_tm"])
    y = mlp(h2, x1, w["mlp_w1"], w["mlp_w2"], g2, t_len, tl["mlp_tm"], tl["mlp_tf"])
    return y.reshape(nb, t_len, d), s_fin, (cmp_new, sel_new, win_new)


def kernel(x_prompt, x_sample, c_prompt, c_sample, cache_cmp_kv, cache_sel_kv, cache_win_kv, state_hgrn, page_table, hgrn_lb_logits, rel_bias, ada_w, ada_b, norm1, norm2, w_in, hgrn_onorm, nsa_q_norm, nsa_kc_norm, nsa_ks_norm, nsa_kw_norm, cmp_k_w1, cmp_k_b1, cmp_k_w2, cmp_v_w1, cmp_v_b1, cmp_v_w2, w_branch_a, w_branch_b, w_out, mlp_w1, mlp_w2):
    n_p, t_p, d = x_prompt.shape
    n_s, t_s, _ = x_sample.shape
    past = page_table.shape[1] * PAGE
    win_buf = cache_win_kv.shape[2]
    kvw = 2 * KV_W
    layer = 0

    lb_all = jnp.cumsum(jax.nn.softmax(hgrn_lb_logits.astype(F32), axis=0), axis=0)
    cw1, cb1, cw2 = _compress_weights(cmp_k_w1[layer], cmp_k_b1[layer], cmp_k_w2[layer],
                                      cmp_v_w1[layer], cmp_v_b1[layer], cmp_v_w2[layer])
    w = dict(norm1=norm1[layer], norm2=norm2[layer], w_in=_split_in_proj(w_in[layer]),
             hgrn_onorm=hgrn_onorm[layer], lb=lb_all[layer], q_norm=nsa_q_norm[layer],
             ks_norm=nsa_ks_norm[layer], kw_norm=nsa_kw_norm[layer],
             w_ba=w_branch_a[layer].astype(BF16), w_bb=w_branch_b[layer].astype(BF16),
             w_out=w_out[layer].astype(BF16), mlp_w1=mlp_w1[layer].astype(BF16), mlp_w2=mlp_w2[layer].astype(BF16))
    tab = rel_bias.astype(F32).reshape(NUM_BUCKETS * N_HEADS_B)

    mods = ada_mods(jnp.concatenate([c_prompt, c_sample], axis=0).astype(F32), ada_w[layer], ada_b[layer])
    mods = jnp.split(mods, 6, axis=-1)
    mods_p = [a[:n_p].reshape(n_p, 1, d) for a in mods]
    mods_s = [jnp.repeat(a[n_p:], t_s, axis=0).reshape(1, n_s * t_s, d) for a in mods]

    def nsa_p(proj, qn, cmp_new, sel_new, win_new, att):
        tail = jnp.zeros((n_p * CMP_STRIDE, kvw), F32)
        cmp_tok = compress(cmp_new, tail, cw1, cb1, cw2, nsa_kc_norm[layer], n_p, t_p // CMP_STRIDE)
        return nsa_prompt_attention(tab, qn, cmp_tok, att, proj, n_p, t_p)

    s0_p = jnp.zeros((n_p, N_HEADS_A, HEAD_DIM, HEAD_DIM), F32)
    y_p, hg_p, (cmp_p, sel_p, win_p) = _trunk(x_prompt, mods_p, s0_p, w, nsa_p, True)

    def nsa_s(proj, qn, cmp_new, sel_new, win_new):
        n_phys = cache_cmp_kv.shape[1]
        cmp3 = cache_cmp_kv[layer].reshape(n_phys, PAGE_ROWS, HEAD_DIM)
        sel3 = cache_sel_kv[layer].reshape(n_phys, PAGE_ROWS, HEAD_DIM)
        win3 = cache_win_kv[layer].reshape(n_s, win_buf * TOKEN_ROWS, HEAD_DIM)
        tabl = _lane_table(rel_bias.astype(F32), t_s)
        sel_t, ocmp_t = sample_compressed(page_table, cmp3, cmp_new, qn, cw1, cb1, cw2, nsa_kc_norm[layer], tabl, t_s)
        o_t = sample_selected(page_table, sel3, sel_new, qn, sel_t, ocmp_t, win3, win_new,
                              _gate_lanes(proj, n_s, t_s), tabl, t_s)
        o = o_t[:, :, :N_HEADS_B * t_s].reshape(n_s, HEAD_DIM, N_HEADS_B, t_s)
        return o.transpose(0, 3, 2, 1).reshape(n_s * t_s, NSA_W).astype(BF16)

    y_s, hg_s, (cmp_s, sel_s, win_s) = _trunk(x_sample, mods_s, state_hgrn[layer], w, nsa_s, False)

    dt = x_prompt.dtype
    pages = lambda a: a.reshape(1, n_p, t_p // PAGE, PAGE, 2, N_KV, HEAD_DIM).astype(dt)
    rows_s = lambda a: a.reshape(1, n_s, t_s, 2, N_KV, HEAD_DIM).astype(dt)
    win_keep = min(WINDOW, t_p)
    win_p_out = win_p.reshape(n_p, t_p, kvw)[:, t_p - win_keep:].reshape(1, n_p, win_keep, 2, N_KV, HEAD_DIM)
    win_rows = cache_win_kv[layer].reshape(n_s, win_buf * TOKEN_ROWS, HEAD_DIM).astype(F32)
    win_s_out = jnp.concatenate([win_rows, win_s.reshape(n_s, t_s * TOKEN_ROWS, HEAD_DIM)], axis=1)
    win_s_out = win_s_out[:, t_s * TOKEN_ROWS:].reshape(1, n_s, win_buf, 2, N_KV, HEAD_DIM)
    return (y_p, y_s, pages(cmp_p), rows_s(cmp_s), pages(sel_p), rows_s(sel_s),
            win_p_out.astype(dt), win_s_out.astype(dt), hg_p[None].astype(dt), hg_s[None].astype(dt))
```

```python
import functools
import math

import numpy as np
import jax
import jax.numpy as jnp
from jax import lax
from jax.experimental import pallas as pl
from jax.experimental.pallas import tpu as pltpu

F32 = jnp.float32
BF16 = jnp.bfloat16

D_MODEL = 2048
N_HEADS_A = 8
HEAD_DIM = 128
HGRN_W = N_HEADS_A * HEAD_DIM
N_HEADS_B = 8
N_KV = 2
GROUP = N_HEADS_B // N_KV
NSA_W = N_HEADS_B * HEAD_DIM
KV_W = N_KV * HEAD_DIM
PAGE = 128
CMP_STRIDE = 16
CMP_BLOCK = 32
CMP_HID = 128
SEL_BLOCK = 64
N_SEL = 16
WINDOW = 512
FORCE_SCORE = 1.0e6
NUM_BUCKETS = 32
REL_MAX_DIST = 128
D_FF = 4 * D_MODEL
EPS = 1e-6
N_GATE = 3 * N_HEADS_B

LANES = 128
SUBLANES = 8
VMEM_LIMIT = 56 * 1024 * 1024

COL_HGRN = 0
COL_QB = 4 * HGRN_W
COL_CMP = COL_QB + NSA_W
COL_SEL = COL_CMP + 2 * KV_W
COL_WIN = COL_SEL + 2 * KV_W
COL_GATE = COL_WIN + 2 * KV_W
GATE_COLS = 512
PROJ_COLS = COL_GATE + GATE_COLS
MERGE_COLS = 2 * D_MODEL


def _cparams(sem):
    return pltpu.CompilerParams(dimension_semantics=sem, vmem_limit_bytes=VMEM_LIMIT)


def _sigmoid(x):
    return 1.0 / (1.0 + jnp.exp(-x))


def _silu(x):
    return x * _sigmoid(x)


def _rms(x, gain):
    return x * lax.rsqrt(jnp.mean(x * x, axis=-1, keepdims=True) + EPS) * gain


def _lane_iota(shape):
    return lax.broadcasted_iota(jnp.int32, shape, 1)


def _row_iota(shape):
    return lax.broadcasted_iota(jnp.int32, shape, 0)


def _ada_kernel(c_ref, w_ref, b_ref, o_ref):
    a = _silu(c_ref[...]).astype(BF16)
    o_ref[...] = jnp.dot(a, w_ref[...].astype(BF16), preferred_element_type=F32) + b_ref[...]


def ada_mods(c, w, b):
    r, d = c.shape
    n = w.shape[1]
    tn = 1024
    return pl.pallas_call(
        _ada_kernel,
        grid=(n // tn,),
        in_specs=[pl.BlockSpec((r, d), lambda j: (0, 0)),
                  pl.BlockSpec((d, tn), lambda j: (0, j)),
                  pl.BlockSpec((1, tn), lambda j: (0, j))],
        out_specs=pl.BlockSpec((r, tn), lambda j: (0, j)),
        out_shape=jax.ShapeDtypeStruct((r, n), F32),
        compiler_params=_cparams(("parallel",)),
        name="ada_mods",
    )(c, w, b.reshape(1, n))


def _inproj_kernel(x_ref, gain_ref, sc_ref, sh_ref, wa_ref, wb_ref, oa_ref, ob_ref, h_ref, *, n_a):
    j = pl.program_id(1)

    @pl.when(j == 0)
    def _():
        h = _rms(x_ref[...], gain_ref[...]) * (1.0 + sc_ref[0]) + sh_ref[0]
        h_ref[...] = h.astype(BF16)

    nt = (((1,), (1,)), ((), ()))

    @pl.when(j < n_a)
    def _():
        oa_ref[...] = lax.dot_general(h_ref[...], wa_ref[...], nt, preferred_element_type=F32)

    @pl.when(j >= n_a)
    def _():
        ob_ref[...] = lax.dot_general(h_ref[...], wb_ref[...], nt, preferred_element_type=F32)


def _mod_spec(mod, tm, rows_per_batch):
    d = mod.shape[-1]
    if mod.shape[1] == 1:
        return pl.BlockSpec((1, 1, d), lambda i, *_: ((i * tm) // rows_per_batch, 0, 0))
    return pl.BlockSpec((1, tm, d), lambda i, *_: (0, i, 0))


def in_proj(x2, gain, sc, sh, w_a, w_b, rows_per_batch, tm, tn):
    m, d = x2.shape
    n_a = w_a.shape[0] // tn
    n_b = w_b.shape[0] // tn
    col_a = lambda i, j: jnp.minimum(j, n_a - 1)
    col_b = lambda i, j: jnp.maximum(j - n_a, 0)
    return pl.pallas_call(
        functools.partial(_inproj_kernel, n_a=n_a),
        grid=(m // tm, n_a + n_b),
        in_specs=[pl.BlockSpec((tm, d), lambda i, j: (i, 0), pipeline_mode=pl.Buffered(1)),
                  pl.BlockSpec((1, d), lambda i, j: (0, 0)),
                  _mod_spec(sc, tm, rows_per_batch),
                  _mod_spec(sh, tm, rows_per_batch),
                  pl.BlockSpec((tn, d), lambda i, j: (col_a(i, j), 0)),
                  pl.BlockSpec((tn, d), lambda i, j: (col_b(i, j), 0))],
        out_specs=[pl.BlockSpec((tm, tn), lambda i, j: (i, col_a(i, j))),
                   pl.BlockSpec((tm, tn), lambda i, j: (i, col_b(i, j)))],
        out_shape=[jax.ShapeDtypeStruct((m, w_a.shape[0]), F32), jax.ShapeDtypeStruct((m, w_b.shape[0]), F32)],
        scratch_shapes=[pltpu.VMEM((tm, d), BF16)],
        compiler_params=_cparams(("parallel", "arbitrary")),
        name="in_proj",
    )(x2, gain.reshape(1, d), sc, sh, w_a, w_b)


def _post_kernel(x_ref, oa_ref, ob_ref, mga_ref, mgb_ref, wba_ref, wbb_ref, wout_ref,
                 g1_ref, gain2_ref, sc2_ref, sh2_ref, x1_ref, h2_ref):
    ya = jnp.dot(oa_ref[...], wba_ref[...], preferred_element_type=F32)
    yb = jnp.dot(ob_ref[...], wbb_ref[...], preferred_element_type=F32)
    merged = _sigmoid(mga_ref[...]) * ya + _sigmoid(mgb_ref[...]) * yb
    y = jnp.dot(merged.astype(BF16), wout_ref[...], preferred_element_type=F32)
    x1 = x_ref[...] + g1_ref[0] * y
    x1_ref[...] = x1
    h2_ref[...] = (_rms(x1, gain2_ref[...]) * (1.0 + sc2_ref[0]) + sh2_ref[0]).astype(BF16)


def post_attn(x2, oa, ob, proj, wba, wbb, wout, g1, gain2, sc2, sh2, rows_per_batch, tm):
    m, d = x2.shape
    const = lambda i: (0, 0)
    resident = lambda shape: pl.BlockSpec(shape, const, pipeline_mode=pl.Buffered(1))
    return pl.pallas_call(
        _post_kernel,
        grid=(m // tm,),
        in_specs=[pl.BlockSpec((tm, d), lambda i: (i, 0)),
                  pl.BlockSpec((tm, HGRN_W), lambda i: (i, 0)),
                  pl.BlockSpec((tm, NSA_W), lambda i: (i, 0)),
                  pl.BlockSpec((tm, d), lambda i: (i, 0)),
                  pl.BlockSpec((tm, d), lambda i: (i, 1)),
                  resident((HGRN_W, d)),
                  resident((NSA_W, d)),
                  resident((d, d)),
                  _mod_spec(g1, tm, rows_per_batch),
                  pl.BlockSpec((1, d), const),
                  _mod_spec(sc2, tm, rows_per_batch),
                  _mod_spec(sh2, tm, rows_per_batch)],
        out_specs=[pl.BlockSpec((tm, d), lambda i: (i, 0)),
                   pl.BlockSpec((tm, d), lambda i: (i, 0))],
        out_shape=[jax.ShapeDtypeStruct((m, d), F32), jax.ShapeDtypeStruct((m, d), BF16)],
        compiler_params=_cparams(("parallel",)),
        name="post_attn",
    )(x2, oa, ob, proj, proj, wba, wbb, wout, g1, gain2.reshape(1, d), sc2, sh2)


def _mlp_kernel(h_ref, x1_ref, w1_ref, w2_ref, g2_ref, y_ref, acc_ref):
    f = pl.program_id(1)
    u = jnp.maximum(jnp.dot(h_ref[...], w1_ref[...], preferred_element_type=F32), 0.0)
    part = jnp.dot((u * u).astype(BF16), w2_ref[...], preferred_element_type=F32)

    @pl.when(f == 0)
    def _():
        acc_ref[...] = part

    @pl.when(f > 0)
    def _():
        acc_ref[...] += part

    @pl.when(f == pl.num_programs(1) - 1)
    def _():
        y_ref[...] = x1_ref[...] + g2_ref[0] * acc_ref[...]


def mlp(h2, x1, w1, w2, g2, rows_per_batch, tm, tf):
    m, d = x1.shape
    ff = w1.shape[1]
    return pl.pallas_call(
        _mlp_kernel,
        grid=(m // tm, ff // tf),
        in_specs=[pl.BlockSpec((tm, d), lambda i, f: (i, 0)),
                  pl.BlockSpec((tm, d), lambda i, f: (i, 0)),
                  pl.BlockSpec((d, tf), lambda i, f: (0, f)),
                  pl.BlockSpec((tf, d), lambda i, f: (f, 0)),
                  _mod_spec(g2, tm, rows_per_batch)],
        out_specs=pl.BlockSpec((tm, d), lambda i, f: (i, 0)),
        out_shape=jax.ShapeDtypeStruct((m, d), F32),
        scratch_shapes=[pltpu.VMEM((tm, d), F32)],
        compiler_params=_cparams(("parallel", "arbitrary")),
        name="mlp",
    )(h2, x1, w1, w2, g2)


_NT = (((1,), (1,)), ((), ()))
_TN = (((0,), (0,)), ((), ()))


def _hgrn_kernel(q_ref, z_ref, v_ref, g_ref, lb_ref, on_ref, s0_ref, o_ref, sfin_ref, st_ref, b_ref, *, chunk, heads):
    for h in range(heads):
        _hgrn_head(h, q_ref, z_ref, v_ref, g_ref, lb_ref, on_ref, s0_ref, o_ref, sfin_ref, st_ref, b_ref, chunk)


def _hgrn_head(h, q_ref, z_ref, v_ref, g_ref, lb_ref, on_ref, s0_ref, o_ref, sfin_ref, st_ref, b_ref, chunk):
    c = pl.program_id(2)
    cols = slice(h * HEAD_DIM, (h + 1) * HEAD_DIM)

    @pl.when(c == 0)
    def _():
        st_ref[h] = s0_ref[0, h].T

    q = q_ref[:, cols]
    z = z_ref[:, cols]
    v = v_ref[:, cols]
    lb = lb_ref[h]
    e = jnp.exp(-jnp.abs(z))
    r = 1.0 / (1.0 + e)
    pos = z >= 0.0
    logf = jnp.log(lb + (1.0 - lb) * jnp.where(pos, r, e * r))
    k = (1.0 - lb) * jnp.where(pos, e * r, r)

    t = lax.broadcasted_iota(jnp.int32, (chunk, HEAD_DIM), 0)
    b = logf
    s = 1
    while s < chunk:
        b = b + jnp.where(t >= s, pltpu.roll(b, s, 0), 0.0)
        s *= 2
    b_ref[h] = b

    t8 = t & (SUBLANES - 1)
    if chunk <= SUBLANES:
        o = jnp.zeros((chunk, HEAD_DIM), F32)
        for d in range(SUBLANES):
            kd, bd, vd = (k, b, v) if d == 0 else (pltpu.roll(k, d, 0), pltpu.roll(b, d, 0), pltpu.roll(v, d, 0))
            w = jnp.exp(jnp.where(t8 >= d, b - bd, -jnp.inf))
            o = o + jnp.sum(q * kd * w, axis=-1, keepdims=True) * vd
    else:
        o = jnp.sum(q * k, axis=-1, keepdims=True) * v
        row = lax.broadcasted_iota(jnp.int32, (chunk, chunk), 0)
        col = lax.broadcasted_iota(jnp.int32, (chunk, chunk), 1)
        apart = row ^ col
        att = jnp.zeros((chunk, chunk), F32)
        sub = _row_iota((SUBLANES, HEAD_DIM))
        m = 1
        while m < chunk:
            def ref_row(r, rows):
                return jnp.broadcast_to(b_ref[h, pl.ds(r, 1), :], (rows, HEAD_DIM))

            if 2 * m >= SUBLANES:
                refs = [ref_row(blk * 2 * m + m - 1, 2 * m) for blk in range(chunk // (2 * m))]
            else:
                refs = []
                for tile in range(chunk // SUBLANES):
                    ref_t = ref_row(tile * SUBLANES + m - 1, SUBLANES)
                    for j in range(1, SUBLANES // (2 * m)):
                        ref_t = jnp.where(sub >= j * 2 * m, ref_row(tile * SUBLANES + j * 2 * m + m - 1, SUBLANES), ref_t)
                    refs.append(ref_t)
            ref_b = refs[0] if len(refs) == 1 else jnp.concatenate(refs, axis=0)
            second = (t & (2 * m - 1)) >= m
            w = jnp.exp(jnp.where(second, b - ref_b, ref_b - b))
            qs = jnp.where(second, q * w, 0.0).astype(BF16)
            ks = jnp.where(second, 0.0, k * w).astype(BF16)
            a_m = lax.dot_general(qs, ks, _NT, preferred_element_type=F32)
            att = att + jnp.where(apart < 2 * m, a_m, 0.0)
            m *= 2
        o = o + jnp.dot(att.astype(BF16), v.astype(BF16), preferred_element_type=F32)

    st = st_ref[h]
    b_last = b_ref[h, pl.ds(chunk - 1, 1), :]
    o = o + lax.dot_general((q * jnp.exp(b)).astype(BF16), st.astype(BF16), _NT, preferred_element_type=F32)
    kt = (k * jnp.exp(b_last - b)).astype(BF16)
    vb = v.astype(BF16)
    if chunk < 2 * SUBLANES:
        pad = jnp.zeros((2 * SUBLANES - chunk, HEAD_DIM), BF16)
        kt = jnp.concatenate([kt, pad], axis=0)
        vb = jnp.concatenate([vb, pad], axis=0)
    st_new = jnp.exp(b_last) * st + lax.dot_general(vb, kt, _TN, preferred_element_type=F32)
    st_ref[h] = st_new

    o_ref[:, cols] = (_rms(o, on_ref[h]) * _silu(g_ref[:, cols])).astype(o_ref.dtype)

    @pl.when(c == pl.num_programs(2) - 1)
    def _():
        sfin_ref[0, h] = st_new.T


def hgrn(proj, lb, onorm, s0, n_batch, t_len, chunk, heads):
    m = proj.shape[0]
    n_c = t_len // chunk
    hb = N_HEADS_A // heads
    width = heads * HEAD_DIM

    def col(group):
        return pl.BlockSpec((chunk, width), lambda bi, h, c: (bi * n_c + c, group * hb + h))

    vec = pl.BlockSpec((heads, 1, HEAD_DIM), lambda bi, h, c: (h, 0, 0))
    state = pl.BlockSpec((1, heads, HEAD_DIM, HEAD_DIM), lambda bi, h, c: (bi, h, 0, 0))
    o_dtype = BF16 if chunk % (2 * SUBLANES) == 0 else F32
    o, s_fin = pl.pallas_call(
        functools.partial(_hgrn_kernel, chunk=chunk, heads=heads),
        grid=(n_batch, hb, n_c),
        in_specs=[col(0), col(1), col(2), col(3), vec, vec, state],
        out_specs=[pl.BlockSpec((chunk, width), lambda bi, h, c: (bi * n_c + c, h)), state],
        out_shape=[jax.ShapeDtypeStruct((m, HGRN_W), o_dtype),
                   jax.ShapeDtypeStruct((n_batch, N_HEADS_A, HEAD_DIM, HEAD_DIM), F32)],
        scratch_shapes=[pltpu.VMEM((heads, HEAD_DIM, HEAD_DIM), F32), pltpu.VMEM((heads, chunk, HEAD_DIM), F32)],
        compiler_params=_cparams(("parallel", "parallel", "arbitrary")),
        name="hgrn",
    )(proj, proj, proj, proj, lb.reshape(N_HEADS_A, 1, HEAD_DIM), onorm.reshape(N_HEADS_A, 1, HEAD_DIM), s0)
    return o.astype(BF16), s_fin


def _bucket_steps():
    n = np.arange(REL_MAX_DIST)
    exact = NUM_BUCKETS // 2
    val = np.log(np.maximum(n, 1) / exact) / math.log(REL_MAX_DIST / exact) * (NUM_BUCKETS - exact)
    frac = np.abs(val - np.round(val))[exact + 1:]
    assert frac.min() > 1e-3, "a bucket edge sits on an integer distance"
    lut = np.where(n < exact, n, np.minimum(exact + np.floor(np.maximum(val, 0.0)).astype(np.int64), NUM_BUCKETS - 1))
    assert lut[-1] == NUM_BUCKETS - 1
    return int(lut[0]), [(int(i), int(lut[i])) for i in range(1, REL_MAX_DIST) if lut[i] != lut[i - 1]]


_BUCKET0, _BUCKET_EDGES = _bucket_steps()
FAR_DIST = _BUCKET_EDGES[-1][0]


def _bias(rel, tab_ref, head):
    val = jnp.full(rel.shape, tab_ref[_BUCKET0 * N_HEADS_B + head], F32)
    for edge, bucket in _BUCKET_EDGES:
        val = jnp.where(rel >= edge, tab_ref[bucket * N_HEADS_B + head], val)
    return val


def _stack_heads(x):
    return jnp.concatenate([x[:, g * HEAD_DIM:(g + 1) * HEAD_DIM] for g in range(GROUP)], axis=0)


def _tile_heads(x):
    return jnp.concatenate([x] * GROUP, axis=0)


def _masked_softmax(s, mask):
    s = jnp.where(mask, s, -jnp.inf)
    m = jnp.max(s, axis=-1, keepdims=True)
    m = jnp.where(m == -jnp.inf, 0.0, m)
    e = jnp.exp(s - m)
    return e / jnp.maximum(jnp.sum(e, axis=-1, keepdims=True), 1e-30)


def _block_importance(p, n_cmp, n_blocks, width):
    rows = p.shape[0] // GROUP
    imp = p[0:rows]
    for g in range(1, GROUP):
        imp = imp + p[g * rows:(g + 1) * rows]
    ci = lax.broadcasted_iota(jnp.int32, (n_cmp, width), 0) * CMP_STRIDE
    si = lax.broadcasted_iota(jnp.int32, (n_cmp, width), 1) * SEL_BLOCK
    overlap = (ci < si + SEL_BLOCK) & (ci + CMP_BLOCK > si) & (si < n_blocks * SEL_BLOCK)
    ov = jnp.where(overlap, 1.0, 0.0).astype(BF16)
    hi = imp.astype(BF16)
    lo = (imp - hi.astype(F32)).astype(BF16)
    return jnp.dot(hi, ov, preferred_element_type=F32) + jnp.dot(lo, ov, preferred_element_type=F32)


def _select_blocks_t(imp_t, cur):
    blk = _row_iota(imp_t.shape)
    forced = (blk == 0) | (blk == cur) | (blk == cur - 1)
    score = jnp.where(blk <= cur, imp_t + jnp.where(forced, FORCE_SCORE, 0.0), -jnp.inf)
    sel = jnp.zeros(imp_t.shape, F32)
    n_rows = imp_t.shape[0]
    for _ in range(N_SEL):
        top = jnp.max(score, axis=0, keepdims=True)
        first = jnp.min(jnp.where(score == top, blk, n_rows), axis=0, keepdims=True)
        pick = blk == first
        sel = jnp.where(pick, 1.0, sel)
        score = jnp.where(pick, -jnp.inf, score)
    return sel


def _gate_mix(gate, parts):
    rows = gate.shape[0]
    gt = _sigmoid(gate)
    outs = []
    for g in range(GROUP):
        o = gt[:, 3 * g:3 * g + 1] * parts[0][g * rows:(g + 1) * rows]
        for j in (1, 2):
            o = o + gt[:, 3 * g + j:3 * g + j + 1] * parts[j][g * rows:(g + 1) * rows]
        outs.append(o)
    return jnp.concatenate(outs, axis=1)


ATT_KSA = 0
ATT_VS = N_KV * 2 * HEAD_DIM
ATT_KW = ATT_VS + KV_W
ATT_VW = ATT_KW + KV_W
ATT_COLS = ATT_VW + KV_W


def _nsa_prep_kernel(q_ref, cmp_ref, sel_ref, win_ref, qn_ref, ksn_ref, kwn_ref, qo_ref, co_ref, so_ref, wo_ref,
                     *att_ref, t_len):
    q = q_ref[...]
    scale = HEAD_DIM ** -0.5
    for h in range(N_HEADS_B):
        sl = slice(h * HEAD_DIM, (h + 1) * HEAD_DIM)
        qo_ref[:, sl] = _rms(q[:, sl], qn_ref[...]) * scale
    co_ref[...] = cmp_ref[...]
    for src, gain, dst in ((sel_ref, ksn_ref, so_ref), (win_ref, kwn_ref, wo_ref)):
        x = src[...]
        for h in range(N_KV):
            sl = slice(h * HEAD_DIM, (h + 1) * HEAD_DIM)
            dst[:, sl] = _rms(x[:, sl], gain[...])
        dst[:, KV_W:] = x[:, KV_W:]
    if att_ref:
        att_ref, crow_ref, srow_ref = att_ref
        tm = q.shape[0]
        for combo in range(TOKEN_ROWS):
            sl = slice(combo * HEAD_DIM, (combo + 1) * HEAD_DIM)
            crow_ref[pl.ds(combo, tm, stride=TOKEN_ROWS), :] = co_ref[:, sl]
            srow_ref[pl.ds(combo, tm, stride=TOKEN_ROWS), :] = so_ref[:, sl]
        pos = (pl.program_id(0) * tm + _row_iota((tm, LANES))) % t_len
        onehot = jnp.where(_lane_iota((tm, LANES)) == pos // SEL_BLOCK, 1.0, 0.0).astype(BF16)
        for h in range(N_KV):
            att_ref[:, ATT_KSA + 2 * h * HEAD_DIM:ATT_KSA + (2 * h + 1) * HEAD_DIM] = (
                so_ref[:, h * HEAD_DIM:(h + 1) * HEAD_DIM].astype(BF16))
            att_ref[:, ATT_KSA + (2 * h + 1) * HEAD_DIM:ATT_KSA + (2 * h + 2) * HEAD_DIM] = onehot
        att_ref[:, ATT_VS:ATT_KW] = so_ref[:, KV_W:].astype(BF16)
        att_ref[:, ATT_KW:ATT_VW] = wo_ref[:, :KV_W].astype(BF16)
        att_ref[:, ATT_VW:ATT_COLS] = wo_ref[:, KV_W:].astype(BF16)


def nsa_prep(proj, q_norm, ks_norm, kw_norm, tm, t_len, attention_operands):
    m = proj.shape[0]
    kvw = 2 * KV_W
    vec = pl.BlockSpec((1, HEAD_DIM), lambda i: (0, 0))
    widths = [NSA_W, kvw, kvw, kvw] + ([ATT_COLS] if attention_operands else [])
    dtypes = [F32] * 4 + ([BF16] if attention_operands else [])
    out_specs = [pl.BlockSpec((tm, wd), lambda i: (i, 0)) for wd in widths]
    out_shape = [jax.ShapeDtypeStruct((m, wd), dt) for wd, dt in zip(widths, dtypes)]
    if attention_operands:
        assert t_len // SEL_BLOCK <= LANES
        out_specs += [pl.BlockSpec((tm * TOKEN_ROWS, HEAD_DIM), lambda i: (i, 0))] * 2
        out_shape += [jax.ShapeDtypeStruct((m * TOKEN_ROWS, HEAD_DIM), F32)] * 2
    return pl.pallas_call(
        functools.partial(_nsa_prep_kernel, t_len=t_len),
        grid=(m // tm,),
        in_specs=[pl.BlockSpec((tm, NSA_W), lambda i: (i, COL_QB // NSA_W)),
                  pl.BlockSpec((tm, kvw), lambda i: (i, COL_CMP // kvw)),
                  pl.BlockSpec((tm, kvw), lambda i: (i, COL_SEL // kvw)),
                  pl.BlockSpec((tm, kvw), lambda i: (i, COL_WIN // kvw)),
                  vec, vec, vec],
        out_specs=out_specs,
        out_shape=out_shape,
        compiler_params=_cparams(("parallel",)),
        name="nsa_prep",
    )(proj, proj, proj, proj, q_norm.reshape(1, HEAD_DIM), ks_norm.reshape(1, HEAD_DIM), kw_norm.reshape(1, HEAD_DIM))


def _compress_kernel(x_ref, tail_ref, w1_ref, b1_ref, w2_ref, gain_ref, o_ref, x2_ref, a1_ref, *, n_half):
    for j in range(CMP_STRIDE):
        x2_ref[0:n_half, j * HEAD_DIM:(j + 1) * HEAD_DIM] = x_ref[pl.ds(j, n_half, stride=CMP_STRIDE), :]
        x2_ref[n_half:n_half + SUBLANES, j * HEAD_DIM:(j + 1) * HEAD_DIM] = jnp.broadcast_to(
            tail_ref[pl.ds(j, 1), :], (SUBLANES, HEAD_DIM))
    a = jnp.dot(x2_ref[...].astype(BF16), w1_ref[0], preferred_element_type=F32)
    a1_ref[...] = a[:, CMP_HID:]
    pre = a[0:n_half, :CMP_HID] + a1_ref[pl.ds(1, n_half), :] + b1_ref[0]
    out = jnp.dot(_silu(pre).astype(BF16), w2_ref[0], preferred_element_type=F32)
    is_k = pl.program_id(1) < N_KV
    o_ref[0, 0] = jnp.where(is_k, _rms(out, gain_ref[...]), out).astype(BF16)


def compress(raw, tail, w1, b1, w2, kc_norm, n_batch, n_half):
    t_len = n_half * CMP_STRIDE
    wsel = lambda b, c: (c // N_KV, 0, 0)
    return pl.pallas_call(
        functools.partial(_compress_kernel, n_half=n_half),
        grid=(n_batch, 2 * N_KV),
        in_specs=[pl.BlockSpec((t_len, HEAD_DIM), lambda b, c: (b, c)),
                  pl.BlockSpec((CMP_STRIDE, HEAD_DIM), lambda b, c: (b, c)),
                  pl.BlockSpec((1, CMP_STRIDE * HEAD_DIM, 2 * CMP_HID), wsel),
                  pl.BlockSpec((1, 1, CMP_HID), wsel),
                  pl.BlockSpec((1, CMP_HID, HEAD_DIM), wsel),
                  pl.BlockSpec((1, HEAD_DIM), lambda b, c: (0, 0))],
        out_specs=pl.BlockSpec((1, 1, n_half, HEAD_DIM), lambda b, c: (b, c, 0, 0)),
        out_shape=jax.ShapeDtypeStruct((n_batch, 2 * N_KV, n_half, HEAD_DIM), BF16),
        scratch_shapes=[pltpu.VMEM((n_half + SUBLANES, CMP_STRIDE * HEAD_DIM), F32),
                        pltpu.VMEM((n_half + SUBLANES, CMP_HID), F32)],
        compiler_params=_cparams(("parallel", "parallel")),
        name="compress",
    )(raw, tail, w1, b1, w2, kc_norm.reshape(1, HEAD_DIM))


TQ = 128
TK = 256
STRIP_W = TQ + 2 * TK
STRIP_ORIGIN = STRIP_W - TK
MASK_OFF = 1 << 20
BAND_ORIGIN = 64


UNSELECTED = -2.0 ** 30


def _nsa_prompt_kernel(tab_ref, q_ref, kc_ref, vc_ref, ksa_ref, vs_ref, kw_ref, vw_ref, gate_ref, o_ref,
                       strip_ref, band_ref, s_ref, wide_ref, acc_ref, *, n_half, n_blocks):
    kv = pl.program_id(1)
    n = pl.program_id(2)
    head0 = kv * GROUP
    q0 = n * TQ
    jd = n // 2
    odd = n - 2 * jd
    rows = GROUP * TQ
    far = [tab_ref[(NUM_BUCKETS - 1) * N_HEADS_B + head0 + g] for g in range(GROUP)]

    @pl.when(n == 0)
    def _():
        a = _row_iota((TQ, STRIP_W))
        u = _lane_iota((TQ, STRIP_W))
        for g in range(GROUP):
            strip_ref[g] = _bias(a + STRIP_ORIGIN - u, tab_ref, head0 + g) - far[g]
        rel = _row_iota((TQ, n_half)) - ((_lane_iota((TQ, n_half)) - BAND_ORIGIN) * CMP_STRIDE + CMP_BLOCK - 1)
        for g in range(GROUP):
            band_ref[g] = jnp.where(rel >= 0, _bias(rel, tab_ref, head0 + g) - far[g], 0.0)

    q = _stack_heads(q_ref[...])
    qs = q.astype(BF16)
    a_k = _row_iota((TQ, TK))
    c_k = _lane_iota((TQ, TK))

    def tile_start(jj):
        return pl.multiple_of(jnp.maximum(jj, 0) * TK, TK)

    def near_bias(d):
        u0 = pl.multiple_of(STRIP_ORIGIN - d * TK - odd * TQ, TQ)
        return jnp.concatenate([strip_ref[g, :, pl.ds(u0, TK)] for g in range(GROUP)], axis=0)

    def finish(n_tiles, v_ref):
        m = jnp.max(wide_ref[...], axis=-1, keepdims=True)
        m = jnp.where(m == -jnp.inf, 0.0, m)
        wide_ref[...] = jnp.zeros(wide_ref.shape, F32)
        acc_ref[...] = jnp.zeros(acc_ref.shape, F32)

        def one(jj):
            p = jnp.exp(s_ref[jj] - m)
            wide_ref[...] += p
            acc_ref[...] += jnp.dot(p.astype(BF16), v_ref[pl.ds(tile_start(jj), TK), :], preferred_element_type=F32)

        def pair(t, carry):
            one(2 * t)
            one(2 * t + 1)
            return carry

        lax.fori_loop(0, n_tiles // 2, pair, 0)

        @pl.when(n_tiles % 2 == 1)
        def _():
            one(n_tiles - 1)

        return acc_ref[...] / jnp.maximum(jnp.sum(wide_ref[...], axis=-1, keepdims=True), 1e-30)

    a_c = _row_iota((TQ, n_half))
    rel_c = q0 + a_c - (_lane_iota((TQ, n_half)) * CMP_STRIDE + CMP_BLOCK - 1)
    shift = (n * (TQ // CMP_STRIDE) - BAND_ORIGIN) % n_half
    bias_c = jnp.concatenate([pltpu.roll(band_ref[g], shift, 1) for g in range(GROUP)], axis=0)
    s = lax.dot_general(qs, kc_ref[0, 0], _NT, preferred_element_type=F32)
    p = _masked_softmax(s + bias_c, _tile_heads(rel_c) >= 0)
    o_cmp = jnp.dot(p.astype(BF16), vc_ref[0, 0], preferred_element_type=F32)
    imp = _block_importance(p, n_half, n_blocks, LANES)

    lead = _tile_heads(a_k - c_k) + odd * TQ
    causal = lead >= 0
    for d in range(3):
        k = kw_ref[pl.ds(tile_start(jd - d), TK), :]
        s = lax.dot_general(qs, k, _NT, preferred_element_type=F32)
        if d == 0:
            s = jnp.where(causal, s + near_bias(0), -jnp.inf)
        elif d == 1:
            s = s + near_bias(1) + jnp.where(jd >= 1, 0.0, -jnp.inf)
        else:
            s = jnp.where(lead < jnp.where(jd >= 2, 0, -MASK_OFF), s, -jnp.inf)
        s_ref[d] = s
        wide_ref[...] = s if d == 0 else jnp.maximum(wide_ref[...], s)
    m = jnp.max(wide_ref[...], axis=-1, keepdims=True)
    m = jnp.where(m == -jnp.inf, 0.0, m)
    num = jnp.zeros((rows, HEAD_DIM), F32)
    den = jnp.zeros((rows, TK), F32)
    for d in range(3):
        e = jnp.exp(s_ref[d] - m)
        den = den + e
        num = num + jnp.dot(e.astype(BF16), vw_ref[pl.ds(tile_start(jd - d), TK), :], preferred_element_type=F32)
    o_win = num / jnp.maximum(jnp.sum(den, axis=-1, keepdims=True), 1e-30)

    cur_t = (q0 + _lane_iota((LANES, TQ))) // SEL_BLOCK
    unsel = ((_select_blocks_t(imp.T, cur_t) - 1.0) * -UNSELECTED).T
    qa = jnp.concatenate([q, _tile_heads(unsel)], axis=1).astype(BF16)

    def scores(jj):
        return lax.dot_general(qa, ksa_ref[pl.ds(tile_start(jj), TK), :], _NT, preferred_element_type=F32)

    s = jnp.where(causal, scores(jd) + near_bias(0), -jnp.inf)
    s_ref[jd] = s
    wide_ref[...] = s

    @pl.when(jd >= 1)
    def _():
        s = scores(jd - 1) + near_bias(1)
        s_ref[jd - 1] = s
        wide_ref[...] = jnp.maximum(wide_ref[...], s)

    n_far = jnp.maximum(jd - 1, 0)

    def far_pair(t, carry):
        for jj in (2 * t, jnp.minimum(2 * t + 1, n_far - 1)):
            s = scores(jj)
            s_ref[jj] = s
            wide_ref[...] = jnp.maximum(wide_ref[...], s)
        return carry

    lax.fori_loop(0, (n_far + 1) // 2, far_pair, 0)
    o_sel = finish(jd + 1, vs_ref)

    o_ref[...] = _gate_mix(gate_ref[...], (o_cmp, o_sel, o_win)).astype(o_ref.dtype)


def nsa_prompt_attention(tab, qn, cmp_tok, att, proj, n_batch, t_len):
    m = qn.shape[0]
    n_q = t_len // TQ
    n_half = cmp_tok.shape[2]
    gw = GROUP * HEAD_DIM
    rows = GROUP * TQ
    seq = lambda col, width: pl.BlockSpec((t_len, width), lambda b, kv, n: (b, col // width + kv))
    tok = lambda off: pl.BlockSpec((1, 1, n_half, HEAD_DIM), lambda b, kv, n: (b, off + kv, 0, 0))
    return pl.pallas_call(
        functools.partial(_nsa_prompt_kernel, n_half=n_half, n_blocks=t_len // SEL_BLOCK),
        grid=(n_batch, N_KV, n_q),
        in_specs=[pl.BlockSpec(memory_space=pltpu.SMEM),
                  pl.BlockSpec((TQ, gw), lambda b, kv, n: (b * n_q + n, kv)),
                  tok(0), tok(N_KV), seq(ATT_KSA, 2 * HEAD_DIM), seq(ATT_VS, HEAD_DIM), seq(ATT_KW, HEAD_DIM),
                  seq(ATT_VW, HEAD_DIM),
                  pl.BlockSpec((TQ, LANES), lambda b, kv, n: (b * n_q + n, COL_GATE // LANES + kv))],
        out_specs=pl.BlockSpec((TQ, gw), lambda b, kv, n: (b * n_q + n, kv)),
        out_shape=jax.ShapeDtypeStruct((m, NSA_W), BF16),
        scratch_shapes=[pltpu.VMEM((GROUP, TQ, STRIP_W), F32), pltpu.VMEM((GROUP, TQ, n_half), F32),
                        pltpu.VMEM((t_len // TK, rows, TK), F32),
                        pltpu.VMEM((rows, TK), F32), pltpu.VMEM((rows, HEAD_DIM), F32)],
        compiler_params=_cparams(("parallel", "parallel", "arbitrary")),
        name="nsa_prompt",
    )(tab, qn, cmp_tok, cmp_tok, att, att, att, att, proj)


PAGES_PER_STEP = 32
TOKEN_ROWS = 2 * N_KV
PAGE_ROWS = PAGE * TOKEN_ROWS


def _lane_queries(q, t_new):
    per_kv = GROUP * t_new
    stacked = jnp.concatenate([q[:, h * HEAD_DIM:(h + 1) * HEAD_DIM] for h in range(N_HEADS_B)], axis=0)
    out = []
    for kv in range(N_KV):
        parts = []
        if kv:
            parts.append(jnp.zeros((kv * per_kv, HEAD_DIM), F32))
        parts.append(stacked[kv * per_kv:(kv + 1) * per_kv])
        parts.append(jnp.zeros((LANES - (kv + 1) * per_kv, HEAD_DIM), F32))
        out.append(jnp.concatenate(parts, axis=0).astype(BF16))
    return out


def _lane_scores(keys, qz):
    s = lax.dot_general(keys[0], qz[0], _NT, preferred_element_type=F32)
    for kv in range(1, N_KV):
        s = s + lax.dot_general(keys[kv], qz[kv], _NT, preferred_element_type=F32)
    return s


def _lane_values(vals, p, per_kv):
    pb = p.astype(BF16)
    lane = _lane_iota((HEAD_DIM, LANES))
    out = lax.dot_general(vals[N_KV - 1], pb, _TN, preferred_element_type=F32)
    for kv in range(N_KV - 2, -1, -1):
        out = jnp.where(lane < (kv + 1) * per_kv, lax.dot_general(vals[kv], pb, _TN, preferred_element_type=F32), out)
    return out


def _lane_bias(rel, tabl_ref):
    val = jnp.broadcast_to(tabl_ref[_BUCKET0:_BUCKET0 + 1, :], rel.shape)
    for edge, bucket in _BUCKET_EDGES:
        val = jnp.where(rel >= edge, tabl_ref[bucket:bucket + 1, :], val)
    return val


def _pad_rows(x, rows):
    return jnp.concatenate([x, jnp.zeros((rows - x.shape[0], x.shape[1]), x.dtype)], axis=0)


PAGE_HALVES = PAGE // CMP_STRIDE
SLAB_ROWS = CMP_STRIDE * TOKEN_ROWS
SLAB_PITCH = SLAB_ROWS + SUBLANES
STAGE_ROWS = PAGES_PER_STEP * PAGE_HALVES * SLAB_PITCH


def _stage_copies(pt_ref, cache_ref, stage_ref, sem_ref, step, slot, n_chunks):
    b = step // n_chunks
    c = step - b * n_chunks
    copies = []
    for p in range(PAGES_PER_STEP):
        page = pt_ref[b, c * PAGES_PER_STEP + p]
        for n in range(PAGE_HALVES):
            dst = pl.multiple_of(slot * STAGE_ROWS + (p * PAGE_HALVES + n) * SLAB_PITCH, SUBLANES)
            copies.append(pltpu.make_async_copy(cache_ref.at[page, pl.ds(n * SLAB_ROWS, SLAB_ROWS), :],
                                                stage_ref.at[pl.ds(dst, SLAB_ROWS), :], sem_ref.at[slot]))
    return copies


def _sample_cmp_kernel(pt_ref, cache_ref, new_ref, q_ref, w1_ref, b1_ref, w2_ref, gain_ref, tabl_ref,
                       sel_ref, ocmp_ref, x2_ref, a_ref, stage_ref, sem_ref, *, past, t_new, n_blocks, sel_rows):
    c = pl.program_id(1)
    n_chunks = pl.num_programs(1)
    step = pl.program_id(0) * n_chunks + c
    slot = step % 2
    n_half = past // CMP_STRIDE
    step_halves = PAGES_PER_STEP * PAGE_HALVES
    per_kv = GROUP * t_new

    @pl.when(step == 0)
    def _():
        for cp in _stage_copies(pt_ref, cache_ref, stage_ref, sem_ref, step, slot, n_chunks):
            cp.start()

    @pl.when(step + 1 < pl.num_programs(0) * n_chunks)
    def _():
        for cp in _stage_copies(pt_ref, cache_ref, stage_ref, sem_ref, step + 1, 1 - slot, n_chunks):
            cp.start()

    for cp in _stage_copies(pt_ref, cache_ref, stage_ref, sem_ref, step, slot, n_chunks):
        cp.wait()

    base = slot * STAGE_ROWS
    for combo in range(TOKEN_ROWS):
        for p in range(PAGES_PER_STEP):
            for j in range(CMP_STRIDE):
                start = base + p * PAGE_HALVES * SLAB_PITCH + j * TOKEN_ROWS + combo
                x2_ref[combo, p * PAGE_HALVES:(p + 1) * PAGE_HALVES, j * HEAD_DIM:(j + 1) * HEAD_DIM] = (
                    stage_ref[pl.ds(start, PAGE_HALVES, stride=SLAB_PITCH), :])
        a = jnp.dot(x2_ref[combo].astype(BF16), w1_ref[combo // N_KV], preferred_element_type=F32)
        a_ref[combo, pl.ds(pl.multiple_of(c * step_halves, step_halves), step_halves), :] = a

    @pl.when(c == pl.num_programs(1) - 1)
    def _():
        tok = []
        for combo in range(TOKEN_ROWS):
            kind = combo // N_KV
            sl = slice(combo * HEAD_DIM, (combo + 1) * HEAD_DIM)
            row = jnp.concatenate([new_ref[j:j + 1, sl] for j in range(t_new)]
                                  + [jnp.zeros((1, (CMP_STRIDE - t_new) * HEAD_DIM), F32)], axis=1)
            tail = jnp.broadcast_to(row, (SUBLANES, CMP_STRIDE * HEAD_DIM)).astype(BF16)
            a_ref[combo, n_half:n_half + SUBLANES, :] = jnp.dot(tail, w1_ref[kind], preferred_element_type=F32)
            pre = a_ref[combo, 0:n_half, 0:CMP_HID] + a_ref[combo, pl.ds(1, n_half), CMP_HID:] + b1_ref[kind]
            out = jnp.dot(_silu(pre).astype(BF16), w2_ref[kind], preferred_element_type=F32)
            if kind == 0:
                out = _rms(out, gain_ref[...])
            tok.append(out.astype(BF16))

        qz = _lane_queries(q_ref[...], t_new)
        shape = (n_half, LANES)
        u = _lane_iota(shape) & (t_new - 1)
        rel = past + u - (_row_iota(shape) * CMP_STRIDE + CMP_BLOCK - 1)
        near0 = ((past - (CMP_BLOCK - 1) - FAR_DIST) // CMP_STRIDE + 1) // SUBLANES * SUBLANES
        bias = jnp.concatenate([jnp.broadcast_to(tabl_ref[NUM_BUCKETS - 1:NUM_BUCKETS, :], (near0, LANES)),
                                _lane_bias(rel[near0:], tabl_ref)], axis=0)
        s = jnp.where(rel >= 0, _lane_scores(tok[:N_KV], qz) + bias, -jnp.inf)
        m = jnp.max(s, axis=0, keepdims=True)
        e = jnp.exp(s - jnp.where(m == -jnp.inf, 0.0, m))
        p = e / jnp.maximum(jnp.sum(e, axis=0, keepdims=True), 1e-30)
        ocmp_ref[...] = _lane_values(tok[N_KV:], p, per_kv)

        src = _row_iota((LANES, LANES))
        dst = _lane_iota((LANES, LANES))
        pooled = (src // per_kv) * t_new + (src & (t_new - 1))
        pool = jnp.where((src < N_KV * per_kv) & (dst == pooled), 1.0, 0.0).astype(BF16)
        hi = p.astype(BF16)
        lo = (p - hi.astype(F32)).astype(BF16)
        imp = jnp.dot(hi, pool, preferred_element_type=F32) + jnp.dot(lo, pool, preferred_element_type=F32)
        si = _row_iota((sel_rows, n_half)) * SEL_BLOCK
        ci = _lane_iota((sel_rows, n_half)) * CMP_STRIDE
        overlap = (ci < si + SEL_BLOCK) & (ci + CMP_BLOCK > si) & (si < n_blocks * SEL_BLOCK)
        ov = jnp.where(overlap, 1.0, 0.0).astype(BF16)
        hi = imp.astype(BF16)
        lo = (imp - hi.astype(F32)).astype(BF16)
        imp = jnp.dot(ov, hi, preferred_element_type=F32) + jnp.dot(ov, lo, preferred_element_type=F32)

        cur = (past + (_lane_iota((sel_rows, LANES)) & (t_new - 1))) // SEL_BLOCK
        sel = _select_blocks_t(imp, cur)
        unpool = jnp.where((dst < N_KV * per_kv) & (src == (dst // per_kv) * t_new + (dst & (t_new - 1))), 1.0, 0.0)
        sel_ref[...] = jnp.dot(sel.astype(BF16), unpool.astype(BF16), preferred_element_type=F32)


def _page_specs(n_chunk_pages):
    def spec(j):
        return pl.BlockSpec((None, PAGE_ROWS, HEAD_DIM), lambda b, c, pt: (pt[b, c * n_chunk_pages + j], 0, 0))
    return [spec(j) for j in range(n_chunk_pages)]


def sample_compressed(page_table, cache3, cmp_new, qn, w1, b1, w2, kc_norm, tabl, t_new):
    n_seq, n_pages = page_table.shape
    assert n_pages % PAGES_PER_STEP == 0
    past = n_pages * PAGE
    n_half = past // CMP_STRIDE
    n_blocks = -(-(past + t_new) // SEL_BLOCK)
    sel_rows = -(-n_blocks // SUBLANES) * SUBLANES
    const = lambda *shape: pl.BlockSpec(shape, lambda b, c, pt: (0,) * len(shape))
    row = lambda width: pl.BlockSpec((t_new, width), lambda b, c, pt: (b, 0))
    out = lambda rows: pl.BlockSpec((None, rows, LANES), lambda b, c, pt: (b, 0, 0))
    grid_spec = pltpu.PrefetchScalarGridSpec(
        num_scalar_prefetch=1,
        grid=(n_seq, n_pages // PAGES_PER_STEP),
        in_specs=[pl.BlockSpec(memory_space=pl.ANY),
                  row(2 * KV_W), row(NSA_W), const(2, CMP_STRIDE * HEAD_DIM, 2 * CMP_HID), const(2, 1, CMP_HID),
                  const(2, CMP_HID, HEAD_DIM), const(1, HEAD_DIM), const(NUM_BUCKETS, LANES)],
        out_specs=[out(sel_rows), out(HEAD_DIM)],
        scratch_shapes=[pltpu.VMEM((TOKEN_ROWS, PAGES_PER_STEP * PAGE_HALVES, CMP_STRIDE * HEAD_DIM), F32),
                        pltpu.VMEM((TOKEN_ROWS, n_half + SUBLANES, 2 * CMP_HID), F32),
                        pltpu.VMEM((2 * STAGE_ROWS, HEAD_DIM), F32),
                        pltpu.SemaphoreType.DMA((2,))])
    return pl.pallas_call(
        functools.partial(_sample_cmp_kernel, past=past, t_new=t_new, n_blocks=n_blocks, sel_rows=sel_rows),
        grid_spec=grid_spec,
        out_shape=[jax.ShapeDtypeStruct((n_seq, sel_rows, LANES), F32),
                   jax.ShapeDtypeStruct((n_seq, HEAD_DIM, LANES), F32)],
        compiler_params=_cparams(("arbitrary", "arbitrary")),
        name="sample_cmp",
    )(page_table, cache3, cmp_new, qn, w1, b1, w2, kc_norm.reshape(1, HEAD_DIM), tabl)


def _sample_sel_kernel(pt_ref, *refs, past, t_new, win_buf):
    pages = refs[:PAGES_PER_STEP]
    (new_ref, q_ref, sel_ref, ocmp_ref, win_ref, wnew_ref, gate_ref, tabl_ref,
     o_ref, m_ref, l_ref, acc_ref, owin_ref, s_ref, qz_ref) = refs[PAGES_PER_STEP:]
    c = pl.program_id(1)

    @pl.when(c == 0)
    def _():
        for kv, z in enumerate(_lane_queries(q_ref[...], t_new)):
            qz_ref[kv] = z

    per_kv = GROUP * t_new
    qz = [qz_ref[kv] for kv in range(N_KV)]
    far = tabl_ref[NUM_BUCKETS - 1:NUM_BUCKETS, :]
    new_rows = 2 * SUBLANES

    def new_kv(ref, kind):
        return [_pad_rows(ref[:, (kind * N_KV + kv) * HEAD_DIM:(kind * N_KV + kv + 1) * HEAD_DIM], new_rows).astype(BF16)
                for kv in range(N_KV)]

    def dist(pos0, rows):
        shape = (rows, LANES)
        return past + (_lane_iota(shape) & (t_new - 1)) - pos0 - _row_iota(shape)

    def update(s, values):
        m_prev = m_ref[...]
        m_new = jnp.maximum(m_prev, jnp.max(s, axis=0, keepdims=True))
        m_safe = jnp.where(m_new == -jnp.inf, 0.0, m_new)
        alpha = jnp.exp(m_prev - m_safe)
        p = jnp.exp(s - m_safe)
        l_ref[...] = alpha * l_ref[...] + jnp.sum(p, axis=0, keepdims=True)
        acc = alpha * acc_ref[...]
        for vals, lo, hi_ in values:
            acc = acc + _lane_values(vals, p[lo:hi_], per_kv)
        acc_ref[...] = acc
        m_ref[...] = m_new

    @pl.when(c == 0)
    def _():
        m_ref[...] = jnp.full(m_ref.shape, -jnp.inf, F32)
        l_ref[...] = jnp.zeros(l_ref.shape, F32)
        acc_ref[...] = jnp.zeros(acc_ref.shape, F32)

        kw = [win_ref[pl.ds(kv, win_buf, stride=TOKEN_ROWS), :].astype(BF16) for kv in range(N_KV)]
        vw = [win_ref[pl.ds(N_KV + kv, win_buf, stride=TOKEN_ROWS), :].astype(BF16) for kv in range(N_KV)]
        pieces = []
        near0 = (win_buf - FAR_DIST + 1) // SUBLANES * SUBLANES
        for keys, rel, near in ((kw, dist(past - win_buf, win_buf), near0), (new_kv(wnew_ref, 0), dist(past, new_rows), 0)):
            bias = _lane_bias(rel[near:], tabl_ref)
            if near:
                bias = jnp.concatenate([jnp.broadcast_to(far, (near, LANES)), bias], axis=0)
            s = _lane_scores(keys, qz) + bias
            pieces.append(jnp.where((rel >= 0) & (rel < WINDOW), s, -jnp.inf))
        m = jnp.maximum(jnp.max(pieces[0], axis=0, keepdims=True), jnp.max(pieces[1], axis=0, keepdims=True))
        m = jnp.where(m == -jnp.inf, 0.0, m)
        e0 = jnp.exp(pieces[0] - m)
        e1 = jnp.exp(pieces[1] - m)
        tot = jnp.sum(e0, axis=0, keepdims=True) + jnp.sum(e1, axis=0, keepdims=True)
        num = _lane_values(vw, e0, per_kv) + _lane_values(new_kv(wnew_ref, 1), e1, per_kv)
        owin_ref[...] = num / jnp.maximum(tot, 1e-30)

        rel = dist(past, new_rows)
        s = _lane_scores(new_kv(new_ref, 0), qz) + _lane_bias(rel, tabl_ref)
        chosen = sel_ref[past // SEL_BLOCK:past // SEL_BLOCK + 1, :] > 0.5
        update(jnp.where(chosen & (rel >= 0), s, -jnp.inf), [(new_kv(new_ref, 1), 0, new_rows)])

    values = []
    for p in range(PAGES_PER_STEP):
        page = c * PAGES_PER_STEP + p
        keys = [pages[p][pl.ds(kv, PAGE, stride=TOKEN_ROWS), :].astype(BF16) for kv in range(N_KV)]
        s = _lane_scores(keys, qz) + far
        per_block = PAGE // SEL_BLOCK
        chosen = jnp.concatenate(
            [jnp.broadcast_to(sel_ref[pl.ds(page * per_block + i, 1), :], (SEL_BLOCK, LANES)) for i in range(per_block)],
            axis=0)
        s_ref[p * PAGE:(p + 1) * PAGE, :] = jnp.where(chosen > 0.5, s, -jnp.inf)
        vals = [pages[p][pl.ds(N_KV + kv, PAGE, stride=TOKEN_ROWS), :].astype(BF16) for kv in range(N_KV)]
        values.append((vals, p * PAGE, (p + 1) * PAGE))

    @pl.when(c == pl.num_programs(1) - 1)
    def _():
        rows = slice((PAGES_PER_STEP - 1) * PAGE, PAGES_PER_STEP * PAGE)
        s_ref[rows, :] = s_ref[rows, :] + (_lane_bias(dist(past - PAGE, PAGE), tabl_ref) - far)

    update(s_ref[...], values)

    @pl.when(c == pl.num_programs(1) - 1)
    def _():
        gt = _sigmoid(gate_ref[...])
        osel = acc_ref[...] / jnp.maximum(l_ref[...], 1e-30)
        o_ref[...] = gt[0:1] * ocmp_ref[...] + gt[1:2] * osel + gt[2:3] * owin_ref[...]


def sample_selected(page_table, cache3, sel_new, qn, sel_t, ocmp_t, win3, win_new, gate_l, tabl, t_new):
    n_seq, n_pages = page_table.shape
    past = n_pages * PAGE
    win_buf = win3.shape[1] // TOKEN_ROWS
    assert (past - FAR_DIST) // PAGE >= n_pages - 1, "only the last page may hold near keys"
    sel_rows = sel_t.shape[1]
    const = lambda *shape: pl.BlockSpec(shape, lambda b, c, pt: (0,) * len(shape))
    row = lambda width: pl.BlockSpec((t_new, width), lambda b, c, pt: (b, 0))
    per_seq = lambda rows, width: pl.BlockSpec((None, rows, width), lambda b, c, pt: (b, 0, 0))
    grid_spec = pltpu.PrefetchScalarGridSpec(
        num_scalar_prefetch=1,
        grid=(n_seq, n_pages // PAGES_PER_STEP),
        in_specs=_page_specs(PAGES_PER_STEP) + [
            row(2 * KV_W), row(NSA_W), per_seq(sel_rows, LANES), per_seq(HEAD_DIM, LANES),
            per_seq(win_buf * TOKEN_ROWS, HEAD_DIM), row(2 * KV_W), per_seq(SUBLANES, LANES),
            const(NUM_BUCKETS, LANES)],
        out_specs=per_seq(HEAD_DIM, LANES),
        scratch_shapes=[pltpu.VMEM((1, LANES), F32), pltpu.VMEM((1, LANES), F32), pltpu.VMEM((HEAD_DIM, LANES), F32),
                        pltpu.VMEM((HEAD_DIM, LANES), F32), pltpu.VMEM((PAGES_PER_STEP * PAGE, LANES), F32),
                        pltpu.VMEM((N_KV, LANES, HEAD_DIM), BF16)])
    return pl.pallas_call(
        functools.partial(_sample_sel_kernel, past=past, t_new=t_new, win_buf=win_buf),
        grid_spec=grid_spec,
        out_shape=jax.ShapeDtypeStruct((n_seq, HEAD_DIM, LANES), F32),
        compiler_params=_cparams(("parallel", "arbitrary")),
        name="sample_sel",
    )(page_table, *([cache3] * PAGES_PER_STEP), sel_new, qn, sel_t, ocmp_t, win3, win_new, gate_l, tabl)


def _split_in_proj(w_in):
    d, cols = w_in.shape
    assert cols == COL_GATE + N_GATE + MERGE_COLS
    wt = jnp.swapaxes(w_in, 0, 1)
    per_kv = GROUP * 3
    zeros = lambda n: jnp.zeros((n, d), BF16)
    parts = [wt[:COL_GATE].astype(BF16)]
    for kv in range(N_KV):
        parts += [wt[COL_GATE + kv * per_kv:COL_GATE + (kv + 1) * per_kv].astype(BF16), zeros(LANES - per_kv)]
    parts.append(zeros(GATE_COLS - N_KV * LANES))
    return jnp.concatenate(parts, axis=0), wt[COL_GATE + N_GATE:].astype(BF16)


def _compress_weights(k_w1, k_b1, k_w2, v_w1, v_b1, v_w2):
    half = CMP_STRIDE * HEAD_DIM
    cat = lambda w: jnp.concatenate([w[:half], w[half:]], axis=1)
    w1 = jnp.stack([cat(k_w1), cat(v_w1)]).astype(BF16)
    b1 = jnp.stack([k_b1, v_b1]).reshape(2, 1, CMP_HID)
    w2 = jnp.stack([k_w2, v_w2]).astype(BF16)
    return w1, b1, w2


def _lane_table(rel_bias, t_new):
    tab = jnp.repeat(rel_bias, t_new, axis=1)
    return jnp.pad(tab, ((0, 0), (0, LANES - tab.shape[1])))


def _gate_lanes(proj, n_seq, t_new):
    per_kv = GROUP * 3
    g = jnp.stack([proj[:, COL_GATE + kv * LANES:COL_GATE + kv * LANES + per_kv] for kv in range(N_KV)], axis=1)
    g = g.reshape(n_seq, t_new, N_KV, GROUP, 3).transpose(0, 4, 2, 3, 1).reshape(n_seq, 3, N_HEADS_B * t_new)
    return jnp.pad(g, ((0, 0), (0, SUBLANES - 3), (0, LANES - N_HEADS_B * t_new)))


def _tiles(m, t_len):
    big = m >= 1024
    return dict(proj_tm=1024 if big else m, proj_tn=1024, prep_tm=512 if big else m, post_tm=256,
                mlp_tm=512 if big else m, mlp_tf=1024,
                hgrn_chunk=min(t_len, 256), hgrn_heads=2 if big else N_HEADS_A)


def _trunk(x, mods, s0, w, nsa_fn, attention_operands):
    nb, t_len, d = x.shape
    m = nb * t_len
    tl = _tiles(m, t_len)
    sh1, sc1, g1, sh2, sc2, g2 = mods
    x2 = x.reshape(m, d)
    proj, merge = in_proj(x2, w["norm1"], sc1, sh1, *w["w_in"], t_len, tl["proj_tm"], tl["proj_tn"])
    o_a, s_fin = hgrn(proj, w["lb"], w["hgrn_onorm"], s0, nb, t_len, tl["hgrn_chunk"], tl["hgrn_heads"])
    qn, cmp_new, sel_new, win_new, *att = nsa_prep(proj, w["q_norm"], w["ks_norm"], w["kw_norm"], tl["prep_tm"], t_len,
                                                   attention_operands)
    o_b = nsa_fn(proj, qn, cmp_new, sel_new, win_new, *att[:1])
    x1, h2 = post_attn(x2, o_a, o_b, merge, w["w_ba"], w["w_bb"], w["w_out"], g1, w["norm2"], sc2, sh2, t_len,
                       tl["post_tm"])
    y = mlp(h2, x1, w["mlp_w1"], w["mlp_w2"], g2, t_len, tl["mlp_tm"], tl["mlp_tf"])
    if attention_operands:
        cmp_new, sel_new = att[1:]
    return y.reshape(nb, t_len, d), s_fin, (cmp_new, sel_new, win_new)


def kernel(x_prompt, x_sample, c_prompt, c_sample, cache_cmp_kv, cache_sel_kv, cache_win_kv, state_hgrn, page_table, hgrn_lb_logits, rel_bias, ada_w, ada_b, norm1, norm2, w_in, hgrn_onorm, nsa_q_norm, nsa_kc_norm, nsa_ks_norm, nsa_kw_norm, cmp_k_w1, cmp_k_b1, cmp_k_w2, cmp_v_w1, cmp_v_b1, cmp_v_w2, w_branch_a, w_branch_b, w_out, mlp_w1, mlp_w2):
    n_p, t_p, d = x_prompt.shape
    n_s, t_s, _ = x_sample.shape
    past = page_table.shape[1] * PAGE
    win_buf = cache_win_kv.shape[2]
    kvw = 2 * KV_W
    layer = 0

    lb_all = jnp.cumsum(jax.nn.softmax(hgrn_lb_logits.astype(F32), axis=0), axis=0)
    cw1, cb1, cw2 = _compress_weights(cmp_k_w1[layer], cmp_k_b1[layer], cmp_k_w2[layer],
                                      cmp_v_w1[layer], cmp_v_b1[layer], cmp_v_w2[layer])
    w = dict(norm1=norm1[layer], norm2=norm2[layer], w_in=_split_in_proj(w_in[layer]),
             hgrn_onorm=hgrn_onorm[layer], lb=lb_all[layer], q_norm=nsa_q_norm[layer],
             ks_norm=nsa_ks_norm[layer], kw_norm=nsa_kw_norm[layer],
             w_ba=w_branch_a[layer].astype(BF16), w_bb=w_branch_b[layer].astype(BF16),
             w_out=w_out[layer].astype(BF16), mlp_w1=mlp_w1[layer].astype(BF16), mlp_w2=mlp_w2[layer].astype(BF16))
    tab = rel_bias.astype(F32).reshape(NUM_BUCKETS * N_HEADS_B)

    mods = ada_mods(jnp.concatenate([c_prompt, c_sample], axis=0).astype(F32), ada_w[layer], ada_b[layer])
    mods = jnp.split(mods, 6, axis=-1)
    mods_p = [a[:n_p].reshape(n_p, 1, d) for a in mods]
    mods_s = [jnp.repeat(a[n_p:], t_s, axis=0).reshape(1, n_s * t_s, d) for a in mods]

    def nsa_p(proj, qn, cmp_new, sel_new, win_new, att):
        tail = jnp.zeros((n_p * CMP_STRIDE, kvw), F32)
        cmp_tok = compress(cmp_new, tail, cw1, cb1, cw2, nsa_kc_norm[layer], n_p, t_p // CMP_STRIDE)
        return nsa_prompt_attention(tab, qn, cmp_tok, att, proj, n_p, t_p)

    s0_p = jnp.zeros((n_p, N_HEADS_A, HEAD_DIM, HEAD_DIM), F32)
    y_p, hg_p, (cmp_p, sel_p, win_p) = _trunk(x_prompt, mods_p, s0_p, w, nsa_p, True)

    def nsa_s(proj, qn, cmp_new, sel_new, win_new):
        n_phys = cache_cmp_kv.shape[1]
        cmp3 = cache_cmp_kv[layer].reshape(n_phys, PAGE_ROWS, HEAD_DIM)
        sel3 = cache_sel_kv[layer].reshape(n_phys, PAGE_ROWS, HEAD_DIM)
        win3 = cache_win_kv[layer].reshape(n_s, win_buf * TOKEN_ROWS, HEAD_DIM)
        tabl = _lane_table(rel_bias.astype(F32), t_s)
        sel_t, ocmp_t = sample_compressed(page_table, cmp3, cmp_new, qn, cw1, cb1, cw2, nsa_kc_norm[layer], tabl, t_s)
        o_t = sample_selected(page_table, sel3, sel_new, qn, sel_t, ocmp_t, win3, win_new,
                              _gate_lanes(proj, n_s, t_s), tabl, t_s)
        o = o_t[:, :, :N_HEADS_B * t_s].reshape(n_s, HEAD_DIM, N_HEADS_B, t_s)
        return o.transpose(0, 3, 2, 1).reshape(n_s * t_s, NSA_W).astype(BF16)

    y_s, hg_s, (cmp_s, sel_s, win_s) = _trunk(x_sample, mods_s, state_hgrn[layer], w, nsa_s, False)

    dt = x_prompt.dtype
    pages = lambda a: a.reshape(1, n_p, t_p // PAGE, PAGE, 2, N_KV, HEAD_DIM).astype(dt)
    rows_s = lambda a: a.reshape(1, n_s, t_s, 2, N_KV, HEAD_DIM).astype(dt)
    win_keep = min(WINDOW, t_p)
    win_p_out = win_p.reshape(n_p, t_p, kvw)[:, t_p - win_keep:].reshape(1, n_p, win_keep, 2, N_KV, HEAD_DIM)
    win_rows = cache_win_kv[layer].reshape(n_s, win_buf * TOKEN_ROWS, HEAD_DIM).astype(F32)
    win_s_out = jnp.concatenate([win_rows, win_s.reshape(n_s, t_s * TOKEN_ROWS, HEAD_DIM)], axis=1)
    win_s_out = win_s_out[:, t_s * TOKEN_ROWS:].reshape(1, n_s, win_buf, 2, N_KV, HEAD_DIM)
    return (y_p, y_s, pages(cmp_p), rows_s(cmp_s), pages(sel_p), rows_s(sel_s),
            win_p_out.astype(dt), win_s_out.astype(dt), hg_p[None].astype(dt), hg_s[None].astype(dt))
```

```python
import functools
import math

import numpy as np
import jax
import jax.numpy as jnp
from jax import lax
from jax.experimental import pallas as pl
from jax.experimental.pallas import tpu as pltpu

F32 = jnp.float32
BF16 = jnp.bfloat16

D_MODEL = 2048
N_HEADS_A = 8
HEAD_DIM = 128
HGRN_W = N_HEADS_A * HEAD_DIM
N_HEADS_B = 8
N_KV = 2
GROUP = N_HEADS_B // N_KV
NSA_W = N_HEADS_B * HEAD_DIM
KV_W = N_KV * HEAD_DIM
PAGE = 128
CMP_STRIDE = 16
CMP_BLOCK = 32
CMP_HID = 128
SEL_BLOCK = 64
N_SEL = 16
WINDOW = 512
FORCE_SCORE = 1.0e6
NUM_BUCKETS = 32
REL_MAX_DIST = 128
D_FF = 4 * D_MODEL
EPS = 1e-6
N_GATE = 3 * N_HEADS_B

LANES = 128
SUBLANES = 8
VMEM_LIMIT = 56 * 1024 * 1024

COL_HGRN = 0
COL_QB = 4 * HGRN_W
COL_CMP = COL_QB + NSA_W
COL_SEL = COL_CMP + 2 * KV_W
COL_WIN = COL_SEL + 2 * KV_W
COL_GATE = COL_WIN + 2 * KV_W
GATE_COLS = 512
PROJ_COLS = COL_GATE + GATE_COLS
MERGE_COLS = 2 * D_MODEL


def _cparams(sem):
    return pltpu.CompilerParams(dimension_semantics=sem, vmem_limit_bytes=VMEM_LIMIT)


def _sigmoid(x):
    return 1.0 / (1.0 + jnp.exp(-x))


def _silu(x):
    return x * _sigmoid(x)


def _rms(x, gain):
    return x * lax.rsqrt(jnp.mean(x * x, axis=-1, keepdims=True) + EPS) * gain


def _lane_iota(shape):
    return lax.broadcasted_iota(jnp.int32, shape, 1)


def _row_iota(shape):
    return lax.broadcasted_iota(jnp.int32, shape, 0)


def _ada_kernel(c_ref, w_ref, b_ref, o_ref):
    a = _silu(c_ref[...]).astype(BF16)
    o_ref[...] = jnp.dot(a, w_ref[...].astype(BF16), preferred_element_type=F32) + b_ref[...]


def ada_mods(c, w, b):
    r, d = c.shape
    n = w.shape[1]
    tn = 1024
    return pl.pallas_call(
        _ada_kernel,
        grid=(n // tn,),
        in_specs=[pl.BlockSpec((r, d), lambda j: (0, 0)),
                  pl.BlockSpec((d, tn), lambda j: (0, j)),
                  pl.BlockSpec((1, tn), lambda j: (0, j))],
        out_specs=pl.BlockSpec((r, tn), lambda j: (0, j)),
        out_shape=jax.ShapeDtypeStruct((r, n), F32),
        compiler_params=_cparams(("parallel",)),
        name="ada_mods",
    )(c, w, b.reshape(1, n))


def _inproj_kernel(x_ref, gain_ref, sc_ref, sh_ref, wa_ref, wb_ref, oa_ref, ob_ref, h_ref, *, n_a):
    j = pl.program_id(1)

    @pl.when(j == 0)
    def _():
        h = _rms(x_ref[...], gain_ref[...]) * (1.0 + sc_ref[0]) + sh_ref[0]
        h_ref[...] = h.astype(BF16)

    nt = (((1,), (1,)), ((), ()))

    @pl.when(j < n_a)
    def _():
        oa_ref[...] = lax.dot_general(h_ref[...], wa_ref[...], nt, preferred_element_type=F32)

    @pl.when(j >= n_a)
    def _():
        ob_ref[...] = lax.dot_general(h_ref[...], wb_ref[...], nt, preferred_element_type=F32)


def _mod_spec(mod, tm, rows_per_batch):
    d = mod.shape[-1]
    if mod.shape[1] == 1:
        return pl.BlockSpec((1, 1, d), lambda i, *_: ((i * tm) // rows_per_batch, 0, 0))
    return pl.BlockSpec((1, tm, d), lambda i, *_: (0, i, 0))


def in_proj(x2, gain, sc, sh, w_a, w_b, rows_per_batch, tm, tn):
    m, d = x2.shape
    n_a = w_a.shape[0] // tn
    n_b = w_b.shape[0] // tn
    col_a = lambda i, j: jnp.minimum(j, n_a - 1)
    col_b = lambda i, j: jnp.maximum(j - n_a, 0)
    return pl.pallas_call(
        functools.partial(_inproj_kernel, n_a=n_a),
        grid=(m // tm, n_a + n_b),
        in_specs=[pl.BlockSpec((tm, d), lambda i, j: (i, 0), pipeline_mode=pl.Buffered(1)),
                  pl.BlockSpec((1, d), lambda i, j: (0, 0)),
                  _mod_spec(sc, tm, rows_per_batch),
                  _mod_spec(sh, tm, rows_per_batch),
                  pl.BlockSpec((tn, d), lambda i, j: (col_a(i, j), 0)),
                  pl.BlockSpec((tn, d), lambda i, j: (col_b(i, j), 0))],
        out_specs=[pl.BlockSpec((tm, tn), lambda i, j: (i, col_a(i, j))),
                   pl.BlockSpec((tm, tn), lambda i, j: (i, col_b(i, j)))],
        out_shape=[jax.ShapeDtypeStruct((m, w_a.shape[0]), F32), jax.ShapeDtypeStruct((m, w_b.shape[0]), F32)],
        scratch_shapes=[pltpu.VMEM((tm, d), BF16)],
        compiler_params=_cparams(("parallel", "arbitrary")),
        name="in_proj",
    )(x2, gain.reshape(1, d), sc, sh, w_a, w_b)


def _post_kernel(x_ref, oa_ref, ob_ref, mga_ref, mgb_ref, wba_ref, wbb_ref, wout_ref,
                 g1_ref, gain2_ref, sc2_ref, sh2_ref, x1_ref, h2_ref):
    ya = jnp.dot(oa_ref[...], wba_ref[...], preferred_element_type=F32)
    yb = jnp.dot(ob_ref[...], wbb_ref[...], preferred_element_type=F32)
    merged = _sigmoid(mga_ref[...]) * ya + _sigmoid(mgb_ref[...]) * yb
    y = jnp.dot(merged.astype(BF16), wout_ref[...], preferred_element_type=F32)
    x1 = x_ref[...] + g1_ref[0] * y
    x1_ref[...] = x1
    h2_ref[...] = (_rms(x1, gain2_ref[...]) * (1.0 + sc2_ref[0]) + sh2_ref[0]).astype(BF16)


def post_attn(x2, oa, ob, proj, wba, wbb, wout, g1, gain2, sc2, sh2, rows_per_batch, tm):
    m, d = x2.shape
    const = lambda i: (0, 0)
    resident = lambda shape: pl.BlockSpec(shape, const, pipeline_mode=pl.Buffered(1))
    return pl.pallas_call(
        _post_kernel,
        grid=(m // tm,),
        in_specs=[pl.BlockSpec((tm, d), lambda i: (i, 0)),
                  pl.BlockSpec((tm, HGRN_W), lambda i: (i, 0)),
                  pl.BlockSpec((tm, NSA_W), lambda i: (i, 0)),
                  pl.BlockSpec((tm, d), lambda i: (i, 0)),
                  pl.BlockSpec((tm, d), lambda i: (i, 1)),
                  resident((HGRN_W, d)),
                  resident((NSA_W, d)),
                  resident((d, d)),
                  _mod_spec(g1, tm, rows_per_batch),
                  pl.BlockSpec((1, d), const),
                  _mod_spec(sc2, tm, rows_per_batch),
                  _mod_spec(sh2, tm, rows_per_batch)],
        out_specs=[pl.BlockSpec((tm, d), lambda i: (i, 0)),
                   pl.BlockSpec((tm, d), lambda i: (i, 0))],
        out_shape=[jax.ShapeDtypeStruct((m, d), F32), jax.ShapeDtypeStruct((m, d), BF16)],
        compiler_params=_cparams(("parallel",)),
        name="post_attn",
    )(x2, oa, ob, proj, proj, wba, wbb, wout, g1, gain2.reshape(1, d), sc2, sh2)


def _mlp_kernel(h_ref, x1_ref, w1_ref, w2_ref, g2_ref, y_ref, acc_ref):
    f = pl.program_id(1)
    u = jnp.maximum(jnp.dot(h_ref[...], w1_ref[...], preferred_element_type=F32), 0.0)
    part = jnp.dot((u * u).astype(BF16), w2_ref[...], preferred_element_type=F32)

    @pl.when(f == 0)
    def _():
        acc_ref[...] = part

    @pl.when(f > 0)
    def _():
        acc_ref[...] += part

    @pl.when(f == pl.num_programs(1) - 1)
    def _():
        y_ref[...] = x1_ref[...] + g2_ref[0] * acc_ref[...]


def mlp(h2, x1, w1, w2, g2, rows_per_batch, tm, tf):
    m, d = x1.shape
    ff = w1.shape[1]
    return pl.pallas_call(
        _mlp_kernel,
        grid=(m // tm, ff // tf),
        in_specs=[pl.BlockSpec((tm, d), lambda i, f: (i, 0)),
                  pl.BlockSpec((tm, d), lambda i, f: (i, 0)),
                  pl.BlockSpec((d, tf), lambda i, f: (0, f)),
                  pl.BlockSpec((tf, d), lambda i, f: (f, 0)),
                  _mod_spec(g2, tm, rows_per_batch)],
        out_specs=pl.BlockSpec((tm, d), lambda i, f: (i, 0)),
        out_shape=jax.ShapeDtypeStruct((m, d), F32),
        scratch_shapes=[pltpu.VMEM((tm, d), F32)],
        compiler_params=_cparams(("parallel", "arbitrary")),
        name="mlp",
    )(h2, x1, w1, w2, g2)


_NT = (((1,), (1,)), ((), ()))
_TN = (((0,), (0,)), ((), ()))


def _hgrn_kernel(q_ref, z_ref, v_ref, g_ref, lb_ref, on_ref, s0_ref, o_ref, sfin_ref, st_ref, b_ref, *, chunk, heads):
    for h in range(heads):
        _hgrn_head(h, q_ref, z_ref, v_ref, g_ref, lb_ref, on_ref, s0_ref, o_ref, sfin_ref, st_ref, b_ref, chunk)


def _hgrn_head(h, q_ref, z_ref, v_ref, g_ref, lb_ref, on_ref, s0_ref, o_ref, sfin_ref, st_ref, b_ref, chunk):
    c = pl.program_id(2)
    cols = slice(h * HEAD_DIM, (h + 1) * HEAD_DIM)

    @pl.when(c == 0)
    def _():
        st_ref[h] = s0_ref[0, h].T

    q = q_ref[:, cols]
    z = z_ref[:, cols]
    v = v_ref[:, cols]
    lb = lb_ref[h]
    e = jnp.exp(-jnp.abs(z))
    r = 1.0 / (1.0 + e)
    pos = z >= 0.0
    logf = jnp.log(lb + (1.0 - lb) * jnp.where(pos, r, e * r))
    k = (1.0 - lb) * jnp.where(pos, e * r, r)

    t = lax.broadcasted_iota(jnp.int32, (chunk, HEAD_DIM), 0)
    b = logf
    s = 1
    while s < chunk:
        b = b + jnp.where(t >= s, pltpu.roll(b, s, 0), 0.0)
        s *= 2
    b_ref[h] = b

    t8 = t & (SUBLANES - 1)
    if chunk <= SUBLANES:
        o = jnp.zeros((chunk, HEAD_DIM), F32)
        for d in range(SUBLANES):
            kd, bd, vd = (k, b, v) if d == 0 else (pltpu.roll(k, d, 0), pltpu.roll(b, d, 0), pltpu.roll(v, d, 0))
            w = jnp.exp(jnp.where(t8 >= d, b - bd, -jnp.inf))
            o = o + jnp.sum(q * kd * w, axis=-1, keepdims=True) * vd
    else:
        o = jnp.sum(q * k, axis=-1, keepdims=True) * v
        row = lax.broadcasted_iota(jnp.int32, (chunk, chunk), 0)
        col = lax.broadcasted_iota(jnp.int32, (chunk, chunk), 1)
        apart = row ^ col
        att = jnp.zeros((chunk, chunk), F32)
        sub = _row_iota((SUBLANES, HEAD_DIM))
        m = 1
        while m < chunk:
            def ref_row(r, rows):
                return jnp.broadcast_to(b_ref[h, pl.ds(r, 1), :], (rows, HEAD_DIM))

            if 2 * m >= SUBLANES:
                refs = [ref_row(blk * 2 * m + m - 1, 2 * m) for blk in range(chunk // (2 * m))]
            else:
                refs = []
                for tile in range(chunk // SUBLANES):
                    ref_t = ref_row(tile * SUBLANES + m - 1, SUBLANES)
                    for j in range(1, SUBLANES // (2 * m)):
                        ref_t = jnp.where(sub >= j * 2 * m, ref_row(tile * SUBLANES + j * 2 * m + m - 1, SUBLANES), ref_t)
                    refs.append(ref_t)
            ref_b = refs[0] if len(refs) == 1 else jnp.concatenate(refs, axis=0)
            second = (t & (2 * m - 1)) >= m
            w = jnp.exp(jnp.where(second, b - ref_b, ref_b - b))
            qs = jnp.where(second, q * w, 0.0).astype(BF16)
            ks = jnp.where(second, 0.0, k * w).astype(BF16)
            a_m = lax.dot_general(qs, ks, _NT, preferred_element_type=F32)
            att = att + jnp.where(apart < 2 * m, a_m, 0.0)
            m *= 2
        o = o + jnp.dot(att.astype(BF16), v.astype(BF16), preferred_element_type=F32)

    st = st_ref[h]
    b_last = b_ref[h, pl.ds(chunk - 1, 1), :]
    o = o + lax.dot_general((q * jnp.exp(b)).astype(BF16), st.astype(BF16), _NT, preferred_element_type=F32)
    kt = (k * jnp.exp(b_last - b)).astype(BF16)
    vb = v.astype(BF16)
    if chunk < 2 * SUBLANES:
        pad = jnp.zeros((2 * SUBLANES - chunk, HEAD_DIM), BF16)
        kt = jnp.concatenate([kt, pad], axis=0)
        vb = jnp.concatenate([vb, pad], axis=0)
    st_new = jnp.exp(b_last) * st + lax.dot_general(vb, kt, _TN, preferred_element_type=F32)
    st_ref[h] = st_new

    o_ref[:, cols] = (_rms(o, on_ref[h]) * _silu(g_ref[:, cols])).astype(o_ref.dtype)

    @pl.when(c == pl.num_programs(2) - 1)
    def _():
        sfin_ref[0, h] = st_new.T


def hgrn(proj, lb, onorm, s0, n_batch, t_len, chunk, heads):
    m = proj.shape[0]
    n_c = t_len // chunk
    hb = N_HEADS_A // heads
    width = heads * HEAD_DIM

    def col(group):
        return pl.BlockSpec((chunk, width), lambda bi, h, c: (bi * n_c + c, group * hb + h))

    vec = pl.BlockSpec((heads, 1, HEAD_DIM), lambda bi, h, c: (h, 0, 0))
    state = pl.BlockSpec((1, heads, HEAD_DIM, HEAD_DIM), lambda bi, h, c: (bi, h, 0, 0))
    o_dtype = BF16 if chunk % (2 * SUBLANES) == 0 else F32
    o, s_fin = pl.pallas_call(
        functools.partial(_hgrn_kernel, chunk=chunk, heads=heads),
        grid=(n_batch, hb, n_c),
        in_specs=[col(0), col(1), col(2), col(3), vec, vec, state],
        out_specs=[pl.BlockSpec((chunk, width), lambda bi, h, c: (bi * n_c + c, h)), state],
        out_shape=[jax.ShapeDtypeStruct((m, HGRN_W), o_dtype),
                   jax.ShapeDtypeStruct((n_batch, N_HEADS_A, HEAD_DIM, HEAD_DIM), F32)],
        scratch_shapes=[pltpu.VMEM((heads, HEAD_DIM, HEAD_DIM), F32), pltpu.VMEM((heads, chunk, HEAD_DIM), F32)],
        compiler_params=_cparams(("parallel", "parallel", "arbitrary")),
        name="hgrn",
    )(proj, proj, proj, proj, lb.reshape(N_HEADS_A, 1, HEAD_DIM), onorm.reshape(N_HEADS_A, 1, HEAD_DIM), s0)
    return o.astype(BF16), s_fin


def _bucket_steps():
    n = np.arange(REL_MAX_DIST)
    exact = NUM_BUCKETS // 2
    val = np.log(np.maximum(n, 1) / exact) / math.log(REL_MAX_DIST / exact) * (NUM_BUCKETS - exact)
    frac = np.abs(val - np.round(val))[exact + 1:]
    assert frac.min() > 1e-3, "a bucket edge sits on an integer distance"
    lut = np.where(n < exact, n, np.minimum(exact + np.floor(np.maximum(val, 0.0)).astype(np.int64), NUM_BUCKETS - 1))
    assert lut[-1] == NUM_BUCKETS - 1
    return int(lut[0]), [(int(i), int(lut[i])) for i in range(1, REL_MAX_DIST) if lut[i] != lut[i - 1]]


_BUCKET0, _BUCKET_EDGES = _bucket_steps()
FAR_DIST = _BUCKET_EDGES[-1][0]


def _bias(rel, tab_ref, head):
    val = jnp.full(rel.shape, tab_ref[_BUCKET0 * N_HEADS_B + head], F32)
    for edge, bucket in _BUCKET_EDGES:
        val = jnp.where(rel >= edge, tab_ref[bucket * N_HEADS_B + head], val)
    return val


def _stack_heads(x):
    return jnp.concatenate([x[:, g * HEAD_DIM:(g + 1) * HEAD_DIM] for g in range(GROUP)], axis=0)


def _tile_heads(x):
    return jnp.concatenate([x] * GROUP, axis=0)


def _masked_softmax(s, mask):
    s = jnp.where(mask, s, -jnp.inf)
    m = jnp.max(s, axis=-1, keepdims=True)
    m = jnp.where(m == -jnp.inf, 0.0, m)
    e = jnp.exp(s - m)
    return e / jnp.maximum(jnp.sum(e, axis=-1, keepdims=True), 1e-30)


def _block_importance(p, n_cmp, n_blocks, width):
    rows = p.shape[0] // GROUP
    imp = p[0:rows]
    for g in range(1, GROUP):
        imp = imp + p[g * rows:(g + 1) * rows]
    ci = lax.broadcasted_iota(jnp.int32, (n_cmp, width), 0) * CMP_STRIDE
    si = lax.broadcasted_iota(jnp.int32, (n_cmp, width), 1) * SEL_BLOCK
    overlap = (ci < si + SEL_BLOCK) & (ci + CMP_BLOCK > si) & (si < n_blocks * SEL_BLOCK)
    ov = jnp.where(overlap, 1.0, 0.0).astype(BF16)
    hi = imp.astype(BF16)
    lo = (imp - hi.astype(F32)).astype(BF16)
    return jnp.dot(hi, ov, preferred_element_type=F32) + jnp.dot(lo, ov, preferred_element_type=F32)


def _select_blocks_t(imp_t, cur):
    blk = _row_iota(imp_t.shape)
    forced = (blk == 0) | (blk == cur) | (blk == cur - 1)
    score = jnp.where(blk <= cur, imp_t + jnp.where(forced, FORCE_SCORE, 0.0), -jnp.inf)
    sel = jnp.zeros(imp_t.shape, F32)
    n_rows = imp_t.shape[0]
    for _ in range(N_SEL):
        top = jnp.max(score, axis=0, keepdims=True)
        first = jnp.min(jnp.where(score == top, blk, n_rows), axis=0, keepdims=True)
        pick = blk == first
        sel = jnp.where(pick, 1.0, sel)
        score = jnp.where(pick, -jnp.inf, score)
    return sel


def _gate_mix(gate, parts):
    rows = gate.shape[0]
    gt = _sigmoid(gate)
    outs = []
    for g in range(GROUP):
        o = gt[:, 3 * g:3 * g + 1] * parts[0][g * rows:(g + 1) * rows]
        for j in (1, 2):
            o = o + gt[:, 3 * g + j:3 * g + j + 1] * parts[j][g * rows:(g + 1) * rows]
        outs.append(o)
    return jnp.concatenate(outs, axis=1)


ATT_KSA = 0
ATT_VS = N_KV * 2 * HEAD_DIM
ATT_KW = ATT_VS + KV_W
ATT_VW = ATT_KW + KV_W
ATT_COLS = ATT_VW + KV_W


def _nsa_prep_kernel(q_ref, cmp_ref, sel_ref, win_ref, qn_ref, ksn_ref, kwn_ref, qo_ref, co_ref, so_ref, wo_ref,
                     *att_ref, t_len):
    q = q_ref[...]
    scale = HEAD_DIM ** -0.5
    for h in range(N_HEADS_B):
        sl = slice(h * HEAD_DIM, (h + 1) * HEAD_DIM)
        qo_ref[:, sl] = _rms(q[:, sl], qn_ref[...]) * scale
    co_ref[...] = cmp_ref[...]
    for src, gain, dst in ((sel_ref, ksn_ref, so_ref), (win_ref, kwn_ref, wo_ref)):
        x = src[...]
        for h in range(N_KV):
            sl = slice(h * HEAD_DIM, (h + 1) * HEAD_DIM)
            dst[:, sl] = _rms(x[:, sl], gain[...])
        dst[:, KV_W:] = x[:, KV_W:]
    if att_ref:
        att_ref, crow_ref, srow_ref = att_ref
        tm = q.shape[0]
        for combo in range(TOKEN_ROWS):
            sl = slice(combo * HEAD_DIM, (combo + 1) * HEAD_DIM)
            crow_ref[pl.ds(combo, tm, stride=TOKEN_ROWS), :] = co_ref[:, sl]
            srow_ref[pl.ds(combo, tm, stride=TOKEN_ROWS), :] = so_ref[:, sl]
        pos = (pl.program_id(0) * tm + _row_iota((tm, LANES))) % t_len
        onehot = jnp.where(_lane_iota((tm, LANES)) == pos // SEL_BLOCK, 1.0, 0.0).astype(BF16)
        for h in range(N_KV):
            att_ref[:, ATT_KSA + 2 * h * HEAD_DIM:ATT_KSA + (2 * h + 1) * HEAD_DIM] = (
                so_ref[:, h * HEAD_DIM:(h + 1) * HEAD_DIM].astype(BF16))
            att_ref[:, ATT_KSA + (2 * h + 1) * HEAD_DIM:ATT_KSA + (2 * h + 2) * HEAD_DIM] = onehot
        att_ref[:, ATT_VS:ATT_KW] = so_ref[:, KV_W:].astype(BF16)
        att_ref[:, ATT_KW:ATT_VW] = wo_ref[:, :KV_W].astype(BF16)
        att_ref[:, ATT_VW:ATT_COLS] = wo_ref[:, KV_W:].astype(BF16)


def nsa_prep(proj, q_norm, ks_norm, kw_norm, tm, t_len, attention_operands):
    m = proj.shape[0]
    kvw = 2 * KV_W
    vec = pl.BlockSpec((1, HEAD_DIM), lambda i: (0, 0))
    widths = [NSA_W, kvw, kvw, kvw] + ([ATT_COLS] if attention_operands else [])
    dtypes = [F32] * 4 + ([BF16] if attention_operands else [])
    out_specs = [pl.BlockSpec((tm, wd), lambda i: (i, 0)) for wd in widths]
    out_shape = [jax.ShapeDtypeStruct((m, wd), dt) for wd, dt in zip(widths, dtypes)]
    if attention_operands:
        assert t_len // SEL_BLOCK <= LANES
        out_specs += [pl.BlockSpec((tm * TOKEN_ROWS, HEAD_DIM), lambda i: (i, 0))] * 2
        out_shape += [jax.ShapeDtypeStruct((m * TOKEN_ROWS, HEAD_DIM), F32)] * 2
    return pl.pallas_call(
        functools.partial(_nsa_prep_kernel, t_len=t_len),
        grid=(m // tm,),
        in_specs=[pl.BlockSpec((tm, NSA_W), lambda i: (i, COL_QB // NSA_W)),
                  pl.BlockSpec((tm, kvw), lambda i: (i, COL_CMP // kvw)),
                  pl.BlockSpec((tm, kvw), lambda i: (i, COL_SEL // kvw)),
                  pl.BlockSpec((tm, kvw), lambda i: (i, COL_WIN // kvw)),
                  vec, vec, vec],
        out_specs=out_specs,
        out_shape=out_shape,
        compiler_params=_cparams(("parallel",)),
        name="nsa_prep",
    )(proj, proj, proj, proj, q_norm.reshape(1, HEAD_DIM), ks_norm.reshape(1, HEAD_DIM), kw_norm.reshape(1, HEAD_DIM))


def _compress_kernel(x_ref, tail_ref, w1_ref, b1_ref, w2_ref, gain_ref, o_ref, x2_ref, a1_ref, *, n_half):
    for j in range(CMP_STRIDE):
        x2_ref[0:n_half, j * HEAD_DIM:(j + 1) * HEAD_DIM] = x_ref[pl.ds(j, n_half, stride=CMP_STRIDE), :]
        x2_ref[n_half:n_half + SUBLANES, j * HEAD_DIM:(j + 1) * HEAD_DIM] = jnp.broadcast_to(
            tail_ref[pl.ds(j, 1), :], (SUBLANES, HEAD_DIM))
    a = jnp.dot(x2_ref[...].astype(BF16), w1_ref[0], preferred_element_type=F32)
    a1_ref[...] = a[:, CMP_HID:]
    pre = a[0:n_half, :CMP_HID] + a1_ref[pl.ds(1, n_half), :] + b1_ref[0]
    out = jnp.dot(_silu(pre).astype(BF16), w2_ref[0], preferred_element_type=F32)
    is_k = pl.program_id(1) < N_KV
    o_ref[0, 0] = jnp.where(is_k, _rms(out, gain_ref[...]), out).astype(BF16)


def compress(raw, tail, w1, b1, w2, kc_norm, n_batch, n_half):
    t_len = n_half * CMP_STRIDE
    wsel = lambda b, c: (c // N_KV, 0, 0)
    return pl.pallas_call(
        functools.partial(_compress_kernel, n_half=n_half),
        grid=(n_batch, 2 * N_KV),
        in_specs=[pl.BlockSpec((t_len, HEAD_DIM), lambda b, c: (b, c)),
                  pl.BlockSpec((CMP_STRIDE, HEAD_DIM), lambda b, c: (b, c)),
                  pl.BlockSpec((1, CMP_STRIDE * HEAD_DIM, 2 * CMP_HID), wsel),
                  pl.BlockSpec((1, 1, CMP_HID), wsel),
                  pl.BlockSpec((1, CMP_HID, HEAD_DIM), wsel),
                  pl.BlockSpec((1, HEAD_DIM), lambda b, c: (0, 0))],
        out_specs=pl.BlockSpec((1, 1, n_half, HEAD_DIM), lambda b, c: (b, c, 0, 0)),
        out_shape=jax.ShapeDtypeStruct((n_batch, 2 * N_KV, n_half, HEAD_DIM), BF16),
        scratch_shapes=[pltpu.VMEM((n_half + SUBLANES, CMP_STRIDE * HEAD_DIM), F32),
                        pltpu.VMEM((n_half + SUBLANES, CMP_HID), F32)],
        compiler_params=_cparams(("parallel", "parallel")),
        name="compress",
    )(raw, tail, w1, b1, w2, kc_norm.reshape(1, HEAD_DIM))


TQ = 128
TK = 256
STRIP_W = TQ + 2 * TK
STRIP_ORIGIN = STRIP_W - TK
MASK_OFF = 1 << 20
BAND_ORIGIN = 64


UNSELECTED = -2.0 ** 30


def _nsa_prompt_kernel(tab_ref, q_ref, kc_ref, vc_ref, ksa_ref, vs_ref, kw_ref, vw_ref, gate_ref, o_ref,
                       strip_ref, band_ref, s_ref, wide_ref, acc_ref, *, n_half, n_blocks):
    kv = pl.program_id(1)
    n = pl.program_id(2)
    head0 = kv * GROUP
    q0 = n * TQ
    jd = n // 2
    odd = n - 2 * jd
    rows = GROUP * TQ
    far = [tab_ref[(NUM_BUCKETS - 1) * N_HEADS_B + head0 + g] for g in range(GROUP)]

    @pl.when(n == 0)
    def _():
        a = _row_iota((TQ, STRIP_W))
        u = _lane_iota((TQ, STRIP_W))
        for g in range(GROUP):
            strip_ref[g] = _bias(a + STRIP_ORIGIN - u, tab_ref, head0 + g) - far[g]
        rel = _row_iota((TQ, n_half)) - ((_lane_iota((TQ, n_half)) - BAND_ORIGIN) * CMP_STRIDE + CMP_BLOCK - 1)
        for g in range(GROUP):
            band_ref[g] = jnp.where(rel >= 0, _bias(rel, tab_ref, head0 + g) - far[g], 0.0)

    q = _stack_heads(q_ref[...])
    qs = q.astype(BF16)
    a_k = _row_iota((TQ, TK))
    c_k = _lane_iota((TQ, TK))

    def tile_start(jj):
        return pl.multiple_of(jnp.maximum(jj, 0) * TK, TK)

    def near_bias(d):
        u0 = pl.multiple_of(STRIP_ORIGIN - d * TK - odd * TQ, TQ)
        return jnp.concatenate([strip_ref[g, :, pl.ds(u0, TK)] for g in range(GROUP)], axis=0)

    def finish(n_tiles, v_ref):
        m = jnp.max(wide_ref[...], axis=-1, keepdims=True)
        m = jnp.where(m == -jnp.inf, 0.0, m)
        wide_ref[...] = jnp.zeros(wide_ref.shape, F32)
        acc_ref[...] = jnp.zeros(acc_ref.shape, F32)

        def one(jj):
            p = jnp.exp(s_ref[jj] - m)
            wide_ref[...] += p
            acc_ref[...] += jnp.dot(p.astype(BF16), v_ref[pl.ds(tile_start(jj), TK), :], preferred_element_type=F32)

        def pair(t, carry):
            one(2 * t)
            one(2 * t + 1)
            return carry

        lax.fori_loop(0, n_tiles // 2, pair, 0)

        @pl.when(n_tiles % 2 == 1)
        def _():
            one(n_tiles - 1)

        return acc_ref[...] / jnp.maximum(jnp.sum(wide_ref[...], axis=-1, keepdims=True), 1e-30)

    a_c = _row_iota((TQ, n_half))
    rel_c = q0 + a_c - (_lane_iota((TQ, n_half)) * CMP_STRIDE + CMP_BLOCK - 1)
    shift = (n * (TQ // CMP_STRIDE) - BAND_ORIGIN) % n_half
    bias_c = jnp.concatenate([pltpu.roll(band_ref[g], shift, 1) for g in range(GROUP)], axis=0)
    s = lax.dot_general(qs, kc_ref[0, 0], _NT, preferred_element_type=F32)
    p = _masked_softmax(s + bias_c, _tile_heads(rel_c) >= 0)
    o_cmp = jnp.dot(p.astype(BF16), vc_ref[0, 0], preferred_element_type=F32)
    imp = _block_importance(p, n_half, n_blocks, LANES)

    lead = _tile_heads(a_k - c_k) + odd * TQ
    causal = lead >= 0
    for d in range(3):
        k = kw_ref[pl.ds(tile_start(jd - d), TK), :]
        s = lax.dot_general(qs, k, _NT, preferred_element_type=F32)
        if d == 0:
            s = jnp.where(causal, s + near_bias(0), -jnp.inf)
        elif d == 1:
            s = s + near_bias(1) + jnp.where(jd >= 1, 0.0, -jnp.inf)
        else:
            s = jnp.where(lead < jnp.where(jd >= 2, 0, -MASK_OFF), s, -jnp.inf)
        s_ref[d] = s
        wide_ref[...] = s if d == 0 else jnp.maximum(wide_ref[...], s)
    m = jnp.max(wide_ref[...], axis=-1, keepdims=True)
    m = jnp.where(m == -jnp.inf, 0.0, m)
    num = jnp.zeros((rows, HEAD_DIM), F32)
    den = jnp.zeros((rows, TK), F32)
    for d in range(3):
        e = jnp.exp(s_ref[d] - m)
        den = den + e
        num = num + jnp.dot(e.astype(BF16), vw_ref[pl.ds(tile_start(jd - d), TK), :], preferred_element_type=F32)
    o_win = num / jnp.maximum(jnp.sum(den, axis=-1, keepdims=True), 1e-30)

    cur_t = (q0 + _lane_iota((LANES, TQ))) // SEL_BLOCK
    unsel = ((_select_blocks_t(imp.T, cur_t) - 1.0) * -UNSELECTED).T
    qa = jnp.concatenate([q, _tile_heads(unsel)], axis=1).astype(BF16)

    def scores(jj):
        return lax.dot_general(qa, ksa_ref[pl.ds(tile_start(jj), TK), :], _NT, preferred_element_type=F32)

    s = jnp.where(causal, scores(jd) + near_bias(0), -jnp.inf)
    s_ref[jd] = s
    wide_ref[...] = s

    @pl.when(jd >= 1)
    def _():
        s = scores(jd - 1) + near_bias(1)
        s_ref[jd - 1] = s
        wide_ref[...] = jnp.maximum(wide_ref[...], s)

    n_far = jnp.maximum(jd - 1, 0)

    def far_pair(t, carry):
        for jj in (2 * t, jnp.minimum(2 * t + 1, n_far - 1)):
            s = scores(jj)
            s_ref[jj] = s
            wide_ref[...] = jnp.maximum(wide_ref[...], s)
        return carry

    lax.fori_loop(0, (n_far + 1) // 2, far_pair, 0)
    o_sel = finish(jd + 1, vs_ref)

    gate = pltpu.roll(gate_ref[...], (LANES - kv * GROUP * 3) % LANES, 1)
    o_ref[...] = _gate_mix(gate, (o_cmp, o_sel, o_win)).astype(o_ref.dtype)


def nsa_prompt_attention(tab, qn, cmp_tok, att, proj, n_batch, t_len):
    m = qn.shape[0]
    n_q = t_len // TQ
    n_half = cmp_tok.shape[2]
    gw = GROUP * HEAD_DIM
    rows = GROUP * TQ
    seq = lambda col, width: pl.BlockSpec((t_len, width), lambda b, kv, n: (b, col // width + kv))
    tok = lambda off: pl.BlockSpec((1, 1, n_half, HEAD_DIM), lambda b, kv, n: (b, off + kv, 0, 0))
    return pl.pallas_call(
        functools.partial(_nsa_prompt_kernel, n_half=n_half, n_blocks=t_len // SEL_BLOCK),
        grid=(n_batch, N_KV, n_q),
        in_specs=[pl.BlockSpec(memory_space=pltpu.SMEM),
                  pl.BlockSpec((TQ, gw), lambda b, kv, n: (b * n_q + n, kv)),
                  tok(0), tok(N_KV), seq(ATT_KSA, 2 * HEAD_DIM), seq(ATT_VS, HEAD_DIM), seq(ATT_KW, HEAD_DIM),
                  seq(ATT_VW, HEAD_DIM),
                  pl.BlockSpec((TQ, LANES), lambda b, kv, n: (b * n_q + n, COL_GATE // LANES))],
        out_specs=pl.BlockSpec((TQ, gw), lambda b, kv, n: (b * n_q + n, kv)),
        out_shape=jax.ShapeDtypeStruct((m, NSA_W), BF16),
        scratch_shapes=[pltpu.VMEM((GROUP, TQ, STRIP_W), F32), pltpu.VMEM((GROUP, TQ, n_half), F32),
                        pltpu.VMEM((t_len // TK, rows, TK), F32),
                        pltpu.VMEM((rows, TK), F32), pltpu.VMEM((rows, HEAD_DIM), F32)],
        compiler_params=_cparams(("parallel", "parallel", "arbitrary")),
        name="nsa_prompt",
    )(tab, qn, cmp_tok, cmp_tok, att, att, att, att, proj)


PAGES_PER_STEP = 32
TOKEN_ROWS = 2 * N_KV
PAGE_ROWS = PAGE * TOKEN_ROWS


def _lane_queries(q, t_new):
    per_kv = GROUP * t_new
    stacked = jnp.concatenate([q[:, h * HEAD_DIM:(h + 1) * HEAD_DIM] for h in range(N_HEADS_B)], axis=0)
    out = []
    for kv in range(N_KV):
        parts = []
        if kv:
            parts.append(jnp.zeros((kv * per_kv, HEAD_DIM), F32))
        parts.append(stacked[kv * per_kv:(kv + 1) * per_kv])
        parts.append(jnp.zeros((LANES - (kv + 1) * per_kv, HEAD_DIM), F32))
        out.append(jnp.concatenate(parts, axis=0).astype(BF16))
    return out


def _lane_scores(keys, qz):
    s = lax.dot_general(keys[0], qz[0], _NT, preferred_element_type=F32)
    for kv in range(1, N_KV):
        s = s + lax.dot_general(keys[kv], qz[kv], _NT, preferred_element_type=F32)
    return s


def _lane_values(vals, p, per_kv):
    pb = p.astype(BF16)
    lane = _lane_iota((HEAD_DIM, LANES))
    out = lax.dot_general(vals[N_KV - 1], pb, _TN, preferred_element_type=F32)
    for kv in range(N_KV - 2, -1, -1):
        out = jnp.where(lane < (kv + 1) * per_kv, lax.dot_general(vals[kv], pb, _TN, preferred_element_type=F32), out)
    return out


def _lane_bias(rel, tabl_ref):
    val = jnp.broadcast_to(tabl_ref[_BUCKET0:_BUCKET0 + 1, :], rel.shape)
    for edge, bucket in _BUCKET_EDGES:
        val = jnp.where(rel >= edge, tabl_ref[bucket:bucket + 1, :], val)
    return val


def _pad_rows(x, rows):
    return jnp.concatenate([x, jnp.zeros((rows - x.shape[0], x.shape[1]), x.dtype)], axis=0)


PAGE_HALVES = PAGE // CMP_STRIDE
SLAB_ROWS = CMP_STRIDE * TOKEN_ROWS
SLAB_PITCH = SLAB_ROWS + SUBLANES
STAGE_ROWS = PAGES_PER_STEP * PAGE_HALVES * SLAB_PITCH


def _stage_copies(pt_ref, cache_ref, stage_ref, sem_ref, step, slot, n_chunks):
    b = step // n_chunks
    c = step - b * n_chunks
    copies = []
    for p in range(PAGES_PER_STEP):
        page = pt_ref[b, c * PAGES_PER_STEP + p]
        for n in range(PAGE_HALVES):
            dst = pl.multiple_of(slot * STAGE_ROWS + (p * PAGE_HALVES + n) * SLAB_PITCH, SUBLANES)
            copies.append(pltpu.make_async_copy(cache_ref.at[page, pl.ds(n * SLAB_ROWS, SLAB_ROWS), :],
                                                stage_ref.at[pl.ds(dst, SLAB_ROWS), :], sem_ref.at[slot]))
    return copies


def _sample_cmp_kernel(pt_ref, cache_ref, new_ref, q_ref, w1_ref, b1_ref, w2_ref, gain_ref, tabl_ref,
                       sel_ref, ocmp_ref, x2_ref, a_ref, stage_ref, sem_ref, *, past, t_new, n_blocks, sel_rows):
    c = pl.program_id(1)
    n_chunks = pl.num_programs(1)
    step = pl.program_id(0) * n_chunks + c
    slot = step % 2
    n_half = past // CMP_STRIDE
    step_halves = PAGES_PER_STEP * PAGE_HALVES
    per_kv = GROUP * t_new

    @pl.when(step == 0)
    def _():
        for cp in _stage_copies(pt_ref, cache_ref, stage_ref, sem_ref, step, slot, n_chunks):
            cp.start()

    @pl.when(step + 1 < pl.num_programs(0) * n_chunks)
    def _():
        for cp in _stage_copies(pt_ref, cache_ref, stage_ref, sem_ref, step + 1, 1 - slot, n_chunks):
            cp.start()

    for cp in _stage_copies(pt_ref, cache_ref, stage_ref, sem_ref, step, slot, n_chunks):
        cp.wait()

    base = slot * STAGE_ROWS
    for combo in range(TOKEN_ROWS):
        for p in range(PAGES_PER_STEP):
            for j in range(CMP_STRIDE):
                start = base + p * PAGE_HALVES * SLAB_PITCH + j * TOKEN_ROWS + combo
                x2_ref[combo, p * PAGE_HALVES:(p + 1) * PAGE_HALVES, j * HEAD_DIM:(j + 1) * HEAD_DIM] = (
                    stage_ref[pl.ds(start, PAGE_HALVES, stride=SLAB_PITCH), :])
        a = jnp.dot(x2_ref[combo].astype(BF16), w1_ref[combo // N_KV], preferred_element_type=F32)
        a_ref[combo, pl.ds(pl.multiple_of(c * step_halves, step_halves), step_halves), :] = a

    @pl.when(c == pl.num_programs(1) - 1)
    def _():
        tok = []
        for combo in range(TOKEN_ROWS):
            kind = combo // N_KV
            sl = slice(combo * HEAD_DIM, (combo + 1) * HEAD_DIM)
            row = jnp.concatenate([new_ref[j:j + 1, sl] for j in range(t_new)]
                                  + [jnp.zeros((1, (CMP_STRIDE - t_new) * HEAD_DIM), F32)], axis=1)
            tail = jnp.broadcast_to(row, (SUBLANES, CMP_STRIDE * HEAD_DIM)).astype(BF16)
            a_ref[combo, n_half:n_half + SUBLANES, :] = jnp.dot(tail, w1_ref[kind], preferred_element_type=F32)
            pre = a_ref[combo, 0:n_half, 0:CMP_HID] + a_ref[combo, pl.ds(1, n_half), CMP_HID:] + b1_ref[kind]
            out = jnp.dot(_silu(pre).astype(BF16), w2_ref[kind], preferred_element_type=F32)
            if kind == 0:
                out = _rms(out, gain_ref[...])
            tok.append(out.astype(BF16))

        qz = _lane_queries(q_ref[...], t_new)
        shape = (n_half, LANES)
        u = _lane_iota(shape) & (t_new - 1)
        rel = past + u - (_row_iota(shape) * CMP_STRIDE + CMP_BLOCK - 1)
        near0 = ((past - (CMP_BLOCK - 1) - FAR_DIST) // CMP_STRIDE + 1) // SUBLANES * SUBLANES
        bias = jnp.concatenate([jnp.broadcast_to(tabl_ref[NUM_BUCKETS - 1:NUM_BUCKETS, :], (near0, LANES)),
                                _lane_bias(rel[near0:], tabl_ref)], axis=0)
        s = jnp.where(rel >= 0, _lane_scores(tok[:N_KV], qz) + bias, -jnp.inf)
        m = jnp.max(s, axis=0, keepdims=True)
        e = jnp.exp(s - jnp.where(m == -jnp.inf, 0.0, m))
        p = e / jnp.maximum(jnp.sum(e, axis=0, keepdims=True), 1e-30)
        ocmp_ref[...] = _lane_values(tok[N_KV:], p, per_kv)

        src = _row_iota((LANES, LANES))
        dst = _lane_iota((LANES, LANES))
        pooled = (src // per_kv) * t_new + (src & (t_new - 1))
        pool = jnp.where((src < N_KV * per_kv) & (dst == pooled), 1.0, 0.0).astype(BF16)
        hi = p.astype(BF16)
        lo = (p - hi.astype(F32)).astype(BF16)
        imp = jnp.dot(hi, pool, preferred_element_type=F32) + jnp.dot(lo, pool, preferred_element_type=F32)
        si = _row_iota((sel_rows, n_half)) * SEL_BLOCK
        ci = _lane_iota((sel_rows, n_half)) * CMP_STRIDE
        overlap = (ci < si + SEL_BLOCK) & (ci + CMP_BLOCK > si) & (si < n_blocks * SEL_BLOCK)
        ov = jnp.where(overlap, 1.0, 0.0).astype(BF16)
        hi = imp.astype(BF16)
        lo = (imp - hi.astype(F32)).astype(BF16)
        imp = jnp.dot(ov, hi, preferred_element_type=F32) + jnp.dot(ov, lo, preferred_element_type=F32)

        cur = (past + (_lane_iota((sel_rows, LANES)) & (t_new - 1))) // SEL_BLOCK
        sel = _select_blocks_t(imp, cur)
        unpool = jnp.where((dst < N_KV * per_kv) & (src == (dst // per_kv) * t_new + (dst & (t_new - 1))), 1.0, 0.0)
        sel_ref[...] = jnp.dot(sel.astype(BF16), unpool.astype(BF16), preferred_element_type=F32)


def _page_specs(n_chunk_pages):
    def spec(j):
        return pl.BlockSpec((None, PAGE_ROWS, HEAD_DIM), lambda b, c, pt: (pt[b, c * n_chunk_pages + j], 0, 0))
    return [spec(j) for j in range(n_chunk_pages)]


def sample_compressed(page_table, cache3, cmp_new, qn, w1, b1, w2, kc_norm, tabl, t_new):
    n_seq, n_pages = page_table.shape
    assert n_pages % PAGES_PER_STEP == 0
    past = n_pages * PAGE
    n_half = past // CMP_STRIDE
    n_blocks = -(-(past + t_new) // SEL_BLOCK)
    sel_rows = -(-n_blocks // SUBLANES) * SUBLANES
    const = lambda *shape: pl.BlockSpec(shape, lambda b, c, pt: (0,) * len(shape))
    row = lambda width: pl.BlockSpec((t_new, width), lambda b, c, pt: (b, 0))
    out = lambda rows: pl.BlockSpec((None, rows, LANES), lambda b, c, pt: (b, 0, 0))
    grid_spec = pltpu.PrefetchScalarGridSpec(
        num_scalar_prefetch=1,
        grid=(n_seq, n_pages // PAGES_PER_STEP),
        in_specs=[pl.BlockSpec(memory_space=pl.ANY),
                  row(2 * KV_W), row(NSA_W), const(2, CMP_STRIDE * HEAD_DIM, 2 * CMP_HID), const(2, 1, CMP_HID),
                  const(2, CMP_HID, HEAD_DIM), const(1, HEAD_DIM), const(NUM_BUCKETS, LANES)],
        out_specs=[out(sel_rows), out(HEAD_DIM)],
        scratch_shapes=[pltpu.VMEM((TOKEN_ROWS, PAGES_PER_STEP * PAGE_HALVES, CMP_STRIDE * HEAD_DIM), F32),
                        pltpu.VMEM((TOKEN_ROWS, n_half + SUBLANES, 2 * CMP_HID), F32),
                        pltpu.VMEM((2 * STAGE_ROWS, HEAD_DIM), F32),
                        pltpu.SemaphoreType.DMA((2,))])
    return pl.pallas_call(
        functools.partial(_sample_cmp_kernel, past=past, t_new=t_new, n_blocks=n_blocks, sel_rows=sel_rows),
        grid_spec=grid_spec,
        out_shape=[jax.ShapeDtypeStruct((n_seq, sel_rows, LANES), F32),
                   jax.ShapeDtypeStruct((n_seq, HEAD_DIM, LANES), F32)],
        compiler_params=_cparams(("arbitrary", "arbitrary")),
        name="sample_cmp",
    )(page_table, cache3, cmp_new, qn, w1, b1, w2, kc_norm.reshape(1, HEAD_DIM), tabl)


def _sample_sel_kernel(pt_ref, *refs, past, t_new, win_buf):
    pages = refs[:PAGES_PER_STEP]
    (new_ref, q_ref, sel_ref, ocmp_ref, win_ref, wnew_ref, gate_ref, tabl_ref,
     o_ref, m_ref, l_ref, acc_ref, owin_ref, s_ref) = refs[PAGES_PER_STEP:]
    c = pl.program_id(1)
    per_kv = GROUP * t_new
    qz = _lane_queries(q_ref[...], t_new)
    far = tabl_ref[NUM_BUCKETS - 1:NUM_BUCKETS, :]
    new_rows = 2 * SUBLANES

    def new_kv(ref, kind):
        return [_pad_rows(ref[:, (kind * N_KV + kv) * HEAD_DIM:(kind * N_KV + kv + 1) * HEAD_DIM], new_rows).astype(BF16)
                for kv in range(N_KV)]

    def dist(pos0, rows):
        shape = (rows, LANES)
        return past + (_lane_iota(shape) & (t_new - 1)) - pos0 - _row_iota(shape)

    def update(s, values):
        m_prev = m_ref[...]
        m_new = jnp.maximum(m_prev, jnp.max(s, axis=0, keepdims=True))
        m_safe = jnp.where(m_new == -jnp.inf, 0.0, m_new)
        alpha = jnp.exp(m_prev - m_safe)
        p = jnp.exp(s - m_safe)
        l_ref[...] = alpha * l_ref[...] + jnp.sum(p, axis=0, keepdims=True)
        acc = alpha * acc_ref[...]
        for vals, lo, hi_ in values:
            acc = acc + _lane_values(vals, p[lo:hi_], per_kv)
        acc_ref[...] = acc
        m_ref[...] = m_new

    @pl.when(c == 0)
    def _():
        m_ref[...] = jnp.full(m_ref.shape, -jnp.inf, F32)
        l_ref[...] = jnp.zeros(l_ref.shape, F32)
        acc_ref[...] = jnp.zeros(acc_ref.shape, F32)

        kw = [win_ref[pl.ds(kv, win_buf, stride=TOKEN_ROWS), :].astype(BF16) for kv in range(N_KV)]
        vw = [win_ref[pl.ds(N_KV + kv, win_buf, stride=TOKEN_ROWS), :].astype(BF16) for kv in range(N_KV)]
        pieces = []
        near0 = (win_buf - FAR_DIST + 1) // SUBLANES * SUBLANES
        for keys, rel, near in ((kw, dist(past - win_buf, win_buf), near0), (new_kv(wnew_ref, 0), dist(past, new_rows), 0)):
            bias = _lane_bias(rel[near:], tabl_ref)
            if near:
                bias = jnp.concatenate([jnp.broadcast_to(far, (near, LANES)), bias], axis=0)
            s = _lane_scores(keys, qz) + bias
            pieces.append(jnp.where((rel >= 0) & (rel < WINDOW), s, -jnp.inf))
        m = jnp.maximum(jnp.max(pieces[0], axis=0, keepdims=True), jnp.max(pieces[1], axis=0, keepdims=True))
        m = jnp.where(m == -jnp.inf, 0.0, m)
        e0 = jnp.exp(pieces[0] - m)
        e1 = jnp.exp(pieces[1] - m)
        tot = jnp.sum(e0, axis=0, keepdims=True) + jnp.sum(e1, axis=0, keepdims=True)
        num = _lane_values(vw, e0, per_kv) + _lane_values(new_kv(wnew_ref, 1), e1, per_kv)
        owin_ref[...] = num / jnp.maximum(tot, 1e-30)

        rel = dist(past, new_rows)
        s = _lane_scores(new_kv(new_ref, 0), qz) + _lane_bias(rel, tabl_ref)
        chosen = sel_ref[past // SEL_BLOCK:past // SEL_BLOCK + 1, :] > 0.5
        update(jnp.where(chosen & (rel >= 0), s, -jnp.inf), [(new_kv(new_ref, 1), 0, new_rows)])

    values = []
    for p in range(PAGES_PER_STEP):
        page = c * PAGES_PER_STEP + p
        keys = [pages[p][pl.ds(kv, PAGE, stride=TOKEN_ROWS), :].astype(BF16) for kv in range(N_KV)]
        s = _lane_scores(keys, qz) + far
        per_block = PAGE // SEL_BLOCK
        chosen = jnp.concatenate(
            [jnp.broadcast_to(sel_ref[pl.ds(page * per_block + i, 1), :], (SEL_BLOCK, LANES)) for i in range(per_block)],
            axis=0)
        s_ref[p * PAGE:(p + 1) * PAGE, :] = jnp.where(chosen > 0.5, s, -jnp.inf)
        vals = [pages[p][pl.ds(N_KV + kv, PAGE, stride=TOKEN_ROWS), :].astype(BF16) for kv in range(N_KV)]
        values.append((vals, p * PAGE, (p + 1) * PAGE))

    @pl.when(c == pl.num_programs(1) - 1)
    def _():
        rows = slice((PAGES_PER_STEP - 1) * PAGE, PAGES_PER_STEP * PAGE)
        s_ref[rows, :] = s_ref[rows, :] + (_lane_bias(dist(past - PAGE, PAGE), tabl_ref) - far)

    update(s_ref[...], values)

    @pl.when(c == pl.num_programs(1) - 1)
    def _():
        gt = _sigmoid(gate_ref[...])
        osel = acc_ref[...] / jnp.maximum(l_ref[...], 1e-30)
        o_ref[...] = gt[0:1] * ocmp_ref[...] + gt[1:2] * osel + gt[2:3] * owin_ref[...]


def sample_selected(page_table, cache3, sel_new, qn, sel_t, ocmp_t, win3, win_new, gate_l, tabl, t_new):
    n_seq, n_pages = page_table.shape
    past = n_pages * PAGE
    win_buf = win3.shape[1] // TOKEN_ROWS
    assert (past - FAR_DIST) // PAGE >= n_pages - 1, "only the last page may hold near keys"
    sel_rows = sel_t.shape[1]
    const = lambda *shape: pl.BlockSpec(shape, lambda b, c, pt: (0,) * len(shape))
    row = lambda width: pl.BlockSpec((t_new, width), lambda b, c, pt: (b, 0))
    per_seq = lambda rows, width: pl.BlockSpec((None, rows, width), lambda b, c, pt: (b, 0, 0))
    grid_spec = pltpu.PrefetchScalarGridSpec(
        num_scalar_prefetch=1,
        grid=(n_seq, n_pages // PAGES_PER_STEP),
        in_specs=_page_specs(PAGES_PER_STEP) + [
            row(2 * KV_W), row(NSA_W), per_seq(sel_rows, LANES), per_seq(HEAD_DIM, LANES),
            per_seq(win_buf * TOKEN_ROWS, HEAD_DIM), row(2 * KV_W), per_seq(SUBLANES, LANES),
            const(NUM_BUCKETS, LANES)],
        out_specs=per_seq(HEAD_DIM, LANES),
        scratch_shapes=[pltpu.VMEM((1, LANES), F32), pltpu.VMEM((1, LANES), F32), pltpu.VMEM((HEAD_DIM, LANES), F32),
                        pltpu.VMEM((HEAD_DIM, LANES), F32), pltpu.VMEM((PAGES_PER_STEP * PAGE, LANES), F32)])
    return pl.pallas_call(
        functools.partial(_sample_sel_kernel, past=past, t_new=t_new, win_buf=win_buf),
        grid_spec=grid_spec,
        out_shape=jax.ShapeDtypeStruct((n_seq, HEAD_DIM, LANES), F32),
        compiler_params=_cparams(("parallel", "arbitrary")),
        name="sample_sel",
    )(page_table, *([cache3] * PAGES_PER_STEP), sel_new, qn, sel_t, ocmp_t, win3, win_new, gate_l, tabl)


def _split_in_proj(w_in):
    d, cols = w_in.shape
    assert cols == COL_GATE + N_GATE + MERGE_COLS
    wt = jnp.swapaxes(w_in, 0, 1)
    return wt[:PROJ_COLS].astype(BF16), wt[COL_GATE + N_GATE:].astype(BF16)


def _compress_weights(k_w1, k_b1, k_w2, v_w1, v_b1, v_w2):
    half = CMP_STRIDE * HEAD_DIM
    cat = lambda w: jnp.concatenate([w[:half], w[half:]], axis=1)
    w1 = jnp.stack([cat(k_w1), cat(v_w1)]).astype(BF16)
    b1 = jnp.stack([k_b1, v_b1]).reshape(2, 1, CMP_HID)
    w2 = jnp.stack([k_w2, v_w2]).astype(BF16)
    return w1, b1, w2


def _lane_table(rel_bias, t_new):
    tab = jnp.repeat(rel_bias, t_new, axis=1)
    return jnp.pad(tab, ((0, 0), (0, LANES - tab.shape[1])))


def _gate_lanes(proj, n_seq, t_new):
    per_kv = GROUP * 3
    g = jnp.stack([proj[:, COL_GATE + kv * per_kv:COL_GATE + (kv + 1) * per_kv] for kv in range(N_KV)], axis=1)
    g = g.reshape(n_seq, t_new, N_KV, GROUP, 3).transpose(0, 4, 2, 3, 1).reshape(n_seq, 3, N_HEADS_B * t_new)
    return jnp.pad(g, ((0, 0), (0, SUBLANES - 3), (0, LANES - N_HEADS_B * t_new)))


def _tiles(m, t_len):
    big = m >= 1024
    return dict(proj_tm=1024 if big else m, proj_tn=1024, prep_tm=512 if big else m, post_tm=256,
                mlp_tm=512 if big else m, mlp_tf=1024,
                hgrn_chunk=min(t_len, 256), hgrn_heads=2 if big else N_HEADS_A)


def _trunk(x, mods, s0, w, nsa_fn, attention_operands):
    nb, t_len, d = x.shape
    m = nb * t_len
    tl = _tiles(m, t_len)
    sh1, sc1, g1, sh2, sc2, g2 = mods
    x2 = x.reshape(m, d)
    proj, merge = in_proj(x2, w["norm1"], sc1, sh1, *w["w_in"], t_len, tl["proj_tm"], tl["proj_tn"])
    o_a, s_fin = hgrn(proj, w["lb"], w["hgrn_onorm"], s0, nb, t_len, tl["hgrn_chunk"], tl["hgrn_heads"])
    qn, cmp_new, sel_new, win_new, *att = nsa_prep(proj, w["q_norm"], w["ks_norm"], w["kw_norm"], tl["prep_tm"], t_len,
                                                   attention_operands)
    o_b = nsa_fn(proj, qn, cmp_new, sel_new, win_new, *att[:1])
    x1, h2 = post_attn(x2, o_a, o_b, merge, w["w_ba"], w["w_bb"], w["w_out"], g1, w["norm2"], sc2, sh2, t_len,
                       tl["post_tm"])
    y = mlp(h2, x1, w["mlp_w1"], w["mlp_w2"], g2, t_len, tl["mlp_tm"], tl["mlp_tf"])
    if attention_operands:
        cmp_new, sel_new = att[1:]
    return y.reshape(nb, t_len, d), s_fin, (cmp_new, sel_new, win_new)


def kernel(x_prompt, x_sample, c_prompt, c_sample, cache_cmp_kv, cache_sel_kv, cache_win_kv, state_hgrn, page_table, hgrn_lb_logits, rel_bias, ada_w, ada_b, norm1, norm2, w_in, hgrn_onorm, nsa_q_norm, nsa_kc_norm, nsa_ks_norm, nsa_kw_norm, cmp_k_w1, cmp_k_b1, cmp_k_w2, cmp_v_w1, cmp_v_b1, cmp_v_w2, w_branch_a, w_branch_b, w_out, mlp_w1, mlp_w2):
    n_p, t_p, d = x_prompt.shape
    n_s, t_s, _ = x_sample.shape
    past = page_table.shape[1] * PAGE
    win_buf = cache_win_kv.shape[2]
    kvw = 2 * KV_W
    layer = 0

    lb_all = jnp.cumsum(jax.nn.softmax(hgrn_lb_logits.astype(F32), axis=0), axis=0)
    cw1, cb1, cw2 = _compress_weights(cmp_k_w1[layer], cmp_k_b1[layer], cmp_k_w2[layer],
                                      cmp_v_w1[layer], cmp_v_b1[layer], cmp_v_w2[layer])
    w = dict(norm1=norm1[layer], norm2=norm2[layer], w_in=_split_in_proj(w_in[layer]),
             hgrn_onorm=hgrn_onorm[layer], lb=lb_all[layer], q_norm=nsa_q_norm[layer],
             ks_norm=nsa_ks_norm[layer], kw_norm=nsa_kw_norm[layer],
             w_ba=w_branch_a[layer].astype(BF16), w_bb=w_branch_b[layer].astype(BF16),
             w_out=w_out[layer].astype(BF16), mlp_w1=mlp_w1[layer].astype(BF16), mlp_w2=mlp_w2[layer].astype(BF16))
    tab = rel_bias.astype(F32).reshape(NUM_BUCKETS * N_HEADS_B)

    mods = ada_mods(jnp.concatenate([c_prompt, c_sample], axis=0).astype(F32), ada_w[layer], ada_b[layer])
    mods = jnp.split(mods, 6, axis=-1)
    mods_p = [a[:n_p].reshape(n_p, 1, d) for a in mods]
    mods_s = [jnp.repeat(a[n_p:], t_s, axis=0).reshape(1, n_s * t_s, d) for a in mods]

    def nsa_p(proj, qn, cmp_new, sel_new, win_new, att):
        tail = jnp.zeros((n_p * CMP_STRIDE, kvw), F32)
        cmp_tok = compress(cmp_new, tail, cw1, cb1, cw2, nsa_kc_norm[layer], n_p, t_p // CMP_STRIDE)
        return nsa_prompt_attention(tab, qn, cmp_tok, att, proj, n_p, t_p)

    s0_p = jnp.zeros((n_p, N_HEADS_A, HEAD_DIM, HEAD_DIM), F32)
    y_p, hg_p, (cmp_p, sel_p, win_p) = _trunk(x_prompt, mods_p, s0_p, w, nsa_p, True)

    def nsa_s(proj, qn, cmp_new, sel_new, win_new):
        n_phys = cache_cmp_kv.shape[1]
        cmp3 = cache_cmp_kv[layer].reshape(n_phys, PAGE_ROWS, HEAD_DIM)
        sel3 = cache_sel_kv[layer].reshape(n_phys, PAGE_ROWS, HEAD_DIM)
        win3 = cache_win_kv[layer].reshape(n_s, win_buf * TOKEN_ROWS, HEAD_DIM)
        tabl = _lane_table(rel_bias.astype(F32), t_s)
        sel_t, ocmp_t = sample_compressed(page_table, cmp3, cmp_new, qn, cw1, cb1, cw2, nsa_kc_norm[layer], tabl, t_s)
        o_t = sample_selected(page_table, sel3, sel_new, qn, sel_t, ocmp_t, win3, win_new,
                              _gate_lanes(proj, n_s, t_s), tabl, t_s)
        o = o_t[:, :, :N_HEADS_B * t_s].reshape(n_s, HEAD_DIM, N_HEADS_B, t_s)
        return o.transpose(0, 3, 2, 1).reshape(n_s * t_s, NSA_W).astype(BF16)

    y_s, hg_s, (cmp_s, sel_s, win_s) = _trunk(x_sample, mods_s, state_hgrn[layer], w, nsa_s, False)

    dt = x_prompt.dtype
    pages = lambda a: a.reshape(1, n_p, t_p // PAGE, PAGE, 2, N_KV, HEAD_DIM).astype(dt)
    rows_s = lambda a: a.reshape(1, n_s, t_s, 2, N_KV, HEAD_DIM).astype(dt)
    win_keep = min(WINDOW, t_p)
    win_p_out = win_p.reshape(n_p, t_p, kvw)[:, t_p - win_keep:].reshape(1, n_p, win_keep, 2, N_KV, HEAD_DIM)
    win_rows = cache_win_kv[layer].reshape(n_s, win_buf * TOKEN_ROWS, HEAD_DIM).astype(F32)
    win_s_out = jnp.concatenate([win_rows, win_s.reshape(n_s, t_s * TOKEN_ROWS, HEAD_DIM)], axis=1)
    win_s_out = win_s_out[:, t_s * TOKEN_ROWS:].reshape(1, n_s, win_buf, 2, N_KV, HEAD_DIM)
    return (y_p, y_s, pages(cmp_p), rows_s(cmp_s), pages(sel_p), rows_s(sel_s),
            win_p_out.astype(dt), win_s_out.astype(dt), hg_p[None].astype(dt), hg_s[None].astype(dt))
```

```python
import functools
import math

import numpy as np
import jax
import jax.numpy as jnp
from jax import lax
from jax.experimental import pallas as pl
from jax.experimental.pallas import tpu as pltpu

F32 = jnp.float32
BF16 = jnp.bfloat16

D_MODEL = 2048
N_HEADS_A = 8
HEAD_DIM = 128
HGRN_W = N_HEADS_A * HEAD_DIM
N_HEADS_B = 8
N_KV = 2
GROUP = N_HEADS_B // N_KV
NSA_W = N_HEADS_B * HEAD_DIM
KV_W = N_KV * HEAD_DIM
PAGE = 128
CMP_STRIDE = 16
CMP_BLOCK = 32
CMP_HID = 128
SEL_BLOCK = 64
N_SEL = 16
WINDOW = 512
FORCE_SCORE = 1.0e6
NUM_BUCKETS = 32
REL_MAX_DIST = 128
D_FF = 4 * D_MODEL
EPS = 1e-6
N_GATE = 3 * N_HEADS_B

LANES = 128
SUBLANES = 8
VMEM_LIMIT = 56 * 1024 * 1024
VMEM_LIMIT_MAX = 60 * 1024 * 1024

COL_HGRN = 0
COL_QB = 4 * HGRN_W
COL_CMP = COL_QB + NSA_W
COL_SEL = COL_CMP + 2 * KV_W
COL_WIN = COL_SEL + 2 * KV_W
COL_GATE = COL_WIN + 2 * KV_W
GATE_COLS = 512
PROJ_COLS = COL_GATE + GATE_COLS
MERGE_COLS = 2 * D_MODEL


def _cparams(sem):
    return pltpu.CompilerParams(dimension_semantics=sem, vmem_limit_bytes=VMEM_LIMIT)


def _sigmoid(x):
    return 1.0 / (1.0 + jnp.exp(-x))


def _silu(x):
    return x * _sigmoid(x)


def _rms(x, gain):
    return x * lax.rsqrt(jnp.mean(x * x, axis=-1, keepdims=True) + EPS) * gain


def _lane_iota(shape):
    return lax.broadcasted_iota(jnp.int32, shape, 1)


def _row_iota(shape):
    return lax.broadcasted_iota(jnp.int32, shape, 0)


def _ada_kernel(c_ref, w_ref, b_ref, o_ref):
    a = _silu(c_ref[...]).astype(BF16)
    o_ref[...] = jnp.dot(a, w_ref[...].astype(BF16), preferred_element_type=F32) + b_ref[...]


def ada_mods(c, w, b):
    r, d = c.shape
    n = w.shape[1]
    tn = 1024
    return pl.pallas_call(
        _ada_kernel,
        grid=(n // tn,),
        in_specs=[pl.BlockSpec((r, d), lambda j: (0, 0)),
                  pl.BlockSpec((d, tn), lambda j: (0, j)),
                  pl.BlockSpec((1, tn), lambda j: (0, j))],
        out_specs=pl.BlockSpec((r, tn), lambda j: (0, j)),
        out_shape=jax.ShapeDtypeStruct((r, n), F32),
        compiler_params=_cparams(("parallel",)),
        name="ada_mods",
    )(c, w, b.reshape(1, n))


def _inproj_kernel(x_ref, gain_ref, sc_ref, sh_ref, wa_ref, wb_ref, oa_ref, ob_ref, h_ref, *, n_a):
    j = pl.program_id(1)

    @pl.when(j == 0)
    def _():
        h = _rms(x_ref[...], gain_ref[...]) * (1.0 + sc_ref[0]) + sh_ref[0]
        h_ref[...] = h.astype(BF16)

    nt = (((1,), (1,)), ((), ()))

    @pl.when(j < n_a)
    def _():
        oa_ref[...] = lax.dot_general(h_ref[...], wa_ref[...], nt, preferred_element_type=F32)

    @pl.when(j >= n_a)
    def _():
        ob_ref[...] = lax.dot_general(h_ref[...], wb_ref[...], nt, preferred_element_type=F32)


def _mod_spec(mod, tm, rows_per_batch):
    d = mod.shape[-1]
    if mod.shape[1] == 1:
        return pl.BlockSpec((1, 1, d), lambda i, *_: ((i * tm) // rows_per_batch, 0, 0))
    return pl.BlockSpec((1, tm, d), lambda i, *_: (0, i, 0))


def in_proj(x2, gain, sc, sh, w_a, cols_a, w_b, rows_per_batch, tm, tn):
    m, d = x2.shape
    n_a = cols_a // tn
    n_b = w_b.shape[0] // tn
    col_a = lambda i, j: jnp.minimum(j, n_a - 1)
    col_b = lambda i, j: jnp.maximum(j - n_a, 0)
    return pl.pallas_call(
        functools.partial(_inproj_kernel, n_a=n_a),
        grid=(m // tm, n_a + n_b),
        in_specs=[pl.BlockSpec((tm, d), lambda i, j: (i, 0)),
                  pl.BlockSpec((1, d), lambda i, j: (0, 0)),
                  _mod_spec(sc, tm, rows_per_batch),
                  _mod_spec(sh, tm, rows_per_batch),
                  pl.BlockSpec((tn, d), lambda i, j: (col_a(i, j), 0)),
                  pl.BlockSpec((tn, d), lambda i, j: (col_b(i, j), 0))],
        out_specs=[pl.BlockSpec((tm, tn), lambda i, j: (i, col_a(i, j))),
                   pl.BlockSpec((tm, tn), lambda i, j: (i, col_b(i, j)))],
        out_shape=[jax.ShapeDtypeStruct((m, cols_a), F32), jax.ShapeDtypeStruct((m, w_b.shape[0]), F32)],
        scratch_shapes=[pltpu.VMEM((tm, d), BF16)],
        compiler_params=pltpu.CompilerParams(dimension_semantics=("parallel", "arbitrary"),
                                             vmem_limit_bytes=VMEM_LIMIT_MAX),
        name="in_proj",
    )(x2, gain.reshape(1, d), sc, sh, w_a, w_b)


def _post_kernel(x_ref, oa_ref, ob_ref, mga_ref, mgb_ref, wba_ref, wbb_ref, wout_ref,
                 g1_ref, gain2_ref, sc2_ref, sh2_ref, x1_ref, h2_ref):
    ya = jnp.dot(oa_ref[...], wba_ref[...], preferred_element_type=F32)
    yb = jnp.dot(ob_ref[...], wbb_ref[...], preferred_element_type=F32)
    merged = _sigmoid(mga_ref[...]) * ya + _sigmoid(mgb_ref[...]) * yb
    y = jnp.dot(merged.astype(BF16), wout_ref[...], preferred_element_type=F32)
    x1 = x_ref[...] + g1_ref[0] * y
    x1_ref[...] = x1
    h2_ref[...] = (_rms(x1, gain2_ref[...]) * (1.0 + sc2_ref[0]) + sh2_ref[0]).astype(BF16)


def post_attn(x2, oa, ob, proj, wba, wbb, wout, g1, gain2, sc2, sh2, rows_per_batch, tm):
    m, d = x2.shape
    const = lambda i: (0, 0)
    resident = lambda shape: pl.BlockSpec(shape, const, pipeline_mode=pl.Buffered(1))
    return pl.pallas_call(
        _post_kernel,
        grid=(m // tm,),
        in_specs=[pl.BlockSpec((tm, d), lambda i: (i, 0)),
                  pl.BlockSpec((tm, HGRN_W), lambda i: (i, 0)),
                  pl.BlockSpec((tm, NSA_W), lambda i: (i, 0)),
                  pl.BlockSpec((tm, d), lambda i: (i, 0)),
                  pl.BlockSpec((tm, d), lambda i: (i, 1)),
                  resident((HGRN_W, d)),
                  resident((NSA_W, d)),
                  resident((d, d)),
                  _mod_spec(g1, tm, rows_per_batch),
                  pl.BlockSpec((1, d), const),
                  _mod_spec(sc2, tm, rows_per_batch),
                  _mod_spec(sh2, tm, rows_per_batch)],
        out_specs=[pl.BlockSpec((tm, d), lambda i: (i, 0)),
                   pl.BlockSpec((tm, d), lambda i: (i, 0))],
        out_shape=[jax.ShapeDtypeStruct((m, d), F32), jax.ShapeDtypeStruct((m, d), BF16)],
        compiler_params=_cparams(("parallel",)),
        name="post_attn",
    )(x2, oa, ob, proj, proj, wba, wbb, wout, g1, gain2.reshape(1, d), sc2, sh2)


def _mlp_kernel(h_ref, x1_ref, w1_ref, w2_ref, g2_ref, y_ref, acc_ref):
    f = pl.program_id(1)
    u = jnp.maximum(jnp.dot(h_ref[...], w1_ref[...], preferred_element_type=F32), 0.0)
    part = jnp.dot((u * u).astype(BF16), w2_ref[...], preferred_element_type=F32)

    @pl.when(f == 0)
    def _():
        acc_ref[...] = part

    @pl.when(f > 0)
    def _():
        acc_ref[...] += part

    @pl.when(f == pl.num_programs(1) - 1)
    def _():
        y_ref[...] = x1_ref[...] + g2_ref[0] * acc_ref[...]


def mlp(h2, x1, w1, w2, g2, rows_per_batch, tm, tf):
    m, d = x1.shape
    ff = w1.shape[1]
    return pl.pallas_call(
        _mlp_kernel,
        grid=(m // tm, ff // tf),
        in_specs=[pl.BlockSpec((tm, d), lambda i, f: (i, 0)),
                  pl.BlockSpec((tm, d), lambda i, f: (i, 0)),
                  pl.BlockSpec((d, tf), lambda i, f: (0, f)),
                  pl.BlockSpec((tf, d), lambda i, f: (f, 0)),
                  _mod_spec(g2, tm, rows_per_batch)],
        out_specs=pl.BlockSpec((tm, d), lambda i, f: (i, 0)),
        out_shape=jax.ShapeDtypeStruct((m, d), F32),
        scratch_shapes=[pltpu.VMEM((tm, d), F32)],
        compiler_params=_cparams(("parallel", "arbitrary")),
        name="mlp",
    )(h2, x1, w1, w2, g2)


_NT = (((1,), (1,)), ((), ()))
_TN = (((0,), (0,)), ((), ()))


def _hgrn_kernel(q_ref, z_ref, v_ref, g_ref, lb_ref, on_ref, s0_ref, o_ref, sfin_ref, st_ref, b_ref, *, chunk, heads):
    for h in range(heads):
        _hgrn_head(h, q_ref, z_ref, v_ref, g_ref, lb_ref, on_ref, s0_ref, o_ref, sfin_ref, st_ref, b_ref, chunk)


def _hgrn_head(h, q_ref, z_ref, v_ref, g_ref, lb_ref, on_ref, s0_ref, o_ref, sfin_ref, st_ref, b_ref, chunk):
    c = pl.program_id(2)
    cols = slice(h * HEAD_DIM, (h + 1) * HEAD_DIM)

    @pl.when(c == 0)
    def _():
        st_ref[h] = s0_ref[0, h].T

    q = q_ref[:, cols]
    z = z_ref[:, cols]
    v = v_ref[:, cols]
    lb = lb_ref[h]
    e = jnp.exp(-jnp.abs(z))
    r = 1.0 / (1.0 + e)
    pos = z >= 0.0
    logf = jnp.log(lb + (1.0 - lb) * jnp.where(pos, r, e * r))
    k = (1.0 - lb) * jnp.where(pos, e * r, r)

    t = lax.broadcasted_iota(jnp.int32, (chunk, HEAD_DIM), 0)
    b = logf
    s = 1
    while s < chunk:
        b = b + jnp.where(t >= s, pltpu.roll(b, s, 0), 0.0)
        s *= 2
    b_ref[h] = b

    t8 = t & (SUBLANES - 1)
    if chunk <= SUBLANES:
        o = jnp.zeros((chunk, HEAD_DIM), F32)
        for d in range(SUBLANES):
            kd, bd, vd = (k, b, v) if d == 0 else (pltpu.roll(k, d, 0), pltpu.roll(b, d, 0), pltpu.roll(v, d, 0))
            w = jnp.exp(jnp.where(t8 >= d, b - bd, -jnp.inf))
            o = o + jnp.sum(q * kd * w, axis=-1, keepdims=True) * vd
    else:
        o = jnp.sum(q * k, axis=-1, keepdims=True) * v
        row = lax.broadcasted_iota(jnp.int32, (chunk, chunk), 0)
        col = lax.broadcasted_iota(jnp.int32, (chunk, chunk), 1)
        apart = row ^ col
        att = jnp.zeros((chunk, chunk), F32)
        sub = _row_iota((SUBLANES, HEAD_DIM))
        m = 1
        while m < chunk:
            def ref_row(r, rows):
                return jnp.broadcast_to(b_ref[h, pl.ds(r, 1), :], (rows, HEAD_DIM))

            if 2 * m >= SUBLANES:
                refs = [ref_row(blk * 2 * m + m - 1, 2 * m) for blk in range(chunk // (2 * m))]
            else:
                refs = []
                for tile in range(chunk // SUBLANES):
                    ref_t = ref_row(tile * SUBLANES + m - 1, SUBLANES)
                    for j in range(1, SUBLANES // (2 * m)):
                        ref_t = jnp.where(sub >= j * 2 * m, ref_row(tile * SUBLANES + j * 2 * m + m - 1, SUBLANES), ref_t)
                    refs.append(ref_t)
            ref_b = refs[0] if len(refs) == 1 else jnp.concatenate(refs, axis=0)
            second = (t & (2 * m - 1)) >= m
            w = jnp.exp(jnp.where(second, b - ref_b, ref_b - b))
            qs = jnp.where(second, q * w, 0.0).astype(BF16)
            ks = jnp.where(second, 0.0, k * w).astype(BF16)
            a_m = lax.dot_general(qs, ks, _NT, preferred_element_type=F32)
            att = att + jnp.where(apart < 2 * m, a_m, 0.0)
            m *= 2
        o = o + jnp.dot(att.astype(BF16), v.astype(BF16), preferred_element_type=F32)

    st = st_ref[h]
    b_last = b_ref[h, pl.ds(chunk - 1, 1), :]
    o = o + lax.dot_general((q * jnp.exp(b)).astype(BF16), st.astype(BF16), _NT, preferred_element_type=F32)
    kt = (k * jnp.exp(b_last - b)).astype(BF16)
    vb = v.astype(BF16)
    if chunk < 2 * SUBLANES:
        pad = jnp.zeros((2 * SUBLANES - chunk, HEAD_DIM), BF16)
        kt = jnp.concatenate([kt, pad], axis=0)
        vb = jnp.concatenate([vb, pad], axis=0)
    st_new = jnp.exp(b_last) * st + lax.dot_general(vb, kt, _TN, preferred_element_type=F32)
    st_ref[h] = st_new

    o_ref[:, cols] = (_rms(o, on_ref[h]) * _silu(g_ref[:, cols])).astype(o_ref.dtype)

    @pl.when(c == pl.num_programs(2) - 1)
    def _():
        sfin_ref[0, h] = st_new.T


def hgrn(proj, lb, onorm, s0, n_batch, t_len, chunk, heads):
    m = proj.shape[0]
    n_c = t_len // chunk
    hb = N_HEADS_A // heads
    width = heads * HEAD_DIM

    def col(group):
        return pl.BlockSpec((chunk, width), lambda bi, h, c: (bi * n_c + c, group * hb + h))

    vec = pl.BlockSpec((heads, 1, HEAD_DIM), lambda bi, h, c: (h, 0, 0))
    state = pl.BlockSpec((1, heads, HEAD_DIM, HEAD_DIM), lambda bi, h, c: (bi, h, 0, 0))
    o_dtype = BF16 if chunk % (2 * SUBLANES) == 0 else F32
    o, s_fin = pl.pallas_call(
        functools.partial(_hgrn_kernel, chunk=chunk, heads=heads),
        grid=(n_batch, hb, n_c),
        in_specs=[col(0), col(1), col(2), col(3), vec, vec, state],
        out_specs=[pl.BlockSpec((chunk, width), lambda bi, h, c: (bi * n_c + c, h)), state],
        out_shape=[jax.ShapeDtypeStruct((m, HGRN_W), o_dtype),
                   jax.ShapeDtypeStruct((n_batch, N_HEADS_A, HEAD_DIM, HEAD_DIM), F32)],
        scratch_shapes=[pltpu.VMEM((heads, HEAD_DIM, HEAD_DIM), F32), pltpu.VMEM((heads, chunk, HEAD_DIM), F32)],
        compiler_params=_cparams(("parallel", "parallel", "arbitrary")),
        name="hgrn",
    )(proj, proj, proj, proj, lb.reshape(N_HEADS_A, 1, HEAD_DIM), onorm.reshape(N_HEADS_A, 1, HEAD_DIM), s0)
    return o.astype(BF16), s_fin


def _bucket_steps():
    n = np.arange(REL_MAX_DIST)
    exact = NUM_BUCKETS // 2
    val = np.log(np.maximum(n, 1) / exact) / math.log(REL_MAX_DIST / exact) * (NUM_BUCKETS - exact)
    frac = np.abs(val - np.round(val))[exact + 1:]
    assert frac.min() > 1e-3, "a bucket edge sits on an integer distance"
    lut = np.where(n < exact, n, np.minimum(exact + np.floor(np.maximum(val, 0.0)).astype(np.int64), NUM_BUCKETS - 1))
    assert lut[-1] == NUM_BUCKETS - 1
    return int(lut[0]), [(int(i), int(lut[i])) for i in range(1, REL_MAX_DIST) if lut[i] != lut[i - 1]]


_BUCKET0, _BUCKET_EDGES = _bucket_steps()
FAR_DIST = _BUCKET_EDGES[-1][0]


def _bias(rel, tab_ref, head):
    val = jnp.full(rel.shape, tab_ref[_BUCKET0 * N_HEADS_B + head], F32)
    for edge, bucket in _BUCKET_EDGES:
        val = jnp.where(rel >= edge, tab_ref[bucket * N_HEADS_B + head], val)
    return val


def _stack_heads(x):
    return jnp.concatenate([x[:, g * HEAD_DIM:(g + 1) * HEAD_DIM] for g in range(GROUP)], axis=0)


def _tile_heads(x):
    return jnp.concatenate([x] * GROUP, axis=0)


def _masked_softmax(s, mask):
    s = jnp.where(mask, s, -jnp.inf)
    m = jnp.max(s, axis=-1, keepdims=True)
    m = jnp.where(m == -jnp.inf, 0.0, m)
    e = jnp.exp(s - m)
    return e / jnp.maximum(jnp.sum(e, axis=-1, keepdims=True), 1e-30)


def _block_importance(p, n_cmp, n_blocks, width):
    rows = p.shape[0] // GROUP
    imp = p[0:rows]
    for g in range(1, GROUP):
        imp = imp + p[g * rows:(g + 1) * rows]
    ci = lax.broadcasted_iota(jnp.int32, (n_cmp, width), 0) * CMP_STRIDE
    si = lax.broadcasted_iota(jnp.int32, (n_cmp, width), 1) * SEL_BLOCK
    overlap = (ci < si + SEL_BLOCK) & (ci + CMP_BLOCK > si) & (si < n_blocks * SEL_BLOCK)
    ov = jnp.where(overlap, 1.0, 0.0).astype(BF16)
    hi = imp.astype(BF16)
    lo = (imp - hi.astype(F32)).astype(BF16)
    return jnp.dot(hi, ov, preferred_element_type=F32) + jnp.dot(lo, ov, preferred_element_type=F32)


def _select_blocks_t(imp_t, cur):
    blk = _row_iota(imp_t.shape)
    forced = (blk == 0) | (blk == cur) | (blk == cur - 1)
    score = jnp.where(blk <= cur, imp_t + jnp.where(forced, FORCE_SCORE, 0.0), -jnp.inf)
    sel = jnp.zeros(imp_t.shape, F32)
    n_rows = imp_t.shape[0]
    for _ in range(N_SEL):
        top = jnp.max(score, axis=0, keepdims=True)
        first = jnp.min(jnp.where(score == top, blk, n_rows), axis=0, keepdims=True)
        pick = blk == first
        sel = jnp.where(pick, 1.0, sel)
        score = jnp.where(pick, -jnp.inf, score)
    return sel


def _gate_mix(gate, parts):
    rows = gate.shape[0]
    gt = _sigmoid(gate)
    outs = []
    for g in range(GROUP):
        o = gt[:, 3 * g:3 * g + 1] * parts[0][g * rows:(g + 1) * rows]
        for j in (1, 2):
            o = o + gt[:, 3 * g + j:3 * g + j + 1] * parts[j][g * rows:(g + 1) * rows]
        outs.append(o)
    return jnp.concatenate(outs, axis=1)


ATT_KSA = 0
ATT_VS = N_KV * 2 * HEAD_DIM
ATT_KW = ATT_VS + KV_W
ATT_VW = ATT_KW + KV_W
ATT_COLS = ATT_VW + KV_W


def _nsa_prep_kernel(q_ref, cmp_ref, sel_ref, win_ref, qn_ref, ksn_ref, kwn_ref, qo_ref, co_ref, so_ref, wo_ref,
                     *att_ref, t_len):
    q = q_ref[...]
    scale = HEAD_DIM ** -0.5
    for h in range(N_HEADS_B):
        sl = slice(h * HEAD_DIM, (h + 1) * HEAD_DIM)
        qo_ref[:, sl] = _rms(q[:, sl], qn_ref[...]) * scale
    co_ref[...] = cmp_ref[...]
    for src, gain, dst in ((sel_ref, ksn_ref, so_ref), (win_ref, kwn_ref, wo_ref)):
        x = src[...]
        for h in range(N_KV):
            sl = slice(h * HEAD_DIM, (h + 1) * HEAD_DIM)
            dst[:, sl] = _rms(x[:, sl], gain[...])
        dst[:, KV_W:] = x[:, KV_W:]
    if att_ref:
        att_ref, crow_ref, srow_ref = att_ref
        tm = q.shape[0]
        for combo in range(TOKEN_ROWS):
            sl = slice(combo * HEAD_DIM, (combo + 1) * HEAD_DIM)
            crow_ref[pl.ds(combo, tm, stride=TOKEN_ROWS), :] = co_ref[:, sl]
            srow_ref[pl.ds(combo, tm, stride=TOKEN_ROWS), :] = so_ref[:, sl]
        pos = (pl.program_id(0) * tm + _row_iota((tm, LANES))) % t_len
        onehot = jnp.where(_lane_iota((tm, LANES)) == pos // SEL_BLOCK, 1.0, 0.0).astype(BF16)
        for h in range(N_KV):
            att_ref[:, ATT_KSA + 2 * h * HEAD_DIM:ATT_KSA + (2 * h + 1) * HEAD_DIM] = (
                so_ref[:, h * HEAD_DIM:(h + 1) * HEAD_DIM].astype(BF16))
            att_ref[:, ATT_KSA + (2 * h + 1) * HEAD_DIM:ATT_KSA + (2 * h + 2) * HEAD_DIM] = onehot
        att_ref[:, ATT_VS:ATT_KW] = so_ref[:, KV_W:].astype(BF16)
        att_ref[:, ATT_KW:ATT_VW] = wo_ref[:, :KV_W].astype(BF16)
        att_ref[:, ATT_VW:ATT_COLS] = wo_ref[:, KV_W:].astype(BF16)


def nsa_prep(proj, q_norm, ks_norm, kw_norm, tm, t_len, attention_operands):
    m = proj.shape[0]
    kvw = 2 * KV_W
    vec = pl.BlockSpec((1, HEAD_DIM), lambda i: (0, 0))
    widths = [NSA_W, kvw, kvw, kvw] + ([ATT_COLS] if attention_operands else [])
    dtypes = [F32] * 4 + ([BF16] if attention_operands else [])
    out_specs = [pl.BlockSpec((tm, wd), lambda i: (i, 0)) for wd in widths]
    out_shape = [jax.ShapeDtypeStruct((m, wd), dt) for wd, dt in zip(widths, dtypes)]
    if attention_operands:
        assert t_len // SEL_BLOCK <= LANES
        out_specs += [pl.BlockSpec((tm * TOKEN_ROWS, HEAD_DIM), lambda i: (i, 0))] * 2
        out_shape += [jax.ShapeDtypeStruct((m * TOKEN_ROWS, HEAD_DIM), F32)] * 2
    return pl.pallas_call(
        functools.partial(_nsa_prep_kernel, t_len=t_len),
        grid=(m // tm,),
        in_specs=[pl.BlockSpec((tm, NSA_W), lambda i: (i, COL_QB // NSA_W)),
                  pl.BlockSpec((tm, kvw), lambda i: (i, COL_CMP // kvw)),
                  pl.BlockSpec((tm, kvw), lambda i: (i, COL_SEL // kvw)),
                  pl.BlockSpec((tm, kvw), lambda i: (i, COL_WIN // kvw)),
                  vec, vec, vec],
        out_specs=out_specs,
        out_shape=out_shape,
        compiler_params=_cparams(("parallel",)),
        name="nsa_prep",
    )(proj, proj, proj, proj, q_norm.reshape(1, HEAD_DIM), ks_norm.reshape(1, HEAD_DIM), kw_norm.reshape(1, HEAD_DIM))


def _compress_kernel(x_ref, tail_ref, w1_ref, b1_ref, w2_ref, gain_ref, o_ref, x2_ref, a1_ref, *, n_half):
    for j in range(CMP_STRIDE):
        x2_ref[0:n_half, j * HEAD_DIM:(j + 1) * HEAD_DIM] = x_ref[pl.ds(j, n_half, stride=CMP_STRIDE), :]
        x2_ref[n_half:n_half + SUBLANES, j * HEAD_DIM:(j + 1) * HEAD_DIM] = jnp.broadcast_to(
            tail_ref[pl.ds(j, 1), :], (SUBLANES, HEAD_DIM))
    a = jnp.dot(x2_ref[...].astype(BF16), w1_ref[0], preferred_element_type=F32)
    a1_ref[...] = a[:, CMP_HID:]
    pre = a[0:n_half, :CMP_HID] + a1_ref[pl.ds(1, n_half), :] + b1_ref[0]
    out = jnp.dot(_silu(pre).astype(BF16), w2_ref[0], preferred_element_type=F32)
    is_k = pl.program_id(1) < N_KV
    o_ref[0, 0] = jnp.where(is_k, _rms(out, gain_ref[...]), out).astype(BF16)


def compress(raw, tail, w1, b1, w2, kc_norm, n_batch, n_half):
    t_len = n_half * CMP_STRIDE
    wsel = lambda b, c: (c // N_KV, 0, 0)
    return pl.pallas_call(
        functools.partial(_compress_kernel, n_half=n_half),
        grid=(n_batch, 2 * N_KV),
        in_specs=[pl.BlockSpec((t_len, HEAD_DIM), lambda b, c: (b, c)),
                  pl.BlockSpec((CMP_STRIDE, HEAD_DIM), lambda b, c: (b, c)),
                  pl.BlockSpec((1, CMP_STRIDE * HEAD_DIM, 2 * CMP_HID), wsel),
                  pl.BlockSpec((1, 1, CMP_HID), wsel),
                  pl.BlockSpec((1, CMP_HID, HEAD_DIM), wsel),
                  pl.BlockSpec((1, HEAD_DIM), lambda b, c: (0, 0))],
        out_specs=pl.BlockSpec((1, 1, n_half, HEAD_DIM), lambda b, c: (b, c, 0, 0)),
        out_shape=jax.ShapeDtypeStruct((n_batch, 2 * N_KV, n_half, HEAD_DIM), BF16),
        scratch_shapes=[pltpu.VMEM((n_half + SUBLANES, CMP_STRIDE * HEAD_DIM), F32),
                        pltpu.VMEM((n_half + SUBLANES, CMP_HID), F32)],
        compiler_params=_cparams(("parallel", "parallel")),
        name="compress",
    )(raw, tail, w1, b1, w2, kc_norm.reshape(1, HEAD_DIM))


TQ = 128
TK = 256
STRIP_W = TQ + 2 * TK
STRIP_ORIGIN = STRIP_W - TK
MASK_OFF = 1 << 20
BAND_ORIGIN = 64


UNSELECTED = -2.0 ** 30


def _nsa_prompt_kernel(tab_ref, q_ref, kc_ref, vc_ref, ksa_ref, vs_ref, kw_ref, vw_ref, gate_ref, o_ref,
                       strip_ref, band_ref, s_ref, wide_ref, acc_ref, *, n_half, n_blocks):
    kv = pl.program_id(1)
    n = pl.program_id(2)
    head0 = kv * GROUP
    q0 = n * TQ
    jd = n // 2
    odd = n - 2 * jd
    rows = GROUP * TQ
    far = [tab_ref[(NUM_BUCKETS - 1) * N_HEADS_B + head0 + g] for g in range(GROUP)]

    @pl.when(n == 0)
    def _():
        a = _row_iota((TQ, STRIP_W))
        u = _lane_iota((TQ, STRIP_W))
        for g in range(GROUP):
            strip_ref[g] = _bias(a + STRIP_ORIGIN - u, tab_ref, head0 + g) - far[g]
        rel = _row_iota((TQ, n_half)) - ((_lane_iota((TQ, n_half)) - BAND_ORIGIN) * CMP_STRIDE + CMP_BLOCK - 1)
        for g in range(GROUP):
            band_ref[g] = jnp.where(rel >= 0, _bias(rel, tab_ref, head0 + g) - far[g], 0.0)

    q = _stack_heads(q_ref[...])
    qs = q.astype(BF16)
    a_k = _row_iota((TQ, TK))
    c_k = _lane_iota((TQ, TK))

    def tile_start(jj):
        return pl.multiple_of(jnp.maximum(jj, 0) * TK, TK)

    def near_bias(d):
        u0 = pl.multiple_of(STRIP_ORIGIN - d * TK - odd * TQ, TQ)
        return jnp.concatenate([strip_ref[g, :, pl.ds(u0, TK)] for g in range(GROUP)], axis=0)

    def finish(n_tiles, v_ref):
        m = jnp.max(wide_ref[...], axis=-1, keepdims=True)
        m = jnp.where(m == -jnp.inf, 0.0, m)
        wide_ref[...] = jnp.zeros(wide_ref.shape, F32)
        acc_ref[...] = jnp.zeros(acc_ref.shape, F32)

        def one(jj):
            p = jnp.exp(s_ref[jj] - m)
            wide_ref[...] += p
            acc_ref[...] += jnp.dot(p.astype(BF16), v_ref[pl.ds(tile_start(jj), TK), :], preferred_element_type=F32)

        def pair(t, carry):
            one(2 * t)
            one(2 * t + 1)
            return carry

        lax.fori_loop(0, n_tiles // 2, pair, 0)

        @pl.when(n_tiles % 2 == 1)
        def _():
            one(n_tiles - 1)

        return acc_ref[...] / jnp.maximum(jnp.sum(wide_ref[...], axis=-1, keepdims=True), 1e-30)

    a_c = _row_iota((TQ, n_half))
    rel_c = q0 + a_c - (_lane_iota((TQ, n_half)) * CMP_STRIDE + CMP_BLOCK - 1)
    shift = (n * (TQ // CMP_STRIDE) - BAND_ORIGIN) % n_half
    bias_c = jnp.concatenate([pltpu.roll(band_ref[g], shift, 1) for g in range(GROUP)], axis=0)
    s = lax.dot_general(qs, kc_ref[0, 0], _NT, preferred_element_type=F32)
    p = _masked_softmax(s + bias_c, _tile_heads(rel_c) >= 0)
    o_cmp = jnp.dot(p.astype(BF16), vc_ref[0, 0], preferred_element_type=F32)
    imp = _block_importance(p, n_half, n_blocks, LANES)

    lead = _tile_heads(a_k - c_k) + odd * TQ
    causal = lead >= 0
    for d in range(3):
        k = kw_ref[pl.ds(tile_start(jd - d), TK), :]
        s = lax.dot_general(qs, k, _NT, preferred_element_type=F32)
        if d == 0:
            s = jnp.where(causal, s + near_bias(0), -jnp.inf)
        elif d == 1:
            s = s + near_bias(1) + jnp.where(jd >= 1, 0.0, -jnp.inf)
        else:
            s = jnp.where(lead < jnp.where(jd >= 2, 0, -MASK_OFF), s, -jnp.inf)
        s_ref[d] = s
        wide_ref[...] = s if d == 0 else jnp.maximum(wide_ref[...], s)
    m = jnp.max(wide_ref[...], axis=-1, keepdims=True)
    m = jnp.where(m == -jnp.inf, 0.0, m)
    num = jnp.zeros((rows, HEAD_DIM), F32)
    den = jnp.zeros((rows, TK), F32)
    for d in range(3):
        e = jnp.exp(s_ref[d] - m)
        den = den + e
        num = num + jnp.dot(e.astype(BF16), vw_ref[pl.ds(tile_start(jd - d), TK), :], preferred_element_type=F32)
    o_win = num / jnp.maximum(jnp.sum(den, axis=-1, keepdims=True), 1e-30)

    cur_t = (q0 + _lane_iota((LANES, TQ))) // SEL_BLOCK
    unsel = ((_select_blocks_t(imp.T, cur_t) - 1.0) * -UNSELECTED).T
    qa = jnp.concatenate([q, _tile_heads(unsel)], axis=1).astype(BF16)

    def scores(jj):
        return lax.dot_general(qa, ksa_ref[pl.ds(tile_start(jj), TK), :], _NT, preferred_element_type=F32)

    s = jnp.where(causal, scores(jd) + near_bias(0), -jnp.inf)
    s_ref[jd] = s
    wide_ref[...] = s

    @pl.when(jd >= 1)
    def _():
        s = scores(jd - 1) + near_bias(1)
        s_ref[jd - 1] = s
        wide_ref[...] = jnp.maximum(wide_ref[...], s)

    n_far = jnp.maximum(jd - 1, 0)

    def far_pair(t, carry):
        for jj in (2 * t, jnp.minimum(2 * t + 1, n_far - 1)):
            s = scores(jj)
            s_ref[jj] = s
            wide_ref[...] = jnp.maximum(wide_ref[...], s)
        return carry

    lax.fori_loop(0, (n_far + 1) // 2, far_pair, 0)
    o_sel = finish(jd + 1, vs_ref)

    gate = pltpu.roll(gate_ref[...], (LANES - kv * GROUP * 3) % LANES, 1)
    o_ref[...] = _gate_mix(gate, (o_cmp, o_sel, o_win)).astype(o_ref.dtype)


def nsa_prompt_attention(tab, qn, cmp_tok, att, proj, n_batch, t_len):
    m = qn.shape[0]
    n_q = t_len // TQ
    n_half = cmp_tok.shape[2]
    gw = GROUP * HEAD_DIM
    rows = GROUP * TQ
    seq = lambda col, width: pl.BlockSpec((t_len, width), lambda b, kv, n: (b, col // width + kv))
    tok = lambda off: pl.BlockSpec((1, 1, n_half, HEAD_DIM), lambda b, kv, n: (b, off + kv, 0, 0))
    return pl.pallas_call(
        functools.partial(_nsa_prompt_kernel, n_half=n_half, n_blocks=t_len // SEL_BLOCK),
        grid=(n_batch, N_KV, n_q),
        in_specs=[pl.BlockSpec(memory_space=pltpu.SMEM),
                  pl.BlockSpec((TQ, gw), lambda b, kv, n: (b * n_q + n, kv)),
                  tok(0), tok(N_KV), seq(ATT_KSA, 2 * HEAD_DIM), seq(ATT_VS, HEAD_DIM), seq(ATT_KW, HEAD_DIM),
                  seq(ATT_VW, HEAD_DIM),
                  pl.BlockSpec((TQ, LANES), lambda b, kv, n: (b * n_q + n, COL_GATE // LANES))],
        out_specs=pl.BlockSpec((TQ, gw), lambda b, kv, n: (b * n_q + n, kv)),
        out_shape=jax.ShapeDtypeStruct((m, NSA_W), BF16),
        scratch_shapes=[pltpu.VMEM((GROUP, TQ, STRIP_W), F32), pltpu.VMEM((GROUP, TQ, n_half), F32),
                        pltpu.VMEM((t_len // TK, rows, TK), F32),
                        pltpu.VMEM((rows, TK), F32), pltpu.VMEM((rows, HEAD_DIM), F32)],
        compiler_params=_cparams(("parallel", "parallel", "arbitrary")),
        name="nsa_prompt",
    )(tab, qn, cmp_tok, cmp_tok, att, att, att, att, proj)


PAGES_PER_STEP = 32
TOKEN_ROWS = 2 * N_KV
PAGE_ROWS = PAGE * TOKEN_ROWS


def _lane_queries(q, t_new):
    per_kv = GROUP * t_new
    stacked = jnp.concatenate([q[:, h * HEAD_DIM:(h + 1) * HEAD_DIM] for h in range(N_HEADS_B)], axis=0)
    out = []
    for kv in range(N_KV):
        parts = []
        if kv:
            parts.append(jnp.zeros((kv * per_kv, HEAD_DIM), F32))
        parts.append(stacked[kv * per_kv:(kv + 1) * per_kv])
        parts.append(jnp.zeros((LANES - (kv + 1) * per_kv, HEAD_DIM), F32))
        out.append(jnp.concatenate(parts, axis=0).astype(BF16))
    return out


def _lane_scores(keys, qz):
    s = lax.dot_general(keys[0], qz[0], _NT, preferred_element_type=F32)
    for kv in range(1, N_KV):
        s = s + lax.dot_general(keys[kv], qz[kv], _NT, preferred_element_type=F32)
    return s


def _lane_values(vals, p, per_kv):
    pb = p.astype(BF16)
    lane = _lane_iota((HEAD_DIM, LANES))
    out = lax.dot_general(vals[N_KV - 1], pb, _TN, preferred_element_type=F32)
    for kv in range(N_KV - 2, -1, -1):
        out = jnp.where(lane < (kv + 1) * per_kv, lax.dot_general(vals[kv], pb, _TN, preferred_element_type=F32), out)
    return out


def _lane_bias(rel, tabl_ref):
    val = jnp.broadcast_to(tabl_ref[_BUCKET0:_BUCKET0 + 1, :], rel.shape)
    for edge, bucket in _BUCKET_EDGES:
        val = jnp.where(rel >= edge, tabl_ref[bucket:bucket + 1, :], val)
    return val


def _pad_rows(x, rows):
    return jnp.concatenate([x, jnp.zeros((rows - x.shape[0], x.shape[1]), x.dtype)], axis=0)


PAGE_HALVES = PAGE // CMP_STRIDE
SLAB_ROWS = CMP_STRIDE * TOKEN_ROWS
SLAB_PITCH = SLAB_ROWS + SUBLANES
STAGE_ROWS = PAGES_PER_STEP * PAGE_HALVES * SLAB_PITCH


def _stage_copies(pt_ref, cache_ref, stage_ref, sem_ref, step, slot, n_chunks):
    b = step // n_chunks
    c = step - b * n_chunks
    copies = []
    for p in range(PAGES_PER_STEP):
        page = pt_ref[b, c * PAGES_PER_STEP + p]
        for n in range(PAGE_HALVES):
            dst = pl.multiple_of(slot * STAGE_ROWS + (p * PAGE_HALVES + n) * SLAB_PITCH, SUBLANES)
            copies.append(pltpu.make_async_copy(cache_ref.at[page, pl.ds(n * SLAB_ROWS, SLAB_ROWS), :],
                                                stage_ref.at[pl.ds(dst, SLAB_ROWS), :], sem_ref.at[slot]))
    return copies


def _sample_cmp_kernel(pt_ref, cache_ref, new_ref, q_ref, w1_ref, b1_ref, w2_ref, gain_ref, tabl_ref,
                       sel_ref, ocmp_ref, x2_ref, a_ref, stage_ref, sem_ref, *, past, t_new, n_blocks, sel_rows):
    c = pl.program_id(1)
    n_chunks = pl.num_programs(1)
    step = pl.program_id(0) * n_chunks + c
    slot = step % 2
    n_half = past // CMP_STRIDE
    step_halves = PAGES_PER_STEP * PAGE_HALVES
    per_kv = GROUP * t_new

    @pl.when(step == 0)
    def _():
        for cp in _stage_copies(pt_ref, cache_ref, stage_ref, sem_ref, step, slot, n_chunks):
            cp.start()

    @pl.when(step + 1 < pl.num_programs(0) * n_chunks)
    def _():
        for cp in _stage_copies(pt_ref, cache_ref, stage_ref, sem_ref, step + 1, 1 - slot, n_chunks):
            cp.start()

    for cp in _stage_copies(pt_ref, cache_ref, stage_ref, sem_ref, step, slot, n_chunks):
        cp.wait()

    base = slot * STAGE_ROWS
    for combo in range(TOKEN_ROWS):
        for p in range(PAGES_PER_STEP):
            for j in range(CMP_STRIDE):
                start = base + p * PAGE_HALVES * SLAB_PITCH + j * TOKEN_ROWS + combo
                x2_ref[combo, p * PAGE_HALVES:(p + 1) * PAGE_HALVES, j * HEAD_DIM:(j + 1) * HEAD_DIM] = (
                    stage_ref[pl.ds(start, PAGE_HALVES, stride=SLAB_PITCH), :])
        a = jnp.dot(x2_ref[combo].astype(BF16), w1_ref[combo // N_KV], preferred_element_type=F32)
        a_ref[combo, pl.ds(pl.multiple_of(c * step_halves, step_halves), step_halves), :] = a

    @pl.when(c == pl.num_programs(1) - 1)
    def _():
        tok = []
        for combo in range(TOKEN_ROWS):
            kind = combo // N_KV
            sl = slice(combo * HEAD_DIM, (combo + 1) * HEAD_DIM)
            row = jnp.concatenate([new_ref[j:j + 1, sl] for j in range(t_new)]
                                  + [jnp.zeros((1, (CMP_STRIDE - t_new) * HEAD_DIM), F32)], axis=1)
            tail = jnp.broadcast_to(row, (SUBLANES, CMP_STRIDE * HEAD_DIM)).astype(BF16)
            a_ref[combo, n_half:n_half + SUBLANES, :] = jnp.dot(tail, w1_ref[kind], preferred_element_type=F32)
            pre = a_ref[combo, 0:n_half, 0:CMP_HID] + a_ref[combo, pl.ds(1, n_half), CMP_HID:] + b1_ref[kind]
            out = jnp.dot(_silu(pre).astype(BF16), w2_ref[kind], preferred_element_type=F32)
            if kind == 0:
                out = _rms(out, gain_ref[...])
            tok.append(out.astype(BF16))

        qz = _lane_queries(q_ref[...], t_new)
        shape = (n_half, LANES)
        u = _lane_iota(shape) & (t_new - 1)
        rel = past + u - (_row_iota(shape) * CMP_STRIDE + CMP_BLOCK - 1)
        near0 = ((past - (CMP_BLOCK - 1) - FAR_DIST) // CMP_STRIDE + 1) // SUBLANES * SUBLANES
        bias = jnp.concatenate([jnp.broadcast_to(tabl_ref[NUM_BUCKETS - 1:NUM_BUCKETS, :], (near0, LANES)),
                                _lane_bias(rel[near0:], tabl_ref)], axis=0)
        s = jnp.where(rel >= 0, _lane_scores(tok[:N_KV], qz) + bias, -jnp.inf)
        m = jnp.max(s, axis=0, keepdims=True)
        e = jnp.exp(s - jnp.where(m == -jnp.inf, 0.0, m))
        p = e / jnp.maximum(jnp.sum(e, axis=0, keepdims=True), 1e-30)
        ocmp_ref[...] = _lane_values(tok[N_KV:], p, per_kv)

        src = _row_iota((LANES, LANES))
        dst = _lane_iota((LANES, LANES))
        pooled = (src // per_kv) * t_new + (src & (t_new - 1))
        pool = jnp.where((src < N_KV * per_kv) & (dst == pooled), 1.0, 0.0).astype(BF16)
        hi = p.astype(BF16)
        lo = (p - hi.astype(F32)).astype(BF16)
        imp = jnp.dot(hi, pool, preferred_element_type=F32) + jnp.dot(lo, pool, preferred_element_type=F32)
        si = _row_iota((sel_rows, n_half)) * SEL_BLOCK
        ci = _lane_iota((sel_rows, n_half)) * CMP_STRIDE
        overlap = (ci < si + SEL_BLOCK) & (ci + CMP_BLOCK > si) & (si < n_blocks * SEL_BLOCK)
        ov = jnp.where(overlap, 1.0, 0.0).astype(BF16)
        hi = imp.astype(BF16)
        lo = (imp - hi.astype(F32)).astype(BF16)
        imp = jnp.dot(ov, hi, preferred_element_type=F32) + jnp.dot(ov, lo, preferred_element_type=F32)

        cur = (past + (_lane_iota((sel_rows, LANES)) & (t_new - 1))) // SEL_BLOCK
        sel = _select_blocks_t(imp, cur)
        unpool = jnp.where((dst < N_KV * per_kv) & (src == (dst // per_kv) * t_new + (dst & (t_new - 1))), 1.0, 0.0)
        sel_ref[...] = jnp.dot(sel.astype(BF16), unpool.astype(BF16), preferred_element_type=F32)


def _page_specs(n_chunk_pages):
    def spec(j):
        return pl.BlockSpec((None, PAGE_ROWS, HEAD_DIM), lambda b, c, pt: (pt[b, c * n_chunk_pages + j], 0, 0))
    return [spec(j) for j in range(n_chunk_pages)]


def sample_compressed(page_table, cache3, cmp_new, qn, w1, b1, w2, kc_norm, tabl, t_new):
    n_seq, n_pages = page_table.shape
    assert n_pages % PAGES_PER_STEP == 0
    past = n_pages * PAGE
    n_half = past // CMP_STRIDE
    n_blocks = -(-(past + t_new) // SEL_BLOCK)
    sel_rows = -(-n_blocks // SUBLANES) * SUBLANES
    const = lambda *shape: pl.BlockSpec(shape, lambda b, c, pt: (0,) * len(shape))
    row = lambda width: pl.BlockSpec((t_new, width), lambda b, c, pt: (b, 0))
    out = lambda rows: pl.BlockSpec((None, rows, LANES), lambda b, c, pt: (b, 0, 0))
    grid_spec = pltpu.PrefetchScalarGridSpec(
        num_scalar_prefetch=1,
        grid=(n_seq, n_pages // PAGES_PER_STEP),
        in_specs=[pl.BlockSpec(memory_space=pl.ANY),
                  row(2 * KV_W), row(NSA_W), const(2, CMP_STRIDE * HEAD_DIM, 2 * CMP_HID), const(2, 1, CMP_HID),
                  const(2, CMP_HID, HEAD_DIM), const(1, HEAD_DIM), const(NUM_BUCKETS, LANES)],
        out_specs=[out(sel_rows), out(HEAD_DIM)],
        scratch_shapes=[pltpu.VMEM((TOKEN_ROWS, PAGES_PER_STEP * PAGE_HALVES, CMP_STRIDE * HEAD_DIM), F32),
                        pltpu.VMEM((TOKEN_ROWS, n_half + SUBLANES, 2 * CMP_HID), F32),
                        pltpu.VMEM((2 * STAGE_ROWS, HEAD_DIM), F32),
                        pltpu.SemaphoreType.DMA((2,))])
    return pl.pallas_call(
        functools.partial(_sample_cmp_kernel, past=past, t_new=t_new, n_blocks=n_blocks, sel_rows=sel_rows),
        grid_spec=grid_spec,
        out_shape=[jax.ShapeDtypeStruct((n_seq, sel_rows, LANES), F32),
                   jax.ShapeDtypeStruct((n_seq, HEAD_DIM, LANES), F32)],
        compiler_params=_cparams(("arbitrary", "arbitrary")),
        name="sample_cmp",
    )(page_table, cache3, cmp_new, qn, w1, b1, w2, kc_norm.reshape(1, HEAD_DIM), tabl)


def _sample_sel_kernel(pt_ref, *refs, past, t_new, win_buf):
    pages = refs[:PAGES_PER_STEP]
    (new_ref, q_ref, sel_ref, ocmp_ref, win_ref, wnew_ref, gate_ref, tabl_ref,
     o_ref, m_ref, l_ref, acc_ref, owin_ref, s_ref) = refs[PAGES_PER_STEP:]
    c = pl.program_id(1)
    per_kv = GROUP * t_new
    qz = _lane_queries(q_ref[...], t_new)
    far = tabl_ref[NUM_BUCKETS - 1:NUM_BUCKETS, :]
    new_rows = 2 * SUBLANES

    def new_kv(ref, kind):
        return [_pad_rows(ref[:, (kind * N_KV + kv) * HEAD_DIM:(kind * N_KV + kv + 1) * HEAD_DIM], new_rows).astype(BF16)
                for kv in range(N_KV)]

    def dist(pos0, rows):
        shape = (rows, LANES)
        return past + (_lane_iota(shape) & (t_new - 1)) - pos0 - _row_iota(shape)

    def update(s, values):
        m_prev = m_ref[...]
        m_new = jnp.maximum(m_prev, jnp.max(s, axis=0, keepdims=True))
        m_safe = jnp.where(m_new == -jnp.inf, 0.0, m_new)
        alpha = jnp.exp(m_prev - m_safe)
        p = jnp.exp(s - m_safe)
        l_ref[...] = alpha * l_ref[...] + jnp.sum(p, axis=0, keepdims=True)
        acc = alpha * acc_ref[...]
        for vals, lo, hi_ in values:
            acc = acc + _lane_values(vals, p[lo:hi_], per_kv)
        acc_ref[...] = acc
        m_ref[...] = m_new

    @pl.when(c == 0)
    def _():
        m_ref[...] = jnp.full(m_ref.shape, -jnp.inf, F32)
        l_ref[...] = jnp.zeros(l_ref.shape, F32)
        acc_ref[...] = jnp.zeros(acc_ref.shape, F32)

        kw = [win_ref[pl.ds(kv, win_buf, stride=TOKEN_ROWS), :].astype(BF16) for kv in range(N_KV)]
        vw = [win_ref[pl.ds(N_KV + kv, win_buf, stride=TOKEN_ROWS), :].astype(BF16) for kv in range(N_KV)]
        pieces = []
        near0 = (win_buf - FAR_DIST + 1) // SUBLANES * SUBLANES
        for keys, rel, near in ((kw, dist(past - win_buf, win_buf), near0), (new_kv(wnew_ref, 0), dist(past, new_rows), 0)):
            bias = _lane_bias(rel[near:], tabl_ref)
            if near:
                bias = jnp.concatenate([jnp.broadcast_to(far, (near, LANES)), bias], axis=0)
            s = _lane_scores(keys, qz) + bias
            pieces.append(jnp.where((rel >= 0) & (rel < WINDOW), s, -jnp.inf))
        m = jnp.maximum(jnp.max(pieces[0], axis=0, keepdims=True), jnp.max(pieces[1], axis=0, keepdims=True))
        m = jnp.where(m == -jnp.inf, 0.0, m)
        e0 = jnp.exp(pieces[0] - m)
        e1 = jnp.exp(pieces[1] - m)
        tot = jnp.sum(e0, axis=0, keepdims=True) + jnp.sum(e1, axis=0, keepdims=True)
        num = _lane_values(vw, e0, per_kv) + _lane_values(new_kv(wnew_ref, 1), e1, per_kv)
        owin_ref[...] = num / jnp.maximum(tot, 1e-30)

        rel = dist(past, new_rows)
        s = _lane_scores(new_kv(new_ref, 0), qz) + _lane_bias(rel, tabl_ref)
        chosen = sel_ref[past // SEL_BLOCK:past // SEL_BLOCK + 1, :] > 0.5
        update(jnp.where(chosen & (rel >= 0), s, -jnp.inf), [(new_kv(new_ref, 1), 0, new_rows)])

    values = []
    for p in range(PAGES_PER_STEP):
        page = c * PAGES_PER_STEP + p
        keys = [pages[p][pl.ds(kv, PAGE, stride=TOKEN_ROWS), :].astype(BF16) for kv in range(N_KV)]
        s = _lane_scores(keys, qz) + far
        per_block = PAGE // SEL_BLOCK
        chosen = jnp.concatenate(
            [jnp.broadcast_to(sel_ref[pl.ds(page * per_block + i, 1), :], (SEL_BLOCK, LANES)) for i in range(per_block)],
            axis=0)
        s_ref[p * PAGE:(p + 1) * PAGE, :] = jnp.where(chosen > 0.5, s, -jnp.inf)
        vals = [pages[p][pl.ds(N_KV + kv, PAGE, stride=TOKEN_ROWS), :].astype(BF16) for kv in range(N_KV)]
        values.append((vals, p * PAGE, (p + 1) * PAGE))

    @pl.when(c == pl.num_programs(1) - 1)
    def _():
        rows = slice((PAGES_PER_STEP - 1) * PAGE, PAGES_PER_STEP * PAGE)
        s_ref[rows, :] = s_ref[rows, :] + (_lane_bias(dist(past - PAGE, PAGE), tabl_ref) - far)

    update(s_ref[...], values)

    @pl.when(c == pl.num_programs(1) - 1)
    def _():
        gt = _sigmoid(gate_ref[...])
        osel = acc_ref[...] / jnp.maximum(l_ref[...], 1e-30)
        o_ref[...] = gt[0:1] * ocmp_ref[...] + gt[1:2] * osel + gt[2:3] * owin_ref[...]


def sample_selected(page_table, cache3, sel_new, qn, sel_t, ocmp_t, win3, win_new, gate_l, tabl, t_new):
    n_seq, n_pages = page_table.shape
    past = n_pages * PAGE
    win_buf = win3.shape[1] // TOKEN_ROWS
    assert (past - FAR_DIST) // PAGE >= n_pages - 1, "only the last page may hold near keys"
    sel_rows = sel_t.shape[1]
    const = lambda *shape: pl.BlockSpec(shape, lambda b, c, pt: (0,) * len(shape))
    row = lambda width: pl.BlockSpec((t_new, width), lambda b, c, pt: (b, 0))
    per_seq = lambda rows, width: pl.BlockSpec((None, rows, width), lambda b, c, pt: (b, 0, 0))
    grid_spec = pltpu.PrefetchScalarGridSpec(
        num_scalar_prefetch=1,
        grid=(n_seq, n_pages // PAGES_PER_STEP),
        in_specs=_page_specs(PAGES_PER_STEP) + [
            row(2 * KV_W), row(NSA_W), per_seq(sel_rows, LANES), per_seq(HEAD_DIM, LANES),
            per_seq(win_buf * TOKEN_ROWS, HEAD_DIM), row(2 * KV_W), per_seq(SUBLANES, LANES),
            const(NUM_BUCKETS, LANES)],
        out_specs=per_seq(HEAD_DIM, LANES),
        scratch_shapes=[pltpu.VMEM((1, LANES), F32), pltpu.VMEM((1, LANES), F32), pltpu.VMEM((HEAD_DIM, LANES), F32),
                        pltpu.VMEM((HEAD_DIM, LANES), F32), pltpu.VMEM((PAGES_PER_STEP * PAGE, LANES), F32)])
    return pl.pallas_call(
        functools.partial(_sample_sel_kernel, past=past, t_new=t_new, win_buf=win_buf),
        grid_spec=grid_spec,
        out_shape=jax.ShapeDtypeStruct((n_seq, HEAD_DIM, LANES), F32),
        compiler_params=_cparams(("parallel", "arbitrary")),
        name="sample_sel",
    )(page_table, *([cache3] * PAGES_PER_STEP), sel_new, qn, sel_t, ocmp_t, win3, win_new, gate_l, tabl)


def _split_in_proj(w_in):
    d, cols = w_in.shape
    assert cols == COL_GATE + N_GATE + MERGE_COLS
    wt = jnp.swapaxes(w_in, 0, 1).astype(BF16)
    return wt, PROJ_COLS, wt[COL_GATE + N_GATE:]


def _compress_weights(k_w1, k_b1, k_w2, v_w1, v_b1, v_w2):
    half = CMP_STRIDE * HEAD_DIM
    cat = lambda w: jnp.concatenate([w[:half], w[half:]], axis=1)
    w1 = jnp.stack([cat(k_w1), cat(v_w1)]).astype(BF16)
    b1 = jnp.stack([k_b1, v_b1]).reshape(2, 1, CMP_HID)
    w2 = jnp.stack([k_w2, v_w2]).astype(BF16)
    return w1, b1, w2


def _lane_table(rel_bias, t_new):
    tab = jnp.repeat(rel_bias, t_new, axis=1)
    return jnp.pad(tab, ((0, 0), (0, LANES - tab.shape[1])))


def _gate_lanes(proj, n_seq, t_new):
    per_kv = GROUP * 3
    g = jnp.stack([proj[:, COL_GATE + kv * per_kv:COL_GATE + (kv + 1) * per_kv] for kv in range(N_KV)], axis=1)
    g = g.reshape(n_seq, t_new, N_KV, GROUP, 3).transpose(0, 4, 2, 3, 1).reshape(n_seq, 3, N_HEADS_B * t_new)
    return jnp.pad(g, ((0, 0), (0, SUBLANES - 3), (0, LANES - N_HEADS_B * t_new)))


def _tiles(m, t_len):
    big = m >= 1024
    return dict(proj_tm=1024 if big else m, proj_tn=1024, prep_tm=512 if big else m, post_tm=256,
                mlp_tm=512 if big else m, mlp_tf=1024,
                hgrn_chunk=min(t_len, 256), hgrn_heads=4 if big else N_HEADS_A)


def _trunk(x, mods, s0, w, nsa_fn, attention_operands):
    nb, t_len, d = x.shape
    m = nb * t_len
    tl = _tiles(m, t_len)
    sh1, sc1, g1, sh2, sc2, g2 = mods
    x2 = x.reshape(m, d)
    proj, merge = in_proj(x2, w["norm1"], sc1, sh1, *w["w_in"], t_len, tl["proj_tm"], tl["proj_tn"])
    o_a, s_fin = hgrn(proj, w["lb"], w["hgrn_onorm"], s0, nb, t_len, tl["hgrn_chunk"], tl["hgrn_heads"])
    qn, cmp_new, sel_new, win_new, *att = nsa_prep(proj, w["q_norm"], w["ks_norm"], w["kw_norm"], tl["prep_tm"], t_len,
                                                   attention_operands)
    o_b = nsa_fn(proj, qn, cmp_new, sel_new, win_new, *att[:1])
    x1, h2 = post_attn(x2, o_a, o_b, merge, w["w_ba"], w["w_bb"], w["w_out"], g1, w["norm2"], sc2, sh2, t_len,
                       tl["post_tm"])
    y = mlp(h2, x1, w["mlp_w1"], w["mlp_w2"], g2, t_len, tl["mlp_tm"], tl["mlp_tf"])
    if attention_operands:
        cmp_new, sel_new = att[1:]
    return y.reshape(nb, t_len, d), s_fin, (cmp_new, sel_new, win_new)


def kernel(x_prompt, x_sample, c_prompt, c_sample, cache_cmp_kv, cache_sel_kv, cache_win_kv, state_hgrn, page_table, hgrn_lb_logits, rel_bias, ada_w, ada_b, norm1, norm2, w_in, hgrn_onorm, nsa_q_norm, nsa_kc_norm, nsa_ks_norm, nsa_kw_norm, cmp_k_w1, cmp_k_b1, cmp_k_w2, cmp_v_w1, cmp_v_b1, cmp_v_w2, w_branch_a, w_branch_b, w_out, mlp_w1, mlp_w2):
    n_p, t_p, d = x_prompt.shape
    n_s, t_s, _ = x_sample.shape
    past = page_table.shape[1] * PAGE
    win_buf = cache_win_kv.shape[2]
    kvw = 2 * KV_W
    layer = 0

    lb_all = jnp.cumsum(jax.nn.softmax(hgrn_lb_logits.astype(F32), axis=0), axis=0)
    cw1, cb1, cw2 = _compress_weights(cmp_k_w1[layer], cmp_k_b1[layer], cmp_k_w2[layer],
                                      cmp_v_w1[layer], cmp_v_b1[layer], cmp_v_w2[layer])
    w = dict(norm1=norm1[layer], norm2=norm2[layer], w_in=_split_in_proj(w_in[layer]),
             hgrn_onorm=hgrn_onorm[layer], lb=lb_all[layer], q_norm=nsa_q_norm[layer],
             ks_norm=nsa_ks_norm[layer], kw_norm=nsa_kw_norm[layer],
             w_ba=w_branch_a[layer].astype(BF16), w_bb=w_branch_b[layer].astype(BF16),
             w_out=w_out[layer].astype(BF16), mlp_w1=mlp_w1[layer].astype(BF16), mlp_w2=mlp_w2[layer].astype(BF16))
    tab = rel_bias.astype(F32).reshape(NUM_BUCKETS * N_HEADS_B)

    mods = ada_mods(jnp.concatenate([c_prompt, c_sample], axis=0).astype(F32), ada_w[layer], ada_b[layer])
    mods = jnp.split(mods, 6, axis=-1)
    mods_p = [a[:n_p].reshape(n_p, 1, d) for a in mods]
    mods_s = [jnp.repeat(a[n_p:], t_s, axis=0).reshape(1, n_s * t_s, d) for a in mods]

    def nsa_p(proj, qn, cmp_new, sel_new, win_new, att):
        tail = jnp.zeros((n_p * CMP_STRIDE, kvw), F32)
        cmp_tok = compress(cmp_new, tail, cw1, cb1, cw2, nsa_kc_norm[layer], n_p, t_p // CMP_STRIDE)
        return nsa_prompt_attention(tab, qn, cmp_tok, att, proj, n_p, t_p)

    s0_p = jnp.zeros((n_p, N_HEADS_A, HEAD_DIM, HEAD_DIM), F32)
    y_p, hg_p, (cmp_p, sel_p, win_p) = _trunk(x_prompt, mods_p, s0_p, w, nsa_p, True)

    def nsa_s(proj, qn, cmp_new, sel_new, win_new):
        n_phys = cache_cmp_kv.shape[1]
        cmp3 = cache_cmp_kv[layer].reshape(n_phys, PAGE_ROWS, HEAD_DIM)
        sel3 = cache_sel_kv[layer].reshape(n_phys, PAGE_ROWS, HEAD_DIM)
        win3 = cache_win_kv[layer].reshape(n_s, win_buf * TOKEN_ROWS, HEAD_DIM)
        tabl = _lane_table(rel_bias.astype(F32), t_s)
        sel_t, ocmp_t = sample_compressed(page_table, cmp3, cmp_new, qn, cw1, cb1, cw2, nsa_kc_norm[layer], tabl, t_s)
        o_t = sample_selected(page_table, sel3, sel_new, qn, sel_t, ocmp_t, win3, win_new,
                              _gate_lanes(proj, n_s, t_s), tabl, t_s)
        o = o_t[:, :, :N_HEADS_B * t_s].reshape(n_s, HEAD_DIM, N_HEADS_B, t_s)
        return o.transpose(0, 3, 2, 1).reshape(n_s * t_s, NSA_W).astype(BF16)

    y_s, hg_s, (cmp_s, sel_s, win_s) = _trunk(x_sample, mods_s, state_hgrn[layer], w, nsa_s, False)

    dt = x_prompt.dtype
    pages = lambda a: a.reshape(1, n_p, t_p // PAGE, PAGE, 2, N_KV, HEAD_DIM).astype(dt)
    rows_s = lambda a: a.reshape(1, n_s, t_s, 2, N_KV, HEAD_DIM).astype(dt)
    win_keep = min(WINDOW, t_p)
    win_p_out = win_p.reshape(n_p, t_p, kvw)[:, t_p - win_keep:].reshape(1, n_p, win_keep, 2, N_KV, HEAD_DIM)
    win_rows = cache_win_kv[layer].reshape(n_s, win_buf * TOKEN_ROWS, HEAD_DIM).astype(F32)
    win_s_out = jnp.concatenate([win_rows, win_s.reshape(n_s, t_s * TOKEN_ROWS, HEAD_DIM)], axis=1)
    win_s_out = win_s_out[:, t_s * TOKEN_ROWS:].reshape(1, n_s, win_buf, 2, N_KV, HEAD_DIM)
    return (y_p, y_s, pages(cmp_p), rows_s(cmp_s), pages(sel_p), rows_s(sel_s),
            win_p_out.astype(dt), win_s_out.astype(dt), hg_p[None].astype(dt), hg_s[None].astype(dt))
```

```python
import functools
import math

import numpy as np
import jax
import jax.numpy as jnp
from jax import lax
from jax.experimental import pallas as pl
from jax.experimental.pallas import tpu as pltpu

F32 = jnp.float32
BF16 = jnp.bfloat16

D_MODEL = 2048
N_HEADS_A = 8
HEAD_DIM = 128
HGRN_W = N_HEADS_A * HEAD_DIM
N_HEADS_B = 8
N_KV = 2
GROUP = N_HEADS_B // N_KV
NSA_W = N_HEADS_B * HEAD_DIM
KV_W = N_KV * HEAD_DIM
PAGE = 128
CMP_STRIDE = 16
CMP_BLOCK = 32
CMP_HID = 128
SEL_BLOCK = 64
N_SEL = 16
WINDOW = 512
FORCE_SCORE = 1.0e6
NUM_BUCKETS = 32
REL_MAX_DIST = 128
EPS = 1e-6
N_GATE = 3 * N_HEADS_B

LANES = 128
SUBLANES = 8
VMEM_LIMIT = 56 * 1024 * 1024
VMEM_LIMIT_MAX = 60 * 1024 * 1024

COL_HGRN = 0
COL_QB = 4 * HGRN_W
COL_CMP = COL_QB + NSA_W
COL_SEL = COL_CMP + 2 * KV_W
COL_WIN = COL_SEL + 2 * KV_W
COL_GATE = COL_WIN + 2 * KV_W
GATE_COLS = 512
PROJ_COLS = COL_GATE + GATE_COLS
MERGE_COLS = 2 * D_MODEL


def _cparams(sem):
    return pltpu.CompilerParams(dimension_semantics=sem, vmem_limit_bytes=VMEM_LIMIT)


def _sigmoid(x):
    return 1.0 / (1.0 + jnp.exp(-x))


def _silu(x):
    return x * _sigmoid(x)


def _rms(x, gain):
    return x * lax.rsqrt(jnp.mean(x * x, axis=-1, keepdims=True) + EPS) * gain


def _lane_iota(shape):
    return lax.broadcasted_iota(jnp.int32, shape, 1)


def _row_iota(shape):
    return lax.broadcasted_iota(jnp.int32, shape, 0)


def _ada_kernel(c_ref, w_ref, b_ref, o_ref):
    a = _silu(c_ref[...]).astype(BF16)
    o_ref[...] = jnp.dot(a, w_ref[...].astype(BF16), preferred_element_type=F32) + b_ref[...]


def ada_mods(c, w, b):
    r, d = c.shape
    n = w.shape[1]
    tn = 1024
    return pl.pallas_call(
        _ada_kernel,
        grid=(n // tn,),
        in_specs=[pl.BlockSpec((r, d), lambda j: (0, 0)),
                  pl.BlockSpec((d, tn), lambda j: (0, j)),
                  pl.BlockSpec((1, tn), lambda j: (0, j))],
        out_specs=pl.BlockSpec((r, tn), lambda j: (0, j)),
        out_shape=jax.ShapeDtypeStruct((r, n), F32),
        compiler_params=_cparams(("parallel",)),
        name="ada_mods",
    )(c, w, b.reshape(1, n))


def _inproj_kernel(x_ref, gain_ref, sc_ref, sh_ref, wa_ref, wb_ref, oa_ref, ob_ref, h_ref, *, n_a):
    j = pl.program_id(1)

    @pl.when(j == 0)
    def _():
        h = _rms(x_ref[...], gain_ref[...]) * (1.0 + sc_ref[0]) + sh_ref[0]
        h_ref[...] = h.astype(BF16)

    nt = (((1,), (1,)), ((), ()))

    @pl.when(j < n_a)
    def _():
        oa_ref[...] = lax.dot_general(h_ref[...], wa_ref[...], nt, preferred_element_type=F32)

    @pl.when(j >= n_a)
    def _():
        ob_ref[...] = lax.dot_general(h_ref[...], wb_ref[...], nt, preferred_element_type=F32)


def _mod_spec(mod, tm, rows_per_batch):
    d = mod.shape[-1]
    if mod.shape[1] == 1:
        return pl.BlockSpec((1, 1, d), lambda i, *_: ((i * tm) // rows_per_batch, 0, 0))
    return pl.BlockSpec((1, tm, d), lambda i, *_: (0, i, 0))


def in_proj(x2, gain, sc, sh, w_a, cols_a, w_b, rows_per_batch, tm, tn):
    m, d = x2.shape
    n_a = cols_a // tn
    n_b = w_b.shape[0] // tn
    col_a = lambda i, j: jnp.minimum(j, n_a - 1)
    col_b = lambda i, j: jnp.maximum(j - n_a, 0)
    return pl.pallas_call(
        functools.partial(_inproj_kernel, n_a=n_a),
        grid=(m // tm, n_a + n_b),
        in_specs=[pl.BlockSpec((tm, d), lambda i, j: (i, 0)),
                  pl.BlockSpec((1, d), lambda i, j: (0, 0)),
                  _mod_spec(sc, tm, rows_per_batch),
                  _mod_spec(sh, tm, rows_per_batch),
                  pl.BlockSpec((tn, d), lambda i, j: (col_a(i, j), 0)),
                  pl.BlockSpec((tn, d), lambda i, j: (col_b(i, j), 0))],
        out_specs=[pl.BlockSpec((tm, tn), lambda i, j: (i, col_a(i, j))),
                   pl.BlockSpec((tm, tn), lambda i, j: (i, col_b(i, j)))],
        out_shape=[jax.ShapeDtypeStruct((m, cols_a), F32), jax.ShapeDtypeStruct((m, w_b.shape[0]), F32)],
        scratch_shapes=[pltpu.VMEM((tm, d), BF16)],
        compiler_params=pltpu.CompilerParams(dimension_semantics=("parallel", "arbitrary"),
                                             vmem_limit_bytes=VMEM_LIMIT_MAX),
        name="in_proj",
    )(x2, gain.reshape(1, d), sc, sh, w_a, w_b)


def _post_kernel(x_ref, oa_ref, ob_ref, mga_ref, mgb_ref, wba_ref, wbb_ref, wout_ref,
                 g1_ref, gain2_ref, sc2_ref, sh2_ref, x1_ref, h2_ref):
    ya = jnp.dot(oa_ref[...], wba_ref[...], preferred_element_type=F32)
    yb = jnp.dot(ob_ref[...], wbb_ref[...], preferred_element_type=F32)
    merged = _sigmoid(mga_ref[...]) * ya + _sigmoid(mgb_ref[...]) * yb
    y = jnp.dot(merged.astype(BF16), wout_ref[...], preferred_element_type=F32)
    x1 = x_ref[...] + g1_ref[0] * y
    x1_ref[...] = x1
    h2_ref[...] = (_rms(x1, gain2_ref[...]) * (1.0 + sc2_ref[0]) + sh2_ref[0]).astype(BF16)


def post_attn(x2, oa, ob, proj, wba, wbb, wout, g1, gain2, sc2, sh2, rows_per_batch, tm):
    m, d = x2.shape
    const = lambda i: (0, 0)
    resident = lambda shape: pl.BlockSpec(shape, const, pipeline_mode=pl.Buffered(1))
    return pl.pallas_call(
        _post_kernel,
        grid=(m // tm,),
        in_specs=[pl.BlockSpec((tm, d), lambda i: (i, 0)),
                  pl.BlockSpec((tm, HGRN_W), lambda i: (i, 0)),
                  pl.BlockSpec((tm, NSA_W), lambda i: (i, 0)),
                  pl.BlockSpec((tm, d), lambda i: (i, 0)),
                  pl.BlockSpec((tm, d), lambda i: (i, 1)),
                  resident((HGRN_W, d)),
                  resident((NSA_W, d)),
                  resident((d, d)),
                  _mod_spec(g1, tm, rows_per_batch),
                  pl.BlockSpec((1, d), const),
                  _mod_spec(sc2, tm, rows_per_batch),
                  _mod_spec(sh2, tm, rows_per_batch)],
        out_specs=[pl.BlockSpec((tm, d), lambda i: (i, 0)),
                   pl.BlockSpec((tm, d), lambda i: (i, 0))],
        out_shape=[jax.ShapeDtypeStruct((m, d), F32), jax.ShapeDtypeStruct((m, d), BF16)],
        compiler_params=_cparams(("parallel",)),
        name="post_attn",
    )(x2, oa, ob, proj, proj, wba, wbb, wout, g1, gain2.reshape(1, d), sc2, sh2)


def _mlp_kernel(h_ref, x1_ref, w1_ref, w2_ref, g2_ref, y_ref, acc_ref):
    f = pl.program_id(1)
    u = jnp.maximum(jnp.dot(h_ref[...], w1_ref[...], preferred_element_type=F32), 0.0)
    part = jnp.dot((u * u).astype(BF16), w2_ref[...], preferred_element_type=F32)

    @pl.when(f == 0)
    def _():
        acc_ref[...] = part

    @pl.when(f > 0)
    def _():
        acc_ref[...] += part

    @pl.when(f == pl.num_programs(1) - 1)
    def _():
        y_ref[...] = x1_ref[...] + g2_ref[0] * acc_ref[...]


def mlp(h2, x1, w1, w2, g2, rows_per_batch, tm, tf):
    m, d = x1.shape
    ff = w1.shape[1]
    return pl.pallas_call(
        _mlp_kernel,
        grid=(m // tm, ff // tf),
        in_specs=[pl.BlockSpec((tm, d), lambda i, f: (i, 0)),
                  pl.BlockSpec((tm, d), lambda i, f: (i, 0)),
                  pl.BlockSpec((d, tf), lambda i, f: (0, f)),
                  pl.BlockSpec((tf, d), lambda i, f: (f, 0)),
                  _mod_spec(g2, tm, rows_per_batch)],
        out_specs=pl.BlockSpec((tm, d), lambda i, f: (i, 0)),
        out_shape=jax.ShapeDtypeStruct((m, d), F32),
        scratch_shapes=[pltpu.VMEM((tm, d), F32)],
        compiler_params=_cparams(("parallel", "arbitrary")),
        name="mlp",
    )(h2, x1, w1, w2, g2)


_NT = (((1,), (1,)), ((), ()))
_TN = (((0,), (0,)), ((), ()))


def _hgrn_kernel(q_ref, z_ref, v_ref, g_ref, lb_ref, on_ref, s0_ref, o_ref, sfin_ref, st_ref, b_ref, *, chunk, heads):
    for h in range(heads):
        _hgrn_head(h, q_ref, z_ref, v_ref, g_ref, lb_ref, on_ref, s0_ref, o_ref, sfin_ref, st_ref, b_ref, chunk)


def _hgrn_head(h, q_ref, z_ref, v_ref, g_ref, lb_ref, on_ref, s0_ref, o_ref, sfin_ref, st_ref, b_ref, chunk):
    c = pl.program_id(2)
    cols = slice(h * HEAD_DIM, (h + 1) * HEAD_DIM)

    @pl.when(c == 0)
    def _():
        st_ref[h] = s0_ref[0, h].T

    q = q_ref[:, cols]
    z = z_ref[:, cols]
    v = v_ref[:, cols]
    lb = lb_ref[h]
    e = jnp.exp(-jnp.abs(z))
    r = 1.0 / (1.0 + e)
    pos = z >= 0.0
    logf = jnp.log(lb + (1.0 - lb) * jnp.where(pos, r, e * r))
    k = (1.0 - lb) * jnp.where(pos, e * r, r)

    t = lax.broadcasted_iota(jnp.int32, (chunk, HEAD_DIM), 0)
    b = logf
    s = 1
    while s < chunk:
        b = b + jnp.where(t >= s, pltpu.roll(b, s, 0), 0.0)
        s *= 2
    b_ref[h] = b

    t8 = t & (SUBLANES - 1)
    if chunk <= SUBLANES:
        o = jnp.zeros((chunk, HEAD_DIM), F32)
        for d in range(SUBLANES):
            kd, bd, vd = (k, b, v) if d == 0 else (pltpu.roll(k, d, 0), pltpu.roll(b, d, 0), pltpu.roll(v, d, 0))
            w = jnp.exp(jnp.where(t8 >= d, b - bd, -jnp.inf))
            o = o + jnp.sum(q * kd * w, axis=-1, keepdims=True) * vd
    else:
        o = jnp.sum(q * k, axis=-1, keepdims=True) * v
        row = lax.broadcasted_iota(jnp.int32, (chunk, chunk), 0)
        col = lax.broadcasted_iota(jnp.int32, (chunk, chunk), 1)
        apart = row ^ col
        att = jnp.zeros((chunk, chunk), F32)
        sub = _row_iota((SUBLANES, HEAD_DIM))
        m = 1
        while m < chunk:
            def ref_row(r, rows):
                return jnp.broadcast_to(b_ref[h, pl.ds(r, 1), :], (rows, HEAD_DIM))

            if 2 * m >= SUBLANES:
                refs = [ref_row(blk * 2 * m + m - 1, 2 * m) for blk in range(chunk // (2 * m))]
            else:
                refs = []
                for tile in range(chunk // SUBLANES):
                    ref_t = ref_row(tile * SUBLANES + m - 1, SUBLANES)
                    for j in range(1, SUBLANES // (2 * m)):
                        ref_t = jnp.where(sub >= j * 2 * m, ref_row(tile * SUBLANES + j * 2 * m + m - 1, SUBLANES), ref_t)
                    refs.append(ref_t)
            ref_b = refs[0] if len(refs) == 1 else jnp.concatenate(refs, axis=0)
            second = (t & (2 * m - 1)) >= m
            w = jnp.exp(jnp.where(second, b - ref_b, ref_b - b))
            qs = jnp.where(second, q * w, 0.0).astype(BF16)
            ks = jnp.where(second, 0.0, k * w).astype(BF16)
            a_m = lax.dot_general(qs, ks, _NT, preferred_element_type=F32)
            att = att + jnp.where(apart < 2 * m, a_m, 0.0)
            m *= 2
        o = o + jnp.dot(att.astype(BF16), v.astype(BF16), preferred_element_type=F32)

    st = st_ref[h]
    b_last = b_ref[h, pl.ds(chunk - 1, 1), :]
    o = o + lax.dot_general((q * jnp.exp(b)).astype(BF16), st.astype(BF16), _NT, preferred_element_type=F32)
    kt = (k * jnp.exp(b_last - b)).astype(BF16)
    vb = v.astype(BF16)
    if chunk < 2 * SUBLANES:
        pad = jnp.zeros((2 * SUBLANES - chunk, HEAD_DIM), BF16)
        kt = jnp.concatenate([kt, pad], axis=0)
        vb = jnp.concatenate([vb, pad], axis=0)
    st_new = jnp.exp(b_last) * st + lax.dot_general(vb, kt, _TN, preferred_element_type=F32)
    st_ref[h] = st_new

    o_ref[:, cols] = (_rms(o, on_ref[h]) * _silu(g_ref[:, cols])).astype(o_ref.dtype)

    @pl.when(c == pl.num_programs(2) - 1)
    def _():
        sfin_ref[0, h] = st_new.T


def hgrn(proj, lb, onorm, s0, n_batch, t_len, chunk, heads):
    m = proj.shape[0]
    n_c = t_len // chunk
    hb = N_HEADS_A // heads
    width = heads * HEAD_DIM

    def col(group):
        return pl.BlockSpec((chunk, width), lambda bi, h, c: (bi * n_c + c, COL_HGRN // width + group * hb + h))

    vec = pl.BlockSpec((heads, 1, HEAD_DIM), lambda bi, h, c: (h, 0, 0))
    state = pl.BlockSpec((1, heads, HEAD_DIM, HEAD_DIM), lambda bi, h, c: (bi, h, 0, 0))
    o_dtype = BF16 if chunk % (2 * SUBLANES) == 0 else F32
    o, s_fin = pl.pallas_call(
        functools.partial(_hgrn_kernel, chunk=chunk, heads=heads),
        grid=(n_batch, hb, n_c),
        in_specs=[col(0), col(1), col(2), col(3), vec, vec, state],
        out_specs=[pl.BlockSpec((chunk, width), lambda bi, h, c: (bi * n_c + c, h)), state],
        out_shape=[jax.ShapeDtypeStruct((m, HGRN_W), o_dtype),
                   jax.ShapeDtypeStruct((n_batch, N_HEADS_A, HEAD_DIM, HEAD_DIM), F32)],
        scratch_shapes=[pltpu.VMEM((heads, HEAD_DIM, HEAD_DIM), F32), pltpu.VMEM((heads, chunk, HEAD_DIM), F32)],
        compiler_params=_cparams(("parallel", "parallel", "arbitrary")),
        name="hgrn",
    )(proj, proj, proj, proj, lb.reshape(N_HEADS_A, 1, HEAD_DIM), onorm.reshape(N_HEADS_A, 1, HEAD_DIM), s0)
    return o.astype(BF16), s_fin


def _bucket_steps():
    n = np.arange(REL_MAX_DIST)
    exact = NUM_BUCKETS // 2
    val = np.log(np.maximum(n, 1) / exact) / math.log(REL_MAX_DIST / exact) * (NUM_BUCKETS - exact)
    frac = np.abs(val - np.round(val))[exact + 1:]
    assert frac.min() > 1e-3, "a bucket edge sits on an integer distance"
    lut = np.where(n < exact, n, np.minimum(exact + np.floor(np.maximum(val, 0.0)).astype(np.int64), NUM_BUCKETS - 1))
    assert lut[-1] == NUM_BUCKETS - 1
    return int(lut[0]), [(int(i), int(lut[i])) for i in range(1, REL_MAX_DIST) if lut[i] != lut[i - 1]]


_BUCKET0, _BUCKET_EDGES = _bucket_steps()
FAR_DIST = _BUCKET_EDGES[-1][0]


def _bias(rel, tab_ref, head):
    val = jnp.full(rel.shape, tab_ref[_BUCKET0 * N_HEADS_B + head], F32)
    for edge, bucket in _BUCKET_EDGES:
        val = jnp.where(rel >= edge, tab_ref[bucket * N_HEADS_B + head], val)
    return val


def _stack_heads(x):
    return jnp.concatenate([x[:, g * HEAD_DIM:(g + 1) * HEAD_DIM] for g in range(GROUP)], axis=0)


def _tile_heads(x):
    return jnp.concatenate([x] * GROUP, axis=0)


def _masked_softmax(s, mask):
    s = jnp.where(mask, s, -jnp.inf)
    m = jnp.max(s, axis=-1, keepdims=True)
    m = jnp.where(m == -jnp.inf, 0.0, m)
    e = jnp.exp(s - m)
    return e / jnp.maximum(jnp.sum(e, axis=-1, keepdims=True), 1e-30)


def _block_importance(p, n_cmp, n_blocks, width):
    rows = p.shape[0] // GROUP
    imp = p[0:rows]
    for g in range(1, GROUP):
        imp = imp + p[g * rows:(g + 1) * rows]
    ci = lax.broadcasted_iota(jnp.int32, (n_cmp, width), 0) * CMP_STRIDE
    si = lax.broadcasted_iota(jnp.int32, (n_cmp, width), 1) * SEL_BLOCK
    overlap = (ci < si + SEL_BLOCK) & (ci + CMP_BLOCK > si) & (si < n_blocks * SEL_BLOCK)
    ov = jnp.where(overlap, 1.0, 0.0).astype(BF16)
    hi = imp.astype(BF16)
    lo = (imp - hi.astype(F32)).astype(BF16)
    return jnp.dot(hi, ov, preferred_element_type=F32) + jnp.dot(lo, ov, preferred_element_type=F32)


def _select_blocks_t(imp_t, cur):
    blk = _row_iota(imp_t.shape)
    forced = (blk == 0) | (blk == cur) | (blk == cur - 1)
    score = jnp.where(blk <= cur, imp_t + jnp.where(forced, FORCE_SCORE, 0.0), -jnp.inf)
    sel = jnp.zeros(imp_t.shape, F32)
    n_rows = imp_t.shape[0]
    for _ in range(N_SEL):
        top = jnp.max(score, axis=0, keepdims=True)
        first = jnp.min(jnp.where(score == top, blk, n_rows), axis=0, keepdims=True)
        pick = blk == first
        sel = jnp.where(pick, 1.0, sel)
        score = jnp.where(pick, -jnp.inf, score)
    return sel


def _gate_mix(gate, parts):
    rows = gate.shape[0]
    gt = _sigmoid(gate)
    outs = []
    for g in range(GROUP):
        o = gt[:, 3 * g:3 * g + 1] * parts[0][g * rows:(g + 1) * rows]
        for j in (1, 2):
            o = o + gt[:, 3 * g + j:3 * g + j + 1] * parts[j][g * rows:(g + 1) * rows]
        outs.append(o)
    return jnp.concatenate(outs, axis=1)


ATT_KSA = 0
ATT_VS = N_KV * 2 * HEAD_DIM
ATT_KW = ATT_VS + KV_W
ATT_VW = ATT_KW + KV_W
ATT_COLS = ATT_VW + KV_W


def _nsa_prep_kernel(q_ref, cmp_ref, sel_ref, win_ref, qn_ref, ksn_ref, kwn_ref, qo_ref, co_ref, so_ref, wo_ref,
                     *att_ref, t_len):
    q = q_ref[...]
    scale = HEAD_DIM ** -0.5
    for h in range(N_HEADS_B):
        sl = slice(h * HEAD_DIM, (h + 1) * HEAD_DIM)
        qo_ref[:, sl] = _rms(q[:, sl], qn_ref[...]) * scale
    co_ref[...] = cmp_ref[...]
    for src, gain, dst in ((sel_ref, ksn_ref, so_ref), (win_ref, kwn_ref, wo_ref)):
        x = src[...]
        for h in range(N_KV):
            sl = slice(h * HEAD_DIM, (h + 1) * HEAD_DIM)
            dst[:, sl] = _rms(x[:, sl], gain[...])
        dst[:, KV_W:] = x[:, KV_W:]
    if att_ref:
        att_ref, crow_ref, srow_ref = att_ref
        tm = q.shape[0]
        for combo in range(TOKEN_ROWS):
            sl = slice(combo * HEAD_DIM, (combo + 1) * HEAD_DIM)
            crow_ref[pl.ds(combo, tm, stride=TOKEN_ROWS), :] = co_ref[:, sl]
            srow_ref[pl.ds(combo, tm, stride=TOKEN_ROWS), :] = so_ref[:, sl]
        pos = (pl.program_id(0) * tm + _row_iota((tm, LANES))) % t_len
        onehot = jnp.where(_lane_iota((tm, LANES)) == pos // SEL_BLOCK, 1.0, 0.0).astype(BF16)
        for h in range(N_KV):
            att_ref[:, ATT_KSA + 2 * h * HEAD_DIM:ATT_KSA + (2 * h + 1) * HEAD_DIM] = (
                so_ref[:, h * HEAD_DIM:(h + 1) * HEAD_DIM].astype(BF16))
            att_ref[:, ATT_KSA + (2 * h + 1) * HEAD_DIM:ATT_KSA + (2 * h + 2) * HEAD_DIM] = onehot
        att_ref[:, ATT_VS:ATT_KW] = so_ref[:, KV_W:].astype(BF16)
        att_ref[:, ATT_KW:ATT_VW] = wo_ref[:, :KV_W].astype(BF16)
        att_ref[:, ATT_VW:ATT_COLS] = wo_ref[:, KV_W:].astype(BF16)


def nsa_prep(proj, q_norm, ks_norm, kw_norm, tm, t_len, attention_operands):
    m = proj.shape[0]
    kvw = 2 * KV_W
    vec = pl.BlockSpec((1, HEAD_DIM), lambda i: (0, 0))
    widths = [NSA_W, kvw, kvw, kvw] + ([ATT_COLS] if attention_operands else [])
    dtypes = [F32] * 4 + ([BF16] if attention_operands else [])
    out_specs = [pl.BlockSpec((tm, wd), lambda i: (i, 0)) for wd in widths]
    out_shape = [jax.ShapeDtypeStruct((m, wd), dt) for wd, dt in zip(widths, dtypes)]
    if attention_operands:
        assert t_len // SEL_BLOCK <= LANES
        out_specs += [pl.BlockSpec((tm * TOKEN_ROWS, HEAD_DIM), lambda i: (i, 0))] * 2
        out_shape += [jax.ShapeDtypeStruct((m * TOKEN_ROWS, HEAD_DIM), F32)] * 2
    return pl.pallas_call(
        functools.partial(_nsa_prep_kernel, t_len=t_len),
        grid=(m // tm,),
        in_specs=[pl.BlockSpec((tm, NSA_W), lambda i: (i, COL_QB // NSA_W)),
                  pl.BlockSpec((tm, kvw), lambda i: (i, COL_CMP // kvw)),
                  pl.BlockSpec((tm, kvw), lambda i: (i, COL_SEL // kvw)),
                  pl.BlockSpec((tm, kvw), lambda i: (i, COL_WIN // kvw)),
                  vec, vec, vec],
        out_specs=out_specs,
        out_shape=out_shape,
        compiler_params=_cparams(("parallel",)),
        name="nsa_prep",
    )(proj, proj, proj, proj, q_norm.reshape(1, HEAD_DIM), ks_norm.reshape(1, HEAD_DIM), kw_norm.reshape(1, HEAD_DIM))


def _compress_kernel(x_ref, tail_ref, w1_ref, b1_ref, w2_ref, gain_ref, o_ref, x2_ref, a1_ref, *, n_half):
    for j in range(CMP_STRIDE):
        x2_ref[0:n_half, j * HEAD_DIM:(j + 1) * HEAD_DIM] = x_ref[pl.ds(j, n_half, stride=CMP_STRIDE), :]
        x2_ref[n_half:n_half + SUBLANES, j * HEAD_DIM:(j + 1) * HEAD_DIM] = jnp.broadcast_to(
            tail_ref[pl.ds(j, 1), :], (SUBLANES, HEAD_DIM))
    a = jnp.dot(x2_ref[...].astype(BF16), w1_ref[0], preferred_element_type=F32)
    a1_ref[...] = a[:, CMP_HID:]
    pre = a[0:n_half, :CMP_HID] + a1_ref[pl.ds(1, n_half), :] + b1_ref[0]
    out = jnp.dot(_silu(pre).astype(BF16), w2_ref[0], preferred_element_type=F32)
    is_k = pl.program_id(1) < N_KV
    o_ref[0, 0] = jnp.where(is_k, _rms(out, gain_ref[...]), out).astype(BF16)


def compress(raw, tail, w1, b1, w2, kc_norm, n_batch, n_half):
    t_len = n_half * CMP_STRIDE
    wsel = lambda b, c: (c // N_KV, 0, 0)
    return pl.pallas_call(
        functools.partial(_compress_kernel, n_half=n_half),
        grid=(n_batch, 2 * N_KV),
        in_specs=[pl.BlockSpec((t_len, HEAD_DIM), lambda b, c: (b, c)),
                  pl.BlockSpec((CMP_STRIDE, HEAD_DIM), lambda b, c: (b, c)),
                  pl.BlockSpec((1, CMP_STRIDE * HEAD_DIM, 2 * CMP_HID), wsel),
                  pl.BlockSpec((1, 1, CMP_HID), wsel),
                  pl.BlockSpec((1, CMP_HID, HEAD_DIM), wsel),
                  pl.BlockSpec((1, HEAD_DIM), lambda b, c: (0, 0))],
        out_specs=pl.BlockSpec((1, 1, n_half, HEAD_DIM), lambda b, c: (b, c, 0, 0)),
        out_shape=jax.ShapeDtypeStruct((n_batch, 2 * N_KV, n_half, HEAD_DIM), BF16),
        scratch_shapes=[pltpu.VMEM((n_half + SUBLANES, CMP_STRIDE * HEAD_DIM), F32),
                        pltpu.VMEM((n_half + SUBLANES, CMP_HID), F32)],
        compiler_params=_cparams(("parallel", "parallel")),
        name="compress",
    )(raw, tail, w1, b1, w2, kc_norm.reshape(1, HEAD_DIM))


TQ = 128
TK = 256
STRIP_W = TQ + 2 * TK
STRIP_ORIGIN = STRIP_W - TK
MASK_OFF = 1 << 20
BAND_ORIGIN = 64


UNSELECTED = -2.0 ** 30


def _nsa_prompt_kernel(tab_ref, q_ref, kc_ref, vc_ref, ksa_ref, vs_ref, kw_ref, vw_ref, gate_ref, o_ref,
                       strip_ref, band_ref, s_ref, wide_ref, acc_ref, *, n_half, n_blocks):
    kv = pl.program_id(1)
    n = pl.program_id(2)
    head0 = kv * GROUP
    q0 = n * TQ
    jd = n // 2
    odd = n - 2 * jd
    rows = GROUP * TQ
    far = [tab_ref[(NUM_BUCKETS - 1) * N_HEADS_B + head0 + g] for g in range(GROUP)]

    @pl.when(n == 0)
    def _():
        a = _row_iota((TQ, STRIP_W))
        u = _lane_iota((TQ, STRIP_W))
        for g in range(GROUP):
            strip_ref[g] = _bias(a + STRIP_ORIGIN - u, tab_ref, head0 + g) - far[g]
        rel = _row_iota((TQ, n_half)) - ((_lane_iota((TQ, n_half)) - BAND_ORIGIN) * CMP_STRIDE + CMP_BLOCK - 1)
        for g in range(GROUP):
            band_ref[g] = jnp.where(rel >= 0, _bias(rel, tab_ref, head0 + g) - far[g], 0.0)

    q = _stack_heads(q_ref[...])
    qs = q.astype(BF16)
    a_k = _row_iota((TQ, TK))
    c_k = _lane_iota((TQ, TK))

    def tile_start(jj):
        return pl.multiple_of(jnp.maximum(jj, 0) * TK, TK)

    def near_bias(d):
        u0 = pl.multiple_of(STRIP_ORIGIN - d * TK - odd * TQ, TQ)
        return jnp.concatenate([strip_ref[g, :, pl.ds(u0, TK)] for g in range(GROUP)], axis=0)

    def finish(n_tiles, v_ref):
        m = jnp.max(wide_ref[...], axis=-1, keepdims=True)
        m = jnp.where(m == -jnp.inf, 0.0, m)
        wide_ref[...] = jnp.zeros(wide_ref.shape, F32)
        acc_ref[...] = jnp.zeros(acc_ref.shape, F32)

        def one(jj):
            p = jnp.exp(s_ref[jj] - m)
            wide_ref[...] += p
            acc_ref[...] += jnp.dot(p.astype(BF16), v_ref[pl.ds(tile_start(jj), TK), :], preferred_element_type=F32)

        def pair(t, carry):
            one(2 * t)
            one(2 * t + 1)
            return carry

        lax.fori_loop(0, n_tiles // 2, pair, 0)

        @pl.when(n_tiles % 2 == 1)
        def _():
            one(n_tiles - 1)

        return acc_ref[...] / jnp.maximum(jnp.sum(wide_ref[...], axis=-1, keepdims=True), 1e-30)

    a_c = _row_iota((TQ, n_half))
    rel_c = q0 + a_c - (_lane_iota((TQ, n_half)) * CMP_STRIDE + CMP_BLOCK - 1)
    shift = (n * (TQ // CMP_STRIDE) - BAND_ORIGIN) % n_half
    bias_c = jnp.concatenate([pltpu.roll(band_ref[g], shift, 1) for g in range(GROUP)], axis=0)
    s = lax.dot_general(qs, kc_ref[0, 0], _NT, preferred_element_type=F32)
    p = _masked_softmax(s + bias_c, _tile_heads(rel_c) >= 0)
    o_cmp = jnp.dot(p.astype(BF16), vc_ref[0, 0], preferred_element_type=F32)
    imp = _block_importance(p, n_half, n_blocks, LANES)

    lead = _tile_heads(a_k - c_k) + odd * TQ
    causal = lead >= 0
    for d in range(3):
        k = kw_ref[pl.ds(tile_start(jd - d), TK), :]
        s = lax.dot_general(qs, k, _NT, preferred_element_type=F32)
        if d == 0:
            s = jnp.where(causal, s + near_bias(0), -jnp.inf)
        elif d == 1:
            s = s + near_bias(1) + jnp.where(jd >= 1, 0.0, -jnp.inf)
        else:
            s = jnp.where(lead < jnp.where(jd >= 2, 0, -MASK_OFF), s, -jnp.inf)
        s_ref[d] = s
        wide_ref[...] = s if d == 0 else jnp.maximum(wide_ref[...], s)
    m = jnp.max(wide_ref[...], axis=-1, keepdims=True)
    m = jnp.where(m == -jnp.inf, 0.0, m)
    num = jnp.zeros((rows, HEAD_DIM), F32)
    den = jnp.zeros((rows, TK), F32)
    for d in range(3):
        e = jnp.exp(s_ref[d] - m)
        den = den + e
        num = num + jnp.dot(e.astype(BF16), vw_ref[pl.ds(tile_start(jd - d), TK), :], preferred_element_type=F32)
    o_win = num / jnp.maximum(jnp.sum(den, axis=-1, keepdims=True), 1e-30)

    cur_t = (q0 + _lane_iota((LANES, TQ))) // SEL_BLOCK
    unsel = ((_select_blocks_t(imp.T, cur_t) - 1.0) * -UNSELECTED).T
    qa = jnp.concatenate([q, _tile_heads(unsel)], axis=1).astype(BF16)

    def scores(jj):
        return lax.dot_general(qa, ksa_ref[pl.ds(tile_start(jj), TK), :], _NT, preferred_element_type=F32)

    s = jnp.where(causal, scores(jd) + near_bias(0), -jnp.inf)
    s_ref[jd] = s
    wide_ref[...] = s

    @pl.when(jd >= 1)
    def _():
        s = scores(jd - 1) + near_bias(1)
        s_ref[jd - 1] = s
        wide_ref[...] = jnp.maximum(wide_ref[...], s)

    n_far = jnp.maximum(jd - 1, 0)

    def far_pair(t, carry):
        for jj in (2 * t, jnp.minimum(2 * t + 1, n_far - 1)):
            s = scores(jj)
            s_ref[jj] = s
            wide_ref[...] = jnp.maximum(wide_ref[...], s)
        return carry

    lax.fori_loop(0, (n_far + 1) // 2, far_pair, 0)
    o_sel = finish(jd + 1, vs_ref)

    gate = pltpu.roll(gate_ref[...], (LANES - kv * GROUP * 3) % LANES, 1)
    o_ref[...] = _gate_mix(gate, (o_cmp, o_sel, o_win)).astype(o_ref.dtype)


def nsa_prompt_attention(tab, qn, cmp_tok, att, proj, n_batch, t_len):
    m = qn.shape[0]
    n_q = t_len // TQ
    n_half = cmp_tok.shape[2]
    gw = GROUP * HEAD_DIM
    rows = GROUP * TQ
    seq = lambda col, width: pl.BlockSpec((t_len, width), lambda b, kv, n: (b, col // width + kv))
    tok = lambda off: pl.BlockSpec((1, 1, n_half, HEAD_DIM), lambda b, kv, n: (b, off + kv, 0, 0))
    return pl.pallas_call(
        functools.partial(_nsa_prompt_kernel, n_half=n_half, n_blocks=t_len // SEL_BLOCK),
        grid=(n_batch, N_KV, n_q),
        in_specs=[pl.BlockSpec(memory_space=pltpu.SMEM),
                  pl.BlockSpec((TQ, gw), lambda b, kv, n: (b * n_q + n, kv)),
                  tok(0), tok(N_KV), seq(ATT_KSA, 2 * HEAD_DIM), seq(ATT_VS, HEAD_DIM), seq(ATT_KW, HEAD_DIM),
                  seq(ATT_VW, HEAD_DIM),
                  pl.BlockSpec((TQ, LANES), lambda b, kv, n: (b * n_q + n, COL_GATE // LANES))],
        out_specs=pl.BlockSpec((TQ, gw), lambda b, kv, n: (b * n_q + n, kv)),
        out_shape=jax.ShapeDtypeStruct((m, NSA_W), BF16),
        scratch_shapes=[pltpu.VMEM((GROUP, TQ, STRIP_W), F32), pltpu.VMEM((GROUP, TQ, n_half), F32),
                        pltpu.VMEM((t_len // TK, rows, TK), F32),
                        pltpu.VMEM((rows, TK), F32), pltpu.VMEM((rows, HEAD_DIM), F32)],
        compiler_params=_cparams(("parallel", "parallel", "arbitrary")),
        name="nsa_prompt",
    )(tab, qn, cmp_tok, cmp_tok, att, att, att, att, proj)


PAGES_PER_STEP = 32
TOKEN_ROWS = 2 * N_KV
PAGE_ROWS = PAGE * TOKEN_ROWS


def _lane_queries(q, t_new):
    per_kv = GROUP * t_new
    stacked = jnp.concatenate([q[:, h * HEAD_DIM:(h + 1) * HEAD_DIM] for h in range(N_HEADS_B)], axis=0)
    out = []
    for kv in range(N_KV):
        parts = []
        if kv:
            parts.append(jnp.zeros((kv * per_kv, HEAD_DIM), F32))
        parts.append(stacked[kv * per_kv:(kv + 1) * per_kv])
        parts.append(jnp.zeros((LANES - (kv + 1) * per_kv, HEAD_DIM), F32))
        out.append(jnp.concatenate(parts, axis=0).astype(BF16))
    return out


def _lane_scores(keys, qz):
    s = lax.dot_general(keys[0], qz[0], _NT, preferred_element_type=F32)
    for kv in range(1, N_KV):
        s = s + lax.dot_general(keys[kv], qz[kv], _NT, preferred_element_type=F32)
    return s


def _lane_values(vals, p, per_kv):
    pb = p.astype(BF16)
    lane = _lane_iota((HEAD_DIM, LANES))
    out = lax.dot_general(vals[N_KV - 1], pb, _TN, preferred_element_type=F32)
    for kv in range(N_KV - 2, -1, -1):
        out = jnp.where(lane < (kv + 1) * per_kv, lax.dot_general(vals[kv], pb, _TN, preferred_element_type=F32), out)
    return out


def _lane_bias(rel, tabl_ref):
    val = jnp.broadcast_to(tabl_ref[_BUCKET0:_BUCKET0 + 1, :], rel.shape)
    for edge, bucket in _BUCKET_EDGES:
        val = jnp.where(rel >= edge, tabl_ref[bucket:bucket + 1, :], val)
    return val


def _pad_rows(x, rows):
    return jnp.concatenate([x, jnp.zeros((rows - x.shape[0], x.shape[1]), x.dtype)], axis=0)


PAGE_HALVES = PAGE // CMP_STRIDE
SLAB_ROWS = CMP_STRIDE * TOKEN_ROWS
SLAB_PITCH = SLAB_ROWS + SUBLANES
STAGE_ROWS = PAGES_PER_STEP * PAGE_HALVES * SLAB_PITCH


def _stage_copies(pt_ref, cache_ref, stage_ref, sem_ref, step, slot, n_chunks):
    b = step // n_chunks
    c = step - b * n_chunks
    copies = []
    for p in range(PAGES_PER_STEP):
        page = pt_ref[b, c * PAGES_PER_STEP + p]
        for n in range(PAGE_HALVES):
            dst = pl.multiple_of(slot * STAGE_ROWS + (p * PAGE_HALVES + n) * SLAB_PITCH, SUBLANES)
            copies.append(pltpu.make_async_copy(cache_ref.at[page, pl.ds(n * SLAB_ROWS, SLAB_ROWS), :],
                                                stage_ref.at[pl.ds(dst, SLAB_ROWS), :], sem_ref.at[slot]))
    return copies


def _sample_cmp_kernel(pt_ref, cache_ref, new_ref, q_ref, w1_ref, b1_ref, w2_ref, gain_ref, tabl_ref,
                       sel_ref, ocmp_ref, x2_ref, a_ref, stage_ref, sem_ref, *, past, t_new, n_blocks, sel_rows):
    c = pl.program_id(1)
    n_chunks = pl.num_programs(1)
    step = pl.program_id(0) * n_chunks + c
    slot = step % 2
    n_half = past // CMP_STRIDE
    step_halves = PAGES_PER_STEP * PAGE_HALVES
    per_kv = GROUP * t_new

    @pl.when(step == 0)
    def _():
        for cp in _stage_copies(pt_ref, cache_ref, stage_ref, sem_ref, step, slot, n_chunks):
            cp.start()

    @pl.when(step + 1 < pl.num_programs(0) * n_chunks)
    def _():
        for cp in _stage_copies(pt_ref, cache_ref, stage_ref, sem_ref, step + 1, 1 - slot, n_chunks):
            cp.start()

    for cp in _stage_copies(pt_ref, cache_ref, stage_ref, sem_ref, step, slot, n_chunks):
        cp.wait()

    base = slot * STAGE_ROWS
    for combo in range(TOKEN_ROWS):
        for p in range(PAGES_PER_STEP):
            for j in range(CMP_STRIDE):
                start = base + p * PAGE_HALVES * SLAB_PITCH + j * TOKEN_ROWS + combo
                x2_ref[combo, p * PAGE_HALVES:(p + 1) * PAGE_HALVES, j * HEAD_DIM:(j + 1) * HEAD_DIM] = (
                    stage_ref[pl.ds(start, PAGE_HALVES, stride=SLAB_PITCH), :])
        a = jnp.dot(x2_ref[combo].astype(BF16), w1_ref[combo // N_KV], preferred_element_type=F32)
        a_ref[combo, pl.ds(pl.multiple_of(c * step_halves, step_halves), step_halves), :] = a

    @pl.when(c == pl.num_programs(1) - 1)
    def _():
        tok = []
        for combo in range(TOKEN_ROWS):
            kind = combo // N_KV
            sl = slice(combo * HEAD_DIM, (combo + 1) * HEAD_DIM)
            row = jnp.concatenate([new_ref[j:j + 1, sl] for j in range(t_new)]
                                  + [jnp.zeros((1, (CMP_STRIDE - t_new) * HEAD_DIM), F32)], axis=1)
            tail = jnp.broadcast_to(row, (SUBLANES, CMP_STRIDE * HEAD_DIM)).astype(BF16)
            a_ref[combo, n_half:n_half + SUBLANES, :] = jnp.dot(tail, w1_ref[kind], preferred_element_type=F32)
            pre = a_ref[combo, 0:n_half, 0:CMP_HID] + a_ref[combo, pl.ds(1, n_half), CMP_HID:] + b1_ref[kind]
            out = jnp.dot(_silu(pre).astype(BF16), w2_ref[kind], preferred_element_type=F32)
            if kind == 0:
                out = _rms(out, gain_ref[...])
            tok.append(out.astype(BF16))

        qz = _lane_queries(q_ref[...], t_new)
        shape = (n_half, LANES)
        u = _lane_iota(shape) & (t_new - 1)
        rel = past + u - (_row_iota(shape) * CMP_STRIDE + CMP_BLOCK - 1)
        near0 = ((past - (CMP_BLOCK - 1) - FAR_DIST) // CMP_STRIDE + 1) // SUBLANES * SUBLANES
        bias = jnp.concatenate([jnp.broadcast_to(tabl_ref[NUM_BUCKETS - 1:NUM_BUCKETS, :], (near0, LANES)),
                                _lane_bias(rel[near0:], tabl_ref)], axis=0)
        s = jnp.where(rel >= 0, _lane_scores(tok[:N_KV], qz) + bias, -jnp.inf)
        m = jnp.max(s, axis=0, keepdims=True)
        e = jnp.exp(s - jnp.where(m == -jnp.inf, 0.0, m))
        p = e / jnp.maximum(jnp.sum(e, axis=0, keepdims=True), 1e-30)
        ocmp_ref[...] = _lane_values(tok[N_KV:], p, per_kv)

        src = _row_iota((LANES, LANES))
        dst = _lane_iota((LANES, LANES))
        pooled = (src // per_kv) * t_new + (src & (t_new - 1))
        pool = jnp.where((src < N_KV * per_kv) & (dst == pooled), 1.0, 0.0).astype(BF16)
        hi = p.astype(BF16)
        lo = (p - hi.astype(F32)).astype(BF16)
        imp = jnp.dot(hi, pool, preferred_element_type=F32) + jnp.dot(lo, pool, preferred_element_type=F32)
        si = _row_iota((sel_rows, n_half)) * SEL_BLOCK
        ci = _lane_iota((sel_rows, n_half)) * CMP_STRIDE
        overlap = (ci < si + SEL_BLOCK) & (ci + CMP_BLOCK > si) & (si < n_blocks * SEL_BLOCK)
        ov = jnp.where(overlap, 1.0, 0.0).astype(BF16)
        hi = imp.astype(BF16)
        lo = (imp - hi.astype(F32)).astype(BF16)
        imp = jnp.dot(ov, hi, preferred_element_type=F32) + jnp.dot(ov, lo, preferred_element_type=F32)

        cur = (past + (_lane_iota((sel_rows, LANES)) & (t_new - 1))) // SEL_BLOCK
        sel = _select_blocks_t(imp, cur)
        unpool = jnp.where((dst < N_KV * per_kv) & (src == (dst // per_kv) * t_new + (dst & (t_new - 1))), 1.0, 0.0)
        sel_ref[...] = jnp.dot(sel.astype(BF16), unpool.astype(BF16), preferred_element_type=F32)


def _page_specs(n_chunk_pages):
    def spec(j):
        return pl.BlockSpec((None, PAGE_ROWS, HEAD_DIM), lambda b, c, pt: (pt[b, c * n_chunk_pages + j], 0, 0))
    return [spec(j) for j in range(n_chunk_pages)]


def sample_compressed(page_table, cache3, cmp_new, qn, w1, b1, w2, kc_norm, tabl, t_new):
    n_seq, n_pages = page_table.shape
    assert n_pages % PAGES_PER_STEP == 0
    past = n_pages * PAGE
    n_half = past // CMP_STRIDE
    n_blocks = -(-(past + t_new) // SEL_BLOCK)
    sel_rows = -(-n_blocks // SUBLANES) * SUBLANES
    const = lambda *shape: pl.BlockSpec(shape, lambda b, c, pt: (0,) * len(shape))
    row = lambda width: pl.BlockSpec((t_new, width), lambda b, c, pt: (b, 0))
    out = lambda rows: pl.BlockSpec((None, rows, LANES), lambda b, c, pt: (b, 0, 0))
    grid_spec = pltpu.PrefetchScalarGridSpec(
        num_scalar_prefetch=1,
        grid=(n_seq, n_pages // PAGES_PER_STEP),
        in_specs=[pl.BlockSpec(memory_space=pl.ANY),
                  row(2 * KV_W), row(NSA_W), const(2, CMP_STRIDE * HEAD_DIM, 2 * CMP_HID), const(2, 1, CMP_HID),
                  const(2, CMP_HID, HEAD_DIM), const(1, HEAD_DIM), const(NUM_BUCKETS, LANES)],
        out_specs=[out(sel_rows), out(HEAD_DIM)],
        scratch_shapes=[pltpu.VMEM((TOKEN_ROWS, PAGES_PER_STEP * PAGE_HALVES, CMP_STRIDE * HEAD_DIM), F32),
                        pltpu.VMEM((TOKEN_ROWS, n_half + SUBLANES, 2 * CMP_HID), F32),
                        pltpu.VMEM((2 * STAGE_ROWS, HEAD_DIM), F32),
                        pltpu.SemaphoreType.DMA((2,))])
    return pl.pallas_call(
        functools.partial(_sample_cmp_kernel, past=past, t_new=t_new, n_blocks=n_blocks, sel_rows=sel_rows),
        grid_spec=grid_spec,
        out_shape=[jax.ShapeDtypeStruct((n_seq, sel_rows, LANES), F32),
                   jax.ShapeDtypeStruct((n_seq, HEAD_DIM, LANES), F32)],
        compiler_params=_cparams(("arbitrary", "arbitrary")),
        name="sample_cmp",
    )(page_table, cache3, cmp_new, qn, w1, b1, w2, kc_norm.reshape(1, HEAD_DIM), tabl)


def _sample_sel_kernel(pt_ref, *refs, past, t_new, win_buf):
    pages = refs[:PAGES_PER_STEP]
    (new_ref, q_ref, sel_ref, ocmp_ref, win_ref, wnew_ref, gate_ref, tabl_ref,
     o_ref, m_ref, l_ref, acc_ref, owin_ref, s_ref) = refs[PAGES_PER_STEP:]
    c = pl.program_id(1)
    per_kv = GROUP * t_new
    qz = _lane_queries(q_ref[...], t_new)
    far = tabl_ref[NUM_BUCKETS - 1:NUM_BUCKETS, :]
    new_rows = 2 * SUBLANES

    def new_kv(ref, kind):
        return [_pad_rows(ref[:, (kind * N_KV + kv) * HEAD_DIM:(kind * N_KV + kv + 1) * HEAD_DIM], new_rows).astype(BF16)
                for kv in range(N_KV)]

    def dist(pos0, rows):
        shape = (rows, LANES)
        return past + (_lane_iota(shape) & (t_new - 1)) - pos0 - _row_iota(shape)

    def update(s, values):
        m_prev = m_ref[...]
        m_new = jnp.maximum(m_prev, jnp.max(s, axis=0, keepdims=True))
        m_safe = jnp.where(m_new == -jnp.inf, 0.0, m_new)
        alpha = jnp.exp(m_prev - m_safe)
        p = jnp.exp(s - m_safe)
        l_ref[...] = alpha * l_ref[...] + jnp.sum(p, axis=0, keepdims=True)
        acc = alpha * acc_ref[...]
        for vals, lo, hi_ in values:
            acc = acc + _lane_values(vals, p[lo:hi_], per_kv)
        acc_ref[...] = acc
        m_ref[...] = m_new

    @pl.when(c == 0)
    def _():
        m_ref[...] = jnp.full(m_ref.shape, -jnp.inf, F32)
        l_ref[...] = jnp.zeros(l_ref.shape, F32)
        acc_ref[...] = jnp.zeros(acc_ref.shape, F32)

        kw = [win_ref[pl.ds(kv, win_buf, stride=TOKEN_ROWS), :].astype(BF16) for kv in range(N_KV)]
        vw = [win_ref[pl.ds(N_KV + kv, win_buf, stride=TOKEN_ROWS), :].astype(BF16) for kv in range(N_KV)]
        pieces = []
        near0 = (win_buf - FAR_DIST + 1) // SUBLANES * SUBLANES
        for keys, rel, near in ((kw, dist(past - win_buf, win_buf), near0), (new_kv(wnew_ref, 0), dist(past, new_rows), 0)):
            bias = _lane_bias(rel[near:], tabl_ref)
            if near:
                bias = jnp.concatenate([jnp.broadcast_to(far, (near, LANES)), bias], axis=0)
            s = _lane_scores(keys, qz) + bias
            pieces.append(jnp.where((rel >= 0) & (rel < WINDOW), s, -jnp.inf))
        m = jnp.maximum(jnp.max(pieces[0], axis=0, keepdims=True), jnp.max(pieces[1], axis=0, keepdims=True))
        m = jnp.where(m == -jnp.inf, 0.0, m)
        e0 = jnp.exp(pieces[0] - m)
        e1 = jnp.exp(pieces[1] - m)
        tot = jnp.sum(e0, axis=0, keepdims=True) + jnp.sum(e1, axis=0, keepdims=True)
        num = _lane_values(vw, e0, per_kv) + _lane_values(new_kv(wnew_ref, 1), e1, per_kv)
        owin_ref[...] = num / jnp.maximum(tot, 1e-30)

        rel = dist(past, new_rows)
        s = _lane_scores(new_kv(new_ref, 0), qz) + _lane_bias(rel, tabl_ref)
        chosen = sel_ref[past // SEL_BLOCK:past // SEL_BLOCK + 1, :] > 0.5
        update(jnp.where(chosen & (rel >= 0), s, -jnp.inf), [(new_kv(new_ref, 1), 0, new_rows)])

    values = []
    for p in range(PAGES_PER_STEP):
        page = c * PAGES_PER_STEP + p
        keys = [pages[p][pl.ds(kv, PAGE, stride=TOKEN_ROWS), :].astype(BF16) for kv in range(N_KV)]
        s = _lane_scores(keys, qz) + far
        per_block = PAGE // SEL_BLOCK
        chosen = jnp.concatenate(
            [jnp.broadcast_to(sel_ref[pl.ds(page * per_block + i, 1), :], (SEL_BLOCK, LANES)) for i in range(per_block)],
            axis=0)
        s_ref[p * PAGE:(p + 1) * PAGE, :] = jnp.where(chosen > 0.5, s, -jnp.inf)
        vals = [pages[p][pl.ds(N_KV + kv, PAGE, stride=TOKEN_ROWS), :].astype(BF16) for kv in range(N_KV)]
        values.append((vals, p * PAGE, (p + 1) * PAGE))

    @pl.when(c == pl.num_programs(1) - 1)
    def _():
        rows = slice((PAGES_PER_STEP - 1) * PAGE, PAGES_PER_STEP * PAGE)
        s_ref[rows, :] = s_ref[rows, :] + (_lane_bias(dist(past - PAGE, PAGE), tabl_ref) - far)

    update(s_ref[...], values)

    @pl.when(c == pl.num_programs(1) - 1)
    def _():
        gt = _sigmoid(gate_ref[...])
        osel = acc_ref[...] / jnp.maximum(l_ref[...], 1e-30)
        o_ref[...] = gt[0:1] * ocmp_ref[...] + gt[1:2] * osel + gt[2:3] * owin_ref[...]


def sample_selected(page_table, cache3, sel_new, qn, sel_t, ocmp_t, win3, win_new, gate_l, tabl, t_new):
    n_seq, n_pages = page_table.shape
    past = n_pages * PAGE
    win_buf = win3.shape[1] // TOKEN_ROWS
    assert (past - FAR_DIST) // PAGE >= n_pages - 1, "only the last page may hold near keys"
    sel_rows = sel_t.shape[1]
    const = lambda *shape: pl.BlockSpec(shape, lambda b, c, pt: (0,) * len(shape))
    row = lambda width: pl.BlockSpec((t_new, width), lambda b, c, pt: (b, 0))
    per_seq = lambda rows, width: pl.BlockSpec((None, rows, width), lambda b, c, pt: (b, 0, 0))
    grid_spec = pltpu.PrefetchScalarGridSpec(
        num_scalar_prefetch=1,
        grid=(n_seq, n_pages // PAGES_PER_STEP),
        in_specs=_page_specs(PAGES_PER_STEP) + [
            row(2 * KV_W), row(NSA_W), per_seq(sel_rows, LANES), per_seq(HEAD_DIM, LANES),
            per_seq(win_buf * TOKEN_ROWS, HEAD_DIM), row(2 * KV_W), per_seq(SUBLANES, LANES),
            const(NUM_BUCKETS, LANES)],
        out_specs=per_seq(HEAD_DIM, LANES),
        scratch_shapes=[pltpu.VMEM((1, LANES), F32), pltpu.VMEM((1, LANES), F32), pltpu.VMEM((HEAD_DIM, LANES), F32),
                        pltpu.VMEM((HEAD_DIM, LANES), F32), pltpu.VMEM((PAGES_PER_STEP * PAGE, LANES), F32)])
    return pl.pallas_call(
        functools.partial(_sample_sel_kernel, past=past, t_new=t_new, win_buf=win_buf),
        grid_spec=grid_spec,
        out_shape=jax.ShapeDtypeStruct((n_seq, HEAD_DIM, LANES), F32),
        compiler_params=_cparams(("parallel", "arbitrary")),
        name="sample_sel",
    )(page_table, *([cache3] * PAGES_PER_STEP), sel_new, qn, sel_t, ocmp_t, win3, win_new, gate_l, tabl)


def _split_in_proj(w_in):
    d, cols = w_in.shape
    assert cols == COL_GATE + N_GATE + MERGE_COLS
    wt = jnp.swapaxes(w_in, 0, 1).astype(BF16)
    return wt, PROJ_COLS, wt[COL_GATE + N_GATE:]


def _compress_weights(k_w1, k_b1, k_w2, v_w1, v_b1, v_w2):
    half = CMP_STRIDE * HEAD_DIM
    cat = lambda w: jnp.concatenate([w[:half], w[half:]], axis=1)
    w1 = jnp.stack([cat(k_w1), cat(v_w1)]).astype(BF16)
    b1 = jnp.stack([k_b1, v_b1]).reshape(2, 1, CMP_HID)
    w2 = jnp.stack([k_w2, v_w2]).astype(BF16)
    return w1, b1, w2


def _lane_table(rel_bias, t_new):
    tab = jnp.repeat(rel_bias, t_new, axis=1)
    return jnp.pad(tab, ((0, 0), (0, LANES - tab.shape[1])))


def _gate_lanes(proj, n_seq, t_new):
    per_kv = GROUP * 3
    g = jnp.stack([proj[:, COL_GATE + kv * per_kv:COL_GATE + (kv + 1) * per_kv] for kv in range(N_KV)], axis=1)
    g = g.reshape(n_seq, t_new, N_KV, GROUP, 3).transpose(0, 4, 2, 3, 1).reshape(n_seq, 3, N_HEADS_B * t_new)
    return jnp.pad(g, ((0, 0), (0, SUBLANES - 3), (0, LANES - N_HEADS_B * t_new)))


def _tiles(m, t_len):
    big = m >= 1024
    return dict(proj_tm=1024 if big else m, proj_tn=1024, prep_tm=512 if big else m, post_tm=256,
                mlp_tm=512 if big else m, mlp_tf=1024,
                hgrn_chunk=min(t_len, 256), hgrn_heads=N_HEADS_A)


def _trunk(x, mods, s0, w, nsa_fn, attention_operands):
    nb, t_len, d = x.shape
    m = nb * t_len
    tl = _tiles(m, t_len)
    sh1, sc1, g1, sh2, sc2, g2 = mods
    x2 = x.reshape(m, d)
    proj, merge = in_proj(x2, w["norm1"], sc1, sh1, *w["w_in"], t_len, tl["proj_tm"], tl["proj_tn"])
    o_a, s_fin = hgrn(proj, w["lb"], w["hgrn_onorm"], s0, nb, t_len, tl["hgrn_chunk"], tl["hgrn_heads"])
    qn, cmp_new, sel_new, win_new, *att = nsa_prep(proj, w["q_norm"], w["ks_norm"], w["kw_norm"], tl["prep_tm"], t_len,
                                                   attention_operands)
    o_b = nsa_fn(proj, qn, cmp_new, sel_new, win_new, *att[:1])
    x1, h2 = post_attn(x2, o_a, o_b, merge, w["w_ba"], w["w_bb"], w["w_out"], g1, w["norm2"], sc2, sh2, t_len,
                       tl["post_tm"])
    y = mlp(h2, x1, w["mlp_w1"], w["mlp_w2"], g2, t_len, tl["mlp_tm"], tl["mlp_tf"])
    if attention_operands:
        cmp_new, sel_new = att[1:]
    return y.reshape(nb, t_len, d), s_fin, (cmp_new, sel_new, win_new)


def kernel(x_prompt, x_sample, c_prompt, c_sample, cache_cmp_kv, cache_sel_kv, cache_win_kv, state_hgrn, page_table, hgrn_lb_logits, rel_bias, ada_w, ada_b, norm1, norm2, w_in, hgrn_onorm, nsa_q_norm, nsa_kc_norm, nsa_ks_norm, nsa_kw_norm, cmp_k_w1, cmp_k_b1, cmp_k_w2, cmp_v_w1, cmp_v_b1, cmp_v_w2, w_branch_a, w_branch_b, w_out, mlp_w1, mlp_w2):
    n_p, t_p, d = x_prompt.shape
    n_s, t_s, _ = x_sample.shape
    past = page_table.shape[1] * PAGE
    win_buf = cache_win_kv.shape[2]
    kvw = 2 * KV_W
    layer = 0

    lb_all = jnp.cumsum(jax.nn.softmax(hgrn_lb_logits.astype(F32), axis=0), axis=0)
    cw1, cb1, cw2 = _compress_weights(cmp_k_w1[layer], cmp_k_b1[layer], cmp_k_w2[layer],
                                      cmp_v_w1[layer], cmp_v_b1[layer], cmp_v_w2[layer])
    w = dict(norm1=norm1[layer], norm2=norm2[layer], w_in=_split_in_proj(w_in[layer]),
             hgrn_onorm=hgrn_onorm[layer], lb=lb_all[layer], q_norm=nsa_q_norm[layer],
             ks_norm=nsa_ks_norm[layer], kw_norm=nsa_kw_norm[layer],
             w_ba=w_branch_a[layer].astype(BF16), w_bb=w_branch_b[layer].astype(BF16),
             w_out=w_out[layer].astype(BF16), mlp_w1=mlp_w1[layer].astype(BF16), mlp_w2=mlp_w2[layer].astype(BF16))
    tab = rel_bias.astype(F32).reshape(NUM_BUCKETS * N_HEADS_B)

    mods = ada_mods(jnp.concatenate([c_prompt, c_sample], axis=0).astype(F32), ada_w[layer], ada_b[layer])
    mods = jnp.split(mods, 6, axis=-1)
    mods_p = [a[:n_p].reshape(n_p, 1, d) for a in mods]
    mods_s = [jnp.repeat(a[n_p:], t_s, axis=0).reshape(1, n_s * t_s, d) for a in mods]

    def nsa_p(proj, qn, cmp_new, sel_new, win_new, att):
        tail = jnp.zeros((n_p * CMP_STRIDE, kvw), F32)
        cmp_tok = compress(cmp_new, tail, cw1, cb1, cw2, nsa_kc_norm[layer], n_p, t_p // CMP_STRIDE)
        return nsa_prompt_attention(tab, qn, cmp_tok, att, proj, n_p, t_p)

    s0_p = jnp.zeros((n_p, N_HEADS_A, HEAD_DIM, HEAD_DIM), F32)
    y_p, hg_p, (cmp_p, sel_p, win_p) = _trunk(x_prompt, mods_p, s0_p, w, nsa_p, True)

    def nsa_s(proj, qn, cmp_new, sel_new, win_new):
        n_phys = cache_cmp_kv.shape[1]
        cmp3 = cache_cmp_kv[layer].reshape(n_phys, PAGE_ROWS, HEAD_DIM)
        sel3 = cache_sel_kv[layer].reshape(n_phys, PAGE_ROWS, HEAD_DIM)
        win3 = cache_win_kv[layer].reshape(n_s, win_buf * TOKEN_ROWS, HEAD_DIM)
        tabl = _lane_table(rel_bias.astype(F32), t_s)
        sel_t, ocmp_t = sample_compressed(page_table, cmp3, cmp_new, qn, cw1, cb1, cw2, nsa_kc_norm[layer], tabl, t_s)
        o_t = sample_selected(page_table, sel3, sel_new, qn, sel_t, ocmp_t, win3, win_new,
                              _gate_lanes(proj, n_s, t_s), tabl, t_s)
        o = o_t[:, :, :N_HEADS_B * t_s].reshape(n_s, HEAD_DIM, N_HEADS_B, t_s)
        return o.transpose(0, 3, 2, 1).reshape(n_s * t_s, NSA_W).astype(BF16)

    y_s, hg_s, (cmp_s, sel_s, win_s) = _trunk(x_sample, mods_s, state_hgrn[layer], w, nsa_s, False)

    dt = x_prompt.dtype
    pages = lambda a: a.reshape(1, n_p, t_p // PAGE, PAGE, 2, N_KV, HEAD_DIM).astype(dt)
    rows_s = lambda a: a.reshape(1, n_s, t_s, 2, N_KV, HEAD_DIM).astype(dt)
    win_keep = min(WINDOW, t_p)
    win_p_out = win_p.reshape(n_p, t_p, kvw)[:, t_p - win_keep:].reshape(1, n_p, win_keep, 2, N_KV, HEAD_DIM)
    win_rows = cache_win_kv[layer].reshape(n_s, win_buf * TOKEN_ROWS, HEAD_DIM).astype(F32)
    win_s_out = jnp.concatenate([win_rows, win_s.reshape(n_s, t_s * TOKEN_ROWS, HEAD_DIM)], axis=1)
    win_s_out = win_s_out[:, t_s * TOKEN_ROWS:].reshape(1, n_s, win_buf, 2, N_KV, HEAD_DIM)
    return (y_p, y_s, pages(cmp_p), rows_s(cmp_s), pages(sel_p), rows_s(sel_s),
            win_p_out.astype(dt), win_s_out.astype(dt), hg_p[None].astype(dt), hg_s[None].astype(dt))
```

```python
import functools
import math

import numpy as np
import jax
import jax.numpy as jnp
from jax import lax
from jax.experimental import pallas as pl
from jax.experimental.pallas import tpu as pltpu

F32 = jnp.float32
BF16 = jnp.bfloat16

D_MODEL = 2048
N_HEADS_A = 8
HEAD_DIM = 128
HGRN_W = N_HEADS_A * HEAD_DIM
N_HEADS_B = 8
N_KV = 2
GROUP = N_HEADS_B // N_KV
NSA_W = N_HEADS_B * HEAD_DIM
KV_W = N_KV * HEAD_DIM
PAGE = 128
CMP_STRIDE = 16
CMP_BLOCK = 32
CMP_HID = 128
SEL_BLOCK = 64
N_SEL = 16
WINDOW = 512
FORCE_SCORE = 1.0e6
NUM_BUCKETS = 32
REL_MAX_DIST = 128
EPS = 1e-6
N_GATE = 3 * N_HEADS_B

LANES = 128
SUBLANES = 8
VMEM_LIMIT = 56 * 1024 * 1024
VMEM_LIMIT_MAX = 60 * 1024 * 1024

COL_HGRN = 0
COL_QB = 4 * HGRN_W
COL_CMP = COL_QB + NSA_W
COL_SEL = COL_CMP + 2 * KV_W
COL_WIN = COL_SEL + 2 * KV_W
COL_GATE = COL_WIN + 2 * KV_W
GATE_COLS = 512
PROJ_COLS = COL_GATE + GATE_COLS
MERGE_COLS = 2 * D_MODEL


def _cparams(sem):
    return pltpu.CompilerParams(dimension_semantics=sem, vmem_limit_bytes=VMEM_LIMIT)


def _sigmoid(x):
    return 1.0 / (1.0 + jnp.exp(-x))


def _silu(x):
    return x * _sigmoid(x)


def _rms(x, gain):
    return x * lax.rsqrt(jnp.mean(x * x, axis=-1, keepdims=True) + EPS) * gain


def _lane_iota(shape):
    return lax.broadcasted_iota(jnp.int32, shape, 1)


def _row_iota(shape):
    return lax.broadcasted_iota(jnp.int32, shape, 0)


def _ada_kernel(c_ref, w_ref, b_ref, o_ref):
    a = _silu(c_ref[...]).astype(BF16)
    o_ref[...] = jnp.dot(a, w_ref[...].astype(BF16), preferred_element_type=F32) + b_ref[...]


def ada_mods(c, w, b):
    r, d = c.shape
    n = w.shape[1]
    tn = 1024
    return pl.pallas_call(
        _ada_kernel,
        grid=(n // tn,),
        in_specs=[pl.BlockSpec((r, d), lambda j: (0, 0)),
                  pl.BlockSpec((d, tn), lambda j: (0, j)),
                  pl.BlockSpec((1, tn), lambda j: (0, j))],
        out_specs=pl.BlockSpec((r, tn), lambda j: (0, j)),
        out_shape=jax.ShapeDtypeStruct((r, n), F32),
        compiler_params=_cparams(("parallel",)),
        name="ada_mods",
    )(c, w, b.reshape(1, n))


def _inproj_kernel(x_ref, gain_ref, sc_ref, sh_ref, wa_ref, wb_ref, oa_ref, ob_ref, h_ref, *, n_a):
    j = pl.program_id(1)

    @pl.when(j == 0)
    def _():
        h = _rms(x_ref[...], gain_ref[...]) * (1.0 + sc_ref[0]) + sh_ref[0]
        h_ref[...] = h.astype(BF16)

    nt = (((1,), (1,)), ((), ()))

    @pl.when(j < n_a)
    def _():
        oa_ref[...] = lax.dot_general(h_ref[...], wa_ref[...], nt, preferred_element_type=F32)

    @pl.when(j >= n_a)
    def _():
        ob_ref[...] = lax.dot_general(h_ref[...], wb_ref[...], nt, preferred_element_type=F32)


def _mod_spec(mod, tm, rows_per_batch):
    d = mod.shape[-1]
    if mod.shape[1] == 1:
        return pl.BlockSpec((1, 1, d), lambda i, *_: ((i * tm) // rows_per_batch, 0, 0))
    return pl.BlockSpec((1, tm, d), lambda i, *_: (0, i, 0))


def in_proj(x2, gain, sc, sh, w_a, cols_a, w_b, rows_per_batch, tm, tn):
    m, d = x2.shape
    n_a = cols_a // tn
    n_b = w_b.shape[0] // tn
    col_a = lambda i, j: jnp.minimum(j, n_a - 1)
    col_b = lambda i, j: jnp.maximum(j - n_a, 0)
    return pl.pallas_call(
        functools.partial(_inproj_kernel, n_a=n_a),
        grid=(m // tm, n_a + n_b),
        in_specs=[pl.BlockSpec((tm, d), lambda i, j: (i, 0)),
                  pl.BlockSpec((1, d), lambda i, j: (0, 0)),
                  _mod_spec(sc, tm, rows_per_batch),
                  _mod_spec(sh, tm, rows_per_batch),
                  pl.BlockSpec((tn, d), lambda i, j: (col_a(i, j), 0)),
                  pl.BlockSpec((tn, d), lambda i, j: (col_b(i, j), 0))],
        out_specs=[pl.BlockSpec((tm, tn), lambda i, j: (i, col_a(i, j))),
                   pl.BlockSpec((tm, tn), lambda i, j: (i, col_b(i, j)))],
        out_shape=[jax.ShapeDtypeStruct((m, cols_a), F32), jax.ShapeDtypeStruct((m, w_b.shape[0]), F32)],
        scratch_shapes=[pltpu.VMEM((tm, d), BF16)],
        compiler_params=pltpu.CompilerParams(dimension_semantics=("parallel", "arbitrary"),
                                             vmem_limit_bytes=VMEM_LIMIT_MAX),
        name="in_proj",
    )(x2, gain.reshape(1, d), sc, sh, w_a, w_b)


def _post_kernel(x_ref, oa_ref, ob_ref, mga_ref, mgb_ref, wba_ref, wbb_ref, wout_ref,
                 g1_ref, gain2_ref, sc2_ref, sh2_ref, x1_ref, h2_ref):
    ya = jnp.dot(oa_ref[...], wba_ref[...], preferred_element_type=F32)
    yb = jnp.dot(ob_ref[...], wbb_ref[...], preferred_element_type=F32)
    merged = _sigmoid(mga_ref[...]) * ya + _sigmoid(mgb_ref[...]) * yb
    y = jnp.dot(merged.astype(BF16), wout_ref[...], preferred_element_type=F32)
    x1 = x_ref[...] + g1_ref[0] * y
    x1_ref[...] = x1
    h2_ref[...] = (_rms(x1, gain2_ref[...]) * (1.0 + sc2_ref[0]) + sh2_ref[0]).astype(BF16)


def post_attn(x2, oa, ob, proj, wba, wbb, wout, g1, gain2, sc2, sh2, rows_per_batch, tm):
    m, d = x2.shape
    const = lambda i: (0, 0)
    resident = lambda shape: pl.BlockSpec(shape, const, pipeline_mode=pl.Buffered(1))
    return pl.pallas_call(
        _post_kernel,
        grid=(m // tm,),
        in_specs=[pl.BlockSpec((tm, d), lambda i: (i, 0)),
                  pl.BlockSpec((tm, HGRN_W), lambda i: (i, 0)),
                  pl.BlockSpec((tm, NSA_W), lambda i: (i, 0)),
                  pl.BlockSpec((tm, d), lambda i: (i, 0)),
                  pl.BlockSpec((tm, d), lambda i: (i, 1)),
                  resident((HGRN_W, d)),
                  resident((NSA_W, d)),
                  resident((d, d)),
                  _mod_spec(g1, tm, rows_per_batch),
                  pl.BlockSpec((1, d), const),
                  _mod_spec(sc2, tm, rows_per_batch),
                  _mod_spec(sh2, tm, rows_per_batch)],
        out_specs=[pl.BlockSpec((tm, d), lambda i: (i, 0)),
                   pl.BlockSpec((tm, d), lambda i: (i, 0))],
        out_shape=[jax.ShapeDtypeStruct((m, d), F32), jax.ShapeDtypeStruct((m, d), BF16)],
        compiler_params=_cparams(("parallel",)),
        name="post_attn",
    )(x2, oa, ob, proj, proj, wba, wbb, wout, g1, gain2.reshape(1, d), sc2, sh2)


def _mlp_kernel(h_ref, x1_ref, w1_ref, w2_ref, g2_ref, y_ref, acc_ref):
    f = pl.program_id(1)
    u = jnp.maximum(jnp.dot(h_ref[...], w1_ref[...], preferred_element_type=F32), 0.0)
    part = jnp.dot((u * u).astype(BF16), w2_ref[...], preferred_element_type=F32)

    @pl.when(f == 0)
    def _():
        acc_ref[...] = part

    @pl.when(f > 0)
    def _():
        acc_ref[...] += part

    @pl.when(f == pl.num_programs(1) - 1)
    def _():
        y_ref[...] = x1_ref[...] + g2_ref[0] * acc_ref[...]


def mlp(h2, x1, w1, w2, g2, rows_per_batch, tm, tf):
    m, d = x1.shape
    ff = w1.shape[1]
    return pl.pallas_call(
        _mlp_kernel,
        grid=(m // tm, ff // tf),
        in_specs=[pl.BlockSpec((tm, d), lambda i, f: (i, 0)),
                  pl.BlockSpec((tm, d), lambda i, f: (i, 0)),
                  pl.BlockSpec((d, tf), lambda i, f: (0, f)),
                  pl.BlockSpec((tf, d), lambda i, f: (f, 0)),
                  _mod_spec(g2, tm, rows_per_batch)],
        out_specs=pl.BlockSpec((tm, d), lambda i, f: (i, 0)),
        out_shape=jax.ShapeDtypeStruct((m, d), F32),
        scratch_shapes=[pltpu.VMEM((tm, d), F32)],
        compiler_params=_cparams(("parallel", "arbitrary")),
        name="mlp",
    )(h2, x1, w1, w2, g2)


_NT = (((1,), (1,)), ((), ()))
_TN = (((0,), (0,)), ((), ()))


def _hgrn_kernel(q_ref, z_ref, v_ref, g_ref, lb_ref, on_ref, s0_ref, o_ref, sfin_ref, st_ref, b_ref, *, chunk, heads):
    for h in range(heads):
        _hgrn_head(h, q_ref, z_ref, v_ref, g_ref, lb_ref, on_ref, s0_ref, o_ref, sfin_ref, st_ref, b_ref, chunk)


def _hgrn_head(h, q_ref, z_ref, v_ref, g_ref, lb_ref, on_ref, s0_ref, o_ref, sfin_ref, st_ref, b_ref, chunk):
    c = pl.program_id(2)
    cols = slice(h * HEAD_DIM, (h + 1) * HEAD_DIM)

    @pl.when(c == 0)
    def _():
        st_ref[h] = s0_ref[0, h].T

    q = q_ref[:, cols]
    z = z_ref[:, cols]
    v = v_ref[:, cols]
    lb = lb_ref[h]
    e = jnp.exp(-jnp.abs(z))
    r = 1.0 / (1.0 + e)
    pos = z >= 0.0
    logf = jnp.log(lb + (1.0 - lb) * jnp.where(pos, r, e * r))
    k = (1.0 - lb) * jnp.where(pos, e * r, r)

    t = lax.broadcasted_iota(jnp.int32, (chunk, HEAD_DIM), 0)
    b = logf
    s = 1
    while s < chunk:
        b = b + jnp.where(t >= s, pltpu.roll(b, s, 0), 0.0)
        s *= 2
    b_ref[h] = b

    t8 = t & (SUBLANES - 1)
    if chunk <= SUBLANES:
        o = jnp.zeros((chunk, HEAD_DIM), F32)
        for d in range(SUBLANES):
            kd, bd, vd = (k, b, v) if d == 0 else (pltpu.roll(k, d, 0), pltpu.roll(b, d, 0), pltpu.roll(v, d, 0))
            w = jnp.exp(jnp.where(t8 >= d, b - bd, -jnp.inf))
            o = o + jnp.sum(q * kd * w, axis=-1, keepdims=True) * vd
    else:
        o = jnp.sum(q * k, axis=-1, keepdims=True) * v
        row = lax.broadcasted_iota(jnp.int32, (chunk, chunk), 0)
        col = lax.broadcasted_iota(jnp.int32, (chunk, chunk), 1)
        apart = row ^ col
        att = jnp.zeros((chunk, chunk), F32)
        sub = _row_iota((SUBLANES, HEAD_DIM))
        m = 1
        while m < chunk:
            def ref_row(r, rows):
                return jnp.broadcast_to(b_ref[h, pl.ds(r, 1), :], (rows, HEAD_DIM))

            if 2 * m >= SUBLANES:
                refs = [ref_row(blk * 2 * m + m - 1, 2 * m) for blk in range(chunk // (2 * m))]
            else:
                refs = []
                for tile in range(chunk // SUBLANES):
                    ref_t = ref_row(tile * SUBLANES + m - 1, SUBLANES)
                    for j in range(1, SUBLANES // (2 * m)):
                        ref_t = jnp.where(sub >= j * 2 * m, ref_row(tile * SUBLANES + j * 2 * m + m - 1, SUBLANES), ref_t)
                    refs.append(ref_t)
            ref_b = refs[0] if len(refs) == 1 else jnp.concatenate(refs, axis=0)
            second = (t & (2 * m - 1)) >= m
            w = jnp.exp(jnp.where(second, b - ref_b, ref_b - b))
            qs = jnp.where(second, q * w, 0.0).astype(BF16)
            ks = jnp.where(second, 0.0, k * w).astype(BF16)
            a_m = lax.dot_general(qs, ks, _NT, preferred_element_type=F32)
            att = att + jnp.where(apart < 2 * m, a_m, 0.0)
            m *= 2
        o = o + jnp.dot(att.astype(BF16), v.astype(BF16), preferred_element_type=F32)

    st = st_ref[h]
    b_last = b_ref[h, pl.ds(chunk - 1, 1), :]
    o = o + lax.dot_general((q * jnp.exp(b)).astype(BF16), st.astype(BF16), _NT, preferred_element_type=F32)
    kt = (k * jnp.exp(b_last - b)).astype(BF16)
    vb = v.astype(BF16)
    if chunk < 2 * SUBLANES:
        pad = jnp.zeros((2 * SUBLANES - chunk, HEAD_DIM), BF16)
        kt = jnp.concatenate([kt, pad], axis=0)
        vb = jnp.concatenate([vb, pad], axis=0)
    st_new = jnp.exp(b_last) * st + lax.dot_general(vb, kt, _TN, preferred_element_type=F32)
    st_ref[h] = st_new

    o_ref[:, cols] = (_rms(o, on_ref[h]) * _silu(g_ref[:, cols])).astype(o_ref.dtype)

    @pl.when(c == pl.num_programs(2) - 1)
    def _():
        sfin_ref[0, h] = st_new.T


def hgrn(proj, lb, onorm, s0, n_batch, t_len, chunk, heads):
    m = proj.shape[0]
    n_c = t_len // chunk
    hb = N_HEADS_A // heads
    width = heads * HEAD_DIM

    def col(group):
        return pl.BlockSpec((chunk, width), lambda bi, h, c: (bi * n_c + c, COL_HGRN // width + group * hb + h))

    vec = pl.BlockSpec((heads, 1, HEAD_DIM), lambda bi, h, c: (h, 0, 0))
    state = pl.BlockSpec((1, heads, HEAD_DIM, HEAD_DIM), lambda bi, h, c: (bi, h, 0, 0))
    o_dtype = BF16 if chunk % (2 * SUBLANES) == 0 else F32
    o, s_fin = pl.pallas_call(
        functools.partial(_hgrn_kernel, chunk=chunk, heads=heads),
        grid=(n_batch, hb, n_c),
        in_specs=[col(0), col(1), col(2), col(3), vec, vec, state],
        out_specs=[pl.BlockSpec((chunk, width), lambda bi, h, c: (bi * n_c + c, h)), state],
        out_shape=[jax.ShapeDtypeStruct((m, HGRN_W), o_dtype),
                   jax.ShapeDtypeStruct((n_batch, N_HEADS_A, HEAD_DIM, HEAD_DIM), F32)],
        scratch_shapes=[pltpu.VMEM((heads, HEAD_DIM, HEAD_DIM), F32), pltpu.VMEM((heads, chunk, HEAD_DIM), F32)],
        compiler_params=_cparams(("parallel", "parallel", "arbitrary")),
        name="hgrn",
    )(proj, proj, proj, proj, lb.reshape(N_HEADS_A, 1, HEAD_DIM), onorm.reshape(N_HEADS_A, 1, HEAD_DIM), s0)
    return o.astype(BF16), s_fin


def _bucket_steps():
    n = np.arange(REL_MAX_DIST)
    exact = NUM_BUCKETS // 2
    val = np.log(np.maximum(n, 1) / exact) / math.log(REL_MAX_DIST / exact) * (NUM_BUCKETS - exact)
    frac = np.abs(val - np.round(val))[exact + 1:]
    assert frac.min() > 1e-3, "a bucket edge sits on an integer distance"
    lut = np.where(n < exact, n, np.minimum(exact + np.floor(np.maximum(val, 0.0)).astype(np.int64), NUM_BUCKETS - 1))
    assert lut[-1] == NUM_BUCKETS - 1
    return int(lut[0]), [(int(i), int(lut[i])) for i in range(1, REL_MAX_DIST) if lut[i] != lut[i - 1]]


_BUCKET0, _BUCKET_EDGES = _bucket_steps()
FAR_DIST = _BUCKET_EDGES[-1][0]


def _bias(rel, tab_ref, head):
    val = jnp.full(rel.shape, tab_ref[_BUCKET0 * N_HEADS_B + head], F32)
    for edge, bucket in _BUCKET_EDGES:
        val = jnp.where(rel >= edge, tab_ref[bucket * N_HEADS_B + head], val)
    return val


def _stack_heads(x):
    return jnp.concatenate([x[:, g * HEAD_DIM:(g + 1) * HEAD_DIM] for g in range(GROUP)], axis=0)


def _tile_heads(x):
    return jnp.concatenate([x] * GROUP, axis=0)


def _masked_softmax(s, mask):
    s = jnp.where(mask, s, -jnp.inf)
    m = jnp.max(s, axis=-1, keepdims=True)
    m = jnp.where(m == -jnp.inf, 0.0, m)
    e = jnp.exp(s - m)
    return e / jnp.maximum(jnp.sum(e, axis=-1, keepdims=True), 1e-30)


def _block_importance(p, n_cmp, n_blocks, width):
    rows = p.shape[0] // GROUP
    imp = p[0:rows]
    for g in range(1, GROUP):
        imp = imp + p[g * rows:(g + 1) * rows]
    ci = lax.broadcasted_iota(jnp.int32, (n_cmp, width), 0) * CMP_STRIDE
    si = lax.broadcasted_iota(jnp.int32, (n_cmp, width), 1) * SEL_BLOCK
    overlap = (ci < si + SEL_BLOCK) & (ci + CMP_BLOCK > si) & (si < n_blocks * SEL_BLOCK)
    ov = jnp.where(overlap, 1.0, 0.0).astype(BF16)
    hi = imp.astype(BF16)
    lo = (imp - hi.astype(F32)).astype(BF16)
    return jnp.dot(hi, ov, preferred_element_type=F32) + jnp.dot(lo, ov, preferred_element_type=F32)


def _select_blocks_t(imp_t, cur):
    blk = _row_iota(imp_t.shape)
    forced = (blk == 0) | (blk == cur) | (blk == cur - 1)
    score = jnp.where(blk <= cur, imp_t + jnp.where(forced, FORCE_SCORE, 0.0), -jnp.inf)
    sel = jnp.zeros(imp_t.shape, F32)
    n_rows = imp_t.shape[0]
    for _ in range(N_SEL):
        top = jnp.max(score, axis=0, keepdims=True)
        first = jnp.min(jnp.where(score == top, blk, n_rows), axis=0, keepdims=True)
        pick = blk == first
        sel = jnp.where(pick, 1.0, sel)
        score = jnp.where(pick, -jnp.inf, score)
    return sel


def _gate_mix(gate, parts):
    rows = gate.shape[0]
    gt = _sigmoid(gate)
    outs = []
    for g in range(GROUP):
        o = gt[:, 3 * g:3 * g + 1] * parts[0][g * rows:(g + 1) * rows]
        for j in (1, 2):
            o = o + gt[:, 3 * g + j:3 * g + j + 1] * parts[j][g * rows:(g + 1) * rows]
        outs.append(o)
    return jnp.concatenate(outs, axis=1)


ATT_KSA = 0
ATT_VS = N_KV * 2 * HEAD_DIM
ATT_KW = ATT_VS + KV_W
ATT_VW = ATT_KW + KV_W
ATT_COLS = ATT_VW + KV_W


def _nsa_prep_kernel(q_ref, cmp_ref, sel_ref, win_ref, qn_ref, ksn_ref, kwn_ref, qo_ref, co_ref, so_ref, wo_ref,
                     *att_ref, t_len):
    q = q_ref[...]
    scale = HEAD_DIM ** -0.5
    for h in range(N_HEADS_B):
        sl = slice(h * HEAD_DIM, (h + 1) * HEAD_DIM)
        qo_ref[:, sl] = (_rms(q[:, sl], qn_ref[...]) * scale).astype(qo_ref.dtype)
    co_ref[...] = cmp_ref[...]
    for src, gain, dst in ((sel_ref, ksn_ref, so_ref), (win_ref, kwn_ref, wo_ref)):
        x = src[...]
        for h in range(N_KV):
            sl = slice(h * HEAD_DIM, (h + 1) * HEAD_DIM)
            dst[:, sl] = _rms(x[:, sl], gain[...])
        dst[:, KV_W:] = x[:, KV_W:]
    if att_ref:
        att_ref, crow_ref, srow_ref = att_ref
        tm = q.shape[0]
        for combo in range(TOKEN_ROWS):
            sl = slice(combo * HEAD_DIM, (combo + 1) * HEAD_DIM)
            crow_ref[pl.ds(combo, tm, stride=TOKEN_ROWS), :] = co_ref[:, sl]
            srow_ref[pl.ds(combo, tm, stride=TOKEN_ROWS), :] = so_ref[:, sl]
        pos = (pl.program_id(0) * tm + _row_iota((tm, LANES))) % t_len
        onehot = jnp.where(_lane_iota((tm, LANES)) == pos // SEL_BLOCK, 1.0, 0.0).astype(BF16)
        for h in range(N_KV):
            att_ref[:, ATT_KSA + 2 * h * HEAD_DIM:ATT_KSA + (2 * h + 1) * HEAD_DIM] = (
                so_ref[:, h * HEAD_DIM:(h + 1) * HEAD_DIM].astype(BF16))
            att_ref[:, ATT_KSA + (2 * h + 1) * HEAD_DIM:ATT_KSA + (2 * h + 2) * HEAD_DIM] = onehot
        att_ref[:, ATT_VS:ATT_KW] = so_ref[:, KV_W:].astype(BF16)
        att_ref[:, ATT_KW:ATT_VW] = wo_ref[:, :KV_W].astype(BF16)
        att_ref[:, ATT_VW:ATT_COLS] = wo_ref[:, KV_W:].astype(BF16)


def nsa_prep(proj, q_norm, ks_norm, kw_norm, tm, t_len, attention_operands):
    m = proj.shape[0]
    kvw = 2 * KV_W
    vec = pl.BlockSpec((1, HEAD_DIM), lambda i: (0, 0))
    widths = [NSA_W, kvw, kvw, kvw] + ([ATT_COLS] if attention_operands else [])
    dtypes = [BF16 if attention_operands else F32] + [F32] * 3 + ([BF16] if attention_operands else [])
    out_specs = [pl.BlockSpec((tm, wd), lambda i: (i, 0)) for wd in widths]
    out_shape = [jax.ShapeDtypeStruct((m, wd), dt) for wd, dt in zip(widths, dtypes)]
    if attention_operands:
        assert t_len // SEL_BLOCK <= LANES
        out_specs += [pl.BlockSpec((tm * TOKEN_ROWS, HEAD_DIM), lambda i: (i, 0))] * 2
        out_shape += [jax.ShapeDtypeStruct((m * TOKEN_ROWS, HEAD_DIM), F32)] * 2
    return pl.pallas_call(
        functools.partial(_nsa_prep_kernel, t_len=t_len),
        grid=(m // tm,),
        in_specs=[pl.BlockSpec((tm, NSA_W), lambda i: (i, COL_QB // NSA_W)),
                  pl.BlockSpec((tm, kvw), lambda i: (i, COL_CMP // kvw)),
                  pl.BlockSpec((tm, kvw), lambda i: (i, COL_SEL // kvw)),
                  pl.BlockSpec((tm, kvw), lambda i: (i, COL_WIN // kvw)),
                  vec, vec, vec],
        out_specs=out_specs,
        out_shape=out_shape,
        compiler_params=_cparams(("parallel",)),
        name="nsa_prep",
    )(proj, proj, proj, proj, q_norm.reshape(1, HEAD_DIM), ks_norm.reshape(1, HEAD_DIM), kw_norm.reshape(1, HEAD_DIM))


def _compress_kernel(x_ref, tail_ref, w1_ref, b1_ref, w2_ref, gain_ref, o_ref, x2_ref, a1_ref, *, n_half):
    for j in range(CMP_STRIDE):
        x2_ref[0:n_half, j * HEAD_DIM:(j + 1) * HEAD_DIM] = x_ref[pl.ds(j, n_half, stride=CMP_STRIDE), :]
        x2_ref[n_half:n_half + SUBLANES, j * HEAD_DIM:(j + 1) * HEAD_DIM] = jnp.broadcast_to(
            tail_ref[pl.ds(j, 1), :], (SUBLANES, HEAD_DIM))
    a = jnp.dot(x2_ref[...].astype(BF16), w1_ref[0], preferred_element_type=F32)
    a1_ref[...] = a[:, CMP_HID:]
    pre = a[0:n_half, :CMP_HID] + a1_ref[pl.ds(1, n_half), :] + b1_ref[0]
    out = jnp.dot(_silu(pre).astype(BF16), w2_ref[0], preferred_element_type=F32)
    is_k = pl.program_id(1) < N_KV
    o_ref[0, 0] = jnp.where(is_k, _rms(out, gain_ref[...]), out).astype(BF16)


def compress(raw, tail, w1, b1, w2, kc_norm, n_batch, n_half):
    t_len = n_half * CMP_STRIDE
    wsel = lambda b, c: (c // N_KV, 0, 0)
    return pl.pallas_call(
        functools.partial(_compress_kernel, n_half=n_half),
        grid=(n_batch, 2 * N_KV),
        in_specs=[pl.BlockSpec((t_len, HEAD_DIM), lambda b, c: (b, c)),
                  pl.BlockSpec((CMP_STRIDE, HEAD_DIM), lambda b, c: (b, c)),
                  pl.BlockSpec((1, CMP_STRIDE * HEAD_DIM, 2 * CMP_HID), wsel),
                  pl.BlockSpec((1, 1, CMP_HID), wsel),
                  pl.BlockSpec((1, CMP_HID, HEAD_DIM), wsel),
                  pl.BlockSpec((1, HEAD_DIM), lambda b, c: (0, 0))],
        out_specs=pl.BlockSpec((1, 1, n_half, HEAD_DIM), lambda b, c: (b, c, 0, 0)),
        out_shape=jax.ShapeDtypeStruct((n_batch, 2 * N_KV, n_half, HEAD_DIM), BF16),
        scratch_shapes=[pltpu.VMEM((n_half + SUBLANES, CMP_STRIDE * HEAD_DIM), F32),
                        pltpu.VMEM((n_half + SUBLANES, CMP_HID), F32)],
        compiler_params=_cparams(("parallel", "parallel")),
        name="compress",
    )(raw, tail, w1, b1, w2, kc_norm.reshape(1, HEAD_DIM))


TQ = 128
TK = 256
STRIP_W = TQ + 2 * TK
STRIP_ORIGIN = STRIP_W - TK
MASK_OFF = 1 << 20
BAND_ORIGIN = 64


UNSELECTED = -2.0 ** 30


def _nsa_prompt_kernel(tab_ref, q_ref, kc_ref, vc_ref, ksa_ref, vs_ref, kw_ref, vw_ref, gate_ref, o_ref,
                       strip_ref, band_ref, s_ref, wide_ref, acc_ref, *, n_half, n_blocks):
    kv = pl.program_id(1)
    n = pl.program_id(2)
    head0 = kv * GROUP
    q0 = n * TQ
    jd = n // 2
    odd = n - 2 * jd
    rows = GROUP * TQ
    far = [tab_ref[(NUM_BUCKETS - 1) * N_HEADS_B + head0 + g] for g in range(GROUP)]

    @pl.when(n == 0)
    def _():
        a = _row_iota((TQ, STRIP_W))
        u = _lane_iota((TQ, STRIP_W))
        for g in range(GROUP):
            strip_ref[g] = _bias(a + STRIP_ORIGIN - u, tab_ref, head0 + g) - far[g]
        rel = _row_iota((TQ, n_half)) - ((_lane_iota((TQ, n_half)) - BAND_ORIGIN) * CMP_STRIDE + CMP_BLOCK - 1)
        for g in range(GROUP):
            band_ref[g] = jnp.where(rel >= 0, _bias(rel, tab_ref, head0 + g) - far[g], 0.0)

    qs = _stack_heads(q_ref[...])
    a_k = _row_iota((TQ, TK))
    c_k = _lane_iota((TQ, TK))

    def tile_start(jj):
        return pl.multiple_of(jnp.maximum(jj, 0) * TK, TK)

    def near_bias(d):
        u0 = pl.multiple_of(STRIP_ORIGIN - d * TK - odd * TQ, TQ)
        return jnp.concatenate([strip_ref[g, :, pl.ds(u0, TK)] for g in range(GROUP)], axis=0)

    def finish(n_tiles, v_ref):
        m = jnp.max(wide_ref[...], axis=-1, keepdims=True)
        m = jnp.where(m == -jnp.inf, 0.0, m)
        wide_ref[...] = jnp.zeros(wide_ref.shape, F32)
        acc_ref[...] = jnp.zeros(acc_ref.shape, F32)

        def one(jj):
            p = jnp.exp(s_ref[jj] - m)
            wide_ref[...] += p
            acc_ref[...] += jnp.dot(p.astype(BF16), v_ref[pl.ds(tile_start(jj), TK), :], preferred_element_type=F32)

        def pair(t, carry):
            one(2 * t)
            one(2 * t + 1)
            return carry

        lax.fori_loop(0, n_tiles // 2, pair, 0)

        @pl.when(n_tiles % 2 == 1)
        def _():
            one(n_tiles - 1)

        return acc_ref[...] / jnp.maximum(jnp.sum(wide_ref[...], axis=-1, keepdims=True), 1e-30)

    a_c = _row_iota((TQ, n_half))
    rel_c = q0 + a_c - (_lane_iota((TQ, n_half)) * CMP_STRIDE + CMP_BLOCK - 1)
    shift = (n * (TQ // CMP_STRIDE) - BAND_ORIGIN) % n_half
    bias_c = jnp.concatenate([pltpu.roll(band_ref[g], shift, 1) for g in range(GROUP)], axis=0)
    s = lax.dot_general(qs, kc_ref[0, 0], _NT, preferred_element_type=F32)
    p = _masked_softmax(s + bias_c, _tile_heads(rel_c) >= 0)
    o_cmp = jnp.dot(p.astype(BF16), vc_ref[0, 0], preferred_element_type=F32)
    imp = _block_importance(p, n_half, n_blocks, LANES)

    lead = _tile_heads(a_k - c_k) + odd * TQ
    causal = lead >= 0
    for d in range(3):
        k = kw_ref[pl.ds(tile_start(jd - d), TK), :]
        s = lax.dot_general(qs, k, _NT, preferred_element_type=F32)
        if d == 0:
            s = jnp.where(causal, s + near_bias(0), -jnp.inf)
        elif d == 1:
            s = s + near_bias(1) + jnp.where(jd >= 1, 0.0, -jnp.inf)
        else:
            s = jnp.where(lead < jnp.where(jd >= 2, 0, -MASK_OFF), s, -jnp.inf)
        s_ref[d] = s
        wide_ref[...] = s if d == 0 else jnp.maximum(wide_ref[...], s)
    m = jnp.max(wide_ref[...], axis=-1, keepdims=True)
    m = jnp.where(m == -jnp.inf, 0.0, m)
    num = jnp.zeros((rows, HEAD_DIM), F32)
    den = jnp.zeros((rows, TK), F32)
    for d in range(3):
        e = jnp.exp(s_ref[d] - m)
        den = den + e
        num = num + jnp.dot(e.astype(BF16), vw_ref[pl.ds(tile_start(jd - d), TK), :], preferred_element_type=F32)
    o_win = num / jnp.maximum(jnp.sum(den, axis=-1, keepdims=True), 1e-30)

    cur_t = (q0 + _lane_iota((LANES, TQ))) // SEL_BLOCK
    unsel = ((_select_blocks_t(imp.T, cur_t) - 1.0) * -UNSELECTED).T
    qa = jnp.concatenate([qs, _tile_heads(unsel).astype(BF16)], axis=1)

    def scores(jj):
        return lax.dot_general(qa, ksa_ref[pl.ds(tile_start(jj), TK), :], _NT, preferred_element_type=F32)

    s = jnp.where(causal, scores(jd) + near_bias(0), -jnp.inf)
    s_ref[jd] = s
    wide_ref[...] = s

    @pl.when(jd >= 1)
    def _():
        s = scores(jd - 1) + near_bias(1)
        s_ref[jd - 1] = s
        wide_ref[...] = jnp.maximum(wide_ref[...], s)

    n_far = jnp.maximum(jd - 1, 0)

    def far_pair(t, carry):
        for jj in (2 * t, jnp.minimum(2 * t + 1, n_far - 1)):
            s = scores(jj)
            s_ref[jj] = s
            wide_ref[...] = jnp.maximum(wide_ref[...], s)
        return carry

    lax.fori_loop(0, (n_far + 1) // 2, far_pair, 0)
    o_sel = finish(jd + 1, vs_ref)

    gate = pltpu.roll(gate_ref[...], (LANES - kv * GROUP * 3) % LANES, 1)
    o_ref[...] = _gate_mix(gate, (o_cmp, o_sel, o_win)).astype(o_ref.dtype)


def nsa_prompt_attention(tab, qn, cmp_tok, att, proj, n_batch, t_len):
    m = qn.shape[0]
    n_q = t_len // TQ
    n_half = cmp_tok.shape[2]
    gw = GROUP * HEAD_DIM
    rows = GROUP * TQ
    seq = lambda col, width: pl.BlockSpec((t_len, width), lambda b, kv, n: (b, col // width + kv))
    tok = lambda off: pl.BlockSpec((1, 1, n_half, HEAD_DIM), lambda b, kv, n: (b, off + kv, 0, 0))
    return pl.pallas_call(
        functools.partial(_nsa_prompt_kernel, n_half=n_half, n_blocks=t_len // SEL_BLOCK),
        grid=(n_batch, N_KV, n_q),
        in_specs=[pl.BlockSpec(memory_space=pltpu.SMEM),
                  pl.BlockSpec((TQ, gw), lambda b, kv, n: (b * n_q + n, kv)),
                  tok(0), tok(N_KV), seq(ATT_KSA, 2 * HEAD_DIM), seq(ATT_VS, HEAD_DIM), seq(ATT_KW, HEAD_DIM),
                  seq(ATT_VW, HEAD_DIM),
                  pl.BlockSpec((TQ, LANES), lambda b, kv, n: (b * n_q + n, COL_GATE // LANES))],
        out_specs=pl.BlockSpec((TQ, gw), lambda b, kv, n: (b * n_q + n, kv)),
        out_shape=jax.ShapeDtypeStruct((m, NSA_W), BF16),
        scratch_shapes=[pltpu.VMEM((GROUP, TQ, STRIP_W), F32), pltpu.VMEM((GROUP, TQ, n_half), F32),
                        pltpu.VMEM((t_len // TK, rows, TK), F32),
                        pltpu.VMEM((rows, TK), F32), pltpu.VMEM((rows, HEAD_DIM), F32)],
        compiler_params=_cparams(("parallel", "parallel", "arbitrary")),
        name="nsa_prompt",
    )(tab, qn, cmp_tok, cmp_tok, att, att, att, att, proj)


PAGES_PER_STEP = 32
TOKEN_ROWS = 2 * N_KV
PAGE_ROWS = PAGE * TOKEN_ROWS


def _lane_queries(q, t_new):
    per_kv = GROUP * t_new
    stacked = jnp.concatenate([q[:, h * HEAD_DIM:(h + 1) * HEAD_DIM] for h in range(N_HEADS_B)], axis=0)
    out = []
    for kv in range(N_KV):
        parts = []
        if kv:
            parts.append(jnp.zeros((kv * per_kv, HEAD_DIM), F32))
        parts.append(stacked[kv * per_kv:(kv + 1) * per_kv])
        parts.append(jnp.zeros((LANES - (kv + 1) * per_kv, HEAD_DIM), F32))
        out.append(jnp.concatenate(parts, axis=0).astype(BF16))
    return out


def _lane_scores(keys, qz):
    s = lax.dot_general(keys[0], qz[0], _NT, preferred_element_type=F32)
    for kv in range(1, N_KV):
        s = s + lax.dot_general(keys[kv], qz[kv], _NT, preferred_element_type=F32)
    return s


def _lane_values(vals, p, per_kv):
    pb = p.astype(BF16)
    lane = _lane_iota((HEAD_DIM, LANES))
    out = lax.dot_general(vals[N_KV - 1], pb, _TN, preferred_element_type=F32)
    for kv in range(N_KV - 2, -1, -1):
        out = jnp.where(lane < (kv + 1) * per_kv, lax.dot_general(vals[kv], pb, _TN, preferred_element_type=F32), out)
    return out


def _lane_bias(rel, tabl_ref):
    val = jnp.broadcast_to(tabl_ref[_BUCKET0:_BUCKET0 + 1, :], rel.shape)
    for edge, bucket in _BUCKET_EDGES:
        val = jnp.where(rel >= edge, tabl_ref[bucket:bucket + 1, :], val)
    return val


def _pad_rows(x, rows):
    return jnp.concatenate([x, jnp.zeros((rows - x.shape[0], x.shape[1]), x.dtype)], axis=0)


PAGE_HALVES = PAGE // CMP_STRIDE
SLAB_ROWS = CMP_STRIDE * TOKEN_ROWS
SLAB_PITCH = SLAB_ROWS + SUBLANES
STAGE_ROWS = PAGES_PER_STEP * PAGE_HALVES * SLAB_PITCH


def _stage_copies(pt_ref, cache_ref, stage_ref, sem_ref, step, slot, n_chunks):
    b = step // n_chunks
    c = step - b * n_chunks
    copies = []
    for p in range(PAGES_PER_STEP):
        page = pt_ref[b, c * PAGES_PER_STEP + p]
        for n in range(PAGE_HALVES):
            dst = pl.multiple_of(slot * STAGE_ROWS + (p * PAGE_HALVES + n) * SLAB_PITCH, SUBLANES)
            copies.append(pltpu.make_async_copy(cache_ref.at[page, pl.ds(n * SLAB_ROWS, SLAB_ROWS), :],
                                                stage_ref.at[pl.ds(dst, SLAB_ROWS), :], sem_ref.at[slot]))
    return copies


def _sample_cmp_kernel(pt_ref, cache_ref, new_ref, q_ref, w1_ref, b1_ref, w2_ref, gain_ref, tabl_ref,
                       sel_ref, ocmp_ref, x2_ref, a_ref, stage_ref, sem_ref, *, past, t_new, n_blocks, sel_rows):
    c = pl.program_id(1)
    n_chunks = pl.num_programs(1)
    step = pl.program_id(0) * n_chunks + c
    slot = step % 2
    n_half = past // CMP_STRIDE
    step_halves = PAGES_PER_STEP * PAGE_HALVES
    per_kv = GROUP * t_new

    @pl.when(step == 0)
    def _():
        for cp in _stage_copies(pt_ref, cache_ref, stage_ref, sem_ref, step, slot, n_chunks):
            cp.start()

    @pl.when(step + 1 < pl.num_programs(0) * n_chunks)
    def _():
        for cp in _stage_copies(pt_ref, cache_ref, stage_ref, sem_ref, step + 1, 1 - slot, n_chunks):
            cp.start()

    for cp in _stage_copies(pt_ref, cache_ref, stage_ref, sem_ref, step, slot, n_chunks):
        cp.wait()

    base = slot * STAGE_ROWS
    for combo in range(TOKEN_ROWS):
        for p in range(PAGES_PER_STEP):
            for j in range(CMP_STRIDE):
                start = base + p * PAGE_HALVES * SLAB_PITCH + j * TOKEN_ROWS + combo
                x2_ref[combo, p * PAGE_HALVES:(p + 1) * PAGE_HALVES, j * HEAD_DIM:(j + 1) * HEAD_DIM] = (
                    stage_ref[pl.ds(start, PAGE_HALVES, stride=SLAB_PITCH), :])
        a = jnp.dot(x2_ref[combo].astype(BF16), w1_ref[combo // N_KV], preferred_element_type=F32)
        a_ref[combo, pl.ds(pl.multiple_of(c * step_halves, step_halves), step_halves), :] = a

    @pl.when(c == pl.num_programs(1) - 1)
    def _():
        tok = []
        for combo in range(TOKEN_ROWS):
            kind = combo // N_KV
            sl = slice(combo * HEAD_DIM, (combo + 1) * HEAD_DIM)
            row = jnp.concatenate([new_ref[j:j + 1, sl] for j in range(t_new)]
                                  + [jnp.zeros((1, (CMP_STRIDE - t_new) * HEAD_DIM), F32)], axis=1)
            tail = jnp.broadcast_to(row, (SUBLANES, CMP_STRIDE * HEAD_DIM)).astype(BF16)
            a_ref[combo, n_half:n_half + SUBLANES, :] = jnp.dot(tail, w1_ref[kind], preferred_element_type=F32)
            pre = a_ref[combo, 0:n_half, 0:CMP_HID] + a_ref[combo, pl.ds(1, n_half), CMP_HID:] + b1_ref[kind]
            out = jnp.dot(_silu(pre).astype(BF16), w2_ref[kind], preferred_element_type=F32)
            if kind == 0:
                out = _rms(out, gain_ref[...])
            tok.append(out.astype(BF16))

        qz = _lane_queries(q_ref[...], t_new)
        shape = (n_half, LANES)
        u = _lane_iota(shape) & (t_new - 1)
        rel = past + u - (_row_iota(shape) * CMP_STRIDE + CMP_BLOCK - 1)
        near0 = ((past - (CMP_BLOCK - 1) - FAR_DIST) // CMP_STRIDE + 1) // SUBLANES * SUBLANES
        bias = jnp.concatenate([jnp.broadcast_to(tabl_ref[NUM_BUCKETS - 1:NUM_BUCKETS, :], (near0, LANES)),
                                _lane_bias(rel[near0:], tabl_ref)], axis=0)
        s = jnp.where(rel >= 0, _lane_scores(tok[:N_KV], qz) + bias, -jnp.inf)
        m = jnp.max(s, axis=0, keepdims=True)
        e = jnp.exp(s - jnp.where(m == -jnp.inf, 0.0, m))
        p = e / jnp.maximum(jnp.sum(e, axis=0, keepdims=True), 1e-30)
        ocmp_ref[...] = _lane_values(tok[N_KV:], p, per_kv)

        src = _row_iota((LANES, LANES))
        dst = _lane_iota((LANES, LANES))
        pooled = (src // per_kv) * t_new + (src & (t_new - 1))
        pool = jnp.where((src < N_KV * per_kv) & (dst == pooled), 1.0, 0.0).astype(BF16)
        hi = p.astype(BF16)
        lo = (p - hi.astype(F32)).astype(BF16)
        imp = jnp.dot(hi, pool, preferred_element_type=F32) + jnp.dot(lo, pool, preferred_element_type=F32)
        si = _row_iota((sel_rows, n_half)) * SEL_BLOCK
        ci = _lane_iota((sel_rows, n_half)) * CMP_STRIDE
        overlap = (ci < si + SEL_BLOCK) & (ci + CMP_BLOCK > si) & (si < n_blocks * SEL_BLOCK)
        ov = jnp.where(overlap, 1.0, 0.0).astype(BF16)
        hi = imp.astype(BF16)
        lo = (imp - hi.astype(F32)).astype(BF16)
        imp = jnp.dot(ov, hi, preferred_element_type=F32) + jnp.dot(ov, lo, preferred_element_type=F32)

        cur = (past + (_lane_iota((sel_rows, LANES)) & (t_new - 1))) // SEL_BLOCK
        sel = _select_blocks_t(imp, cur)
        unpool = jnp.where((dst < N_KV * per_kv) & (src == (dst // per_kv) * t_new + (dst & (t_new - 1))), 1.0, 0.0)
        sel_ref[...] = jnp.dot(sel.astype(BF16), unpool.astype(BF16), preferred_element_type=F32)


def _page_specs(n_chunk_pages):
    def spec(j):
        return pl.BlockSpec((None, PAGE_ROWS, HEAD_DIM), lambda b, c, pt: (pt[b, c * n_chunk_pages + j], 0, 0))
    return [spec(j) for j in range(n_chunk_pages)]


def sample_compressed(page_table, cache3, cmp_new, qn, w1, b1, w2, kc_norm, tabl, t_new):
    n_seq, n_pages = page_table.shape
    assert n_pages % PAGES_PER_STEP == 0
    past = n_pages * PAGE
    n_half = past // CMP_STRIDE
    n_blocks = -(-(past + t_new) // SEL_BLOCK)
    sel_rows = -(-n_blocks // SUBLANES) * SUBLANES
    const = lambda *shape: pl.BlockSpec(shape, lambda b, c, pt: (0,) * len(shape))
    row = lambda width: pl.BlockSpec((t_new, width), lambda b, c, pt: (b, 0))
    out = lambda rows: pl.BlockSpec((None, rows, LANES), lambda b, c, pt: (b, 0, 0))
    grid_spec = pltpu.PrefetchScalarGridSpec(
        num_scalar_prefetch=1,
        grid=(n_seq, n_pages // PAGES_PER_STEP),
        in_specs=[pl.BlockSpec(memory_space=pl.ANY),
                  row(2 * KV_W), row(NSA_W), const(2, CMP_STRIDE * HEAD_DIM, 2 * CMP_HID), const(2, 1, CMP_HID),
                  const(2, CMP_HID, HEAD_DIM), const(1, HEAD_DIM), const(NUM_BUCKETS, LANES)],
        out_specs=[out(sel_rows), out(HEAD_DIM)],
        scratch_shapes=[pltpu.VMEM((TOKEN_ROWS, PAGES_PER_STEP * PAGE_HALVES, CMP_STRIDE * HEAD_DIM), F32),
                        pltpu.VMEM((TOKEN_ROWS, n_half + SUBLANES, 2 * CMP_HID), F32),
                        pltpu.VMEM((2 * STAGE_ROWS, HEAD_DIM), F32),
                        pltpu.SemaphoreType.DMA((2,))])
    return pl.pallas_call(
        functools.partial(_sample_cmp_kernel, past=past, t_new=t_new, n_blocks=n_blocks, sel_rows=sel_rows),
        grid_spec=grid_spec,
        out_shape=[jax.ShapeDtypeStruct((n_seq, sel_rows, LANES), F32),
                   jax.ShapeDtypeStruct((n_seq, HEAD_DIM, LANES), F32)],
        compiler_params=_cparams(("arbitrary", "arbitrary")),
        name="sample_cmp",
    )(page_table, cache3, cmp_new, qn, w1, b1, w2, kc_norm.reshape(1, HEAD_DIM), tabl)


def _sample_sel_kernel(pt_ref, *refs, past, t_new, win_buf):
    pages = refs[:PAGES_PER_STEP]
    (new_ref, q_ref, sel_ref, ocmp_ref, win_ref, wnew_ref, gate_ref, tabl_ref,
     o_ref, m_ref, l_ref, acc_ref, owin_ref, s_ref) = refs[PAGES_PER_STEP:]
    c = pl.program_id(1)
    per_kv = GROUP * t_new
    qz = _lane_queries(q_ref[...], t_new)
    far = tabl_ref[NUM_BUCKETS - 1:NUM_BUCKETS, :]
    new_rows = 2 * SUBLANES

    def new_kv(ref, kind):
        return [_pad_rows(ref[:, (kind * N_KV + kv) * HEAD_DIM:(kind * N_KV + kv + 1) * HEAD_DIM], new_rows).astype(BF16)
                for kv in range(N_KV)]

    def dist(pos0, rows):
        shape = (rows, LANES)
        return past + (_lane_iota(shape) & (t_new - 1)) - pos0 - _row_iota(shape)

    def update(s, values):
        m_prev = m_ref[...]
        m_new = jnp.maximum(m_prev, jnp.max(s, axis=0, keepdims=True))
        m_safe = jnp.where(m_new == -jnp.inf, 0.0, m_new)
        alpha = jnp.exp(m_prev - m_safe)
        p = jnp.exp(s - m_safe)
        l_ref[...] = alpha * l_ref[...] + jnp.sum(p, axis=0, keepdims=True)
        acc = alpha * acc_ref[...]
        for vals, lo, hi_ in values:
            acc = acc + _lane_values(vals, p[lo:hi_], per_kv)
        acc_ref[...] = acc
        m_ref[...] = m_new

    @pl.when(c == 0)
    def _():
        m_ref[...] = jnp.full(m_ref.shape, -jnp.inf, F32)
        l_ref[...] = jnp.zeros(l_ref.shape, F32)
        acc_ref[...] = jnp.zeros(acc_ref.shape, F32)

        kw = [win_ref[pl.ds(kv, win_buf, stride=TOKEN_ROWS), :].astype(BF16) for kv in range(N_KV)]
        vw = [win_ref[pl.ds(N_KV + kv, win_buf, stride=TOKEN_ROWS), :].astype(BF16) for kv in range(N_KV)]
        pieces = []
        near0 = (win_buf - FAR_DIST + 1) // SUBLANES * SUBLANES
        for keys, rel, near in ((kw, dist(past - win_buf, win_buf), near0), (new_kv(wnew_ref, 0), dist(past, new_rows), 0)):
            bias = _lane_bias(rel[near:], tabl_ref)
            if near:
                bias = jnp.concatenate([jnp.broadcast_to(far, (near, LANES)), bias], axis=0)
            s = _lane_scores(keys, qz) + bias
            pieces.append(jnp.where((rel >= 0) & (rel < WINDOW), s, -jnp.inf))
        m = jnp.maximum(jnp.max(pieces[0], axis=0, keepdims=True), jnp.max(pieces[1], axis=0, keepdims=True))
        m = jnp.where(m == -jnp.inf, 0.0, m)
        e0 = jnp.exp(pieces[0] - m)
        e1 = jnp.exp(pieces[1] - m)
        tot = jnp.sum(e0, axis=0, keepdims=True) + jnp.sum(e1, axis=0, keepdims=True)
        num = _lane_values(vw, e0, per_kv) + _lane_values(new_kv(wnew_ref, 1), e1, per_kv)
        owin_ref[...] = num / jnp.maximum(tot, 1e-30)

        rel = dist(past, new_rows)
        s = _lane_scores(new_kv(new_ref, 0), qz) + _lane_bias(rel, tabl_ref)
        chosen = sel_ref[past // SEL_BLOCK:past // SEL_BLOCK + 1, :] > 0.5
        update(jnp.where(chosen & (rel >= 0), s, -jnp.inf), [(new_kv(new_ref, 1), 0, new_rows)])

    values = []
    for p in range(PAGES_PER_STEP):
        page = c * PAGES_PER_STEP + p
        keys = [pages[p][pl.ds(kv, PAGE, stride=TOKEN_ROWS), :].astype(BF16) for kv in range(N_KV)]
        s = _lane_scores(keys, qz) + far
        per_block = PAGE // SEL_BLOCK
        chosen = jnp.concatenate(
            [jnp.broadcast_to(sel_ref[pl.ds(page * per_block + i, 1), :], (SEL_BLOCK, LANES)) for i in range(per_block)],
            axis=0)
        s_ref[p * PAGE:(p + 1) * PAGE, :] = jnp.where(chosen > 0.5, s, -jnp.inf)
        vals = [pages[p][pl.ds(N_KV + kv, PAGE, stride=TOKEN_ROWS), :].astype(BF16) for kv in range(N_KV)]
        values.append((vals, p * PAGE, (p + 1) * PAGE))

    @pl.when(c == pl.num_programs(1) - 1)
    def _():
        rows = slice((PAGES_PER_STEP - 1) * PAGE, PAGES_PER_STEP * PAGE)
        s_ref[rows, :] = s_ref[rows, :] + (_lane_bias(dist(past - PAGE, PAGE), tabl_ref) - far)

    update(s_ref[...], values)

    @pl.when(c == pl.num_programs(1) - 1)
    def _():
        gt = _sigmoid(gate_ref[...])
        osel = acc_ref[...] / jnp.maximum(l_ref[...], 1e-30)
        o_ref[...] = gt[0:1] * ocmp_ref[...] + gt[1:2] * osel + gt[2:3] * owin_ref[...]


def sample_selected(page_table, cache3, sel_new, qn, sel_t, ocmp_t, win3, win_new, gate_l, tabl, t_new):
    n_seq, n_pages = page_table.shape
    past = n_pages * PAGE
    win_buf = win3.shape[1] // TOKEN_ROWS
    assert (past - FAR_DIST) // PAGE >= n_pages - 1, "only the last page may hold near keys"
    sel_rows = sel_t.shape[1]
    const = lambda *shape: pl.BlockSpec(shape, lambda b, c, pt: (0,) * len(shape))
    row = lambda width: pl.BlockSpec((t_new, width), lambda b, c, pt: (b, 0))
    per_seq = lambda rows, width: pl.BlockSpec((None, rows, width), lambda b, c, pt: (b, 0, 0))
    grid_spec = pltpu.PrefetchScalarGridSpec(
        num_scalar_prefetch=1,
        grid=(n_seq, n_pages // PAGES_PER_STEP),
        in_specs=_page_specs(PAGES_PER_STEP) + [
            row(2 * KV_W), row(NSA_W), per_seq(sel_rows, LANES), per_seq(HEAD_DIM, LANES),
            per_seq(win_buf * TOKEN_ROWS, HEAD_DIM), row(2 * KV_W), per_seq(SUBLANES, LANES),
            const(NUM_BUCKETS, LANES)],
        out_specs=per_seq(HEAD_DIM, LANES),
        scratch_shapes=[pltpu.VMEM((1, LANES), F32), pltpu.VMEM((1, LANES), F32), pltpu.VMEM((HEAD_DIM, LANES), F32),
                        pltpu.VMEM((HEAD_DIM, LANES), F32), pltpu.VMEM((PAGES_PER_STEP * PAGE, LANES), F32)])
    return pl.pallas_call(
        functools.partial(_sample_sel_kernel, past=past, t_new=t_new, win_buf=win_buf),
        grid_spec=grid_spec,
        out_shape=jax.ShapeDtypeStruct((n_seq, HEAD_DIM, LANES), F32),
        compiler_params=_cparams(("parallel", "arbitrary")),
        name="sample_sel",
    )(page_table, *([cache3] * PAGES_PER_STEP), sel_new, qn, sel_t, ocmp_t, win3, win_new, gate_l, tabl)


def _split_in_proj(w_in):
    d, cols = w_in.shape
    assert cols == COL_GATE + N_GATE + MERGE_COLS
    wt = jnp.swapaxes(w_in, 0, 1).astype(BF16)
    return wt, PROJ_COLS, wt[COL_GATE + N_GATE:]


def _compress_weights(k_w1, k_b1, k_w2, v_w1, v_b1, v_w2):
    half = CMP_STRIDE * HEAD_DIM
    cat = lambda w: jnp.concatenate([w[:half], w[half:]], axis=1)
    w1 = jnp.stack([cat(k_w1), cat(v_w1)]).astype(BF16)
    b1 = jnp.stack([k_b1, v_b1]).reshape(2, 1, CMP_HID)
    w2 = jnp.stack([k_w2, v_w2]).astype(BF16)
    return w1, b1, w2


def _lane_table(rel_bias, t_new):
    tab = jnp.repeat(rel_bias, t_new, axis=1)
    return jnp.pad(tab, ((0, 0), (0, LANES - tab.shape[1])))


def _gate_lanes(proj, n_seq, t_new):
    per_kv = GROUP * 3
    g = jnp.stack([proj[:, COL_GATE + kv * per_kv:COL_GATE + (kv + 1) * per_kv] for kv in range(N_KV)], axis=1)
    g = g.reshape(n_seq, t_new, N_KV, GROUP, 3).transpose(0, 4, 2, 3, 1).reshape(n_seq, 3, N_HEADS_B * t_new)
    return jnp.pad(g, ((0, 0), (0, SUBLANES - 3), (0, LANES - N_HEADS_B * t_new)))


def _tiles(m, t_len):
    big = m >= 1024
    return dict(proj_tm=1024 if big else m, proj_tn=1024, prep_tm=512 if big else m, post_tm=256,
                mlp_tm=512 if big else m, mlp_tf=1024,
                hgrn_chunk=min(t_len, 256), hgrn_heads=N_HEADS_A)


def _trunk(x, mods, s0, w, nsa_fn, attention_operands):
    nb, t_len, d = x.shape
    m = nb * t_len
    tl = _tiles(m, t_len)
    sh1, sc1, g1, sh2, sc2, g2 = mods
    x2 = x.reshape(m, d)
    proj, merge = in_proj(x2, w["norm1"], sc1, sh1, *w["w_in"], t_len, tl["proj_tm"], tl["proj_tn"])
    o_a, s_fin = hgrn(proj, w["lb"], w["hgrn_onorm"], s0, nb, t_len, tl["hgrn_chunk"], tl["hgrn_heads"])
    qn, cmp_new, sel_new, win_new, *att = nsa_prep(proj, w["q_norm"], w["ks_norm"], w["kw_norm"], tl["prep_tm"], t_len,
                                                   attention_operands)
    o_b = nsa_fn(proj, qn, cmp_new, sel_new, win_new, *att[:1])
    x1, h2 = post_attn(x2, o_a, o_b, merge, w["w_ba"], w["w_bb"], w["w_out"], g1, w["norm2"], sc2, sh2, t_len,
                       tl["post_tm"])
    y = mlp(h2, x1, w["mlp_w1"], w["mlp_w2"], g2, t_len, tl["mlp_tm"], tl["mlp_tf"])
    if attention_operands:
        cmp_new, sel_new = att[1:]
    return y.reshape(nb, t_len, d), s_fin, (cmp_new, sel_new, win_new)


def kernel(x_prompt, x_sample, c_prompt, c_sample, cache_cmp_kv, cache_sel_kv, cache_win_kv, state_hgrn, page_table, hgrn_lb_logits, rel_bias, ada_w, ada_b, norm1, norm2, w_in, hgrn_onorm, nsa_q_norm, nsa_kc_norm, nsa_ks_norm, nsa_kw_norm, cmp_k_w1, cmp_k_b1, cmp_k_w2, cmp_v_w1, cmp_v_b1, cmp_v_w2, w_branch_a, w_branch_b, w_out, mlp_w1, mlp_w2):
    n_p, t_p, d = x_prompt.shape
    n_s, t_s, _ = x_sample.shape
    past = page_table.shape[1] * PAGE
    win_buf = cache_win_kv.shape[2]
    kvw = 2 * KV_W
    layer = 0

    lb_all = jnp.cumsum(jax.nn.softmax(hgrn_lb_logits.astype(F32), axis=0), axis=0)
    cw1, cb1, cw2 = _compress_weights(cmp_k_w1[layer], cmp_k_b1[layer], cmp_k_w2[layer],
                                      cmp_v_w1[layer], cmp_v_b1[layer], cmp_v_w2[layer])
    w = dict(norm1=norm1[layer], norm2=norm2[layer], w_in=_split_in_proj(w_in[layer]),
             hgrn_onorm=hgrn_onorm[layer], lb=lb_all[layer], q_norm=nsa_q_norm[layer],
             ks_norm=nsa_ks_norm[layer], kw_norm=nsa_kw_norm[layer],
             w_ba=w_branch_a[layer].astype(BF16), w_bb=w_branch_b[layer].astype(BF16),
             w_out=w_out[layer].astype(BF16), mlp_w1=mlp_w1[layer].astype(BF16), mlp_w2=mlp_w2[layer].astype(BF16))
    tab = rel_bias.astype(F32).reshape(NUM_BUCKETS * N_HEADS_B)

    mods = ada_mods(jnp.concatenate([c_prompt, c_sample], axis=0).astype(F32), ada_w[layer], ada_b[layer])
    mods = jnp.split(mods, 6, axis=-1)
    mods_p = [a[:n_p].reshape(n_p, 1, d) for a in mods]
    mods_s = [jnp.repeat(a[n_p:], t_s, axis=0).reshape(1, n_s * t_s, d) for a in mods]

    def nsa_p(proj, qn, cmp_new, sel_new, win_new, att):
        tail = jnp.zeros((n_p * CMP_STRIDE, kvw), F32)
        cmp_tok = compress(cmp_new, tail, cw1, cb1, cw2, nsa_kc_norm[layer], n_p, t_p // CMP_STRIDE)
        return nsa_prompt_attention(tab, qn, cmp_tok, att, proj, n_p, t_p)

    s0_p = jnp.zeros((n_p, N_HEADS_A, HEAD_DIM, HEAD_DIM), F32)
    y_p, hg_p, (cmp_p, sel_p, win_p) = _trunk(x_prompt, mods_p, s0_p, w, nsa_p, True)

    def nsa_s(proj, qn, cmp_new, sel_new, win_new):
        n_phys = cache_cmp_kv.shape[1]
        cmp3 = cache_cmp_kv[layer].reshape(n_phys, PAGE_ROWS, HEAD_DIM)
        sel3 = cache_sel_kv[layer].reshape(n_phys, PAGE_ROWS, HEAD_DIM)
        win3 = cache_win_kv[layer].reshape(n_s, win_buf * TOKEN_ROWS, HEAD_DIM)
        tabl = _lane_table(rel_bias.astype(F32), t_s)
        sel_t, ocmp_t = sample_compressed(page_table, cmp3, cmp_new, qn, cw1, cb1, cw2, nsa_kc_norm[layer], tabl, t_s)
        o_t = sample_selected(page_table, sel3, sel_new, qn, sel_t, ocmp_t, win3, win_new,
                              _gate_lanes(proj, n_s, t_s), tabl, t_s)
        o = o_t[:, :, :N_HEADS_B * t_s].reshape(n_s, HEAD_DIM, N_HEADS_B, t_s)
        return o.transpose(0, 3, 2, 1).reshape(n_s * t_s, NSA_W).astype(BF16)

    y_s, hg_s, (cmp_s, sel_s, win_s) = _trunk(x_sample, mods_s, state_hgrn[layer], w, nsa_s, False)

    dt = x_prompt.dtype
    pages = lambda a: a.reshape(1, n_p, t_p // PAGE, PAGE, 2, N_KV, HEAD_DIM).astype(dt)
    rows_s = lambda a: a.reshape(1, n_s, t_s, 2, N_KV, HEAD_DIM).astype(dt)
    win_keep = min(WINDOW, t_p)
    win_p_out = win_p.reshape(n_p, t_p, kvw)[:, t_p - win_keep:].reshape(1, n_p, win_keep, 2, N_KV, HEAD_DIM)
    win_rows = cache_win_kv[layer].reshape(n_s, win_buf * TOKEN_ROWS, HEAD_DIM).astype(F32)
    win_s_out = jnp.concatenate([win_rows, win_s.reshape(n_s, t_s * TOKEN_ROWS, HEAD_DIM)], axis=1)
    win_s_out = win_s_out[:, t_s * TOKEN_ROWS:].reshape(1, n_s, win_buf, 2, N_KV, HEAD_DIM)
    return (y_p, y_s, pages(cmp_p), rows_s(cmp_s), pages(sel_p), rows_s(sel_s),
            win_p_out.astype(dt), win_s_out.astype(dt), hg_p[None].astype(dt), hg_s[None].astype(dt))
```
